```python
import math
import numpy as np
import jax
import jax.numpy as jnp
from jax import lax


D_MODEL = 1024
BATCH = 8
SEQ = 4096
DEPTH = 2

PLE_DIM = 256
GRID_W = 64
BRANCH_W = 512
N_BRANCH = 4
N_IN = 10 * BRANCH_W + N_BRANCH * D_MODEL
NORM_EPS = 1e-6
LRU_BLOCKS = 8
LRU_BLOCK_DIM = BRANCH_W // LRU_BLOCKS
LRU_CONV_W = 4
LRU_C = 8.0
NA_HEADS = 8
NA_HEAD_DIM = BRANCH_W // NA_HEADS
NA_ROWS_MAX = 8
NA_COLS = 16
NA_COL_BLOCKS = GRID_W // NA_COLS
NA_BAND = 2 * NA_COLS
SSM_GROUP = 16
SSM_GROUPS = BRANCH_W // SSM_GROUP
SSM_STATE = 64
POOL_WINDOWS = (2, 4, 8, 16)
POOL_GROUP = BRANCH_W // 4

kernel_name = 'hybrid_lru_natten_s5_pool_encoder'


def rms_norm(x, g):
    x32 = x.astype(jnp.float32)
    y = x32 * lax.rsqrt(jnp.mean(x32 * x32, axis=-1, keepdims=True) + NORM_EPS)
    return (y * g.astype(jnp.float32)).astype(x.dtype)


def _linear_combine(left, right):
    a1, b1 = left
    a2, b2 = right
    return a1 * a2, a2 * b1 + b2


def _complex_combine(left, right):
    a1r, a1i, b1r, b1i = left
    a2r, a2i, b2r, b2i = right
    return (a1r * a2r - a1i * a2i,
            a1r * a2i + a1i * a2r,
            a2r * b1r - a2i * b1i + b2r,
            a2r * b1i + a2i * b1r + b2i)


def _rglru_direction(xc, w_r, b_r, w_i, b_i, lam, reverse):
    bsz, s, w = xc.shape
    xb = xc.reshape(bsz, s, LRU_BLOCKS, LRU_BLOCK_DIM)
    r = jax.nn.sigmoid(jnp.einsum('bsnc,ncd->bsnd', xb, w_r).reshape(bsz, s, w) + b_r)
    gi = jax.nn.sigmoid(jnp.einsum('bsnc,ncd->bsnd', xb, w_i).reshape(bsz, s, w) + b_i)
    log_a = -LRU_C * r * jax.nn.softplus(-lam)
    a = jnp.exp(log_a)
    b = jnp.sqrt(-jnp.expm1(2.0 * log_a)) * (gi * xc)
    _, h = lax.associative_scan(_linear_combine, (a, b), axis=1, reverse=reverse)
    return h


def rglru_mixer(u, conv_w, conv_b, w_r, b_r, w_i, b_i, lam):
    f32 = jnp.float32
    w = u.shape[-1]
    pad_l = LRU_CONV_W // 2
    xc = lax.conv_general_dilated(u.astype(f32), conv_w.astype(f32)[:, None, :], window_strides=(1,),
                                  padding=[(pad_l, LRU_CONV_W - 1 - pad_l)],
                                  dimension_numbers=('NWC', 'WIO', 'NWC'),
                                  feature_group_count=w) + conv_b.astype(f32)
    h_f = _rglru_direction(xc, w_r[0].astype(f32), b_r[0].astype(f32), w_i[0].astype(f32),
                           b_i[0].astype(f32), lam[0].astype(f32), False)
    h_b = _rglru_direction(xc, w_r[1].astype(f32), b_r[1].astype(f32), w_i[1].astype(f32),
                           b_i[1].astype(f32), lam[1].astype(f32), True)
    return h_f + h_b


def neighbourhood_attention(q, k, v, q_gain, k_gain, rpb):
    f32 = jnp.float32
    bsz, s, w = q.shape
    rows = s // GRID_W
    kr = min(NA_ROWS_MAX, rows)
    h, dh = NA_HEADS, NA_HEAD_DIM
    qn = rms_norm(q.reshape(bsz, s, h, dh), q_gain).astype(f32) * (dh ** -0.5)
    kn = rms_norm(k.reshape(bsz, s, h, dh), k_gain).astype(f32)
    qg = qn.reshape(bsz, rows, NA_COL_BLOCKS, NA_COLS, h, dh)
    kg = kn.reshape(bsz, rows, GRID_W, h, dh)
    vg = v.astype(f32).reshape(bsz, rows, GRID_W, h, dh)
    qcol = np.arange(GRID_W).reshape(NA_COL_BLOCKS, NA_COLS)
    band_start = np.clip(qcol[:, 0] - NA_COLS // 2, 0, GRID_W - NA_BAND)
    kcol = band_start[:, None] + np.arange(NA_BAND)
    win_start = np.clip(qcol - NA_COLS // 2, 0, GRID_W - NA_COLS)
    col_mask = ((kcol[:, None, :] >= win_start[:, :, None]) &
                (kcol[:, None, :] < win_start[:, :, None] + NA_COLS))
    dc_idx = np.clip(kcol[:, None, :] - qcol[:, :, None], -(NA_COLS - 1), NA_COLS - 1) + NA_COLS - 1
    rpb32 = rpb.astype(f32)
    mask = jnp.asarray(col_mask)[:, :, None, :]

    def one_row(r):
        rs = jnp.clip(r - kr // 2, 0, rows - kr)
        k_blk = lax.dynamic_slice_in_dim(kg, rs, kr, axis=1)[:, :, kcol]
        v_blk = lax.dynamic_slice_in_dim(vg, rs, kr, axis=1)[:, :, kcol]
        q_r = lax.dynamic_index_in_dim(qg, r, axis=1, keepdims=False)
        sc = jnp.einsum('bjqhd,brjkhd->bhjqrk', q_r, k_blk)
        dr_idx = rs + jnp.arange(kr) - r + NA_ROWS_MAX - 1
        bias = rpb32[:, dr_idx][:, :, dc_idx].transpose(0, 2, 3, 1, 4)
        sc = jnp.where(mask, sc + bias[None], -1e30)
        pr = jax.nn.softmax(sc.reshape(sc.shape[:4] + (kr * NA_BAND,)), axis=-1).reshape(sc.shape)
        return jnp.einsum('bhjqrk,brjkhd->bjqhd', pr, v_blk)

    out = lax.map(one_row, jnp.arange(rows))
    return jnp.moveaxis(out, 0, 1).reshape(bsz, s, w)


def s5_mixer(u, a_re, a_im, log_dt, b_re, b_im, c_re, c_im, d_skip, glu_w, glu_b):
    f32 = jnp.float32
    bsz, s, w = u.shape
    u32 = u.astype(f32)
    ug = u32.reshape(bsz, s, SSM_GROUPS, SSM_GROUP)
    y = u32 * d_skip.astype(f32)
    for d in range(2):
        lr = jnp.minimum(a_re[d].astype(f32), -1e-4)
        li = a_im[d].astype(f32)
        dt = jnp.exp(log_dt[d].astype(f32))[:, None]
        mag = jnp.exp(lr * dt)
        abar_r = mag * jnp.cos(li * dt)
        abar_i = mag * jnp.sin(li * dt)
        nr = abar_r - 1.0
        den = lr * lr + li * li
        fr = ((nr * lr + abar_i * li) / den)[..., None]
        fi = ((abar_i * lr - nr * li) / den)[..., None]
        br_, bi_ = b_re[d].astype(f32), b_im[d].astype(f32)
        bbar_r = fr * br_ - fi * bi_
        bbar_i = fr * bi_ + fi * br_
        bu_r = jnp.einsum('bsgc,gpc->bsgp', ug, bbar_r)
        bu_i = jnp.einsum('bsgc,gpc->bsgp', ug, bbar_i)
        a_r = jnp.broadcast_to(abar_r, (1, s, SSM_GROUPS, SSM_STATE))
        a_i = jnp.broadcast_to(abar_i, (1, s, SSM_GROUPS, SSM_STATE))
        _, _, sr, si = lax.associative_scan(_complex_combine, (a_r, a_i, bu_r, bu_i),
                                            axis=1, reverse=(d == 1))
        yd = (jnp.einsum('bsgp,gcp->bsgc', sr, c_re[d].astype(f32)) -
              jnp.einsum('bsgp,gcp->bsgc', si, c_im[d].astype(f32)))
        y = y + yd.reshape(bsz, s, w)
    yg = jax.nn.gelu(y)
    return yg * jax.nn.sigmoid(yg @ glu_w.astype(f32) + glu_b.astype(f32))


def pool_mixer(u, w_pool, scale):
    f32 = jnp.float32
    bsz, s, w = u.shape
    ug = u.astype(f32).reshape(bsz, s, len(POOL_WINDOWS), POOL_GROUP)
    cs = jnp.pad(jnp.cumsum(ug, axis=1), ((0, 0), (1, 0), (0, 0), (0, 0)))
    t = jnp.arange(s)
    outs = []
    for g, win in enumerate(POOL_WINDOWS):
        lo = jnp.clip(t - win // 2, 0, s)
        hi = jnp.clip(t - win // 2 + win, 0, s)
        cnt = (hi - lo).astype(f32)[None, :, None]
        outs.append((cs[:, hi, g] - cs[:, lo, g]) / cnt - ug[:, :, g])
    pooled = jnp.stack(outs, axis=2)
    mixed = jnp.einsum('bsgc,gcd->bsgd', pooled, w_pool.astype(f32)).reshape(bsz, s, w)
    return mixed * scale.astype(f32)


def setup_inputs(seed: int = 0) -> dict:
    key = jax.random.key(seed)
    k = jax.random.split(key, 32)
    f32 = jnp.float32
    W, G, P, C = BRANCH_W, SSM_GROUPS, SSM_STATE, SSM_GROUP

    def nrm(i, shape, scale):
        return scale * jax.random.normal(k[i], shape, f32)

    u = jax.random.uniform(k[10], (DEPTH, 2, W), f32, minval=0.9, maxval=0.999)
    a0 = u ** (1.0 / LRU_C)
    lam = jnp.log(a0) - jnp.log1p(-a0)
    a_im = jnp.pi * jnp.arange(P, dtype=f32) + nrm(15, (DEPTH, 2, G, P), 0.01)
    log_dt = jax.random.uniform(k[16], (DEPTH, 2, G), f32, minval=math.log(1e-3), maxval=math.log(1e-1))
    return {
        'x': nrm(0, (BATCH, SEQ, D_MODEL), 1.0),
        'p': nrm(1, (DEPTH, BATCH, SEQ, PLE_DIM), 1.0),
        'norm_scale': 1.0 + nrm(2, (DEPTH, D_MODEL), 0.02),
        'w_in': nrm(3, (DEPTH, D_MODEL, N_IN), D_MODEL ** -0.5),
        'lru_conv_w': nrm(4, (DEPTH, LRU_CONV_W, W), LRU_CONV_W ** -0.5),
        'lru_conv_b': nrm(5, (DEPTH, W), 0.01),
        'lru_w_r': nrm(6, (DEPTH, 2, LRU_BLOCKS, LRU_BLOCK_DIM, LRU_BLOCK_DIM), LRU_BLOCK_DIM ** -0.5),
        'lru_b_r': nrm(7, (DEPTH, 2, W), 0.01),
        'lru_w_i': nrm(8, (DEPTH, 2, LRU_BLOCKS, LRU_BLOCK_DIM, LRU_BLOCK_DIM), LRU_BLOCK_DIM ** -0.5),
        'lru_b_i': nrm(9, (DEPTH, 2, W), 0.01),
        'lru_lambda': lam,
        'na_q_gain': 1.0 + nrm(11, (DEPTH, NA_HEAD_DIM), 0.02),
        'na_k_gain': 1.0 + nrm(12, (DEPTH, NA_HEAD_DIM), 0.02),
        'na_rel_bias': nrm(13, (DEPTH, NA_HEADS, 2 * NA_ROWS_MAX - 1, 2 * NA_COLS - 1), 0.02),
        'ssm_a_re': -0.5 + nrm(14, (DEPTH, 2, G, P), 0.01),
        'ssm_a_im': a_im,
        'ssm_log_dt': log_dt,
        'ssm_b_re': nrm(17, (DEPTH, 2, G, P, C), (2 * C) ** -0.5),
        'ssm_b_im': nrm(18, (DEPTH, 2, G, P, C), (2 * C) ** -0.5),
        'ssm_c_re': nrm(19, (DEPTH, 2, G, C, P), P ** -0.5),
        'ssm_c_im': nrm(20, (DEPTH, 2, G, C, P), P ** -0.5),
        'ssm_d': nrm(21, (DEPTH, W), 1.0),
        'ssm_glu_w': nrm(22, (DEPTH, W, W), W ** -0.5),
        'ssm_glu_b': nrm(23, (DEPTH, W), 0.01),
        'pool_w': nrm(24, (DEPTH, len(POOL_WINDOWS), POOL_GROUP, POOL_GROUP), POOL_GROUP ** -0.5),
        'pool_scale': 1.0 + nrm(25, (DEPTH, W), 0.02),
        'w_branch': nrm(26, (DEPTH, N_BRANCH, W, D_MODEL), W ** -0.5),
        'w_out': nrm(27, (DEPTH, D_MODEL, D_MODEL), D_MODEL ** -0.5),
        'ple_proj': nrm(28, (DEPTH, PLE_DIM, D_MODEL), PLE_DIM ** -0.5),
        'ple_gate': nrm(29, (DEPTH, D_MODEL, D_MODEL), D_MODEL ** -0.5),
    }


def reference(x, p, norm_scale, w_in, lru_conv_w, lru_conv_b, lru_w_r, lru_b_r, lru_w_i, lru_b_i,
              lru_lambda, na_q_gain, na_k_gain, na_rel_bias, ssm_a_re, ssm_a_im, ssm_log_dt,
              ssm_b_re, ssm_b_im, ssm_c_re, ssm_c_im, ssm_d, ssm_glu_w, ssm_glu_b, pool_w,
              pool_scale, w_branch, w_out, ple_proj, ple_gate):
    bsz, s, _ = x.shape
    split_points = [BRANCH_W * j for j in range(1, 11)]
    for i in range(DEPTH):
        hn = rms_norm(x, norm_scale[i])
        z = hn @ w_in[i]
        a_x, a_g, q, k, v, b_g, c_x, c_g, d_x, d_g, merge = jnp.split(z, split_points, axis=-1)
        y_a = rglru_mixer(a_x, lru_conv_w[i], lru_conv_b[i], lru_w_r[i], lru_b_r[i],
                          lru_w_i[i], lru_b_i[i], lru_lambda[i]) * jax.nn.silu(a_g)
        y_b = neighbourhood_attention(q, k, v, na_q_gain[i], na_k_gain[i], na_rel_bias[i]) * jax.nn.silu(b_g)
        y_c = s5_mixer(c_x, ssm_a_re[i], ssm_a_im[i], ssm_log_dt[i], ssm_b_re[i], ssm_b_im[i],
                       ssm_c_re[i], ssm_c_im[i], ssm_d[i], ssm_glu_w[i], ssm_glu_b[i]) * jax.nn.silu(c_g)
        y_d = pool_mixer(d_x, pool_w[i], pool_scale[i]) * jax.nn.silu(d_g)
        ys = jnp.stack([y_a, y_b, y_c, y_d], axis=2).astype(x.dtype)
        gates = jax.nn.sigmoid(merge.reshape(bsz, s, N_BRANCH, D_MODEL))
        merged = jnp.sum(jnp.einsum('bsnw,nwd->bsnd', ys, w_branch[i]) * gates, axis=2)
        x = x + (merged @ w_out[i]).astype(x.dtype)
        x = x + jax.nn.sigmoid(x @ ple_gate[i]) * (p[i] @ ple_proj[i])
    return x
```

```python
import functools

import jax
import jax.numpy as jnp
import numpy as np
from jax import lax
from jax.experimental import pallas as pl
from jax.experimental.pallas import tpu as pltpu

F32 = jnp.float32
BF16 = jnp.bfloat16

D_MODEL = 1024
BRANCH_W = 512
N_BRANCH = 4
NORM_EPS = 1e-6
GRID_W = 64
LRU_C = 8.0
LRU_CONV_W = 4
NA_HEADS = 8
NA_HEAD_DIM = 64
NA_ROWS = 8
NA_COLS = 16
SSM_GROUP = 16
SSM_GROUPS = 32
SSM_STATE = 64
POOL_WINDOWS = (2, 4, 8, 16)
POOL_GROUP = 128

SUBLANES = 8
LANES = 128
S5_CHUNK = 8
S5_OCTETS = 4
S5_FLAT = S5_CHUNK * LANES

TOK_TILE = 512
MERGE_TILE = 256
LRU_STEPS = 128
POOL_STEPS = 128
ATTN_ROWS = 8
S5_BLOCK_ROWS = 512
ELEM_ROWS = 256

VMEM_LIMIT = 56 * 1024 * 1024


def _params(*sem):
    return pltpu.CompilerParams(dimension_semantics=sem, vmem_limit_bytes=VMEM_LIMIT)


def _dot(a, b):
    return jnp.dot(a, b, preferred_element_type=F32)


def _sigmoid(z):
    return 0.5 * jnp.tanh(0.5 * z) + 0.5


def _silu(z):
    return z * _sigmoid(z)


def _gelu_tanh(y):
    return 0.5 * y * (1.0 + jnp.tanh(0.7978845608028654 * (y + 0.044715 * (y * y * y))))


def _rms_rows(x, g):
    ms = jnp.mean(x * x, axis=-1, keepdims=True)
    return x * lax.rsqrt(ms + NORM_EPS) * g


def _head_mean_square(v, ones_ref):
    v2 = v * v
    hi = v2.astype(BF16)
    lo = (v2 - hi.astype(F32)).astype(BF16)
    return (_dot(hi, ones_ref[...]) + _dot(lo, ones_ref[...])) * (1.0 / NA_HEAD_DIM)


def _inproj_body(x_ref, ns_ref, w_ref, ones_ref, qg_ref, kg_ref,
                 ax_ref, q_ref, k_ref, v_ref, cx_ref, dx_ref, stage_ref):
    w = BRANCH_W
    hn = _rms_rows(x_ref[0], ns_ref[...]).astype(BF16)

    def proj(j):
        return _dot(hn, w_ref[:, j * w:(j + 1) * w])

    ax_ref[...] = proj(0).astype(BF16)
    q = proj(1)
    q_ref[...] = (q * lax.rsqrt(_head_mean_square(q, ones_ref) + NORM_EPS) * qg_ref[...]).astype(BF16)
    k = proj(2)
    k_ref[...] = (k * lax.rsqrt(_head_mean_square(k, ones_ref) + NORM_EPS) * kg_ref[...]).astype(BF16)
    v_ref[...] = proj(3).astype(BF16)
    cxv = proj(4)
    nk = stage_ref.shape[1] // S5_CHUNK
    for o in range(S5_OCTETS):
        stage_ref[o] = cxv[:, o * LANES:(o + 1) * LANES]
    for t in range(S5_CHUNK):
        for o in range(S5_OCTETS):
            cx_ref[o, :, t * LANES:(t + 1) * LANES] = (
                stage_ref[o, pl.ds(t, nk, stride=S5_CHUNK), :].astype(BF16))
    dx_ref[...] = proj(5).astype(BF16)


def _inproj(x, norm_scale, w1, ones_bd, q_gain, k_gain):
    b, s, d = x.shape
    w = BRANCH_W
    ts = TOK_TILE
    nt = s // ts
    const = lambda *_: (0, 0)
    tm_spec = pl.BlockSpec((ts, w), lambda bi, i: (i, bi))
    bm_spec = pl.BlockSpec((ts, w), lambda bi, i: (bi * nt + i, 0))
    tm_shape = jax.ShapeDtypeStruct((s, b * w), BF16)
    bm_shape = jax.ShapeDtypeStruct((b * s, w), BF16)
    return pl.pallas_call(
        _inproj_body,
        grid=(b, nt),
        in_specs=[
            pl.BlockSpec((1, ts, d), lambda bi, i: (bi, i, 0)),
            pl.BlockSpec((1, d), const),
            pl.BlockSpec((d, 6 * w), const),
            pl.BlockSpec((w, w), const),
            pl.BlockSpec((1, w), const),
            pl.BlockSpec((1, w), const),
        ],
        out_specs=[
            tm_spec, bm_spec, bm_spec, bm_spec,
            pl.BlockSpec((S5_OCTETS, ts // S5_CHUNK, S5_FLAT), lambda bi, i: (0, i, bi)),
            tm_spec,
        ],
        out_shape=[
            tm_shape, bm_shape, bm_shape, bm_shape,
            jax.ShapeDtypeStruct((S5_OCTETS, s // S5_CHUNK, b * S5_FLAT), BF16),
            tm_shape,
        ],
        scratch_shapes=[pltpu.VMEM((S5_OCTETS, ts, LANES), F32)],
        compiler_params=_params("parallel", "parallel"),
        name="inproj",
    )(x, norm_scale, w1, ones_bd, q_gain, k_gain)


def _fill_extended(ext_ref, main_ref, prev_ref, next_ref, n_prev, n_next, is_first, is_last):
    r = main_ref.shape[0]
    prev = prev_ref[...].astype(F32)
    ext_ref[0:n_prev, :] = jnp.where(is_first, 0.0, prev[prev.shape[0] - n_prev:, :])
    ext_ref[n_prev:n_prev + r, :] = main_ref[...].astype(F32)
    nxt = next_ref[...].astype(F32)
    ext_ref[n_prev + r:n_prev + r + n_next, :] = jnp.where(is_last, 0.0, nxt[0:n_next, :])


def _lru_body(mf_ref, pf_ref, nf_ref, mb_ref, pb_ref, nb_ref, cw_ref, cb_ref, wg_ref, bg_ref, c8_ref,
              hf_ref, hb_ref, ext_ref, af_ref, bf_ref, ab_ref, bb_ref, carry_ref):
    i = pl.program_id(0)
    nt = pl.num_programs(0)
    w = BRANCH_W
    r = mf_ref.shape[0]
    steps = r // SUBLANES
    n_prev, n_next = 2 * SUBLANES, SUBLANES

    @pl.when(i == 0)
    def _():
        carry_ref[...] = jnp.zeros_like(carry_ref)

    def prepare(direction, main_ref, prev_ref, next_ref, chunk, a_ref, b_ref):
        _fill_extended(ext_ref, main_ref, prev_ref, next_ref, n_prev, n_next, chunk == 0, chunk == nt - 1)

        def sub(sb, _):
            r0 = pl.multiple_of(sb * ELEM_ROWS, ELEM_ROWS)
            xc = cb_ref[...]
            for tap in range(LRU_CONV_W):
                xc = xc + cw_ref[tap:tap + 1, :] * ext_ref[pl.ds(r0 + tap * SUBLANES, ELEM_ROWS), :]
            g = _dot(xc.astype(BF16), wg_ref[direction]) + bg_ref[direction]
            rg = _sigmoid(g[:, 0:w])
            ig = _sigmoid(g[:, w:2 * w])
            log_a = -(c8_ref[direction] * rg)
            a = jnp.exp(log_a)
            a_ref[pl.ds(r0, ELEM_ROWS), :] = a
            b_ref[pl.ds(r0, ELEM_ROWS), :] = jnp.sqrt(1.0 - a * a) * (ig * xc)
            return 0

        lax.fori_loop(0, r // ELEM_ROWS, sub, 0)

    prepare(0, mf_ref, pf_ref, nf_ref, i, af_ref, bf_ref)
    prepare(1, mb_ref, pb_ref, nb_ref, nt - 1 - i, ab_ref, bb_ref)

    def step(t, carry):
        hf, hb = carry
        rf = pl.ds(pl.multiple_of(t * SUBLANES, SUBLANES), SUBLANES)
        hf = af_ref[rf, :] * hf + bf_ref[rf, :]
        bf_ref[rf, :] = hf
        rb = pl.ds(pl.multiple_of((steps - 1 - t) * SUBLANES, SUBLANES), SUBLANES)
        hb = ab_ref[rb, :] * hb + bb_ref[rb, :]
        bb_ref[rb, :] = hb
        return hf, hb

    hf, hb = lax.fori_loop(0, steps, step, (carry_ref[0], carry_ref[1]), unroll=8)
    carry_ref[0] = hf
    carry_ref[1] = hb
    hf_ref[...] = bf_ref[...].astype(BF16)
    hb_ref[...] = bb_ref[...].astype(BF16)


def _lru(ax, conv_w, conv_b, w_gate, b_gate, c8):
    rows, w = ax.shape
    r = LRU_STEPS * SUBLANES
    nt = rows // r
    hb = 2 * SUBLANES
    per = r // hb
    last_hb = rows // hb - 1
    const2 = lambda i: (0, 0)
    const3 = lambda i: (0, 0, 0)

    def main(ci):
        return pl.BlockSpec((r, w), lambda i: (ci(i), 0))

    def prev(ci):
        return pl.BlockSpec((hb, w), lambda i: (jnp.maximum(ci(i) * per - 1, 0), 0))

    def nxt(ci):
        return pl.BlockSpec((hb, w), lambda i: (jnp.minimum((ci(i) + 1) * per, last_hb), 0))

    fwd = lambda i: i
    bwd = lambda i: nt - 1 - i
    out_shape = jax.ShapeDtypeStruct((rows, w), BF16)
    return pl.pallas_call(
        _lru_body,
        grid=(nt,),
        in_specs=[
            main(fwd), prev(fwd), nxt(fwd), main(bwd), prev(bwd), nxt(bwd),
            pl.BlockSpec((LRU_CONV_W, w), const2),
            pl.BlockSpec((1, w), const2),
            pl.BlockSpec((2, w, 2 * w), const3),
            pl.BlockSpec((2, 1, 2 * w), const3),
            pl.BlockSpec((2, 1, w), const3),
        ],
        out_specs=[main(fwd), main(bwd)],
        out_shape=[out_shape, out_shape],
        scratch_shapes=[
            pltpu.VMEM((r + 3 * SUBLANES, w), F32),
            pltpu.VMEM((r, w), F32), pltpu.VMEM((r, w), F32),
            pltpu.VMEM((r, w), F32), pltpu.VMEM((r, w), F32),
            pltpu.VMEM((2, SUBLANES, w), F32),
        ],
        compiler_params=_params("arbitrary"),
        name="rglru",
    )(ax, ax, ax, ax, ax, ax, conv_w, conv_b, w_gate, b_gate, c8)


def _pool_body(m_ref, p_ref, n_ref, wp_ref, sc_ref, o_ref, ext_ref, pooled_ref, *, seq_len):
    i = pl.program_id(0)
    nt = pl.num_programs(0)
    r = m_ref.shape[0]
    steps = r // SUBLANES
    halo = max(POOL_WINDOWS) // 2
    hr = halo * SUBLANES
    _fill_extended(ext_ref, m_ref, p_ref, n_ref, hr, hr, i == 0, i == nt - 1)

    t_glob = i * steps + lax.broadcasted_iota(jnp.int32, (r, POOL_GROUP), 0) // SUBLANES
    for g, win in enumerate(POOL_WINDOWS):
        ls = slice(g * POOL_GROUP, (g + 1) * POOL_GROUP)
        e = ext_ref[:, ls]
        lo_t = -halo
        cur = e
        half = 1
        while half < win:
            n = cur.shape[0] - half * SUBLANES
            if half == 1:
                cur = cur[0:n, :] + cur[SUBLANES:SUBLANES + n, :]
                lo_t += 1
            else:
                sh = (half // 2) * SUBLANES
                cur = cur[0:n, :] + cur[2 * sh:2 * sh + n, :]
                lo_t += half // 2
            half *= 2
        off = (0 - lo_t) * SUBLANES
        wsum = cur[off:off + r, :]
        half_w = win // 2
        cnt = (jnp.clip(t_glob + half_w, 0, seq_len) - jnp.clip(t_glob - half_w, 0, seq_len)).astype(F32)
        pooled_ref[:, ls] = wsum / cnt - e[hr:hr + r, :]
    o_ref[...] = (_dot(pooled_ref[...].astype(BF16), wp_ref[...]) * sc_ref[...]).astype(BF16)


def _pool(dx, w_pool_bd, scale, seq_len):
    rows, w = dx.shape
    r = POOL_STEPS * SUBLANES
    nt = rows // r
    hb = (max(POOL_WINDOWS) // 2) * SUBLANES
    per = r // hb
    last_hb = rows // hb - 1
    const2 = lambda i: (0, 0)
    return pl.pallas_call(
        functools.partial(_pool_body, seq_len=seq_len),
        grid=(nt,),
        in_specs=[
            pl.BlockSpec((r, w), lambda i: (i, 0)),
            pl.BlockSpec((hb, w), lambda i: (jnp.maximum(i * per - 1, 0), 0)),
            pl.BlockSpec((hb, w), lambda i: (jnp.minimum((i + 1) * per, last_hb), 0)),
            pl.BlockSpec((w, w), const2),
            pl.BlockSpec((1, w), const2),
        ],
        out_specs=pl.BlockSpec((r, w), lambda i: (i, 0)),
        out_shape=jax.ShapeDtypeStruct((rows, w), BF16),
        scratch_shapes=[pltpu.VMEM((r + 2 * hb, w), F32), pltpu.VMEM((r, w), F32)],
        compiler_params=_params("parallel"),
        name="pool",
    )(dx, dx, dx, w_pool_bd, scale)


def _attn_body(q_ref, k_ref, v_ref, bias_ref, o_ref, *, grid_rows):
    g = pl.program_id(1)
    lane = lax.broadcasted_iota(jnp.int32, (GRID_W, LANES), 1)
    lo_half = lane < NA_HEAD_DIM
    nkeys = NA_ROWS * GRID_W

    def row_body(rr, _):
        r = g * ATTN_ROWS + rr
        rs = jnp.clip(r - NA_ROWS // 2, 0, grid_rows - NA_ROWS)
        d0 = rs - r + (NA_ROWS - 1)
        k0 = pl.multiple_of(rs * GRID_W, GRID_W)
        q0 = pl.multiple_of(rr * GRID_W, GRID_W)
        for hp in range(NA_HEADS // 2):
            ls = slice(hp * LANES, (hp + 1) * LANES)
            kp = k_ref[pl.ds(k0, nkeys), ls]
            vp = v_ref[pl.ds(k0, nkeys), ls]
            qp = q_ref[pl.ds(q0, GRID_W), ls]
            outs = []
            for hh in range(2):
                h = 2 * hp + hh
                qm = jnp.where(lo_half if hh == 0 else jnp.logical_not(lo_half), qp, jnp.zeros_like(qp))
                sc = lax.dot_general(qm, kp, (((1,), (1,)), ((), ())), preferred_element_type=F32)
                bias = jnp.concatenate([bias_ref[h, d0 + 2 * j] for j in range(NA_ROWS // 2)], axis=-1)
                sc = sc + bias
                m = jnp.max(sc, axis=-1, keepdims=True)
                p = jnp.exp(sc - m)
                l = jnp.sum(p, axis=-1, keepdims=True)
                outs.append(_dot(p.astype(BF16), vp) * (1.0 / l))
            o_ref[pl.ds(q0, GRID_W), ls] = jnp.where(lo_half, outs[0], outs[1]).astype(BF16)
        return 0

    lax.fori_loop(0, ATTN_ROWS, row_body, 0)


def _attn(q, k, v, bias_tab, batch):
    rows_total, w = q.shape
    s = rows_total // batch
    grid_rows = s // GRID_W
    ng = grid_rows // ATTN_ROWS
    qr = ATTN_ROWS * GRID_W
    q_spec = pl.BlockSpec((qr, w), lambda b, g: (b * ng + g, 0))
    kv_spec = pl.BlockSpec((s, w), lambda b, g: (b, 0))
    return pl.pallas_call(
        functools.partial(_attn_body, grid_rows=grid_rows),
        grid=(batch, ng),
        in_specs=[q_spec, kv_spec, kv_spec,
                  pl.BlockSpec(bias_tab.shape, lambda b, g: (0, 0, 0, 0))],
        out_specs=q_spec,
        out_shape=jax.ShapeDtypeStruct((rows_total, w), BF16),
        compiler_params=_params("parallel", "parallel"),
        name="natten",
    )(q, k, v, bias_tab)


def _s5_body(u_ref, g_ref, hf_ref, hb_ref, qf_ref, qb_ref, a8_ref, dsk_ref,
             y_ref, sinb_ref, sloc_ref, sin_ref, carry_ref):
    p = pl.program_id(1)
    j = pl.program_id(2)
    nblk = pl.num_programs(2)
    rb = u_ref.shape[1]
    nk = rb // SUBLANES
    half = S5_FLAT // 2

    @pl.when(j == 0)
    def _():
        carry_ref[...] = jnp.zeros_like(carry_ref)

    def sweep(direction, reverse):
        ar = a8_ref[0, 2 * direction:2 * direction + 1, :]
        ai = a8_ref[0, 2 * direction + 1:2 * direction + 2, :]

        def body(n, s):
            sr, si = s
            kk = (nk - 1 - n) if reverse else n
            rows = pl.ds(pl.multiple_of(kk * SUBLANES, SUBLANES), SUBLANES)
            sin_ref[rows, 0:half] = sr
            sin_ref[rows, half:2 * half] = si
            nr = ar * sr - ai * si + sloc_ref[rows, 0:half]
            ni = ar * si + ai * sr + sloc_ref[rows, half:2 * half]
            return nr, ni

        sr, si = lax.fori_loop(0, nk, body, (carry_ref[:, 0:half], carry_ref[:, half:2 * half]), unroll=4)
        carry_ref[:, 0:half] = sr
        carry_ref[:, half:2 * half] = si

    @pl.when(p == 0)
    def _():
        sloc_ref[...] = _dot(u_ref[0], hb_ref[0])
        sweep(1, True)
        blk = nblk - 1 - j
        sinb_ref[pl.ds(pl.multiple_of(blk * rb, rb), rb), :] = sin_ref[...].astype(BF16)

    @pl.when(p == 1)
    def _():
        u = u_ref[0]
        sloc_ref[...] = _dot(u, hf_ref[0])
        sweep(0, False)
        y = _dot(u, g_ref[0])
        y = y + _dot(sin_ref[...].astype(BF16), qf_ref[0])
        y = y + _dot(sinb_ref[pl.ds(pl.multiple_of(j * rb, rb), rb), :], qb_ref[0])
        y = y + dsk_ref[0] * u.astype(F32)
        y_ref[0] = y.astype(BF16)


def _s5(cx, mats):
    g_m, hf_m, hb_m, qf_m, qb_m, a8, dsk = mats
    no, rows, flat = cx.shape
    rb = S5_BLOCK_ROWS
    nblk = rows // rb
    wspec = pl.BlockSpec((1, flat, flat), lambda o, p, j: (o, 0, 0))
    return pl.pallas_call(
        _s5_body,
        grid=(no, 2, nblk),
        in_specs=[
            pl.BlockSpec((1, rb, flat), lambda o, p, j: (o, j + (1 - p) * (nblk - 1 - 2 * j), 0)),
            wspec, wspec, wspec, wspec, wspec,
            pl.BlockSpec((1, 4, flat // 2), lambda o, p, j: (o, 0, 0)),
            pl.BlockSpec((1, 1, flat), lambda o, p, j: (o, 0, 0)),
        ],
        out_specs=pl.BlockSpec((1, rb, flat), lambda o, p, j: (o, p * j, 0)),
        out_shape=jax.ShapeDtypeStruct((no, rows, flat), BF16),
        scratch_shapes=[
            pltpu.VMEM((rows, flat), BF16),
            pltpu.VMEM((rb, flat), F32),
            pltpu.VMEM((rb, flat), F32),
            pltpu.VMEM((SUBLANES, flat), F32),
        ],
        compiler_params=_params("arbitrary", "arbitrary", "arbitrary"),
        name="s5",
    )(cx, g_m, hf_m, hb_m, qf_m, qb_m, a8, dsk)


def _merge_body(x_ref, p_ref, hf_ref, hb_ref, yb_ref, yc_ref, yd_ref,
                ns_ref, wg_ref, wbr_ref, wout_ref, pg_ref, pp_ref, gluw_ref, glub_ref,
                o_ref, stage_ref):
    w, d = BRANCH_W, D_MODEL
    x = x_ref[0]
    hn = _rms_rows(x, ns_ref[...]).astype(BF16)

    def gate(c0, n):
        return _dot(hn, wg_ref[:, c0:c0 + n])

    def branch(n, y):
        return _dot(y.astype(BF16), wbr_ref[n]) * _sigmoid(gate(N_BRANCH * w + n * d, d))

    ya = (hf_ref[...].astype(F32) + hb_ref[...].astype(F32)) * _silu(gate(0, w))
    merged = branch(0, ya)
    merged = merged + branch(1, yb_ref[...].astype(F32) * _silu(gate(w, w)))
    nk = stage_ref.shape[1] // S5_CHUNK
    for t in range(S5_CHUNK):
        for o in range(S5_OCTETS):
            stage_ref[o, pl.ds(t, nk, stride=S5_CHUNK), :] = yc_ref[o, :, t * LANES:(t + 1) * LANES].astype(F32)
    yg = _gelu_tanh(jnp.concatenate([stage_ref[o] for o in range(S5_OCTETS)], axis=-1))
    yc = yg * _sigmoid(_dot(yg.astype(BF16), gluw_ref[...]) + glub_ref[...])
    merged = merged + branch(2, yc * _silu(gate(2 * w, w)))
    merged = merged + branch(3, yd_ref[...].astype(F32) * _silu(gate(3 * w, w)))
    x1 = x + _dot(merged.astype(BF16), wout_ref[...])
    emb = _dot(p_ref[0].astype(BF16), pp_ref[...])
    o_ref[0] = x1 + _sigmoid(_dot(x1.astype(BF16), pg_ref[...])) * emb


def _merge(x, p, hf, hb, yb, yc, yd, norm_scale, wg, wbr, wout, pgate, pproj, glu_w, glu_b):
    b, s, d = x.shape
    w = BRANCH_W
    ts = MERGE_TILE
    nt = s // ts
    pdim = p.shape[-1]
    once = pl.Buffered(1)

    def resident(shape):
        return pl.BlockSpec(shape, lambda bi, i: (0,) * len(shape), pipeline_mode=once)

    tm_spec = pl.BlockSpec((ts, w), lambda bi, i: (i, bi))
    return pl.pallas_call(
        _merge_body,
        grid=(b, nt),
        in_specs=[
            pl.BlockSpec((1, ts, d), lambda bi, i: (bi, i, 0)),
            pl.BlockSpec((1, ts, pdim), lambda bi, i: (bi, i, 0)),
            tm_spec, tm_spec,
            pl.BlockSpec((ts, w), lambda bi, i: (bi * nt + i, 0)),
            pl.BlockSpec((S5_OCTETS, ts // S5_CHUNK, S5_FLAT), lambda bi, i: (0, i, bi)),
            tm_spec,
            resident((1, d)),
            resident(wg.shape), resident(wbr.shape), resident(wout.shape),
            resident(pgate.shape), resident(pproj.shape), resident(glu_w.shape), resident((1, w)),
        ],
        out_specs=pl.BlockSpec((1, ts, d), lambda bi, i: (bi, i, 0)),
        out_shape=jax.ShapeDtypeStruct((b, s, d), F32),
        scratch_shapes=[pltpu.VMEM((S5_OCTETS, ts, LANES), F32)],
        compiler_params=_params("parallel", "parallel"),
        name="merge",
    )(x, p, hf, hb, yb, yc, yd, norm_scale, wg, wbr, wout, pgate, pproj, glu_w, glu_b)


def _block_diag(blocks):
    n, r, c = blocks.shape
    eye = jnp.eye(n, dtype=blocks.dtype)
    return jnp.einsum('nrc,nm->nrmc', blocks, eye).reshape(n * r, n * c)


def _attn_bias_tables(rpb):
    qc = np.arange(GRID_W)[:, None]
    kc = np.arange(GRID_W)[None, :]
    ws = np.clip(qc - NA_COLS // 2, 0, GRID_W - NA_COLS)
    in_win = (kc >= ws) & (kc < ws + NA_COLS)
    dc = np.clip(kc - qc, -(NA_COLS - 1), NA_COLS - 1) + NA_COLS - 1
    tab = jnp.where(jnp.asarray(in_win)[None, None], rpb.astype(F32)[:, :, dc], -1e30)
    return jnp.concatenate([tab[:, :-1], tab[:, 1:]], axis=-1)


def _s5_matrices(a_re, a_im, log_dt, b_re, b_im, c_re, c_im, d_skip):
    L, G, P, C = S5_CHUNK, SSM_GROUPS, SSM_STATE, SSM_GROUP
    f32 = F32
    lr = jnp.minimum(a_re.astype(f32), -1e-4)
    li = a_im.astype(f32)
    dt = jnp.exp(log_dt.astype(f32))[..., None]
    mag = jnp.exp(lr * dt)
    ab_r = mag * jnp.cos(li * dt)
    ab_i = mag * jnp.sin(li * dt)
    nr = ab_r - 1.0
    den = lr * lr + li * li
    fr = ((nr * lr + ab_i * li) / den)[..., None]
    fi = ((ab_i * lr - nr * li) / den)[..., None]
    br, bi = b_re.astype(f32), b_im.astype(f32)
    bb_r = fr * br - fi * bi
    bb_i = fr * bi + fi * br
    cr, ci = c_re.astype(f32), c_im.astype(f32)
    pr, pi = [jnp.ones_like(ab_r)], [jnp.zeros_like(ab_r)]
    for _ in range(L):
        pr.append(pr[-1] * ab_r - pi[-1] * ab_i)
        pi.append(pr[-2] * ab_i + pi[-1] * ab_r)
    pw_r = jnp.stack(pr)
    pw_i = jnp.stack(pi)
    m_r = pw_r[..., None] * bb_r[None] - pw_i[..., None] * bb_i[None]
    m_i = pw_r[..., None] * bb_i[None] + pw_i[..., None] * bb_r[None]
    k_lag = (jnp.einsum('dgop,ndgpi->ndgio', cr, m_r) - jnp.einsum('dgop,ndgpi->ndgio', ci, m_i))
    eye8 = jnp.eye(8, dtype=f32)

    def expand_rows(m):
        t_in = m.shape[0]
        rest = m.shape[3:]
        m = m.reshape((t_in, S5_OCTETS, 8, C) + rest)
        letters = 'uvwxyz'[:len(rest)]
        out = jnp.einsum(f'tojc{letters},jk->otjck{letters}', m, eye8)
        return out.reshape((S5_OCTETS, t_in * 8 * C, 8) + rest)

    tin = np.arange(L)[:, None]
    tout = np.arange(L)[None, :]
    lag = tout - tin
    kf = k_lag[:, 0][np.clip(lag, 0, L - 1)] * jnp.asarray(lag >= 0, f32)[:, :, None, None, None]
    kb = k_lag[:, 1][np.clip(-lag, 0, L - 1)] * jnp.asarray(lag <= 0, f32)[:, :, None, None, None]
    ktot = kf + kb
    g_rows = expand_rows(jnp.transpose(ktot, (0, 2, 3, 1, 4)))
    g_mat = jnp.transpose(g_rows, (0, 1, 3, 2, 4)).reshape(S5_OCTETS, S5_FLAT, S5_FLAT)

    def h_mat(direction, powers):
        mr = jnp.transpose(m_r[powers, direction], (0, 1, 3, 2))
        mi = jnp.transpose(m_i[powers, direction], (0, 1, 3, 2))
        both = jnp.stack([mr, mi], axis=3)
        rows = expand_rows(both)
        return jnp.transpose(rows, (0, 1, 3, 2, 4)).reshape(S5_OCTETS, S5_FLAT, S5_FLAT)

    hf_mat = h_mat(0, np.arange(L - 1, -1, -1))
    hb_mat = h_mat(1, np.arange(L))

    def q_mat(direction, powers):
        pwr = pw_r[powers, direction]
        pwi = pw_i[powers, direction]
        c_r, c_i = cr[direction], ci[direction]
        q_re = jnp.einsum('gcp,tgp->gptc', c_r, pwr) - jnp.einsum('gcp,tgp->gptc', c_i, pwi)
        q_im = -(jnp.einsum('gcp,tgp->gptc', c_r, pwi) + jnp.einsum('gcp,tgp->gptc', c_i, pwr))
        q = jnp.stack([q_re, q_im]).reshape(2, S5_OCTETS, 8, P, L, C)
        q = jnp.einsum('rojptc,jk->orjptkc', q, eye8)
        return q.reshape(S5_OCTETS, S5_FLAT, S5_FLAT)

    qf_mat = q_mat(0, np.arange(1, L + 1))
    qb_mat = q_mat(1, np.arange(L, 0, -1))
    a8 = jnp.stack([pw_r[L, 0], pw_i[L, 0], pw_r[L, 1], pw_i[L, 1]])
    a8 = jnp.transpose(a8.reshape(4, S5_OCTETS, 8 * P), (1, 0, 2))
    dsk = jnp.tile(d_skip.astype(f32).reshape(S5_OCTETS, 1, LANES), (1, 1, S5_CHUNK))
    bf = lambda m: m.astype(BF16)
    return bf(g_mat), bf(hf_mat), bf(hb_mat), bf(qf_mat), bf(qb_mat), a8, dsk


def _layer_params(i, norm_scale, w_in, lru_conv_w, lru_conv_b, lru_w_r, lru_b_r, lru_w_i, lru_b_i,
                  lru_lambda, na_q_gain, na_k_gain, na_rel_bias, ssm_a_re, ssm_a_im, ssm_log_dt,
                  ssm_b_re, ssm_b_im, ssm_c_re, ssm_c_im, ssm_d, ssm_glu_w, ssm_glu_b, pool_w,
                  pool_scale, w_branch, w_out, ple_proj, ple_gate):
    w, d = BRANCH_W, D_MODEL
    wi = w_in[i]
    col = lambda j: wi[:, j * w:(j + 1) * w]
    w1 = jnp.concatenate([col(0), col(2), col(3), col(4), col(6), col(8)], axis=1).astype(BF16)
    wg = jnp.concatenate([col(1), col(5), col(7), col(9), wi[:, 10 * w:]], axis=1).astype(BF16)
    gate_w = jnp.stack([
        jnp.concatenate([_block_diag(lru_w_r[i, dr]), _block_diag(lru_w_i[i, dr])], axis=1) for dr in range(2)
    ]).astype(BF16)
    gate_b = jnp.concatenate([lru_b_r[i], lru_b_i[i]], axis=-1).astype(F32)[:, None, :]
    c8 = (LRU_C * jax.nn.softplus(-lru_lambda[i].astype(F32)))[:, None, :]
    return dict(
        norm=norm_scale[i].astype(F32)[None, :], w1=w1, wg=wg,
        ones=_block_diag(jnp.ones((NA_HEADS, NA_HEAD_DIM, NA_HEAD_DIM), BF16)),
        q_gain=(jnp.tile(na_q_gain[i].astype(F32), NA_HEADS) * NA_HEAD_DIM ** -0.5)[None, :],
        k_gain=jnp.tile(na_k_gain[i].astype(F32), NA_HEADS)[None, :],
        conv_w=lru_conv_w[i].astype(F32), conv_b=lru_conv_b[i].astype(F32)[None, :],
        gate_w=gate_w, gate_b=gate_b, c8=c8,
        bias=_attn_bias_tables(na_rel_bias[i]),
        s5=_s5_matrices(ssm_a_re[i], ssm_a_im[i], ssm_log_dt[i], ssm_b_re[i], ssm_b_im[i],
                        ssm_c_re[i], ssm_c_im[i], ssm_d[i]),
        glu_w=ssm_glu_w[i].astype(BF16), glu_b=ssm_glu_b[i].astype(F32)[None, :],
        pool_w=_block_diag(pool_w[i]).astype(BF16), pool_scale=pool_scale[i].astype(F32)[None, :],
        w_branch=w_branch[i].astype(BF16), w_out=w_out[i].astype(BF16),
        ple_proj=ple_proj[i].astype(BF16), ple_gate=ple_gate[i].astype(BF16),
    )


def _layer(x, p_i, prm):
    b, s, _ = x.shape
    w = BRANCH_W
    ax, q, k, v, cx, dx = _inproj(x, prm['norm'], prm['w1'], prm['ones'], prm['q_gain'], prm['k_gain'])
    hf, hb = _lru(ax.reshape(s * b, w), prm['conv_w'], prm['conv_b'], prm['gate_w'], prm['gate_b'], prm['c8'])
    yb = _attn(q, k, v, prm['bias'], b)
    yc = _s5(cx.reshape(S5_OCTETS, (s // S5_CHUNK) * b, S5_FLAT), prm['s5'])
    yd = _pool(dx.reshape(s * b, w), prm['pool_w'], prm['pool_scale'], s)
    return _merge(x, p_i, hf.reshape(s, b * w), hb.reshape(s, b * w), yb,
                  yc.reshape(S5_OCTETS, s // S5_CHUNK, b * S5_FLAT), yd.reshape(s, b * w),
                  prm['norm'], prm['wg'], prm['w_branch'], prm['w_out'], prm['ple_gate'], prm['ple_proj'],
                  prm['glu_w'], prm['glu_b'])


def kernel(x, p, norm_scale, w_in, lru_conv_w, lru_conv_b, lru_w_r, lru_b_r, lru_w_i, lru_b_i, lru_lambda, na_q_gain, na_k_gain, na_rel_bias, ssm_a_re, ssm_a_im, ssm_log_dt, ssm_b_re, ssm_b_im, ssm_c_re, ssm_c_im, ssm_d, ssm_glu_w, ssm_glu_b, pool_w, pool_scale, w_branch, w_out, ple_proj, ple_gate):
    b, s, d = x.shape
    assert b == SUBLANES and d == D_MODEL and s % (LRU_STEPS * 4) == 0 and s // GRID_W >= NA_ROWS
    depth = w_in.shape[0]
    weights = (norm_scale, w_in, lru_conv_w, lru_conv_b, lru_w_r, lru_b_r, lru_w_i, lru_b_i, lru_lambda,
               na_q_gain, na_k_gain, na_rel_bias, ssm_a_re, ssm_a_im, ssm_log_dt, ssm_b_re, ssm_b_im,
               ssm_c_re, ssm_c_im, ssm_d, ssm_glu_w, ssm_glu_b, pool_w, pool_scale, w_branch, w_out,
               ple_proj, ple_gate)
    for i in range(depth):
        x = _layer(x, p[i], _layer_params(i, *weights))
    return x
```

```python
import functools

import jax
import jax.numpy as jnp
import numpy as np
from jax import lax
from jax.experimental import pallas as pl
from jax.experimental.pallas import tpu as pltpu

F32 = jnp.float32
BF16 = jnp.bfloat16

D_MODEL = 1024
BRANCH_W = 512
N_BRANCH = 4
NORM_EPS = 1e-6
GRID_W = 64
LRU_C = 8.0
LRU_CONV_W = 4
NA_HEADS = 8
NA_HEAD_DIM = 64
NA_ROWS = 8
NA_COLS = 16
SSM_GROUP = 16
SSM_GROUPS = 32
SSM_STATE = 64
POOL_WINDOWS = (2, 4, 8, 16)
POOL_GROUP = 128

SUBLANES = 8
LANES = 128
W_TILES = BRANCH_W // LANES
S5_CHUNK = 8
S5_OCTETS = 4
S5_FLAT = S5_CHUNK * LANES

COL_AX, COL_AG, COL_Q, COL_K, COL_V, COL_BG, COL_CX, COL_CG, COL_DX, COL_DG = range(10)
COL_MERGE = 10 * BRANCH_W // D_MODEL

PROJ_ROWS = 512
MERGE_ROWS = 256
LRU_STEPS = 128
POOL_STEPS = 128
ATTN_ROWS = 8
S5_BLOCK_CHUNKS = 64
ELEM_ROWS = 256

VMEM_LIMIT = 56 * 1024 * 1024


def _params(*sem):
    return pltpu.CompilerParams(dimension_semantics=sem, vmem_limit_bytes=VMEM_LIMIT)


def _dot(a, b):
    return jnp.dot(a, b, preferred_element_type=F32)


def _sigmoid(z):
    return 0.5 * jnp.tanh(0.5 * z) + 0.5


def _silu(z):
    return z * _sigmoid(z)


def _gelu_tanh(y):
    return 0.5 * y * (1.0 + jnp.tanh(0.7978845608028654 * (y + 0.044715 * (y * y * y))))


def _rms_rows(x, g):
    ms = jnp.mean(x * x, axis=-1, keepdims=True)
    return x * lax.rsqrt(ms + NORM_EPS) * g


def _head_mean_square(v, ones_ref):
    v2 = v * v
    hi = v2.astype(BF16)
    lo = (v2 - hi.astype(F32)).astype(BF16)
    return (_dot(hi, ones_ref[...]) + _dot(lo, ones_ref[...])) * (1.0 / NA_HEAD_DIM)


def _layer_spec(shape, layer, *tail):
    tail = tail or (0,) * len(shape)
    return pl.BlockSpec((None,) + tuple(shape), lambda *_: (layer,) + tuple(tail))


def _inproj_body(x_ref, ns_ref, wax_ref, wq_ref, wk_ref, wv_ref, wcx_ref, wdx_ref, ones_ref, qg_ref, kg_ref,
                 ax_ref, q_ref, k_ref, v_ref, cx_ref, dx_ref, stage_ref):
    hn = _rms_rows(x_ref[...], ns_ref[...]).astype(BF16)
    steps = stage_ref.shape[1] // SUBLANES

    def to_batch_lanes(val, out_ref):
        for l in range(W_TILES):
            stage_ref[l] = val[:, l * LANES:(l + 1) * LANES]
        for b in range(SUBLANES):
            for l in range(W_TILES):
                c0 = b * BRANCH_W + l * LANES
                out_ref[:, c0:c0 + LANES] = stage_ref[l, pl.ds(b, steps, stride=SUBLANES), :].astype(BF16)

    ax_ref[...] = _dot(hn, wax_ref[...]).astype(BF16)
    q = _dot(hn, wq_ref[...])
    to_batch_lanes(q * lax.rsqrt(_head_mean_square(q, ones_ref) + NORM_EPS) * qg_ref[...], q_ref)
    k = _dot(hn, wk_ref[...])
    to_batch_lanes(k * lax.rsqrt(_head_mean_square(k, ones_ref) + NORM_EPS) * kg_ref[...], k_ref)
    to_batch_lanes(_dot(hn, wv_ref[...]), v_ref)
    cx_ref[...] = _dot(hn, wcx_ref[...]).astype(BF16)
    dx_ref[...] = _dot(hn, wdx_ref[...]).astype(BF16)


def _inproj(x, layer, norm_scale, w_in, ones_bd, q_gain, k_gain, batch):
    rows, d = x.shape
    w = BRANCH_W
    tm = PROJ_ROWS
    s = rows // batch
    wcol = lambda j: pl.BlockSpec((None, d, w), lambda i: (layer, 0, j))
    tm_spec = pl.BlockSpec((tm, w), lambda i: (i, 0))
    bl_spec = pl.BlockSpec((tm // batch, batch * w), lambda i: (i, 0))
    tm_shape = jax.ShapeDtypeStruct((rows, w), BF16)
    bl_shape = jax.ShapeDtypeStruct((s, batch * w), BF16)
    return pl.pallas_call(
        _inproj_body,
        grid=(rows // tm,),
        in_specs=[
            pl.BlockSpec((tm, d), lambda i: (i, 0)),
            _layer_spec((1, d), layer),
            wcol(COL_AX), wcol(COL_Q), wcol(COL_K), wcol(COL_V), wcol(COL_CX), wcol(COL_DX),
            pl.BlockSpec((w, w), lambda i: (0, 0)),
            _layer_spec((1, w), layer), _layer_spec((1, w), layer),
        ],
        out_specs=[tm_spec, bl_spec, bl_spec, bl_spec, tm_spec, tm_spec],
        out_shape=[tm_shape, bl_shape, bl_shape, bl_shape, tm_shape, tm_shape],
        scratch_shapes=[pltpu.VMEM((W_TILES, tm, LANES), F32)],
        compiler_params=_params("parallel"),
        name="inproj",
    )(x, norm_scale, w_in, w_in, w_in, w_in, w_in, w_in, ones_bd, q_gain, k_gain)


def _fill_extended(ext_ref, main_ref, prev_ref, next_ref, n_prev, n_next, is_first, is_last):
    r = main_ref.shape[0]
    prev = prev_ref[...].astype(F32)
    ext_ref[0:n_prev, :] = jnp.where(is_first, 0.0, prev[prev.shape[0] - n_prev:, :])
    ext_ref[n_prev:n_prev + r, :] = main_ref[...].astype(F32)
    nxt = next_ref[...].astype(F32)
    ext_ref[n_prev + r:n_prev + r + n_next, :] = jnp.where(is_last, 0.0, nxt[0:n_next, :])


def _lru_body(mf_ref, pf_ref, nf_ref, mb_ref, pb_ref, nb_ref, cw_ref, cb_ref, wg_ref, bg_ref, c8_ref,
              hf_ref, hb_ref, ext_ref, af_ref, bf_ref, ab_ref, bb_ref, carry_ref):
    i = pl.program_id(0)
    nt = pl.num_programs(0)
    w = BRANCH_W
    r = mf_ref.shape[0]
    steps = r // SUBLANES
    n_prev, n_next = 2 * SUBLANES, SUBLANES

    @pl.when(i == 0)
    def _():
        carry_ref[...] = jnp.zeros_like(carry_ref)

    def prepare(direction, main_ref, prev_ref, next_ref, chunk, a_ref, b_ref):
        _fill_extended(ext_ref, main_ref, prev_ref, next_ref, n_prev, n_next, chunk == 0, chunk == nt - 1)

        def sub(sb, _):
            r0 = pl.multiple_of(sb * ELEM_ROWS, ELEM_ROWS)
            xc = cb_ref[...]
            for tap in range(LRU_CONV_W):
                xc = xc + cw_ref[tap:tap + 1, :] * ext_ref[pl.ds(r0 + tap * SUBLANES, ELEM_ROWS), :]
            g = _dot(xc.astype(BF16), wg_ref[direction]) + bg_ref[direction]
            rg = _sigmoid(g[:, 0:w])
            ig = _sigmoid(g[:, w:2 * w])
            log_a = -(c8_ref[direction] * rg)
            a = jnp.exp(log_a)
            a_ref[pl.ds(r0, ELEM_ROWS), :] = a
            b_ref[pl.ds(r0, ELEM_ROWS), :] = jnp.sqrt(1.0 - a * a) * (ig * xc)
            return 0

        lax.fori_loop(0, r // ELEM_ROWS, sub, 0)

    prepare(0, mf_ref, pf_ref, nf_ref, i, af_ref, bf_ref)
    prepare(1, mb_ref, pb_ref, nb_ref, nt - 1 - i, ab_ref, bb_ref)

    def step(t, carry):
        hf, hb = carry
        rf = pl.ds(pl.multiple_of(t * SUBLANES, SUBLANES), SUBLANES)
        hf = af_ref[rf, :] * hf + bf_ref[rf, :]
        bf_ref[rf, :] = hf
        rb = pl.ds(pl.multiple_of((steps - 1 - t) * SUBLANES, SUBLANES), SUBLANES)
        hb = ab_ref[rb, :] * hb + bb_ref[rb, :]
        bb_ref[rb, :] = hb
        return hf, hb

    hf, hb = lax.fori_loop(0, steps, step, (carry_ref[0], carry_ref[1]), unroll=8)
    carry_ref[0] = hf
    carry_ref[1] = hb
    hf_ref[...] = bf_ref[...].astype(BF16)
    hb_ref[...] = bb_ref[...].astype(BF16)


def _lru(ax, layer, conv_w, conv_b, w_gate, b_gate, c8):
    rows, w = ax.shape
    r = LRU_STEPS * SUBLANES
    nt = rows // r
    hb = 2 * SUBLANES
    per = r // hb
    last_hb = rows // hb - 1

    def main(ci):
        return pl.BlockSpec((r, w), lambda i: (ci(i), 0))

    def prev(ci):
        return pl.BlockSpec((hb, w), lambda i: (jnp.maximum(ci(i) * per - 1, 0), 0))

    def nxt(ci):
        return pl.BlockSpec((hb, w), lambda i: (jnp.minimum((ci(i) + 1) * per, last_hb), 0))

    fwd = lambda i: i
    bwd = lambda i: nt - 1 - i
    out_shape = jax.ShapeDtypeStruct((rows, w), BF16)
    return pl.pallas_call(
        _lru_body,
        grid=(nt,),
        in_specs=[
            main(fwd), prev(fwd), nxt(fwd), main(bwd), prev(bwd), nxt(bwd),
            _layer_spec((LRU_CONV_W, w), layer),
            _layer_spec((1, w), layer),
            _layer_spec((2, w, 2 * w), layer),
            _layer_spec((2, 1, 2 * w), layer),
            _layer_spec((2, 1, w), layer),
        ],
        out_specs=[main(fwd), main(bwd)],
        out_shape=[out_shape, out_shape],
        scratch_shapes=[
            pltpu.VMEM((r + 3 * SUBLANES, w), F32),
            pltpu.VMEM((r, w), F32), pltpu.VMEM((r, w), F32),
            pltpu.VMEM((r, w), F32), pltpu.VMEM((r, w), F32),
            pltpu.VMEM((2, SUBLANES, w), F32),
        ],
        compiler_params=_params("arbitrary"),
        name="rglru",
    )(ax, ax, ax, ax, ax, ax, conv_w, conv_b, w_gate, b_gate, c8)


def _pool_body(m_ref, p_ref, n_ref, wp_ref, sc_ref, o_ref, ext_ref, pooled_ref, *, seq_len):
    i = pl.program_id(0)
    nt = pl.num_programs(0)
    r = m_ref.shape[0]
    steps = r // SUBLANES
    halo = max(POOL_WINDOWS) // 2
    hr = halo * SUBLANES
    _fill_extended(ext_ref, m_ref, p_ref, n_ref, hr, hr, i == 0, i == nt - 1)

    t_glob = i * steps + lax.broadcasted_iota(jnp.int32, (r, POOL_GROUP), 0) // SUBLANES
    for g, win in enumerate(POOL_WINDOWS):
        ls = slice(g * POOL_GROUP, (g + 1) * POOL_GROUP)
        e = ext_ref[:, ls]
        lo_t = -halo
        cur = e
        half = 1
        while half < win:
            n = cur.shape[0] - half * SUBLANES
            if half == 1:
                cur = cur[0:n, :] + cur[SUBLANES:SUBLANES + n, :]
                lo_t += 1
            else:
                sh = (half // 2) * SUBLANES
                cur = cur[0:n, :] + cur[2 * sh:2 * sh + n, :]
                lo_t += half // 2
            half *= 2
        off = (0 - lo_t) * SUBLANES
        wsum = cur[off:off + r, :]
        half_w = win // 2
        cnt = (jnp.clip(t_glob + half_w, 0, seq_len) - jnp.clip(t_glob - half_w, 0, seq_len)).astype(F32)
        pooled_ref[:, ls] = wsum / cnt - e[hr:hr + r, :]
    o_ref[...] = (_dot(pooled_ref[...].astype(BF16), wp_ref[...]) * sc_ref[...]).astype(BF16)


def _pool(dx, layer, w_pool_bd, scale, seq_len):
    rows, w = dx.shape
    r = POOL_STEPS * SUBLANES
    nt = rows // r
    hb = (max(POOL_WINDOWS) // 2) * SUBLANES
    per = r // hb
    last_hb = rows // hb - 1
    return pl.pallas_call(
        functools.partial(_pool_body, seq_len=seq_len),
        grid=(nt,),
        in_specs=[
            pl.BlockSpec((r, w), lambda i: (i, 0)),
            pl.BlockSpec((hb, w), lambda i: (jnp.maximum(i * per - 1, 0), 0)),
            pl.BlockSpec((hb, w), lambda i: (jnp.minimum((i + 1) * per, last_hb), 0)),
            _layer_spec((w, w), layer),
            _layer_spec((1, w), layer),
        ],
        out_specs=pl.BlockSpec((r, w), lambda i: (i, 0)),
        out_shape=jax.ShapeDtypeStruct((rows, w), BF16),
        scratch_shapes=[pltpu.VMEM((r + 2 * hb, w), F32), pltpu.VMEM((r, w), F32)],
        compiler_params=_params("parallel"),
        name="pool",
    )(dx, dx, dx, w_pool_bd, scale)


def _attn_body(q_ref, k_ref, v_ref, bias_ref, o_ref, *, grid_rows):
    g = pl.program_id(1)
    lane = lax.broadcasted_iota(jnp.int32, (GRID_W, LANES), 1)
    lo_half = lane < NA_HEAD_DIM
    nkeys = NA_ROWS * GRID_W

    def row_body(rr, _):
        r = g * ATTN_ROWS + rr
        rs = jnp.clip(r - NA_ROWS // 2, 0, grid_rows - NA_ROWS)
        d0 = rs - r + (NA_ROWS - 1)
        k0 = pl.multiple_of(rs * GRID_W, GRID_W)
        q0 = pl.multiple_of(rr * GRID_W, GRID_W)
        scores = []
        for hp in range(NA_HEADS // 2):
            ls = slice(hp * LANES, (hp + 1) * LANES)
            kp = k_ref[pl.ds(k0, nkeys), ls]
            qp = q_ref[pl.ds(q0, GRID_W), ls]
            for hh in range(2):
                h = 2 * hp + hh
                qm = jnp.where(lo_half if hh == 0 else jnp.logical_not(lo_half), qp, jnp.zeros_like(qp))
                sc = lax.dot_general(qm, kp, (((1,), (1,)), ((), ())), preferred_element_type=F32)
                bias = jnp.concatenate([bias_ref[h, d0 + 2 * j] for j in range(NA_ROWS // 2)], axis=-1)
                scores.append(sc + bias)
        maxes = [jnp.max(sc, axis=-1, keepdims=True) for sc in scores]
        probs = [jnp.exp(sc - m) for sc, m in zip(scores, maxes)]
        inv = [1.0 / jnp.sum(p, axis=-1, keepdims=True) for p in probs]
        for hp in range(NA_HEADS // 2):
            ls = slice(hp * LANES, (hp + 1) * LANES)
            vp = v_ref[pl.ds(k0, nkeys), ls]
            o0 = _dot(probs[2 * hp].astype(BF16), vp) * inv[2 * hp]
            o1 = _dot(probs[2 * hp + 1].astype(BF16), vp) * inv[2 * hp + 1]
            o_ref[pl.ds(q0, GRID_W), ls] = jnp.where(lo_half, o0, o1).astype(BF16)
        return 0

    lax.fori_loop(0, ATTN_ROWS, row_body, 0)


def _attn(q, k, v, layer, bias_tab, batch):
    s, bw = q.shape
    w = bw // batch
    grid_rows = s // GRID_W
    ng = grid_rows // ATTN_ROWS
    qr = ATTN_ROWS * GRID_W
    q_spec = pl.BlockSpec((qr, w), lambda b, g: (g, b))
    kv_spec = pl.BlockSpec((s, w), lambda b, g: (0, b))
    return pl.pallas_call(
        functools.partial(_attn_body, grid_rows=grid_rows),
        grid=(batch, ng),
        in_specs=[q_spec, kv_spec, kv_spec, _layer_spec(bias_tab.shape[1:], layer)],
        out_specs=q_spec,
        out_shape=jax.ShapeDtypeStruct((s, bw), BF16),
        compiler_params=_params("parallel", "parallel"),
        name="natten",
    )(q, k, v, bias_tab)


def _s5_body(u_ref, g_ref, hf_ref, hb_ref, qf_ref, qb_ref, a8_ref, dsk_ref,
             y_ref, sinb_ref, sloc_ref, sin_ref, carry_ref):
    p = pl.program_id(1)
    j = pl.program_id(2)
    nblk = pl.num_programs(2)
    nk = u_ref.shape[0]
    rb = nk * SUBLANES
    half = S5_FLAT // 2

    @pl.when(j == 0)
    def _():
        carry_ref[...] = jnp.zeros_like(carry_ref)

    u3 = u_ref[...].astype(F32)
    u32 = jnp.concatenate(
        [u3[:, t * SUBLANES:(t + 1) * SUBLANES, :].reshape(rb, LANES) for t in range(S5_CHUNK)], axis=-1)
    u = u32.astype(BF16)

    def sweep(direction, reverse):
        ar = a8_ref[2 * direction:2 * direction + 1, :]
        ai = a8_ref[2 * direction + 1:2 * direction + 2, :]

        def body(n, s):
            sr, si = s
            kk = (nk - 1 - n) if reverse else n
            rows = pl.ds(pl.multiple_of(kk * SUBLANES, SUBLANES), SUBLANES)
            sin_ref[rows, 0:half] = sr
            sin_ref[rows, half:2 * half] = si
            nr = ar * sr - ai * si + sloc_ref[rows, 0:half]
            ni = ar * si + ai * sr + sloc_ref[rows, half:2 * half]
            return nr, ni

        sr, si = lax.fori_loop(0, nk, body, (carry_ref[:, 0:half], carry_ref[:, half:2 * half]), unroll=4)
        carry_ref[:, 0:half] = sr
        carry_ref[:, half:2 * half] = si

    @pl.when(p == 0)
    def _():
        sloc_ref[...] = _dot(u, hb_ref[...])
        sweep(1, True)
        blk = nblk - 1 - j
        sinb_ref[pl.ds(pl.multiple_of(blk * rb, rb), rb), :] = sin_ref[...].astype(BF16)

    @pl.when(p == 1)
    def _():
        sloc_ref[...] = _dot(u, hf_ref[...])
        sweep(0, False)
        y = _dot(u, g_ref[...])
        y = y + _dot(sin_ref[...].astype(BF16), qf_ref[...])
        y = y + _dot(sinb_ref[pl.ds(pl.multiple_of(j * rb, rb), rb), :], qb_ref[...])
        y = y + dsk_ref[...] * u32
        for t in range(0, S5_CHUNK, 2):
            pair = jnp.concatenate(
                [y[:, t * LANES:(t + 1) * LANES].reshape(nk, SUBLANES, LANES),
                 y[:, (t + 1) * LANES:(t + 2) * LANES].reshape(nk, SUBLANES, LANES)], axis=1)
            y_ref[:, t * SUBLANES:(t + 2) * SUBLANES, :] = pair.astype(BF16)


def _s5(cx3, layer, mats):
    g_m, hf_m, hb_m, qf_m, qb_m, a8, dsk = mats
    nchunks, tb, w = cx3.shape
    nk = S5_BLOCK_CHUNKS
    nblk = nchunks // nk
    rows = nchunks * SUBLANES
    flat = S5_FLAT
    wspec = pl.BlockSpec((None, None, flat, flat), lambda o, p, j: (layer, o, 0, 0))
    return pl.pallas_call(
        _s5_body,
        grid=(S5_OCTETS, 2, nblk),
        in_specs=[
            pl.BlockSpec((nk, tb, LANES), lambda o, p, j: (j + (1 - p) * (nblk - 1 - 2 * j), 0, o)),
            wspec, wspec, wspec, wspec, wspec,
            pl.BlockSpec((None, None, 4, flat // 2), lambda o, p, j: (layer, o, 0, 0)),
            pl.BlockSpec((None, None, 1, flat), lambda o, p, j: (layer, o, 0, 0)),
        ],
        out_specs=pl.BlockSpec((nk, tb, LANES), lambda o, p, j: (p * j, 0, o)),
        out_shape=jax.ShapeDtypeStruct((nchunks, tb, w), BF16),
        scratch_shapes=[
            pltpu.VMEM((rows, flat), BF16),
            pltpu.VMEM((nk * SUBLANES, flat), F32),
            pltpu.VMEM((nk * SUBLANES, flat), F32),
            pltpu.VMEM((SUBLANES, flat), F32),
        ],
        compiler_params=_params("arbitrary", "arbitrary", "arbitrary"),
        name="s5",
    )(cx3, g_m, hf_m, hb_m, qf_m, qb_m, a8, dsk)


def _merge_body(x_ref, p_ref, hf_ref, hb_ref, yb_ref, yc_ref, yd_ref, ns_ref,
                wag_ref, wbg_ref, wcg_ref, wdg_ref, wm0_ref, wm1_ref, wm2_ref, wm3_ref,
                wbr_ref, wout_ref, pg_ref, pp_ref, gluw_ref, glub_ref,
                o_ref, stage_ref):
    x = x_ref[...]
    hn = _rms_rows(x, ns_ref[...]).astype(BF16)

    def branch(n, y, wm_ref):
        return _dot(y.astype(BF16), wbr_ref[n]) * _sigmoid(_dot(hn, wm_ref[...]))

    ya = (hf_ref[...].astype(F32) + hb_ref[...].astype(F32)) * _silu(_dot(hn, wag_ref[...]))
    merged = branch(0, ya, wm0_ref)
    steps = stage_ref.shape[1] // SUBLANES
    for b in range(SUBLANES):
        for l in range(W_TILES):
            c0 = b * BRANCH_W + l * LANES
            stage_ref[l, pl.ds(b, steps, stride=SUBLANES), :] = yb_ref[:, c0:c0 + LANES].astype(F32)
    yb = jnp.concatenate([stage_ref[l] for l in range(W_TILES)], axis=-1)
    merged = merged + branch(1, yb * _silu(_dot(hn, wbg_ref[...])), wm1_ref)
    yg = _gelu_tanh(yc_ref[...].astype(F32))
    yc = yg * _sigmoid(_dot(yg.astype(BF16), gluw_ref[...]) + glub_ref[...])
    merged = merged + branch(2, yc * _silu(_dot(hn, wcg_ref[...])), wm2_ref)
    merged = merged + branch(3, yd_ref[...].astype(F32) * _silu(_dot(hn, wdg_ref[...])), wm3_ref)
    x1 = x + _dot(merged.astype(BF16), wout_ref[...])
    emb = _dot(p_ref[...].astype(BF16), pp_ref[...])
    o_ref[...] = x1 + _sigmoid(_dot(x1.astype(BF16), pg_ref[...])) * emb


def _merge(x, p, layer, hf, hb, yb, yc, yd, norm_scale, w_in, wbr, wout, pgate, pproj, glu_w, glu_b, batch):
    rows, d = x.shape
    w = BRANCH_W
    tm = MERGE_ROWS
    pdim = p.shape[-1]
    once = pl.Buffered(1)

    def resident(shape, *tail):
        tail = tail or (0,) * len(shape)
        return pl.BlockSpec((None,) + tuple(shape), lambda i: (layer,) + tuple(tail), pipeline_mode=once)

    row_spec = lambda n: pl.BlockSpec((tm, n), lambda i: (i, 0))
    gate_cols = [resident((d, w), 0, j) for j in (COL_AG, COL_BG, COL_CG, COL_DG)]
    merge_cols = [resident((d, d), 0, COL_MERGE + n) for n in range(N_BRANCH)]
    return pl.pallas_call(
        _merge_body,
        grid=(rows // tm,),
        in_specs=[
            row_spec(d),
            pl.BlockSpec((None, tm, pdim), lambda i: (layer, i, 0)),
            row_spec(w), row_spec(w),
            pl.BlockSpec((tm // batch, batch * w), lambda i: (i, 0)),
            row_spec(w), row_spec(w),
            resident((1, d)),
            *gate_cols, *merge_cols,
            resident((N_BRANCH, w, d)), resident((d, d)), resident((d, d)), resident((pdim, d)),
            resident((w, w)), resident((1, w)),
        ],
        out_specs=row_spec(d),
        out_shape=jax.ShapeDtypeStruct((rows, d), F32),
        scratch_shapes=[pltpu.VMEM((W_TILES, tm, LANES), F32)],
        compiler_params=_params("parallel"),
        name="merge",
    )(x, p, hf, hb, yb, yc, yd, norm_scale, *([w_in] * 8), wbr, wout, pgate, pproj, glu_w, glu_b)


def _block_diag(blocks):
    n, r, c = blocks.shape[-3:]
    eye = jnp.eye(n, dtype=blocks.dtype)
    out = jnp.einsum('...nrc,nm->...nrmc', blocks, eye)
    return out.reshape(blocks.shape[:-3] + (n * r, n * c))


def _attn_bias_tables(rpb):
    qc = np.arange(GRID_W)[:, None]
    kc = np.arange(GRID_W)[None, :]
    ws = np.clip(qc - NA_COLS // 2, 0, GRID_W - NA_COLS)
    in_win = (kc >= ws) & (kc < ws + NA_COLS)
    dc = np.clip(kc - qc, -(NA_COLS - 1), NA_COLS - 1) + NA_COLS - 1
    tab = jnp.where(jnp.asarray(in_win)[None, None], rpb.astype(F32)[:, :, dc], -1e30)
    return jnp.concatenate([tab[:, :-1], tab[:, 1:]], axis=-1)


def _s5_matrices(a_re, a_im, log_dt, b_re, b_im, c_re, c_im, d_skip):
    L, G, P, C = S5_CHUNK, SSM_GROUPS, SSM_STATE, SSM_GROUP
    f32 = F32
    lr = jnp.minimum(a_re.astype(f32), -1e-4)
    li = a_im.astype(f32)
    dt = jnp.exp(log_dt.astype(f32))[..., None]
    mag = jnp.exp(lr * dt)
    ab_r = mag * jnp.cos(li * dt)
    ab_i = mag * jnp.sin(li * dt)
    nr = ab_r - 1.0
    den = lr * lr + li * li
    fr = ((nr * lr + ab_i * li) / den)[..., None]
    fi = ((ab_i * lr - nr * li) / den)[..., None]
    br, bi = b_re.astype(f32), b_im.astype(f32)
    bb_r = fr * br - fi * bi
    bb_i = fr * bi + fi * br
    cr, ci = c_re.astype(f32), c_im.astype(f32)
    pr, pi = [jnp.ones_like(ab_r)], [jnp.zeros_like(ab_r)]
    for _ in range(L):
        pr.append(pr[-1] * ab_r - pi[-1] * ab_i)
        pi.append(pr[-2] * ab_i + pi[-1] * ab_r)
    pw_r = jnp.stack(pr)
    pw_i = jnp.stack(pi)
    m_r = pw_r[..., None] * bb_r[None] - pw_i[..., None] * bb_i[None]
    m_i = pw_r[..., None] * bb_i[None] + pw_i[..., None] * bb_r[None]
    k_lag = (jnp.einsum('dgop,ndgpi->ndgio', cr, m_r) - jnp.einsum('dgop,ndgpi->ndgio', ci, m_i))
    eye8 = jnp.eye(8, dtype=f32)

    def expand_rows(m):
        t_in = m.shape[0]
        rest = m.shape[3:]
        m = m.reshape((t_in, S5_OCTETS, 8, C) + rest)
        letters = 'uvwxyz'[:len(rest)]
        out = jnp.einsum(f'tojc{letters},jk->otjck{letters}', m, eye8)
        return out.reshape((S5_OCTETS, t_in * 8 * C, 8) + rest)

    tin = np.arange(L)[:, None]
    tout = np.arange(L)[None, :]
    lag = tout - tin
    kf = k_lag[:, 0][np.clip(lag, 0, L - 1)] * jnp.asarray(lag >= 0, f32)[:, :, None, None, None]
    kb = k_lag[:, 1][np.clip(-lag, 0, L - 1)] * jnp.asarray(lag <= 0, f32)[:, :, None, None, None]
    ktot = kf + kb
    g_rows = expand_rows(jnp.transpose(ktot, (0, 2, 3, 1, 4)))
    g_mat = jnp.transpose(g_rows, (0, 1, 3, 2, 4)).reshape(S5_OCTETS, S5_FLAT, S5_FLAT)

    def h_mat(direction, powers):
        mr = jnp.transpose(m_r[powers, direction], (0, 1, 3, 2))
        mi = jnp.transpose(m_i[powers, direction], (0, 1, 3, 2))
        both = jnp.stack([mr, mi], axis=3)
        rows = expand_rows(both)
        return jnp.transpose(rows, (0, 1, 3, 2, 4)).reshape(S5_OCTETS, S5_FLAT, S5_FLAT)

    hf_mat = h_mat(0, np.arange(L - 1, -1, -1))
    hb_mat = h_mat(1, np.arange(L))

    def q_mat(direction, powers):
        pwr = pw_r[powers, direction]
        pwi = pw_i[powers, direction]
        c_r, c_i = cr[direction], ci[direction]
        q_re = jnp.einsum('gcp,tgp->gptc', c_r, pwr) - jnp.einsum('gcp,tgp->gptc', c_i, pwi)
        q_im = -(jnp.einsum('gcp,tgp->gptc', c_r, pwi) + jnp.einsum('gcp,tgp->gptc', c_i, pwr))
        q = jnp.stack([q_re, q_im]).reshape(2, S5_OCTETS, 8, P, L, C)
        q = jnp.einsum('rojptc,jk->orjptkc', q, eye8)
        return q.reshape(S5_OCTETS, S5_FLAT, S5_FLAT)

    qf_mat = q_mat(0, np.arange(1, L + 1))
    qb_mat = q_mat(1, np.arange(L, 0, -1))
    a8 = jnp.stack([pw_r[L, 0], pw_i[L, 0], pw_r[L, 1], pw_i[L, 1]])
    a8 = jnp.transpose(a8.reshape(4, S5_OCTETS, 8 * P), (1, 0, 2))
    dsk = jnp.tile(d_skip.astype(f32).reshape(S5_OCTETS, 1, LANES), (1, 1, S5_CHUNK))
    bf = lambda m: m.astype(BF16)
    return bf(g_mat), bf(hf_mat), bf(hb_mat), bf(qf_mat), bf(qb_mat), a8, dsk


def kernel(x, p, norm_scale, w_in, lru_conv_w, lru_conv_b, lru_w_r, lru_b_r, lru_w_i, lru_b_i, lru_lambda, na_q_gain, na_k_gain, na_rel_bias, ssm_a_re, ssm_a_im, ssm_log_dt, ssm_b_re, ssm_b_im, ssm_c_re, ssm_c_im, ssm_d, ssm_glu_w, ssm_glu_b, pool_w, pool_scale, w_branch, w_out, ple_proj, ple_gate):
    b, s, d = x.shape
    w = BRANCH_W
    assert b == SUBLANES and d == D_MODEL and s % (LRU_STEPS * 4) == 0 and s // GRID_W >= NA_ROWS
    depth = w_in.shape[0]
    rows = s * b

    norm = norm_scale.astype(F32)[:, None, :]
    w_in16 = w_in.astype(BF16)
    ones_bd = _block_diag(jnp.ones((NA_HEADS, NA_HEAD_DIM, NA_HEAD_DIM), BF16))
    q_gain = (jnp.tile(na_q_gain.astype(F32), (1, NA_HEADS)) * NA_HEAD_DIM ** -0.5)[:, None, :]
    k_gain = jnp.tile(na_k_gain.astype(F32), (1, NA_HEADS))[:, None, :]
    conv_w = lru_conv_w.astype(F32)
    conv_b = lru_conv_b.astype(F32)[:, None, :]
    gate_w = jnp.concatenate([_block_diag(lru_w_r), _block_diag(lru_w_i)], axis=-1).astype(BF16)
    gate_b = jnp.concatenate([lru_b_r, lru_b_i], axis=-1).astype(F32)[:, :, None, :]
    c8 = (LRU_C * jax.nn.softplus(-lru_lambda.astype(F32)))[:, :, None, :]
    bias_tab = jax.vmap(_attn_bias_tables)(na_rel_bias)
    s5_mats = jax.vmap(_s5_matrices)(ssm_a_re, ssm_a_im, ssm_log_dt, ssm_b_re, ssm_b_im, ssm_c_re, ssm_c_im, ssm_d)
    glu_w = ssm_glu_w.astype(BF16)
    glu_b = ssm_glu_b.astype(F32)[:, None, :]
    pool_bd = _block_diag(pool_w).astype(BF16)
    pool_sc = pool_scale.astype(F32)[:, None, :]
    wbr = w_branch.astype(BF16)
    wout = w_out.astype(BF16)
    pproj = ple_proj.astype(BF16)
    pgate = ple_gate.astype(BF16)

    xt = jnp.transpose(x, (1, 0, 2)).reshape(rows, d)
    pt = jnp.transpose(p, (0, 2, 1, 3)).reshape(depth, rows, p.shape[-1])
    nchunks = s // S5_CHUNK
    for i in range(depth):
        ax, q, k, v, cx, dx = _inproj(xt, i, norm, w_in16, ones_bd, q_gain, k_gain, b)
        hf, hb = _lru(ax, i, conv_w, conv_b, gate_w, gate_b, c8)
        yb = _attn(q, k, v, i, bias_tab, b)
        yc = _s5(cx.reshape(nchunks, S5_CHUNK * b, w), i, s5_mats).reshape(rows, w)
        yd = _pool(dx, i, pool_bd, pool_sc, s)
        xt = _merge(xt, pt, i, hf, hb, yb, yc, yd, norm, w_in16, wbr, wout, pgate, pproj, glu_w, glu_b, b)
    return jnp.transpose(xt.reshape(s, b, d), (1, 0, 2))
```

```python
import functools

import jax
import jax.numpy as jnp
import numpy as np
from jax import lax
from jax.experimental import pallas as pl
from jax.experimental.pallas import tpu as pltpu

F32 = jnp.float32
BF16 = jnp.bfloat16

D_MODEL = 1024
BRANCH_W = 512
N_BRANCH = 4
NORM_EPS = 1e-6
GRID_W = 64
LRU_C = 8.0
LRU_CONV_W = 4
NA_HEADS = 8
NA_HEAD_DIM = 64
NA_ROWS = 8
NA_COLS = 16
SSM_GROUP = 16
SSM_GROUPS = 32
SSM_STATE = 64
POOL_WINDOWS = (2, 4, 8, 16)
POOL_GROUP = 128

SUBLANES = 8
LANES = 128
W_TILES = BRANCH_W // LANES
S5_CHUNK = 8
S5_OCTETS = 4
S5_FLAT = S5_CHUNK * LANES

COL_AX, COL_AG, COL_Q, COL_K, COL_V, COL_BG, COL_CX, COL_CG, COL_DX, COL_DG = range(10)
COL_MERGE = 10 * BRANCH_W // D_MODEL

PROJ_ROWS = 512
MERGE_ROWS = 256
LRU_STEPS = 128
POOL_STEPS = 128
ATTN_ROWS = 8
S5_BLOCK_CHUNKS = 64
ELEM_ROWS = 256

VMEM_LIMIT = 56 * 1024 * 1024


def _params(*sem):
    return pltpu.CompilerParams(dimension_semantics=sem, vmem_limit_bytes=VMEM_LIMIT)


def _dot(a, b):
    return jnp.dot(a, b, preferred_element_type=F32)


def _sigmoid(z):
    return 0.5 * jnp.tanh(0.5 * z) + 0.5


def _silu(z):
    return z * _sigmoid(z)


def _gelu_tanh(y):
    return 0.5 * y * (1.0 + jnp.tanh(0.7978845608028654 * (y + 0.044715 * (y * y * y))))


def _rms_rows(x, g):
    ms = jnp.mean(x * x, axis=-1, keepdims=True)
    return x * lax.rsqrt(ms + NORM_EPS) * g


def _head_mean_square(v, ones_ref):
    v2 = v * v
    hi = v2.astype(BF16)
    lo = (v2 - hi.astype(F32)).astype(BF16)
    return (_dot(hi, ones_ref[...]) + _dot(lo, ones_ref[...])) * (1.0 / NA_HEAD_DIM)


def _layer_spec(shape, layer, *tail):
    tail = tail or (0,) * len(shape)
    return pl.BlockSpec((None,) + tuple(shape), lambda *_: (layer,) + tuple(tail))


def _inproj_body(x_ref, ns_ref, wax_ref, wq_ref, wk_ref, wv_ref, wcx_ref, wdx_ref, ones_ref, qg_ref, kg_ref,
                 ax_ref, q_ref, k_ref, v_ref, cx_ref, dx_ref, stage_ref):
    hn = _rms_rows(x_ref[...], ns_ref[...]).astype(BF16)
    steps = stage_ref.shape[1] // SUBLANES

    def to_batch_lanes(val, out_ref):
        for l in range(W_TILES):
            stage_ref[l] = val[:, l * LANES:(l + 1) * LANES]
        for b in range(SUBLANES):
            for l in range(W_TILES):
                c0 = b * BRANCH_W + l * LANES
                out_ref[:, c0:c0 + LANES] = stage_ref[l, pl.ds(b, steps, stride=SUBLANES), :].astype(BF16)

    ax_ref[...] = _dot(hn, wax_ref[...]).astype(BF16)
    q = _dot(hn, wq_ref[...])
    to_batch_lanes(q * lax.rsqrt(_head_mean_square(q, ones_ref) + NORM_EPS) * qg_ref[...], q_ref)
    k = _dot(hn, wk_ref[...])
    to_batch_lanes(k * lax.rsqrt(_head_mean_square(k, ones_ref) + NORM_EPS) * kg_ref[...], k_ref)
    to_batch_lanes(_dot(hn, wv_ref[...]), v_ref)
    cx_ref[...] = _dot(hn, wcx_ref[...]).astype(BF16)
    dx_ref[...] = _dot(hn, wdx_ref[...]).astype(BF16)


def _inproj(x, layer, norm_scale, w_in, ones_bd, q_gain, k_gain, batch):
    rows, d = x.shape
    w = BRANCH_W
    tm = PROJ_ROWS
    s = rows // batch
    wcol = lambda j: pl.BlockSpec((None, d, w), lambda i: (layer, 0, j))
    tm_spec = pl.BlockSpec((tm, w), lambda i: (i, 0))
    bl_spec = pl.BlockSpec((tm // batch, batch * w), lambda i: (i, 0))
    tm_shape = jax.ShapeDtypeStruct((rows, w), BF16)
    bl_shape = jax.ShapeDtypeStruct((s, batch * w), BF16)
    return pl.pallas_call(
        _inproj_body,
        grid=(rows // tm,),
        in_specs=[
            pl.BlockSpec((tm, d), lambda i: (i, 0)),
            _layer_spec((1, d), layer),
            wcol(COL_AX), wcol(COL_Q), wcol(COL_K), wcol(COL_V), wcol(COL_CX), wcol(COL_DX),
            pl.BlockSpec((w, w), lambda i: (0, 0)),
            _layer_spec((1, w), layer), _layer_spec((1, w), layer),
        ],
        out_specs=[tm_spec, bl_spec, bl_spec, bl_spec, tm_spec, tm_spec],
        out_shape=[tm_shape, bl_shape, bl_shape, bl_shape, tm_shape, tm_shape],
        scratch_shapes=[pltpu.VMEM((W_TILES, tm, LANES), F32)],
        compiler_params=_params("parallel"),
        name="inproj",
    )(x, norm_scale, w_in, w_in, w_in, w_in, w_in, w_in, ones_bd, q_gain, k_gain)


def _fill_extended(ext_ref, main_ref, prev_ref, next_ref, n_prev, n_next, is_first, is_last):
    r = main_ref.shape[0]
    prev = prev_ref[...].astype(F32)
    ext_ref[0:n_prev, :] = jnp.where(is_first, 0.0, prev[prev.shape[0] - n_prev:, :])
    ext_ref[n_prev:n_prev + r, :] = main_ref[...].astype(F32)
    nxt = next_ref[...].astype(F32)
    ext_ref[n_prev + r:n_prev + r + n_next, :] = jnp.where(is_last, 0.0, nxt[0:n_next, :])


def _lru_body(mf_ref, pf_ref, nf_ref, mb_ref, pb_ref, nb_ref, cw_ref, cb_ref, wg_ref, bg_ref, c8_ref,
              hf_ref, hb_ref, ext_ref, af_ref, bf_ref, ab_ref, bb_ref, carry_ref):
    i = pl.program_id(0)
    nt = pl.num_programs(0)
    w = BRANCH_W
    r = mf_ref.shape[0]
    steps = r // SUBLANES
    n_prev, n_next = 2 * SUBLANES, SUBLANES

    @pl.when(i == 0)
    def _():
        carry_ref[...] = jnp.zeros_like(carry_ref)

    def prepare(direction, main_ref, prev_ref, next_ref, chunk, a_ref, b_ref):
        _fill_extended(ext_ref, main_ref, prev_ref, next_ref, n_prev, n_next, chunk == 0, chunk == nt - 1)

        def sub(sb, _):
            r0 = pl.multiple_of(sb * ELEM_ROWS, ELEM_ROWS)
            xc = cb_ref[...]
            for tap in range(LRU_CONV_W):
                xc = xc + cw_ref[tap:tap + 1, :] * ext_ref[pl.ds(r0 + tap * SUBLANES, ELEM_ROWS), :]
            g = _dot(xc.astype(BF16), wg_ref[direction]) + bg_ref[direction]
            rg = _sigmoid(g[:, 0:w])
            ig = _sigmoid(g[:, w:2 * w])
            log_a = -(c8_ref[direction] * rg)
            a = jnp.exp(log_a)
            a_ref[pl.ds(r0, ELEM_ROWS), :] = a
            b_ref[pl.ds(r0, ELEM_ROWS), :] = jnp.sqrt(1.0 - a * a) * (ig * xc)
            return 0

        lax.fori_loop(0, r // ELEM_ROWS, sub, 0)

    prepare(0, mf_ref, pf_ref, nf_ref, i, af_ref, bf_ref)
    prepare(1, mb_ref, pb_ref, nb_ref, nt - 1 - i, ab_ref, bb_ref)

    def step(t, carry):
        hf, hb = carry
        rf = pl.ds(pl.multiple_of(t * SUBLANES, SUBLANES), SUBLANES)
        hf = af_ref[rf, :] * hf + bf_ref[rf, :]
        bf_ref[rf, :] = hf
        rb = pl.ds(pl.multiple_of((steps - 1 - t) * SUBLANES, SUBLANES), SUBLANES)
        hb = ab_ref[rb, :] * hb + bb_ref[rb, :]
        bb_ref[rb, :] = hb
        return hf, hb

    hf, hb = lax.fori_loop(0, steps, step, (carry_ref[0], carry_ref[1]), unroll=8)
    carry_ref[0] = hf
    carry_ref[1] = hb
    hf_ref[...] = bf_ref[...].astype(BF16)
    hb_ref[...] = bb_ref[...].astype(BF16)


def _lru(ax, layer, conv_w, conv_b, w_gate, b_gate, c8):
    rows, w = ax.shape
    r = LRU_STEPS * SUBLANES
    nt = rows // r
    hb = 2 * SUBLANES
    per = r // hb
    last_hb = rows // hb - 1

    def main(ci):
        return pl.BlockSpec((r, w), lambda i: (ci(i), 0))

    def prev(ci):
        return pl.BlockSpec((hb, w), lambda i: (jnp.maximum(ci(i) * per - 1, 0), 0))

    def nxt(ci):
        return pl.BlockSpec((hb, w), lambda i: (jnp.minimum((ci(i) + 1) * per, last_hb), 0))

    fwd = lambda i: i
    bwd = lambda i: nt - 1 - i
    out_shape = jax.ShapeDtypeStruct((rows, w), BF16)
    return pl.pallas_call(
        _lru_body,
        grid=(nt,),
        in_specs=[
            main(fwd), prev(fwd), nxt(fwd), main(bwd), prev(bwd), nxt(bwd),
            _layer_spec((LRU_CONV_W, w), layer),
            _layer_spec((1, w), layer),
            _layer_spec((2, w, 2 * w), layer),
            _layer_spec((2, 1, 2 * w), layer),
            _layer_spec((2, 1, w), layer),
        ],
        out_specs=[main(fwd), main(bwd)],
        out_shape=[out_shape, out_shape],
        scratch_shapes=[
            pltpu.VMEM((r + 3 * SUBLANES, w), F32),
            pltpu.VMEM((r, w), F32), pltpu.VMEM((r, w), F32),
            pltpu.VMEM((r, w), F32), pltpu.VMEM((r, w), F32),
            pltpu.VMEM((2, SUBLANES, w), F32),
        ],
        compiler_params=_params("arbitrary"),
        name="rglru",
    )(ax, ax, ax, ax, ax, ax, conv_w, conv_b, w_gate, b_gate, c8)


def _pool_body(m_ref, p_ref, n_ref, wp_ref, sc_ref, o_ref, ext_ref, pooled_ref, *, seq_len):
    i = pl.program_id(0)
    nt = pl.num_programs(0)
    r = m_ref.shape[0]
    steps = r // SUBLANES
    halo = max(POOL_WINDOWS) // 2
    hr = halo * SUBLANES
    _fill_extended(ext_ref, m_ref, p_ref, n_ref, hr, hr, i == 0, i == nt - 1)

    t_glob = i * steps + lax.broadcasted_iota(jnp.int32, (r, POOL_GROUP), 0) // SUBLANES
    for g, win in enumerate(POOL_WINDOWS):
        ls = slice(g * POOL_GROUP, (g + 1) * POOL_GROUP)
        e = ext_ref[:, ls]
        lo_t = -halo
        cur = e
        half = 1
        while half < win:
            n = cur.shape[0] - half * SUBLANES
            if half == 1:
                cur = cur[0:n, :] + cur[SUBLANES:SUBLANES + n, :]
                lo_t += 1
            else:
                sh = (half // 2) * SUBLANES
                cur = cur[0:n, :] + cur[2 * sh:2 * sh + n, :]
                lo_t += half // 2
            half *= 2
        off = (0 - lo_t) * SUBLANES
        wsum = cur[off:off + r, :]
        half_w = win // 2
        cnt = (jnp.clip(t_glob + half_w, 0, seq_len) - jnp.clip(t_glob - half_w, 0, seq_len)).astype(F32)
        pooled_ref[:, ls] = wsum / cnt - e[hr:hr + r, :]
    o_ref[...] = (_dot(pooled_ref[...].astype(BF16), wp_ref[...]) * sc_ref[...]).astype(BF16)


def _pool(dx, layer, w_pool_bd, scale, seq_len):
    rows, w = dx.shape
    r = POOL_STEPS * SUBLANES
    nt = rows // r
    hb = (max(POOL_WINDOWS) // 2) * SUBLANES
    per = r // hb
    last_hb = rows // hb - 1
    return pl.pallas_call(
        functools.partial(_pool_body, seq_len=seq_len),
        grid=(nt,),
        in_specs=[
            pl.BlockSpec((r, w), lambda i: (i, 0)),
            pl.BlockSpec((hb, w), lambda i: (jnp.maximum(i * per - 1, 0), 0)),
            pl.BlockSpec((hb, w), lambda i: (jnp.minimum((i + 1) * per, last_hb), 0)),
            _layer_spec((w, w), layer),
            _layer_spec((1, w), layer),
        ],
        out_specs=pl.BlockSpec((r, w), lambda i: (i, 0)),
        out_shape=jax.ShapeDtypeStruct((rows, w), BF16),
        scratch_shapes=[pltpu.VMEM((r + 2 * hb, w), F32), pltpu.VMEM((r, w), F32)],
        compiler_params=_params("parallel"),
        name="pool",
    )(dx, dx, dx, w_pool_bd, scale)


def _attn_body(q_ref, k_ref, v_ref, bias_ref, o_ref, *, grid_rows):
    g = pl.program_id(1)
    lane = lax.broadcasted_iota(jnp.int32, (GRID_W, LANES), 1)
    lo_half = lane < NA_HEAD_DIM
    nkeys = NA_ROWS * GRID_W

    def row_body(rr, _):
        r = g * ATTN_ROWS + rr
        rs = jnp.clip(r - NA_ROWS // 2, 0, grid_rows - NA_ROWS)
        d0 = rs - r + (NA_ROWS - 1)
        k0 = pl.multiple_of(rs * GRID_W, GRID_W)
        q0 = pl.multiple_of(rr * GRID_W, GRID_W)
        scores = []
        for hp in range(NA_HEADS // 2):
            ls = slice(hp * LANES, (hp + 1) * LANES)
            kp = k_ref[pl.ds(k0, nkeys), ls]
            qp = q_ref[pl.ds(q0, GRID_W), ls]
            for hh in range(2):
                h = 2 * hp + hh
                qm = jnp.where(lo_half if hh == 0 else jnp.logical_not(lo_half), qp, jnp.zeros_like(qp))
                sc = lax.dot_general(qm, kp, (((1,), (1,)), ((), ())), preferred_element_type=F32)
                bias = jnp.concatenate([bias_ref[h, d0 + 2 * j] for j in range(NA_ROWS // 2)], axis=-1)
                scores.append(sc + bias)
        maxes = [jnp.max(sc, axis=-1, keepdims=True) for sc in scores]
        probs = [jnp.exp(sc - m) for sc, m in zip(scores, maxes)]
        inv = [1.0 / jnp.sum(p, axis=-1, keepdims=True) for p in probs]
        for hp in range(NA_HEADS // 2):
            ls = slice(hp * LANES, (hp + 1) * LANES)
            vp = v_ref[pl.ds(k0, nkeys), ls]
            o0 = _dot(probs[2 * hp].astype(BF16), vp) * inv[2 * hp]
            o1 = _dot(probs[2 * hp + 1].astype(BF16), vp) * inv[2 * hp + 1]
            o_ref[pl.ds(q0, GRID_W), ls] = jnp.where(lo_half, o0, o1).astype(BF16)
        return 0

    lax.fori_loop(0, ATTN_ROWS, row_body, 0)


def _attn(q, k, v, layer, bias_tab, batch):
    s, bw = q.shape
    w = bw // batch
    grid_rows = s // GRID_W
    ng = grid_rows // ATTN_ROWS
    qr = ATTN_ROWS * GRID_W
    q_spec = pl.BlockSpec((qr, w), lambda b, g: (g, b))
    kv_spec = pl.BlockSpec((s, w), lambda b, g: (0, b))
    return pl.pallas_call(
        functools.partial(_attn_body, grid_rows=grid_rows),
        grid=(batch, ng),
        in_specs=[q_spec, kv_spec, kv_spec, _layer_spec(bias_tab.shape[1:], layer)],
        out_specs=q_spec,
        out_shape=jax.ShapeDtypeStruct((s, bw), BF16),
        compiler_params=_params("parallel", "parallel"),
        name="natten",
    )(q, k, v, bias_tab)


S5_TAB_ROWS = S5_CHUNK * SSM_GROUP
S5_EXPAND_ROWS = 256


def _s5_expand(tab_ref, mats_ref):
    def i32(shape, dim):
        return lax.broadcasted_iota(jnp.int32, shape, dim)

    def chunk(c, _):
        r0 = pl.multiple_of(c * S5_EXPAND_ROWS, S5_EXPAND_ROWS)
        row_s = r0 + i32((S5_EXPAND_ROWS, S5_TAB_ROWS), 0)
        col_s = i32((S5_EXPAND_ROWS, S5_TAB_ROWS), 1)
        row_m = r0 + i32((S5_EXPAND_ROWS, S5_FLAT), 0)
        col_m = i32((S5_EXPAND_ROWS, S5_FLAT), 1)
        lg_c, lg_p, lg_l = 4, 6, 7
        sel_u = jnp.where(((row_s >> lg_l) == (col_s >> lg_c)) & ((row_s & (SSM_GROUP - 1)) == (col_s & (SSM_GROUP - 1))),
                          1.0, 0.0).astype(BF16)
        sel_s = jnp.where(((row_s >> (lg_p + 3)) == (col_s >> lg_p)) & ((row_s & (SSM_STATE - 1)) == (col_s & (SSM_STATE - 1))),
                          1.0, 0.0).astype(BF16)
        grp_row_u = (row_m >> lg_c) & 7
        grp_row_s = (row_m >> lg_p) & 7
        grp_col_u = (col_m >> lg_c) & 7
        grp_col_s = (col_m >> lg_p) & 7
        plan = ((sel_u, grp_row_u == grp_col_u),
                (sel_u, grp_row_u == grp_col_s),
                (sel_u, grp_row_u == grp_col_s),
                (sel_s, grp_row_s == grp_col_u),
                (sel_s, grp_row_s == grp_col_u))
        for m, (sel, keep) in enumerate(plan):
            full = _dot(sel, tab_ref[m])
            mats_ref[m, pl.ds(r0, S5_EXPAND_ROWS), :] = jnp.where(keep, full, 0.0).astype(BF16)
        return 0

    lax.fori_loop(0, S5_FLAT // S5_EXPAND_ROWS, chunk, 0)


def _s5_body(u_ref, tab_ref, a8_ref, dsk_ref, y_ref, mats_ref, sinb_ref, sloc_ref, sin_ref, carry_ref):
    p = pl.program_id(1)
    j = pl.program_id(2)
    nblk = pl.num_programs(2)
    nk = u_ref.shape[0]
    rb = nk * SUBLANES
    half = S5_FLAT // 2
    m_g, m_hf, m_hb, m_qf, m_qb = range(5)

    @pl.when(j == 0)
    def _():
        carry_ref[...] = jnp.zeros_like(carry_ref)

    @pl.when((j == 0) & (p == 0))
    def _():
        _s5_expand(tab_ref, mats_ref)

    u3 = u_ref[...].astype(F32)
    u32 = jnp.concatenate(
        [u3[:, t * SUBLANES:(t + 1) * SUBLANES, :].reshape(rb, LANES) for t in range(S5_CHUNK)], axis=-1)
    u = u32.astype(BF16)

    def sweep(direction, reverse):
        ar = a8_ref[2 * direction:2 * direction + 1, :]
        ai = a8_ref[2 * direction + 1:2 * direction + 2, :]

        def body(n, s):
            sr, si = s
            kk = (nk - 1 - n) if reverse else n
            rows = pl.ds(pl.multiple_of(kk * SUBLANES, SUBLANES), SUBLANES)
            sin_ref[rows, 0:half] = sr
            sin_ref[rows, half:2 * half] = si
            nr = ar * sr - ai * si + sloc_ref[rows, 0:half]
            ni = ar * si + ai * sr + sloc_ref[rows, half:2 * half]
            return nr, ni

        sr, si = lax.fori_loop(0, nk, body, (carry_ref[:, 0:half], carry_ref[:, half:2 * half]), unroll=4)
        carry_ref[:, 0:half] = sr
        carry_ref[:, half:2 * half] = si

    @pl.when(p == 0)
    def _():
        sloc_ref[...] = _dot(u, mats_ref[m_hb])
        sweep(1, True)
        blk = nblk - 1 - j
        sinb_ref[pl.ds(pl.multiple_of(blk * rb, rb), rb), :] = sin_ref[...].astype(BF16)

    @pl.when(p == 1)
    def _():
        sloc_ref[...] = _dot(u, mats_ref[m_hf])
        sweep(0, False)
        y = _dot(u, mats_ref[m_g])
        y = y + _dot(sin_ref[...].astype(BF16), mats_ref[m_qf])
        y = y + _dot(sinb_ref[pl.ds(pl.multiple_of(j * rb, rb), rb), :], mats_ref[m_qb])
        y = y + dsk_ref[...] * u32
        for t in range(0, S5_CHUNK, 2):
            pair = jnp.concatenate(
                [y[:, t * LANES:(t + 1) * LANES].reshape(nk, SUBLANES, LANES),
                 y[:, (t + 1) * LANES:(t + 2) * LANES].reshape(nk, SUBLANES, LANES)], axis=1)
            y_ref[:, t * SUBLANES:(t + 2) * SUBLANES, :] = pair.astype(BF16)


def _s5(cx3, layer, mats):
    tabs, a8, dsk = mats
    nchunks, tb, w = cx3.shape
    nk = S5_BLOCK_CHUNKS
    nblk = nchunks // nk
    rows = nchunks * SUBLANES
    flat = S5_FLAT
    per_octet = lambda shape: pl.BlockSpec((None, None) + shape, lambda o, p, j: (layer, o) + (0,) * len(shape))
    return pl.pallas_call(
        _s5_body,
        grid=(S5_OCTETS, 2, nblk),
        in_specs=[
            pl.BlockSpec((nk, tb, LANES), lambda o, p, j: (j + (1 - p) * (nblk - 1 - 2 * j), 0, o)),
            per_octet((5, S5_TAB_ROWS, flat)),
            per_octet((4, flat // 2)),
            per_octet((1, flat)),
        ],
        out_specs=pl.BlockSpec((nk, tb, LANES), lambda o, p, j: (p * j, 0, o)),
        out_shape=jax.ShapeDtypeStruct((nchunks, tb, w), BF16),
        scratch_shapes=[
            pltpu.VMEM((5, flat, flat), BF16),
            pltpu.VMEM((rows, flat), BF16),
            pltpu.VMEM((nk * SUBLANES, flat), F32),
            pltpu.VMEM((nk * SUBLANES, flat), F32),
            pltpu.VMEM((SUBLANES, flat), F32),
        ],
        compiler_params=_params("arbitrary", "arbitrary", "arbitrary"),
        name="s5",
    )(cx3, tabs, a8, dsk)


def _merge_body(x_ref, p_ref, hf_ref, hb_ref, yb_ref, yc_ref, yd_ref, ns_ref,
                wag_ref, wbg_ref, wcg_ref, wdg_ref, wm0_ref, wm1_ref, wm2_ref, wm3_ref,
                wbr_ref, wout_ref, pg_ref, pp_ref, gluw_ref, glub_ref,
                o_ref, stage_ref):
    x = x_ref[...]
    hn = _rms_rows(x, ns_ref[...]).astype(BF16)

    def branch(n, y, wm_ref):
        return _dot(y.astype(BF16), wbr_ref[n]) * _sigmoid(_dot(hn, wm_ref[...]))

    ya = (hf_ref[...].astype(F32) + hb_ref[...].astype(F32)) * _silu(_dot(hn, wag_ref[...]))
    merged = branch(0, ya, wm0_ref)
    steps = stage_ref.shape[1] // SUBLANES
    for b in range(SUBLANES):
        for l in range(W_TILES):
            c0 = b * BRANCH_W + l * LANES
            stage_ref[l, pl.ds(b, steps, stride=SUBLANES), :] = yb_ref[:, c0:c0 + LANES].astype(F32)
    yb = jnp.concatenate([stage_ref[l] for l in range(W_TILES)], axis=-1)
    merged = merged + branch(1, yb * _silu(_dot(hn, wbg_ref[...])), wm1_ref)
    yg = _gelu_tanh(yc_ref[...].astype(F32))
    yc = yg * _sigmoid(_dot(yg.astype(BF16), gluw_ref[...]) + glub_ref[...])
    merged = merged + branch(2, yc * _silu(_dot(hn, wcg_ref[...])), wm2_ref)
    merged = merged + branch(3, yd_ref[...].astype(F32) * _silu(_dot(hn, wdg_ref[...])), wm3_ref)
    x1 = x + _dot(merged.astype(BF16), wout_ref[...])
    emb = _dot(p_ref[...].astype(BF16), pp_ref[...])
    o_ref[...] = x1 + _sigmoid(_dot(x1.astype(BF16), pg_ref[...])) * emb


def _merge(x, p, layer, hf, hb, yb, yc, yd, norm_scale, w_in, wbr, wout, pgate, pproj, glu_w, glu_b, batch):
    rows, d = x.shape
    w = BRANCH_W
    tm = MERGE_ROWS
    pdim = p.shape[-1]
    once = pl.Buffered(1)

    def resident(shape, *tail):
        tail = tail or (0,) * len(shape)
        return pl.BlockSpec((None,) + tuple(shape), lambda i: (layer,) + tuple(tail), pipeline_mode=once)

    row_spec = lambda n: pl.BlockSpec((tm, n), lambda i: (i, 0))
    gate_cols = [resident((d, w), 0, j) for j in (COL_AG, COL_BG, COL_CG, COL_DG)]
    merge_cols = [resident((d, d), 0, COL_MERGE + n) for n in range(N_BRANCH)]
    return pl.pallas_call(
        _merge_body,
        grid=(rows // tm,),
        in_specs=[
            row_spec(d),
            pl.BlockSpec((None, tm, pdim), lambda i: (layer, i, 0)),
            row_spec(w), row_spec(w),
            pl.BlockSpec((tm // batch, batch * w), lambda i: (i, 0)),
            row_spec(w), row_spec(w),
            resident((1, d)),
            *gate_cols, *merge_cols,
            resident((N_BRANCH, w, d)), resident((d, d)), resident((d, d)), resident((pdim, d)),
            resident((w, w)), resident((1, w)),
        ],
        out_specs=row_spec(d),
        out_shape=jax.ShapeDtypeStruct((rows, d), F32),
        scratch_shapes=[pltpu.VMEM((W_TILES, tm, LANES), F32)],
        compiler_params=_params("parallel"),
        name="merge",
    )(x, p, hf, hb, yb, yc, yd, norm_scale, *([w_in] * 8), wbr, wout, pgate, pproj, glu_w, glu_b)


def _block_diag(blocks):
    n, r, c = blocks.shape[-3:]
    eye = jnp.eye(n, dtype=blocks.dtype)
    out = jnp.einsum('...nrc,nm->...nrmc', blocks, eye)
    return out.reshape(blocks.shape[:-3] + (n * r, n * c))


def _attn_bias_tables(rpb):
    qc = np.arange(GRID_W)[:, None]
    kc = np.arange(GRID_W)[None, :]
    ws = np.clip(qc - NA_COLS // 2, 0, GRID_W - NA_COLS)
    in_win = (kc >= ws) & (kc < ws + NA_COLS)
    dc = np.clip(kc - qc, -(NA_COLS - 1), NA_COLS - 1) + NA_COLS - 1
    tab = jnp.where(jnp.asarray(in_win)[None, None], rpb.astype(F32)[:, :, dc], -1e30)
    return jnp.concatenate([tab[:, :-1], tab[:, 1:]], axis=-1)


def _s5_matrices(a_re, a_im, log_dt, b_re, b_im, c_re, c_im, d_skip):
    L, G, P, C = S5_CHUNK, SSM_GROUPS, SSM_STATE, SSM_GROUP
    f32 = F32
    lr = jnp.minimum(a_re.astype(f32), -1e-4)
    li = a_im.astype(f32)
    dt = jnp.exp(log_dt.astype(f32))[..., None]
    mag = jnp.exp(lr * dt)
    ab_r = mag * jnp.cos(li * dt)
    ab_i = mag * jnp.sin(li * dt)
    nr = ab_r - 1.0
    den = lr * lr + li * li
    fr = ((nr * lr + ab_i * li) / den)[..., None]
    fi = ((ab_i * lr - nr * li) / den)[..., None]
    br, bi = b_re.astype(f32), b_im.astype(f32)
    bb_r = fr * br - fi * bi
    bb_i = fr * bi + fi * br
    cr, ci = c_re.astype(f32), c_im.astype(f32)
    pr, pi = [jnp.ones_like(ab_r)], [jnp.zeros_like(ab_r)]
    for _ in range(L):
        pr.append(pr[-1] * ab_r - pi[-1] * ab_i)
        pi.append(pr[-2] * ab_i + pi[-1] * ab_r)
    pw_r = jnp.stack(pr)
    pw_i = jnp.stack(pi)
    m_r = pw_r[..., None] * bb_r[None] - pw_i[..., None] * bb_i[None]
    m_i = pw_r[..., None] * bb_i[None] + pw_i[..., None] * bb_r[None]
    k_lag = (jnp.einsum('dgop,ndgpi->ndgio', cr, m_r) - jnp.einsum('dgop,ndgpi->ndgio', ci, m_i))
    no = S5_OCTETS

    tin = np.arange(L)[:, None]
    tout = np.arange(L)[None, :]
    lag = tout - tin
    kf = k_lag[:, 0][np.clip(lag, 0, L - 1)] * jnp.asarray(lag >= 0, f32)[:, :, None, None, None]
    kb = k_lag[:, 1][np.clip(-lag, 0, L - 1)] * jnp.asarray(lag <= 0, f32)[:, :, None, None, None]
    ktot = (kf + kb).reshape(L, L, no, 8, C, C)
    g_tab = jnp.transpose(ktot, (2, 0, 4, 1, 3, 5)).reshape(no, L * C, S5_FLAT)

    def h_tab(direction, powers):
        both = jnp.stack([m_r[powers, direction], m_i[powers, direction]])
        both = both.reshape(2, L, no, 8, P, C)
        return jnp.transpose(both, (2, 1, 5, 0, 3, 4)).reshape(no, L * C, S5_FLAT)

    hf_tab = h_tab(0, np.arange(L - 1, -1, -1))
    hb_tab = h_tab(1, np.arange(L))

    def q_tab(direction, powers):
        pwr = pw_r[powers, direction]
        pwi = pw_i[powers, direction]
        c_r, c_i = cr[direction], ci[direction]
        q_re = jnp.einsum('gcp,tgp->gptc', c_r, pwr) - jnp.einsum('gcp,tgp->gptc', c_i, pwi)
        q_im = -(jnp.einsum('gcp,tgp->gptc', c_r, pwi) + jnp.einsum('gcp,tgp->gptc', c_i, pwr))
        q = jnp.stack([q_re, q_im]).reshape(2, no, 8, P, L, C)
        return jnp.transpose(q, (1, 0, 3, 4, 2, 5)).reshape(no, 2 * P, S5_FLAT)

    qf_tab = q_tab(0, np.arange(1, L + 1))
    qb_tab = q_tab(1, np.arange(L, 0, -1))
    tabs = jnp.stack([g_tab, hf_tab, hb_tab, qf_tab, qb_tab], axis=1).astype(BF16)
    a8 = jnp.stack([pw_r[L, 0], pw_i[L, 0], pw_r[L, 1], pw_i[L, 1]])
    a8 = jnp.transpose(a8.reshape(4, no, 8 * P), (1, 0, 2))
    dsk = jnp.tile(d_skip.astype(f32).reshape(no, 1, LANES), (1, 1, S5_CHUNK))
    return tabs, a8, dsk


def kernel(x, p, norm_scale, w_in, lru_conv_w, lru_conv_b, lru_w_r, lru_b_r, lru_w_i, lru_b_i, lru_lambda, na_q_gain, na_k_gain, na_rel_bias, ssm_a_re, ssm_a_im, ssm_log_dt, ssm_b_re, ssm_b_im, ssm_c_re, ssm_c_im, ssm_d, ssm_glu_w, ssm_glu_b, pool_w, pool_scale, w_branch, w_out, ple_proj, ple_gate):
    b, s, d = x.shape
    w = BRANCH_W
    assert b == SUBLANES and d == D_MODEL and s % (LRU_STEPS * 4) == 0 and s // GRID_W >= NA_ROWS
    depth = w_in.shape[0]
    rows = s * b

    norm = norm_scale.astype(F32)[:, None, :]
    w_in16 = w_in.astype(BF16)
    ones_bd = _block_diag(jnp.ones((NA_HEADS, NA_HEAD_DIM, NA_HEAD_DIM), BF16))
    q_gain = (jnp.tile(na_q_gain.astype(F32), (1, NA_HEADS)) * NA_HEAD_DIM ** -0.5)[:, None, :]
    k_gain = jnp.tile(na_k_gain.astype(F32), (1, NA_HEADS))[:, None, :]
    conv_w = lru_conv_w.astype(F32)
    conv_b = lru_conv_b.astype(F32)[:, None, :]
    gate_w = jnp.concatenate([_block_diag(lru_w_r), _block_diag(lru_w_i)], axis=-1).astype(BF16)
    gate_b = jnp.concatenate([lru_b_r, lru_b_i], axis=-1).astype(F32)[:, :, None, :]
    c8 = (LRU_C * jax.nn.softplus(-lru_lambda.astype(F32)))[:, :, None, :]
    bias_tab = jax.vmap(_attn_bias_tables)(na_rel_bias)
    s5_mats = jax.vmap(_s5_matrices)(ssm_a_re, ssm_a_im, ssm_log_dt, ssm_b_re, ssm_b_im, ssm_c_re, ssm_c_im, ssm_d)
    glu_w = ssm_glu_w.astype(BF16)
    glu_b = ssm_glu_b.astype(F32)[:, None, :]
    pool_bd = _block_diag(pool_w).astype(BF16)
    pool_sc = pool_scale.astype(F32)[:, None, :]
    wbr = w_branch.astype(BF16)
    wout = w_out.astype(BF16)
    pproj = ple_proj.astype(BF16)
    pgate = ple_gate.astype(BF16)

    xt = jnp.transpose(x, (1, 0, 2)).reshape(rows, d)
    pt = jnp.transpose(p, (0, 2, 1, 3)).reshape(depth, rows, p.shape[-1])
    nchunks = s // S5_CHUNK
    for i in range(depth):
        ax, q, k, v, cx, dx = _inproj(xt, i, norm, w_in16, ones_bd, q_gain, k_gain, b)
        hf, hb = _lru(ax, i, conv_w, conv_b, gate_w, gate_b, c8)
        yb = _attn(q, k, v, i, bias_tab, b)
        yc = _s5(cx.reshape(nchunks, S5_CHUNK * b, w), i, s5_mats).reshape(rows, w)
        yd = _pool(dx, i, pool_bd, pool_sc, s)
        xt = _merge(xt, pt, i, hf, hb, yb, yc, yd, norm, w_in16, wbr, wout, pgate, pproj, glu_w, glu_b, b)
    return jnp.transpose(xt.reshape(s, b, d), (1, 0, 2))
```

```python
import functools

import jax
import jax.numpy as jnp
import numpy as np
from jax import lax
from jax.experimental import pallas as pl
from jax.experimental.pallas import tpu as pltpu

F32 = jnp.float32
BF16 = jnp.bfloat16

D_MODEL = 1024
BRANCH_W = 512
N_BRANCH = 4
NORM_EPS = 1e-6
LOG2E = 1.4426950408889634
GRID_W = 64
LRU_C = 8.0
LRU_CONV_W = 4
NA_HEADS = 8
NA_HEAD_DIM = 64
NA_ROWS = 8
NA_COLS = 16
SSM_GROUP = 16
SSM_GROUPS = 32
SSM_STATE = 64
POOL_WINDOWS = (2, 4, 8, 16)
POOL_GROUP = 128

SUBLANES = 8
LANES = 128
MXU_DIM = 256
W_TILES = BRANCH_W // LANES
S5_CHUNK = 8
S5_OCTETS = 4
S5_FLAT = S5_CHUNK * LANES

COL_AX, COL_AG, COL_Q, COL_K, COL_V, COL_BG, COL_CX, COL_CG, COL_DX, COL_DG = range(10)
COL_MERGE = 10 * BRANCH_W // D_MODEL

PROJ_ROWS = 1024
MERGE_ROWS = 512
LRU_STEPS = 128
POOL_STEPS = 128
ATTN_ROWS = 8
S5_BLOCK_CHUNKS = 64
ELEM_ROWS = 256

VMEM_LIMIT = 56 * 1024 * 1024


def _params(*sem):
    return pltpu.CompilerParams(dimension_semantics=sem, vmem_limit_bytes=VMEM_LIMIT)


def _dot(a, b):
    return jnp.dot(a, b, preferred_element_type=F32)


def _sigmoid(z):
    return 0.5 * jnp.tanh(0.5 * z) + 0.5


def _silu(z):
    return z * _sigmoid(z)


def _gelu_tanh(y):
    return 0.5 * y * (1.0 + jnp.tanh(0.7978845608028654 * (y + 0.044715 * (y * y * y))))


def _rms_rows(x, g):
    ms = jnp.mean(x * x, axis=-1, keepdims=True)
    return x * lax.rsqrt(ms + NORM_EPS) * g


def _head_mean_square(v, ones_ref):
    v2 = (v * v).astype(BF16)
    n = ones_ref.shape[0]
    sums = [_dot(v2[:, c:c + n], ones_ref[...]) for c in range(0, v.shape[1], n)]
    return jnp.concatenate(sums, axis=-1) * (1.0 / NA_HEAD_DIM)


def _layer_spec(shape, layer, *tail):
    tail = tail or (0,) * len(shape)
    return pl.BlockSpec((None,) + tuple(shape), lambda *_: (layer,) + tuple(tail))


def _inproj_body(x_ref, ns_ref, wax_ref, wq_ref, wk_ref, wv_ref, wcx_ref, wdx_ref, ones_ref, qg_ref, kg_ref,
                 ax_ref, q_ref, k_ref, v_ref, cx_ref, dx_ref, stage_ref):
    hn = _rms_rows(x_ref[...], ns_ref[...]).astype(BF16)
    steps = stage_ref.shape[1] // SUBLANES

    def to_batch_lanes(val, out_ref):
        for l in range(W_TILES):
            stage_ref[l] = val[:, l * LANES:(l + 1) * LANES]
        for b in range(SUBLANES):
            for l in range(W_TILES):
                c0 = b * BRANCH_W + l * LANES
                out_ref[:, c0:c0 + LANES] = stage_ref[l, pl.ds(b, steps, stride=SUBLANES), :].astype(BF16)

    ax_ref[...] = _dot(hn, wax_ref[...]).astype(BF16)
    q = _dot(hn, wq_ref[...])
    to_batch_lanes(q * lax.rsqrt(_head_mean_square(q, ones_ref) + NORM_EPS) * qg_ref[...], q_ref)
    k = _dot(hn, wk_ref[...])
    to_batch_lanes(k * lax.rsqrt(_head_mean_square(k, ones_ref) + NORM_EPS) * kg_ref[...], k_ref)
    to_batch_lanes(_dot(hn, wv_ref[...]), v_ref)
    cx_ref[...] = _dot(hn, wcx_ref[...]).astype(BF16)
    dx_ref[...] = _dot(hn, wdx_ref[...]).astype(BF16)


def _inproj(x, layer, norm_scale, w_in, ones_bd, q_gain, k_gain, batch):
    rows, d = x.shape
    w = BRANCH_W
    tm = PROJ_ROWS
    s = rows // batch
    wcol = lambda j: pl.BlockSpec((None, d, w), lambda i: (layer, 0, j))
    tm_spec = pl.BlockSpec((tm, w), lambda i: (i, 0))
    bl_spec = pl.BlockSpec((tm // batch, batch * w), lambda i: (i, 0))
    tm_shape = jax.ShapeDtypeStruct((rows, w), BF16)
    bl_shape = jax.ShapeDtypeStruct((s, batch * w), BF16)
    return pl.pallas_call(
        _inproj_body,
        grid=(rows // tm,),
        in_specs=[
            pl.BlockSpec((tm, d), lambda i: (i, 0)),
            _layer_spec((1, d), layer),
            wcol(COL_AX), wcol(COL_Q), wcol(COL_K), wcol(COL_V), wcol(COL_CX), wcol(COL_DX),
            pl.BlockSpec(ones_bd.shape, lambda i: (0, 0)),
            _layer_spec((1, w), layer), _layer_spec((1, w), layer),
        ],
        out_specs=[tm_spec, bl_spec, bl_spec, bl_spec, tm_spec, tm_spec],
        out_shape=[tm_shape, bl_shape, bl_shape, bl_shape, tm_shape, tm_shape],
        scratch_shapes=[pltpu.VMEM((W_TILES, tm, LANES), F32)],
        compiler_params=_params("parallel"),
        name="inproj",
    )(x, norm_scale, w_in, w_in, w_in, w_in, w_in, w_in, ones_bd, q_gain, k_gain)


def _fill_extended(ext_ref, main_ref, prev_ref, next_ref, n_prev, n_next, is_first, is_last):
    r = main_ref.shape[0]
    prev = prev_ref[...].astype(F32)
    ext_ref[0:n_prev, :] = jnp.where(is_first, 0.0, prev[prev.shape[0] - n_prev:, :])
    ext_ref[n_prev:n_prev + r, :] = main_ref[...].astype(F32)
    nxt = next_ref[...].astype(F32)
    ext_ref[n_prev + r:n_prev + r + n_next, :] = jnp.where(is_last, 0.0, nxt[0:n_next, :])


def _lru_body(mf_ref, pf_ref, nf_ref, mb_ref, pb_ref, nb_ref, cw_ref, cb_ref, wg_ref, bg_ref, c8_ref,
              hf_ref, hb_ref, ext_ref, af_ref, bf_ref, ab_ref, bb_ref, carry_ref):
    i = pl.program_id(0)
    nt = pl.num_programs(0)
    w = BRANCH_W
    r = mf_ref.shape[0]
    steps = r // SUBLANES
    n_prev, n_next = 2 * SUBLANES, SUBLANES

    @pl.when(i == 0)
    def _():
        carry_ref[...] = jnp.zeros_like(carry_ref)

    def prepare(direction, main_ref, prev_ref, next_ref, chunk, a_ref, b_ref):
        _fill_extended(ext_ref, main_ref, prev_ref, next_ref, n_prev, n_next, chunk == 0, chunk == nt - 1)

        def sub(sb, _):
            r0 = pl.multiple_of(sb * ELEM_ROWS, ELEM_ROWS)
            xc = cb_ref[...]
            for tap in range(LRU_CONV_W):
                xc = xc + cw_ref[tap:tap + 1, :] * ext_ref[pl.ds(r0 + tap * SUBLANES, ELEM_ROWS), :]
            g = _dot(xc.astype(BF16), wg_ref[direction]) + bg_ref[direction]
            rg = _sigmoid(g[:, 0:w])
            ig = _sigmoid(g[:, w:2 * w])
            log_a = -(c8_ref[direction] * rg)
            a = jnp.exp(log_a)
            a_ref[pl.ds(r0, ELEM_ROWS), :] = a
            om = 1.0 - a * a
            b_ref[pl.ds(r0, ELEM_ROWS), :] = om * lax.rsqrt(jnp.maximum(om, 1e-37)) * (ig * xc)
            return 0

        lax.fori_loop(0, r // ELEM_ROWS, sub, 0)

    prepare(0, mf_ref, pf_ref, nf_ref, i, af_ref, bf_ref)
    prepare(1, mb_ref, pb_ref, nb_ref, nt - 1 - i, ab_ref, bb_ref)

    def step(t, carry):
        hf, hb = carry
        rf = pl.ds(pl.multiple_of(t * SUBLANES, SUBLANES), SUBLANES)
        hf = af_ref[rf, :] * hf + bf_ref[rf, :]
        bf_ref[rf, :] = hf
        rb = pl.ds(pl.multiple_of((steps - 1 - t) * SUBLANES, SUBLANES), SUBLANES)
        hb = ab_ref[rb, :] * hb + bb_ref[rb, :]
        bb_ref[rb, :] = hb
        return hf, hb

    hf, hb = lax.fori_loop(0, steps, step, (carry_ref[0], carry_ref[1]), unroll=8)
    carry_ref[0] = hf
    carry_ref[1] = hb
    hf_ref[...] = bf_ref[...].astype(BF16)
    hb_ref[...] = bb_ref[...].astype(BF16)


def _lru(ax, layer, conv_w, conv_b, w_gate, b_gate, c8):
    rows, w = ax.shape
    r = LRU_STEPS * SUBLANES
    nt = rows // r
    hb = 2 * SUBLANES
    per = r // hb
    last_hb = rows // hb - 1

    def main(ci):
        return pl.BlockSpec((r, w), lambda i: (ci(i), 0))

    def prev(ci):
        return pl.BlockSpec((hb, w), lambda i: (jnp.maximum(ci(i) * per - 1, 0), 0))

    def nxt(ci):
        return pl.BlockSpec((hb, w), lambda i: (jnp.minimum((ci(i) + 1) * per, last_hb), 0))

    fwd = lambda i: i
    bwd = lambda i: nt - 1 - i
    out_shape = jax.ShapeDtypeStruct((rows, w), BF16)
    return pl.pallas_call(
        _lru_body,
        grid=(nt,),
        in_specs=[
            main(fwd), prev(fwd), nxt(fwd), main(bwd), prev(bwd), nxt(bwd),
            _layer_spec((LRU_CONV_W, w), layer),
            _layer_spec((1, w), layer),
            _layer_spec((2, w, 2 * w), layer),
            _layer_spec((2, 1, 2 * w), layer),
            _layer_spec((2, 1, w), layer),
        ],
        out_specs=[main(fwd), main(bwd)],
        out_shape=[out_shape, out_shape],
        scratch_shapes=[
            pltpu.VMEM((r + 3 * SUBLANES, w), F32),
            pltpu.VMEM((r, w), F32), pltpu.VMEM((r, w), F32),
            pltpu.VMEM((r, w), F32), pltpu.VMEM((r, w), F32),
            pltpu.VMEM((2, SUBLANES, w), F32),
        ],
        compiler_params=_params("arbitrary"),
        name="rglru",
    )(ax, ax, ax, ax, ax, ax, conv_w, conv_b, w_gate, b_gate, c8)


def _pool_body(m_ref, p_ref, n_ref, wp_ref, sc_ref, o_ref, ext_ref, pooled_ref, *, seq_len):
    i = pl.program_id(0)
    nt = pl.num_programs(0)
    r = m_ref.shape[0]
    steps = r // SUBLANES
    halo = max(POOL_WINDOWS) // 2
    hr = halo * SUBLANES
    _fill_extended(ext_ref, m_ref, p_ref, n_ref, hr, hr, i == 0, i == nt - 1)

    t_glob = i * steps + lax.broadcasted_iota(jnp.int32, (r, POOL_GROUP), 0) // SUBLANES
    for g, win in enumerate(POOL_WINDOWS):
        ls = slice(g * POOL_GROUP, (g + 1) * POOL_GROUP)
        e = ext_ref[:, ls]
        lo_t = -halo
        cur = e
        half = 1
        while half < win:
            n = cur.shape[0] - half * SUBLANES
            if half == 1:
                cur = cur[0:n, :] + cur[SUBLANES:SUBLANES + n, :]
                lo_t += 1
            else:
                sh = (half // 2) * SUBLANES
                cur = cur[0:n, :] + cur[2 * sh:2 * sh + n, :]
                lo_t += half // 2
            half *= 2
        off = (0 - lo_t) * SUBLANES
        wsum = cur[off:off + r, :]
        half_w = win // 2
        cnt = (jnp.clip(t_glob + half_w, 0, seq_len) - jnp.clip(t_glob - half_w, 0, seq_len)).astype(F32)
        pooled_ref[:, ls] = wsum / cnt - e[hr:hr + r, :]
    o_ref[...] = (_dot(pooled_ref[...].astype(BF16), wp_ref[...]) * sc_ref[...]).astype(BF16)


def _pool(dx, layer, w_pool_bd, scale, seq_len):
    rows, w = dx.shape
    r = POOL_STEPS * SUBLANES
    nt = rows // r
    hb = (max(POOL_WINDOWS) // 2) * SUBLANES
    per = r // hb
    last_hb = rows // hb - 1
    return pl.pallas_call(
        functools.partial(_pool_body, seq_len=seq_len),
        grid=(nt,),
        in_specs=[
            pl.BlockSpec((r, w), lambda i: (i, 0)),
            pl.BlockSpec((hb, w), lambda i: (jnp.maximum(i * per - 1, 0), 0)),
            pl.BlockSpec((hb, w), lambda i: (jnp.minimum((i + 1) * per, last_hb), 0)),
            _layer_spec((w, w), layer),
            _layer_spec((1, w), layer),
        ],
        out_specs=pl.BlockSpec((r, w), lambda i: (i, 0)),
        out_shape=jax.ShapeDtypeStruct((rows, w), BF16),
        scratch_shapes=[pltpu.VMEM((r + 2 * hb, w), F32), pltpu.VMEM((r, w), F32)],
        compiler_params=_params("parallel"),
        name="pool",
    )(dx, dx, dx, w_pool_bd, scale)


def _attn_body(q_ref, k_ref, v_ref, bias_ref, o_ref, *, grid_rows):
    g = pl.program_id(1)
    lane = lax.broadcasted_iota(jnp.int32, (GRID_W, LANES), 1)
    lo_half = lane < NA_HEAD_DIM
    nkeys = NA_ROWS * GRID_W

    def row_body(rr, _):
        r = g * ATTN_ROWS + rr
        rs = jnp.clip(r - NA_ROWS // 2, 0, grid_rows - NA_ROWS)
        d0 = rs - r + (NA_ROWS - 1)
        k0 = pl.multiple_of(rs * GRID_W, GRID_W)
        q0 = pl.multiple_of(rr * GRID_W, GRID_W)
        scores = []
        for hp in range(NA_HEADS // 2):
            ls = slice(hp * LANES, (hp + 1) * LANES)
            kp = k_ref[pl.ds(k0, nkeys), ls]
            qp = q_ref[pl.ds(q0, GRID_W), ls]
            zero = jnp.zeros_like(qp)
            qm = jnp.concatenate([jnp.where(lo_half, qp, zero), jnp.where(lo_half, zero, qp)], axis=0)
            sc = lax.dot_general(qm, kp, (((1,), (1,)), ((), ())), preferred_element_type=F32)
            bias = jnp.concatenate([bias_ref[hp, d0 + 2 * j] for j in range(NA_ROWS // 2)], axis=-1)
            scores.append(sc + bias)
        maxes = [jnp.max(sc, axis=-1, keepdims=True) for sc in scores]
        probs = [jnp.exp2(sc - m) for sc, m in zip(scores, maxes)]
        inv = [1.0 / jnp.sum(p, axis=-1, keepdims=True) for p in probs]
        for hp in range(NA_HEADS // 2):
            ls = slice(hp * LANES, (hp + 1) * LANES)
            vp = v_ref[pl.ds(k0, nkeys), ls]
            o = _dot(probs[hp].astype(BF16), vp) * inv[hp]
            o_ref[pl.ds(q0, GRID_W), ls] = jnp.where(lo_half, o[0:GRID_W], o[GRID_W:2 * GRID_W]).astype(BF16)
        return 0

    lax.fori_loop(0, ATTN_ROWS, row_body, 0, unroll=2)


def _attn(q, k, v, layer, bias_tab, batch):
    s, bw = q.shape
    w = bw // batch
    grid_rows = s // GRID_W
    ng = grid_rows // ATTN_ROWS
    qr = ATTN_ROWS * GRID_W
    q_spec = pl.BlockSpec((qr, w), lambda b, g: (g, b))
    kv_spec = pl.BlockSpec((s, w), lambda b, g: (0, b))
    return pl.pallas_call(
        functools.partial(_attn_body, grid_rows=grid_rows),
        grid=(batch, ng),
        in_specs=[q_spec, kv_spec, kv_spec, _layer_spec(bias_tab.shape[1:], layer)],
        out_specs=q_spec,
        out_shape=jax.ShapeDtypeStruct((s, bw), BF16),
        compiler_params=_params("parallel", "parallel"),
        name="natten",
    )(q, k, v, bias_tab)


S5_TAB_ROWS = S5_CHUNK * SSM_GROUP
S5_EXPAND_ROWS = 256


def _s5_expand(tab_ref, mats_ref):
    def i32(shape, dim):
        return lax.broadcasted_iota(jnp.int32, shape, dim)

    def chunk(c, _):
        r0 = pl.multiple_of(c * S5_EXPAND_ROWS, S5_EXPAND_ROWS)
        row_s = r0 + i32((S5_EXPAND_ROWS, S5_TAB_ROWS), 0)
        col_s = i32((S5_EXPAND_ROWS, S5_TAB_ROWS), 1)
        row_m = r0 + i32((S5_EXPAND_ROWS, S5_FLAT), 0)
        col_m = i32((S5_EXPAND_ROWS, S5_FLAT), 1)
        lg_c, lg_p, lg_l = 4, 6, 7
        sel_u = jnp.where(((row_s >> lg_l) == (col_s >> lg_c)) & ((row_s & (SSM_GROUP - 1)) == (col_s & (SSM_GROUP - 1))),
                          1.0, 0.0).astype(BF16)
        sel_s = jnp.where(((row_s >> (lg_p + 3)) == (col_s >> lg_p)) & ((row_s & (SSM_STATE - 1)) == (col_s & (SSM_STATE - 1))),
                          1.0, 0.0).astype(BF16)
        grp_row_u = (row_m >> lg_c) & 7
        grp_row_s = (row_m >> lg_p) & 7
        grp_col_u = (col_m >> lg_c) & 7
        grp_col_s = (col_m >> lg_p) & 7
        plan = ((sel_u, grp_row_u == grp_col_u),
                (sel_u, grp_row_u == grp_col_s),
                (sel_u, grp_row_u == grp_col_s),
                (sel_s, grp_row_s == grp_col_u),
                (sel_s, grp_row_s == grp_col_u))
        for m, (sel, keep) in enumerate(plan):
            full = _dot(sel, tab_ref[m])
            mats_ref[m, pl.ds(r0, S5_EXPAND_ROWS), :] = jnp.where(keep, full, 0.0).astype(BF16)
        return 0

    lax.fori_loop(0, S5_FLAT // S5_EXPAND_ROWS, chunk, 0)


def _s5_body(u_ref, tab_ref, a8_ref, dsk_ref, y_ref, mats_ref, sinb_ref, sloc_ref, sin_ref, carry_ref):
    p = pl.program_id(1)
    j = pl.program_id(2)
    nblk = pl.num_programs(2)
    nk = u_ref.shape[0]
    rb = nk * SUBLANES
    half = S5_FLAT // 2
    m_g, m_hf, m_hb, m_qf, m_qb = range(5)

    @pl.when(j == 0)
    def _():
        carry_ref[...] = jnp.zeros_like(carry_ref)

    @pl.when((j == 0) & (p == 0))
    def _():
        _s5_expand(tab_ref, mats_ref)

    u3 = u_ref[...].astype(F32)
    u32 = jnp.concatenate(
        [u3[:, t * SUBLANES:(t + 1) * SUBLANES, :].reshape(rb, LANES) for t in range(S5_CHUNK)], axis=-1)
    u = u32.astype(BF16)

    def sweep(direction, reverse):
        ar = a8_ref[2 * direction:2 * direction + 1, :]
        ai = a8_ref[2 * direction + 1:2 * direction + 2, :]

        def body(n, s):
            sr, si = s
            kk = (nk - 1 - n) if reverse else n
            rows = pl.ds(pl.multiple_of(kk * SUBLANES, SUBLANES), SUBLANES)
            sin_ref[rows, 0:half] = sr
            sin_ref[rows, half:2 * half] = si
            nr = ar * sr - ai * si + sloc_ref[rows, 0:half]
            ni = ar * si + ai * sr + sloc_ref[rows, half:2 * half]
            return nr, ni

        sr, si = lax.fori_loop(0, nk, body, (carry_ref[:, 0:half], carry_ref[:, half:2 * half]), unroll=4)
        carry_ref[:, 0:half] = sr
        carry_ref[:, half:2 * half] = si

    @pl.when(p == 0)
    def _():
        sloc_ref[...] = _dot(u, mats_ref[m_hb])
        sweep(1, True)
        blk = nblk - 1 - j
        sinb_ref[pl.ds(pl.multiple_of(blk * rb, rb), rb), :] = sin_ref[...].astype(BF16)

    @pl.when(p == 1)
    def _():
        sloc_ref[...] = _dot(u, mats_ref[m_hf])
        sweep(0, False)
        y = _dot(u, mats_ref[m_g])
        y = y + _dot(sin_ref[...].astype(BF16), mats_ref[m_qf])
        y = y + _dot(sinb_ref[pl.ds(pl.multiple_of(j * rb, rb), rb), :], mats_ref[m_qb])
        y = y + dsk_ref[...] * u32
        for t in range(0, S5_CHUNK, 2):
            pair = jnp.concatenate(
                [y[:, t * LANES:(t + 1) * LANES].reshape(nk, SUBLANES, LANES),
                 y[:, (t + 1) * LANES:(t + 2) * LANES].reshape(nk, SUBLANES, LANES)], axis=1)
            y_ref[:, t * SUBLANES:(t + 2) * SUBLANES, :] = pair.astype(BF16)


def _s5(cx3, layer, mats):
    tabs, a8, dsk = mats
    nchunks, tb, w = cx3.shape
    nk = S5_BLOCK_CHUNKS
    nblk = nchunks // nk
    rows = nchunks * SUBLANES
    flat = S5_FLAT
    per_octet = lambda shape: pl.BlockSpec((None, None) + shape, lambda o, p, j: (layer, o) + (0,) * len(shape))
    return pl.pallas_call(
        _s5_body,
        grid=(S5_OCTETS, 2, nblk),
        in_specs=[
            pl.BlockSpec((nk, tb, LANES), lambda o, p, j: (j + (1 - p) * (nblk - 1 - 2 * j), 0, o)),
            per_octet((5, S5_TAB_ROWS, flat)),
            per_octet((4, flat // 2)),
            per_octet((1, flat)),
        ],
        out_specs=pl.BlockSpec((nk, tb, LANES), lambda o, p, j: (p * j, 0, o)),
        out_shape=jax.ShapeDtypeStruct((nchunks, tb, w), BF16),
        scratch_shapes=[
            pltpu.VMEM((5, flat, flat), BF16),
            pltpu.VMEM((rows, flat), BF16),
            pltpu.VMEM((nk * SUBLANES, flat), F32),
            pltpu.VMEM((nk * SUBLANES, flat), F32),
            pltpu.VMEM((SUBLANES, flat), F32),
        ],
        compiler_params=_params("arbitrary", "arbitrary", "arbitrary"),
        name="s5",
    )(cx3, tabs, a8, dsk)


def _merge_body(x_ref, p_ref, hf_ref, hb_ref, yb_ref, yc_ref, yd_ref, ns_ref,
                wag_ref, wbg_ref, wcg_ref, wdg_ref, wm0_ref, wm1_ref, wm2_ref, wm3_ref,
                wb0_ref, wb1_ref, wb2_ref, wb3_ref, wout_ref, pg_ref, pp_ref, gluw_ref, glub_ref,
                o_ref, stage_ref):
    x = x_ref[...]
    hn = _rms_rows(x, ns_ref[...]).astype(BF16)

    def branch(y, wb_ref, wm_ref):
        return _dot(y.astype(BF16), wb_ref[...]) * _sigmoid(_dot(hn, wm_ref[...]))

    ya = (hf_ref[...].astype(F32) + hb_ref[...].astype(F32)) * _silu(_dot(hn, wag_ref[...]))
    merged = branch(ya, wb0_ref, wm0_ref)
    steps = stage_ref.shape[1] // SUBLANES
    for b in range(SUBLANES):
        for l in range(W_TILES):
            c0 = b * BRANCH_W + l * LANES
            stage_ref[l, pl.ds(b, steps, stride=SUBLANES), :] = yb_ref[:, c0:c0 + LANES].astype(F32)
    yb = jnp.concatenate([stage_ref[l] for l in range(W_TILES)], axis=-1)
    merged = merged + branch(yb * _silu(_dot(hn, wbg_ref[...])), wb1_ref, wm1_ref)
    yg = _gelu_tanh(yc_ref[...].astype(F32))
    yc = yg * _sigmoid(_dot(yg.astype(BF16), gluw_ref[...]) + glub_ref[...])
    merged = merged + branch(yc * _silu(_dot(hn, wcg_ref[...])), wb2_ref, wm2_ref)
    merged = merged + branch(yd_ref[...].astype(F32) * _silu(_dot(hn, wdg_ref[...])), wb3_ref, wm3_ref)
    x1 = x + _dot(merged.astype(BF16), wout_ref[...])
    emb = _dot(p_ref[...].astype(BF16), pp_ref[...])
    o_ref[...] = x1 + _sigmoid(_dot(x1.astype(BF16), pg_ref[...])) * emb


def _merge(x, p, layer, hf, hb, yb, yc, yd, norm_scale, w_in, wbr, wout, pgate, pproj, glu_w, glu_b, batch):
    rows, d = x.shape
    w = BRANCH_W
    tm = MERGE_ROWS
    pdim = p.shape[-1]
    once = pl.Buffered(1)

    def resident(shape, *tail):
        tail = tail or (0,) * len(shape)
        return pl.BlockSpec((None,) + tuple(shape), lambda i: (layer,) + tuple(tail), pipeline_mode=once)

    row_spec = lambda n: pl.BlockSpec((tm, n), lambda i: (i, 0))
    gate_cols = [resident((d, w), 0, j) for j in (COL_AG, COL_BG, COL_CG, COL_DG)]
    merge_cols = [resident((d, d), 0, COL_MERGE + n) for n in range(N_BRANCH)]
    return pl.pallas_call(
        _merge_body,
        grid=(rows // tm,),
        in_specs=[
            row_spec(d),
            pl.BlockSpec((None, tm, pdim), lambda i: (layer, i, 0)),
            row_spec(w), row_spec(w),
            pl.BlockSpec((tm // batch, batch * w), lambda i: (i, 0)),
            row_spec(w), row_spec(w),
            resident((1, d)),
            *gate_cols, *merge_cols,
            *[pl.BlockSpec((None, None, w, d), lambda i, n=n: (layer, n, 0, 0), pipeline_mode=once)
              for n in range(N_BRANCH)],
            resident((d, d)), resident((d, d)), resident((pdim, d)),
            resident((w, w)), resident((1, w)),
        ],
        out_specs=row_spec(d),
        out_shape=jax.ShapeDtypeStruct((rows, d), F32),
        scratch_shapes=[pltpu.VMEM((W_TILES, tm, LANES), F32)],
        compiler_params=_params("parallel"),
        name="merge",
    )(x, p, hf, hb, yb, yc, yd, norm_scale, *([w_in] * 8), *([wbr] * N_BRANCH), wout, pgate, pproj, glu_w, glu_b)


def _block_diag(blocks):
    n, r, c = blocks.shape[-3:]
    eye = jnp.eye(n, dtype=blocks.dtype)
    out = jnp.einsum('...nrc,nm->...nrmc', blocks, eye)
    return out.reshape(blocks.shape[:-3] + (n * r, n * c))


def _attn_bias_tables(rpb):
    qc = np.arange(GRID_W)[:, None]
    kc = np.arange(GRID_W)[None, :]
    ws = np.clip(qc - NA_COLS // 2, 0, GRID_W - NA_COLS)
    in_win = (kc >= ws) & (kc < ws + NA_COLS)
    dc = np.clip(kc - qc, -(NA_COLS - 1), NA_COLS - 1) + NA_COLS - 1
    onehot = ((dc[None] == np.arange(2 * NA_COLS - 1)[:, None, None]) & in_win[None]).astype(np.float32)
    tab = jnp.einsum('hrm,mqk->hrqk', rpb.astype(F32), jnp.asarray(onehot), precision=lax.Precision.HIGHEST)
    tab = tab + jnp.asarray(np.where(in_win, 0.0, -1e30).astype(np.float32))
    pairs = jnp.concatenate([tab[:, :-1], tab[:, 1:]], axis=-1)
    nr = pairs.shape[1]
    pairs = pairs.reshape(NA_HEADS // 2, 2, nr, GRID_W, 2 * GRID_W)
    return jnp.transpose(pairs, (0, 2, 1, 3, 4)).reshape(NA_HEADS // 2, nr, 2 * GRID_W, 2 * GRID_W)


def _s5_matrices(a_re, a_im, log_dt, b_re, b_im, c_re, c_im, d_skip):
    L, G, P, C = S5_CHUNK, SSM_GROUPS, SSM_STATE, SSM_GROUP
    f32 = F32
    lr = jnp.minimum(a_re.astype(f32), -1e-4)
    li = a_im.astype(f32)
    dt = jnp.exp(log_dt.astype(f32))[..., None]
    mag = jnp.exp(lr * dt)
    ab_r = mag * jnp.cos(li * dt)
    ab_i = mag * jnp.sin(li * dt)
    nr = ab_r - 1.0
    den = lr * lr + li * li
    fr = ((nr * lr + ab_i * li) / den)[..., None]
    fi = ((ab_i * lr - nr * li) / den)[..., None]
    br, bi = b_re.astype(f32), b_im.astype(f32)
    bb_r = fr * br - fi * bi
    bb_i = fr * bi + fi * br
    cr, ci = c_re.astype(f32), c_im.astype(f32)
    pr, pi = [jnp.ones_like(ab_r)], [jnp.zeros_like(ab_r)]
    for _ in range(L):
        pr.append(pr[-1] * ab_r - pi[-1] * ab_i)
        pi.append(pr[-2] * ab_i + pi[-1] * ab_r)
    pw_r = jnp.stack(pr)
    pw_i = jnp.stack(pi)
    m_r = pw_r[..., None] * bb_r[None] - pw_i[..., None] * bb_i[None]
    m_i = pw_r[..., None] * bb_i[None] + pw_i[..., None] * bb_r[None]
    k_lag = (jnp.einsum('dgop,ndgpi->ndgio', cr, m_r) - jnp.einsum('dgop,ndgpi->ndgio', ci, m_i))
    no = S5_OCTETS

    tin = np.arange(L)[:, None]
    tout = np.arange(L)[None, :]
    lag = tout - tin
    kf = k_lag[:, 0][np.clip(lag, 0, L - 1)] * jnp.asarray(lag >= 0, f32)[:, :, None, None, None]
    kb = k_lag[:, 1][np.clip(-lag, 0, L - 1)] * jnp.asarray(lag <= 0, f32)[:, :, None, None, None]
    ktot = (kf + kb).reshape(L, L, no, 8, C, C)
    g_tab = jnp.transpose(ktot, (2, 0, 4, 1, 3, 5)).reshape(no, L * C, S5_FLAT)

    def h_tab(direction, powers):
        both = jnp.stack([m_r[powers, direction], m_i[powers, direction]])
        both = both.reshape(2, L, no, 8, P, C)
        return jnp.transpose(both, (2, 1, 5, 0, 3, 4)).reshape(no, L * C, S5_FLAT)

    hf_tab = h_tab(0, np.arange(L - 1, -1, -1))
    hb_tab = h_tab(1, np.arange(L))

    def q_tab(direction, powers):
        pwr = pw_r[powers, direction]
        pwi = pw_i[powers, direction]
        c_r, c_i = cr[direction], ci[direction]
        q_re = jnp.einsum('gcp,tgp->gptc', c_r, pwr) - jnp.einsum('gcp,tgp->gptc', c_i, pwi)
        q_im = -(jnp.einsum('gcp,tgp->gptc', c_r, pwi) + jnp.einsum('gcp,tgp->gptc', c_i, pwr))
        q = jnp.stack([q_re, q_im]).reshape(2, no, 8, P, L, C)
        return jnp.transpose(q, (1, 0, 3, 4, 2, 5)).reshape(no, 2 * P, S5_FLAT)

    qf_tab = q_tab(0, np.arange(1, L + 1))
    qb_tab = q_tab(1, np.arange(L, 0, -1))
    tabs = jnp.stack([g_tab, hf_tab, hb_tab, qf_tab, qb_tab], axis=1).astype(BF16)
    a8 = jnp.stack([pw_r[L, 0], pw_i[L, 0], pw_r[L, 1], pw_i[L, 1]])
    a8 = jnp.transpose(a8.reshape(4, no, 8 * P), (1, 0, 2))
    dsk = jnp.tile(d_skip.astype(f32).reshape(no, 1, LANES), (1, 1, S5_CHUNK))
    return tabs, a8, dsk


def kernel(x, p, norm_scale, w_in, lru_conv_w, lru_conv_b, lru_w_r, lru_b_r, lru_w_i, lru_b_i, lru_lambda, na_q_gain, na_k_gain, na_rel_bias, ssm_a_re, ssm_a_im, ssm_log_dt, ssm_b_re, ssm_b_im, ssm_c_re, ssm_c_im, ssm_d, ssm_glu_w, ssm_glu_b, pool_w, pool_scale, w_branch, w_out, ple_proj, ple_gate):
    b, s, d = x.shape
    w = BRANCH_W
    assert b == SUBLANES and d == D_MODEL and s % (LRU_STEPS * 4) == 0 and s // GRID_W >= NA_ROWS
    depth = w_in.shape[0]
    rows = s * b

    norm = norm_scale.astype(F32)[:, None, :]
    w_in16 = w_in.astype(BF16)
    ones_bd = _block_diag(jnp.ones((MXU_DIM // NA_HEAD_DIM, NA_HEAD_DIM, NA_HEAD_DIM), BF16))
    q_gain = (jnp.tile(na_q_gain.astype(F32), (1, NA_HEADS)) * (NA_HEAD_DIM ** -0.5 * LOG2E))[:, None, :]
    k_gain = jnp.tile(na_k_gain.astype(F32), (1, NA_HEADS))[:, None, :]
    conv_w = lru_conv_w.astype(F32)
    conv_b = lru_conv_b.astype(F32)[:, None, :]
    gate_w = jnp.concatenate([_block_diag(lru_w_r), _block_diag(lru_w_i)], axis=-1).astype(BF16)
    gate_b = jnp.concatenate([lru_b_r, lru_b_i], axis=-1).astype(F32)[:, :, None, :]
    c8 = (LRU_C * jax.nn.softplus(-lru_lambda.astype(F32)))[:, :, None, :]
    bias_tab = jax.vmap(_attn_bias_tables)(na_rel_bias.astype(F32) * LOG2E)
    s5_mats = jax.vmap(_s5_matrices)(ssm_a_re, ssm_a_im, ssm_log_dt, ssm_b_re, ssm_b_im, ssm_c_re, ssm_c_im, ssm_d)
    glu_w = ssm_glu_w.astype(BF16)
    glu_b = ssm_glu_b.astype(F32)[:, None, :]
    pool_bd = _block_diag(pool_w).astype(BF16)
    pool_sc = pool_scale.astype(F32)[:, None, :]
    wbr = w_branch.astype(BF16)
    wout = w_out.astype(BF16)
    pproj = ple_proj.astype(BF16)
    pgate = ple_gate.astype(BF16)

    xt = jnp.transpose(x, (1, 0, 2)).reshape(rows, d)
    pt = jnp.transpose(p, (0, 2, 1, 3)).reshape(depth, rows, p.shape[-1])
    nchunks = s // S5_CHUNK
    for i in range(depth):
        ax, q, k, v, cx, dx = _inproj(xt, i, norm, w_in16, ones_bd, q_gain, k_gain, b)
        hf, hb = _lru(ax, i, conv_w, conv_b, gate_w, gate_b, c8)
        yb = _attn(q, k, v, i, bias_tab, b)
        yc = _s5(cx.reshape(nchunks, S5_CHUNK * b, w), i, s5_mats).reshape(rows, w)
        yd = _pool(dx, i, pool_bd, pool_sc, s)
        xt = _merge(xt, pt, i, hf, hb, yb, yc, yd, norm, w_in16, wbr, wout, pgate, pproj, glu_w, glu_b, b)
    return jnp.transpose(xt.reshape(s, b, d), (1, 0, 2))
```

```python
import functools

import jax
import jax.numpy as jnp
import numpy as np
from jax import lax
from jax.experimental import pallas as pl
from jax.experimental.pallas import tpu as pltpu

F32 = jnp.float32
BF16 = jnp.bfloat16

D_MODEL = 1024
BRANCH_W = 512
N_BRANCH = 4
NORM_EPS = 1e-6
LOG2E = 1.4426950408889634
GRID_W = 64
LRU_C = 8.0
LRU_CONV_W = 4
NA_HEADS = 8
NA_HEAD_DIM = 64
NA_ROWS = 8
NA_COLS = 16
SSM_GROUP = 16
SSM_GROUPS = 32
SSM_STATE = 64
POOL_WINDOWS = (2, 4, 8, 16)
POOL_GROUP = 128

SUBLANES = 8
LANES = 128
MXU_DIM = 256
W_TILES = BRANCH_W // LANES
S5_CHUNK = 8
S5_OCTETS = 4
S5_FLAT = S5_CHUNK * LANES

COL_AX, COL_AG, COL_Q, COL_K, COL_V, COL_BG, COL_CX, COL_CG, COL_DX, COL_DG = range(10)
COL_MERGE = 10 * BRANCH_W // D_MODEL

PROJ_ROWS = 1024
MERGE_ROWS = 512
LRU_STEPS = 128
POOL_STEPS = 128
ATTN_ROWS = 8
S5_BLOCK_CHUNKS = 64
ELEM_ROWS = 256

VMEM_LIMIT = 56 * 1024 * 1024


def _params(*sem):
    return pltpu.CompilerParams(dimension_semantics=sem, vmem_limit_bytes=VMEM_LIMIT)


def _dot(a, b):
    return jnp.dot(a, b, preferred_element_type=F32)


def _sigmoid(z):
    return 0.5 * jnp.tanh(0.5 * z) + 0.5


def _silu(z):
    return z * _sigmoid(z)


def _gelu_tanh(y):
    return 0.5 * y * (1.0 + jnp.tanh(0.7978845608028654 * (y + 0.044715 * (y * y * y))))


def _rms_rows(x, g):
    ms = jnp.mean(x * x, axis=-1, keepdims=True)
    return x * lax.rsqrt(ms + NORM_EPS) * g


def _head_mean_square(v, ones_ref):
    v2 = (v * v).astype(BF16)
    n = ones_ref.shape[0]
    sums = [_dot(v2[:, c:c + n], ones_ref[...]) for c in range(0, v.shape[1], n)]
    return jnp.concatenate(sums, axis=-1) * (1.0 / NA_HEAD_DIM)


def _layer_spec(shape, layer, *tail):
    tail = tail or (0,) * len(shape)
    return pl.BlockSpec((None,) + tuple(shape), lambda *_: (layer,) + tuple(tail))


def _load_time_major(src_ref, slab_ref):
    nb, steps, n = src_ref.shape
    tiles = n // LANES
    for b in range(nb):
        for l in range(tiles):
            slab_ref[l, pl.ds(b, steps, stride=nb), :] = src_ref[b, :, l * LANES:(l + 1) * LANES]
    return jnp.concatenate([slab_ref[l] for l in range(tiles)], axis=-1)


def _store_batch_major(val, dst_ref, slab_ref):
    nb, steps, n = dst_ref.shape
    tiles = n // LANES
    for l in range(tiles):
        slab_ref[l] = val[:, l * LANES:(l + 1) * LANES]
    for b in range(nb):
        for l in range(tiles):
            dst_ref[b, :, l * LANES:(l + 1) * LANES] = slab_ref[l, pl.ds(b, steps, stride=nb), :]


def _inproj_body(x_ref, ns_ref, wax_ref, wq_ref, wk_ref, wv_ref, wcx_ref, wdx_ref, ones_ref, qg_ref, kg_ref,
                 ax_ref, q_ref, k_ref, v_ref, cx_ref, dx_ref, stage_ref, *slabs):
    x = _load_time_major(x_ref, slabs[0]) if slabs else x_ref[...]
    hn = _rms_rows(x, ns_ref[...]).astype(BF16)
    steps = stage_ref.shape[1] // SUBLANES

    def to_batch_lanes(val, out_ref):
        for l in range(W_TILES):
            stage_ref[l] = val[:, l * LANES:(l + 1) * LANES]
        for b in range(SUBLANES):
            for l in range(W_TILES):
                c0 = b * BRANCH_W + l * LANES
                out_ref[:, c0:c0 + LANES] = stage_ref[l, pl.ds(b, steps, stride=SUBLANES), :].astype(BF16)

    ax_ref[...] = _dot(hn, wax_ref[...]).astype(BF16)
    q = _dot(hn, wq_ref[...])
    to_batch_lanes(q * lax.rsqrt(_head_mean_square(q, ones_ref) + NORM_EPS) * qg_ref[...], q_ref)
    k = _dot(hn, wk_ref[...])
    to_batch_lanes(k * lax.rsqrt(_head_mean_square(k, ones_ref) + NORM_EPS) * kg_ref[...], k_ref)
    to_batch_lanes(_dot(hn, wv_ref[...]), v_ref)
    cx_ref[...] = _dot(hn, wcx_ref[...]).astype(BF16)
    dx_ref[...] = _dot(hn, wdx_ref[...]).astype(BF16)


def _inproj(x, layer, norm_scale, w_in, ones_bd, q_gain, k_gain, batch):
    w = BRANCH_W
    tm = PROJ_ROWS
    batch_major = x.ndim == 3
    d = x.shape[-1]
    rows = x.shape[0] * x.shape[1] if batch_major else x.shape[0]
    s = rows // batch
    if batch_major:
        x_spec = pl.BlockSpec((batch, tm // batch, d), lambda i: (0, i, 0))
        slabs = [pltpu.VMEM((d // LANES, tm, LANES), F32)]
    else:
        x_spec = pl.BlockSpec((tm, d), lambda i: (i, 0))
        slabs = []
    wcol = lambda j: pl.BlockSpec((None, d, w), lambda i: (layer, 0, j))
    tm_spec = pl.BlockSpec((tm, w), lambda i: (i, 0))
    bl_spec = pl.BlockSpec((tm // batch, batch * w), lambda i: (i, 0))
    tm_shape = jax.ShapeDtypeStruct((rows, w), BF16)
    bl_shape = jax.ShapeDtypeStruct((s, batch * w), BF16)
    return pl.pallas_call(
        _inproj_body,
        grid=(rows // tm,),
        in_specs=[
            x_spec,
            _layer_spec((1, d), layer),
            wcol(COL_AX), wcol(COL_Q), wcol(COL_K), wcol(COL_V), wcol(COL_CX), wcol(COL_DX),
            pl.BlockSpec(ones_bd.shape, lambda i: (0, 0)),
            _layer_spec((1, w), layer), _layer_spec((1, w), layer),
        ],
        out_specs=[tm_spec, bl_spec, bl_spec, bl_spec, tm_spec, tm_spec],
        out_shape=[tm_shape, bl_shape, bl_shape, bl_shape, tm_shape, tm_shape],
        scratch_shapes=[pltpu.VMEM((W_TILES, tm, LANES), F32)] + slabs,
        compiler_params=_params("parallel"),
        name="inproj",
    )(x, norm_scale, w_in, w_in, w_in, w_in, w_in, w_in, ones_bd, q_gain, k_gain)


def _fill_extended(ext_ref, main_ref, prev_ref, next_ref, n_prev, n_next, is_first, is_last):
    r = main_ref.shape[0]
    prev = prev_ref[...].astype(F32)
    ext_ref[0:n_prev, :] = jnp.where(is_first, 0.0, prev[prev.shape[0] - n_prev:, :])
    ext_ref[n_prev:n_prev + r, :] = main_ref[...].astype(F32)
    nxt = next_ref[...].astype(F32)
    ext_ref[n_prev + r:n_prev + r + n_next, :] = jnp.where(is_last, 0.0, nxt[0:n_next, :])


def _lru_body(mf_ref, pf_ref, nf_ref, mb_ref, pb_ref, nb_ref, cw_ref, cb_ref, wg_ref, bg_ref, c8_ref,
              hf_ref, hb_ref, ext_ref, af_ref, bf_ref, ab_ref, bb_ref, carry_ref):
    i = pl.program_id(0)
    nt = pl.num_programs(0)
    w = BRANCH_W
    r = mf_ref.shape[0]
    steps = r // SUBLANES
    n_prev, n_next = 2 * SUBLANES, SUBLANES

    @pl.when(i == 0)
    def _():
        carry_ref[...] = jnp.zeros_like(carry_ref)

    def prepare(direction, main_ref, prev_ref, next_ref, chunk, a_ref, b_ref):
        _fill_extended(ext_ref, main_ref, prev_ref, next_ref, n_prev, n_next, chunk == 0, chunk == nt - 1)

        def sub(sb, _):
            r0 = pl.multiple_of(sb * ELEM_ROWS, ELEM_ROWS)
            xc = cb_ref[...]
            for tap in range(LRU_CONV_W):
                xc = xc + cw_ref[tap:tap + 1, :] * ext_ref[pl.ds(r0 + tap * SUBLANES, ELEM_ROWS), :]
            g = _dot(xc.astype(BF16), wg_ref[direction]) + bg_ref[direction]
            rg = _sigmoid(g[:, 0:w])
            ig = _sigmoid(g[:, w:2 * w])
            log_a = -(c8_ref[direction] * rg)
            a = jnp.exp(log_a)
            a_ref[pl.ds(r0, ELEM_ROWS), :] = a
            om = 1.0 - a * a
            b_ref[pl.ds(r0, ELEM_ROWS), :] = om * lax.rsqrt(jnp.maximum(om, 1e-37)) * (ig * xc)
            return 0

        lax.fori_loop(0, r // ELEM_ROWS, sub, 0)

    prepare(0, mf_ref, pf_ref, nf_ref, i, af_ref, bf_ref)
    prepare(1, mb_ref, pb_ref, nb_ref, nt - 1 - i, ab_ref, bb_ref)

    def step(t, carry):
        hf, hb = carry
        rf = pl.ds(pl.multiple_of(t * SUBLANES, SUBLANES), SUBLANES)
        hf = af_ref[rf, :] * hf + bf_ref[rf, :]
        bf_ref[rf, :] = hf
        rb = pl.ds(pl.multiple_of((steps - 1 - t) * SUBLANES, SUBLANES), SUBLANES)
        hb = ab_ref[rb, :] * hb + bb_ref[rb, :]
        bb_ref[rb, :] = hb
        return hf, hb

    hf, hb = lax.fori_loop(0, steps, step, (carry_ref[0], carry_ref[1]), unroll=8)
    carry_ref[0] = hf
    carry_ref[1] = hb
    hf_ref[...] = bf_ref[...].astype(BF16)
    hb_ref[...] = bb_ref[...].astype(BF16)


def _lru(ax, layer, conv_w, conv_b, w_gate, b_gate, c8):
    rows, w = ax.shape
    r = LRU_STEPS * SUBLANES
    nt = rows // r
    hb = 2 * SUBLANES
    per = r // hb
    last_hb = rows // hb - 1

    def main(ci):
        return pl.BlockSpec((r, w), lambda i: (ci(i), 0))

    def prev(ci):
        return pl.BlockSpec((hb, w), lambda i: (jnp.maximum(ci(i) * per - 1, 0), 0))

    def nxt(ci):
        return pl.BlockSpec((hb, w), lambda i: (jnp.minimum((ci(i) + 1) * per, last_hb), 0))

    fwd = lambda i: i
    bwd = lambda i: nt - 1 - i
    out_shape = jax.ShapeDtypeStruct((rows, w), BF16)
    return pl.pallas_call(
        _lru_body,
        grid=(nt,),
        in_specs=[
            main(fwd), prev(fwd), nxt(fwd), main(bwd), prev(bwd), nxt(bwd),
            _layer_spec((LRU_CONV_W, w), layer),
            _layer_spec((1, w), layer),
            _layer_spec((2, w, 2 * w), layer),
            _layer_spec((2, 1, 2 * w), layer),
            _layer_spec((2, 1, w), layer),
        ],
        out_specs=[main(fwd), main(bwd)],
        out_shape=[out_shape, out_shape],
        scratch_shapes=[
            pltpu.VMEM((r + 3 * SUBLANES, w), F32),
            pltpu.VMEM((r, w), F32), pltpu.VMEM((r, w), F32),
            pltpu.VMEM((r, w), F32), pltpu.VMEM((r, w), F32),
            pltpu.VMEM((2, SUBLANES, w), F32),
        ],
        compiler_params=_params("arbitrary"),
        name="rglru",
    )(ax, ax, ax, ax, ax, ax, conv_w, conv_b, w_gate, b_gate, c8)


def _pool_body(m_ref, p_ref, n_ref, wp_ref, sc_ref, o_ref, ext_ref, pooled_ref, *, seq_len):
    i = pl.program_id(0)
    nt = pl.num_programs(0)
    r = m_ref.shape[0]
    steps = r // SUBLANES
    halo = max(POOL_WINDOWS) // 2
    hr = halo * SUBLANES
    _fill_extended(ext_ref, m_ref, p_ref, n_ref, hr, hr, i == 0, i == nt - 1)

    t_glob = i * steps + lax.broadcasted_iota(jnp.int32, (r, POOL_GROUP), 0) // SUBLANES
    for g, win in enumerate(POOL_WINDOWS):
        ls = slice(g * POOL_GROUP, (g + 1) * POOL_GROUP)
        e = ext_ref[:, ls]
        lo_t = -halo
        cur = e
        half = 1
        while half < win:
            n = cur.shape[0] - half * SUBLANES
            if half == 1:
                cur = cur[0:n, :] + cur[SUBLANES:SUBLANES + n, :]
                lo_t += 1
            else:
                sh = (half // 2) * SUBLANES
                cur = cur[0:n, :] + cur[2 * sh:2 * sh + n, :]
                lo_t += half // 2
            half *= 2
        off = (0 - lo_t) * SUBLANES
        wsum = cur[off:off + r, :]
        half_w = win // 2
        cnt = (jnp.clip(t_glob + half_w, 0, seq_len) - jnp.clip(t_glob - half_w, 0, seq_len)).astype(F32)
        pooled_ref[:, ls] = wsum / cnt - e[hr:hr + r, :]
    o_ref[...] = (_dot(pooled_ref[...].astype(BF16), wp_ref[...]) * sc_ref[...]).astype(BF16)


def _pool(dx, layer, w_pool_bd, scale, seq_len):
    rows, w = dx.shape
    r = POOL_STEPS * SUBLANES
    nt = rows // r
    hb = (max(POOL_WINDOWS) // 2) * SUBLANES
    per = r // hb
    last_hb = rows // hb - 1
    return pl.pallas_call(
        functools.partial(_pool_body, seq_len=seq_len),
        grid=(nt,),
        in_specs=[
            pl.BlockSpec((r, w), lambda i: (i, 0)),
            pl.BlockSpec((hb, w), lambda i: (jnp.maximum(i * per - 1, 0), 0)),
            pl.BlockSpec((hb, w), lambda i: (jnp.minimum((i + 1) * per, last_hb), 0)),
            _layer_spec((w, w), layer),
            _layer_spec((1, w), layer),
        ],
        out_specs=pl.BlockSpec((r, w), lambda i: (i, 0)),
        out_shape=jax.ShapeDtypeStruct((rows, w), BF16),
        scratch_shapes=[pltpu.VMEM((r + 2 * hb, w), F32), pltpu.VMEM((r, w), F32)],
        compiler_params=_params("parallel"),
        name="pool",
    )(dx, dx, dx, w_pool_bd, scale)


def _attn_body(q_ref, k_ref, v_ref, bias_ref, o_ref, *, grid_rows):
    g = pl.program_id(1)
    lane = lax.broadcasted_iota(jnp.int32, (GRID_W, LANES), 1)
    lo_half = lane < NA_HEAD_DIM
    nkeys = NA_ROWS * GRID_W

    def row_body(rr, _):
        r = g * ATTN_ROWS + rr
        rs = jnp.clip(r - NA_ROWS // 2, 0, grid_rows - NA_ROWS)
        d0 = rs - r + (NA_ROWS - 1)
        k0 = pl.multiple_of(rs * GRID_W, GRID_W)
        q0 = pl.multiple_of(rr * GRID_W, GRID_W)
        scores = []
        for hp in range(NA_HEADS // 2):
            ls = slice(hp * LANES, (hp + 1) * LANES)
            kp = k_ref[pl.ds(k0, nkeys), ls]
            qp = q_ref[pl.ds(q0, GRID_W), ls]
            zero = jnp.zeros_like(qp)
            qm = jnp.concatenate([jnp.where(lo_half, qp, zero), jnp.where(lo_half, zero, qp)], axis=0)
            sc = lax.dot_general(qm, kp, (((1,), (1,)), ((), ())), preferred_element_type=F32)
            bias = jnp.concatenate([bias_ref[hp, d0 + 2 * j] for j in range(NA_ROWS // 2)], axis=-1)
            scores.append(sc + bias)
        maxes = [jnp.max(sc, axis=-1, keepdims=True) for sc in scores]
        probs = [jnp.exp2(sc - m) for sc, m in zip(scores, maxes)]
        inv = [1.0 / jnp.sum(p, axis=-1, keepdims=True) for p in probs]
        for hp in range(NA_HEADS // 2):
            ls = slice(hp * LANES, (hp + 1) * LANES)
            vp = v_ref[pl.ds(k0, nkeys), ls]
            o = _dot(probs[hp].astype(BF16), vp) * inv[hp]
            o_ref[pl.ds(q0, GRID_W), ls] = jnp.where(lo_half, o[0:GRID_W], o[GRID_W:2 * GRID_W]).astype(BF16)
        return 0

    lax.fori_loop(0, ATTN_ROWS, row_body, 0, unroll=2)


def _attn(q, k, v, layer, bias_tab, batch):
    s, bw = q.shape
    w = bw // batch
    grid_rows = s // GRID_W
    ng = grid_rows // ATTN_ROWS
    qr = ATTN_ROWS * GRID_W
    q_spec = pl.BlockSpec((qr, w), lambda b, g: (g, b))
    kv_spec = pl.BlockSpec((s, w), lambda b, g: (0, b))
    return pl.pallas_call(
        functools.partial(_attn_body, grid_rows=grid_rows),
        grid=(batch, ng),
        in_specs=[q_spec, kv_spec, kv_spec, _layer_spec(bias_tab.shape[1:], layer)],
        out_specs=q_spec,
        out_shape=jax.ShapeDtypeStruct((s, bw), BF16),
        compiler_params=_params("parallel", "parallel"),
        name="natten",
    )(q, k, v, bias_tab)


S5_TAB_ROWS = S5_CHUNK * SSM_GROUP
S5_EXPAND_ROWS = 256


def _s5_expand(klag_ref, tab_ref, gtab_ref, mats_ref):
    for t_in in range(S5_CHUNK):
        for t_out in range(S5_CHUNK):
            gtab_ref[t_in * SSM_GROUP:(t_in + 1) * SSM_GROUP, t_out * LANES:(t_out + 1) * LANES] = (
                klag_ref[t_out - t_in + S5_CHUNK - 1])

    def i32(shape, dim):
        return lax.broadcasted_iota(jnp.int32, shape, dim)

    def chunk(c, _):
        r0 = pl.multiple_of(c * S5_EXPAND_ROWS, S5_EXPAND_ROWS)
        row_s = r0 + i32((S5_EXPAND_ROWS, S5_TAB_ROWS), 0)
        col_s = i32((S5_EXPAND_ROWS, S5_TAB_ROWS), 1)
        row_m = r0 + i32((S5_EXPAND_ROWS, S5_FLAT), 0)
        col_m = i32((S5_EXPAND_ROWS, S5_FLAT), 1)
        lg_c, lg_p, lg_l = 4, 6, 7
        sel_u = jnp.where(((row_s >> lg_l) == (col_s >> lg_c)) & ((row_s & (SSM_GROUP - 1)) == (col_s & (SSM_GROUP - 1))),
                          1.0, 0.0).astype(BF16)
        sel_s = jnp.where(((row_s >> (lg_p + 3)) == (col_s >> lg_p)) & ((row_s & (SSM_STATE - 1)) == (col_s & (SSM_STATE - 1))),
                          1.0, 0.0).astype(BF16)
        grp_row_u = (row_m >> lg_c) & 7
        grp_row_s = (row_m >> lg_p) & 7
        grp_col_u = (col_m >> lg_c) & 7
        grp_col_s = (col_m >> lg_p) & 7
        plan = ((sel_u, grp_row_u == grp_col_u),
                (sel_u, grp_row_u == grp_col_s),
                (sel_u, grp_row_u == grp_col_s),
                (sel_s, grp_row_s == grp_col_u),
                (sel_s, grp_row_s == grp_col_u))
        for m, (sel, keep) in enumerate(plan):
            full = _dot(sel, gtab_ref[...] if m == 0 else tab_ref[m - 1])
            mats_ref[m, pl.ds(r0, S5_EXPAND_ROWS), :] = jnp.where(keep, full, 0.0).astype(BF16)
        return 0

    lax.fori_loop(0, S5_FLAT // S5_EXPAND_ROWS, chunk, 0)


def _s5_body(u_ref, klag_ref, tab_ref, a8_ref, dsk_ref,
             y_ref, gtab_ref, mats_ref, sinb_ref, sloc_ref, sin_ref, carry_ref):
    p = pl.program_id(1)
    j = pl.program_id(2)
    nblk = pl.num_programs(2)
    nk = u_ref.shape[0]
    rb = nk * SUBLANES
    half = S5_FLAT // 2
    m_g, m_hf, m_hb, m_qf, m_qb = range(5)

    @pl.when(j == 0)
    def _():
        carry_ref[...] = jnp.zeros_like(carry_ref)

    @pl.when((j == 0) & (p == 0))
    def _():
        _s5_expand(klag_ref, tab_ref, gtab_ref, mats_ref)

    u3 = u_ref[...].astype(F32)
    u32 = jnp.concatenate(
        [u3[:, t * SUBLANES:(t + 1) * SUBLANES, :].reshape(rb, LANES) for t in range(S5_CHUNK)], axis=-1)
    u = u32.astype(BF16)

    def sweep(direction, reverse):
        ar = a8_ref[2 * direction:2 * direction + 1, :]
        ai = a8_ref[2 * direction + 1:2 * direction + 2, :]

        def body(n, s):
            sr, si = s
            kk = (nk - 1 - n) if reverse else n
            rows = pl.ds(pl.multiple_of(kk * SUBLANES, SUBLANES), SUBLANES)
            sin_ref[rows, 0:half] = sr
            sin_ref[rows, half:2 * half] = si
            nr = ar * sr - ai * si + sloc_ref[rows, 0:half]
            ni = ar * si + ai * sr + sloc_ref[rows, half:2 * half]
            return nr, ni

        sr, si = lax.fori_loop(0, nk, body, (carry_ref[:, 0:half], carry_ref[:, half:2 * half]), unroll=4)
        carry_ref[:, 0:half] = sr
        carry_ref[:, half:2 * half] = si

    @pl.when(p == 0)
    def _():
        sloc_ref[...] = _dot(u, mats_ref[m_hb])
        sweep(1, True)
        blk = nblk - 1 - j
        sinb_ref[pl.ds(pl.multiple_of(blk * rb, rb), rb), :] = sin_ref[...].astype(BF16)

    @pl.when(p == 1)
    def _():
        sloc_ref[...] = _dot(u, mats_ref[m_hf])
        sweep(0, False)
        y = _dot(u, mats_ref[m_g])
        y = y + _dot(sin_ref[...].astype(BF16), mats_ref[m_qf])
        y = y + _dot(sinb_ref[pl.ds(pl.multiple_of(j * rb, rb), rb), :], mats_ref[m_qb])
        y = y + dsk_ref[...] * u32
        for t in range(0, S5_CHUNK, 2):
            pair = jnp.concatenate(
                [y[:, t * LANES:(t + 1) * LANES].reshape(nk, SUBLANES, LANES),
                 y[:, (t + 1) * LANES:(t + 2) * LANES].reshape(nk, SUBLANES, LANES)], axis=1)
            y_ref[:, t * SUBLANES:(t + 2) * SUBLANES, :] = pair.astype(BF16)


def _s5(cx3, layer, mats):
    klag, tabs, a8, dsk = mats
    nchunks, tb, w = cx3.shape
    nk = S5_BLOCK_CHUNKS
    nblk = nchunks // nk
    rows = nchunks * SUBLANES
    flat = S5_FLAT
    per_octet = lambda shape: pl.BlockSpec((None, None) + shape, lambda o, p, j: (layer, o) + (0,) * len(shape))
    return pl.pallas_call(
        _s5_body,
        grid=(S5_OCTETS, 2, nblk),
        in_specs=[
            pl.BlockSpec((nk, tb, LANES), lambda o, p, j: (j + (1 - p) * (nblk - 1 - 2 * j), 0, o)),
            per_octet((2 * S5_CHUNK - 1, SSM_GROUP, LANES)),
            per_octet((4, S5_TAB_ROWS, flat)),
            per_octet((4, flat // 2)),
            per_octet((1, flat)),
        ],
        out_specs=pl.BlockSpec((nk, tb, LANES), lambda o, p, j: (p * j, 0, o)),
        out_shape=jax.ShapeDtypeStruct((nchunks, tb, w), BF16),
        scratch_shapes=[
            pltpu.VMEM((S5_TAB_ROWS, flat), BF16),
            pltpu.VMEM((5, flat, flat), BF16),
            pltpu.VMEM((rows, flat), BF16),
            pltpu.VMEM((nk * SUBLANES, flat), F32),
            pltpu.VMEM((nk * SUBLANES, flat), F32),
            pltpu.VMEM((SUBLANES, flat), F32),
        ],
        compiler_params=_params("arbitrary", "arbitrary", "arbitrary"),
        name="s5",
    )(cx3, klag, tabs, a8, dsk)


def _merge_body(x_ref, p_ref, hf_ref, hb_ref, yb_ref, yc_ref, yd_ref, ns_ref,
                wag_ref, wbg_ref, wcg_ref, wdg_ref, wm0_ref, wm1_ref, wm2_ref, wm3_ref,
                wb0_ref, wb1_ref, wb2_ref, wb3_ref, wout_ref, pg_ref, pp_ref, gluw_ref, glub_ref,
                o_ref, stage_ref, pslab_ref, xslab_ref, *, x_batch_major, out_batch_major):
    x = _load_time_major(x_ref, xslab_ref) if x_batch_major else x_ref[...]
    hn = _rms_rows(x, ns_ref[...]).astype(BF16)

    def branch(y, wb_ref, wm_ref):
        return _dot(y.astype(BF16), wb_ref[...]) * _sigmoid(_dot(hn, wm_ref[...]))

    ya = (hf_ref[...].astype(F32) + hb_ref[...].astype(F32)) * _silu(_dot(hn, wag_ref[...]))
    merged = branch(ya, wb0_ref, wm0_ref)
    steps = stage_ref.shape[1] // SUBLANES
    for b in range(SUBLANES):
        for l in range(W_TILES):
            c0 = b * BRANCH_W + l * LANES
            stage_ref[l, pl.ds(b, steps, stride=SUBLANES), :] = yb_ref[:, c0:c0 + LANES].astype(F32)
    yb = jnp.concatenate([stage_ref[l] for l in range(W_TILES)], axis=-1)
    merged = merged + branch(yb * _silu(_dot(hn, wbg_ref[...])), wb1_ref, wm1_ref)
    yg = _gelu_tanh(yc_ref[...].astype(F32))
    yc = yg * _sigmoid(_dot(yg.astype(BF16), gluw_ref[...]) + glub_ref[...])
    merged = merged + branch(yc * _silu(_dot(hn, wcg_ref[...])), wb2_ref, wm2_ref)
    merged = merged + branch(yd_ref[...].astype(F32) * _silu(_dot(hn, wdg_ref[...])), wb3_ref, wm3_ref)
    x1 = x + _dot(merged.astype(BF16), wout_ref[...])
    emb = _dot(_load_time_major(p_ref, pslab_ref).astype(BF16), pp_ref[...])
    out = x1 + _sigmoid(_dot(x1.astype(BF16), pg_ref[...])) * emb
    if out_batch_major:
        _store_batch_major(out, o_ref, xslab_ref)
    else:
        o_ref[...] = out


def _merge(x, p, layer, hf, hb, yb, yc, yd, norm_scale, w_in, wbr, wout, pgate, pproj, glu_w, glu_b, batch,
           out_batch_major):
    w = BRANCH_W
    tm = MERGE_ROWS
    pdim = p.shape[-1]
    once = pl.Buffered(1)
    x_batch_major = x.ndim == 3
    d = x.shape[-1]
    rows = x.shape[0] * x.shape[1] if x_batch_major else x.shape[0]
    steps = tm // batch
    bm_spec = pl.BlockSpec((batch, steps, d), lambda i: (0, i, 0))
    x_spec = bm_spec if x_batch_major else pl.BlockSpec((tm, d), lambda i: (i, 0))
    if out_batch_major:
        out_spec, out_shape = bm_spec, jax.ShapeDtypeStruct((batch, rows // batch, d), F32)
    else:
        out_spec, out_shape = pl.BlockSpec((tm, d), lambda i: (i, 0)), jax.ShapeDtypeStruct((rows, d), F32)

    def resident(shape, *tail):
        tail = tail or (0,) * len(shape)
        return pl.BlockSpec((None,) + tuple(shape), lambda i: (layer,) + tuple(tail), pipeline_mode=once)

    row_spec = lambda n: pl.BlockSpec((tm, n), lambda i: (i, 0))
    gate_cols = [resident((d, w), 0, j) for j in (COL_AG, COL_BG, COL_CG, COL_DG)]
    merge_cols = [resident((d, d), 0, COL_MERGE + n) for n in range(N_BRANCH)]
    return pl.pallas_call(
        functools.partial(_merge_body, x_batch_major=x_batch_major, out_batch_major=out_batch_major),
        grid=(rows // tm,),
        in_specs=[
            x_spec,
            pl.BlockSpec((None, batch, steps, pdim), lambda i: (layer, 0, i, 0)),
            row_spec(w), row_spec(w),
            pl.BlockSpec((tm // batch, batch * w), lambda i: (i, 0)),
            row_spec(w), row_spec(w),
            resident((1, d)),
            *gate_cols, *merge_cols,
            *[pl.BlockSpec((None, None, w, d), lambda i, n=n: (layer, n, 0, 0), pipeline_mode=once)
              for n in range(N_BRANCH)],
            resident((d, d)), resident((d, d)), resident((pdim, d)),
            resident((w, w)), resident((1, w)),
        ],
        out_specs=out_spec,
        out_shape=out_shape,
        scratch_shapes=[pltpu.VMEM((W_TILES, tm, LANES), F32),
                        pltpu.VMEM((pdim // LANES, tm, LANES), F32),
                        pltpu.VMEM((d // LANES, tm, LANES), F32)],
        compiler_params=_params("parallel"),
        name="merge",
    )(x, p, hf, hb, yb, yc, yd, norm_scale, *([w_in] * 8), *([wbr] * N_BRANCH), wout, pgate, pproj, glu_w, glu_b)


def _block_diag(blocks):
    n, r, c = blocks.shape[-3:]
    eye = jnp.eye(n, dtype=blocks.dtype)
    out = jnp.einsum('...nrc,nm->...nrmc', blocks, eye)
    return out.reshape(blocks.shape[:-3] + (n * r, n * c))


def _attn_bias_tables(rpb):
    qc = np.arange(GRID_W)[:, None]
    kc = np.arange(GRID_W)[None, :]
    ws = np.clip(qc - NA_COLS // 2, 0, GRID_W - NA_COLS)
    in_win = (kc >= ws) & (kc < ws + NA_COLS)
    dc = np.clip(kc - qc, -(NA_COLS - 1), NA_COLS - 1) + NA_COLS - 1
    onehot = ((dc[None] == np.arange(2 * NA_COLS - 1)[:, None, None]) & in_win[None]).astype(np.float32)
    tab = jnp.einsum('hrm,mqk->hrqk', rpb.astype(F32), jnp.asarray(onehot), precision=lax.Precision.HIGHEST)
    tab = tab + jnp.asarray(np.where(in_win, 0.0, -1e30).astype(np.float32))
    pairs = jnp.concatenate([tab[:, :-1], tab[:, 1:]], axis=-1)
    nr = pairs.shape[1]
    pairs = pairs.reshape(NA_HEADS // 2, 2, nr, GRID_W, 2 * GRID_W)
    return jnp.transpose(pairs, (0, 2, 1, 3, 4)).reshape(NA_HEADS // 2, nr, 2 * GRID_W, 2 * GRID_W)


def _s5_matrices(a_re, a_im, log_dt, b_re, b_im, c_re, c_im, d_skip):
    L, P, C = S5_CHUNK, SSM_STATE, SSM_GROUP
    no, ng = S5_OCTETS, SSM_GROUPS // S5_OCTETS
    jp, jc = ng * P, ng * C
    f32 = F32
    hi = lax.Precision.HIGHEST
    wide = lambda v: v.astype(f32).reshape(2, no, jp)
    lr = jnp.minimum(wide(a_re), -1e-4)
    li = wide(a_im)
    dt = jnp.repeat(jnp.exp(log_dt.astype(f32)), P, axis=-1).reshape(2, no, jp)
    mag = jnp.exp(lr * dt)
    ab_r = mag * jnp.cos(li * dt)
    ab_i = mag * jnp.sin(li * dt)
    nr = ab_r - 1.0
    den = lr * lr + li * li
    fr = ((nr * lr + ab_i * li) / den)[:, :, None]
    fi = ((ab_i * lr - nr * li) / den)[:, :, None]
    bt = lambda v: jnp.transpose(v.astype(f32).reshape(2, no, ng, P, C), (0, 1, 4, 2, 3)).reshape(2, no, C, jp)
    ct = lambda v: jnp.transpose(v.astype(f32).reshape(2, no, ng, C, P), (0, 1, 3, 2, 4)).reshape(2, no, C, jp)
    br, bi = bt(b_re), bt(b_im)
    bb_r = fr * br - fi * bi
    bb_i = fr * bi + fi * br
    cr, ci = ct(c_re), ct(c_im)
    pr, pi = [jnp.ones_like(ab_r)], [jnp.zeros_like(ab_r)]
    for _ in range(L):
        pr.append(pr[-1] * ab_r - pi[-1] * ab_i)
        pi.append(pr[-2] * ab_i + pi[-1] * ab_r)
    pw_r = jnp.stack(pr)
    pw_i = jnp.stack(pi)
    m_r = pw_r[:, :, :, None] * bb_r[None] - pw_i[:, :, :, None] * bb_i[None]
    m_i = pw_r[:, :, :, None] * bb_i[None] + pw_i[:, :, :, None] * bb_r[None]
    m6 = lambda m: m[:L].reshape(L, 2, no, C, ng, P)
    c5 = lambda c: c.reshape(2, no, C, ng, P)
    k_lag = (jnp.einsum('doqjp,ndoijp->ndoijq', c5(cr), m6(m_r), precision=hi)
             - jnp.einsum('doqjp,ndoijp->ndoijq', c5(ci), m6(m_i), precision=hi)).reshape(L, 2, no, C, jc)
    klag = jnp.concatenate([k_lag[:0:-1, 1], k_lag[:1, 0] + k_lag[:1, 1], k_lag[1:, 0]])
    klag = jnp.transpose(klag, (1, 0, 2, 3)).astype(BF16)

    def h_tab(direction, powers):
        both = jnp.stack([m_r[powers, direction], m_i[powers, direction]], axis=3)
        return jnp.transpose(both, (1, 0, 2, 3, 4)).reshape(no, L * C, 2 * jp)

    hf_tab = h_tab(0, np.arange(L - 1, -1, -1))
    hb_tab = h_tab(1, np.arange(L))

    cq = lambda v: jnp.transpose(v.astype(f32).reshape(2, no, ng, C, P), (0, 1, 4, 2, 3)).reshape(2, no, P, jc)
    pq = lambda v: jnp.repeat(jnp.transpose(v.reshape(L + 1, 2, no, ng, P), (0, 1, 2, 4, 3)), C, axis=-1)
    cq_r, cq_i = cq(c_re), cq(c_im)
    pq_r, pq_i = pq(pw_r), pq(pw_i)

    def q_tab(direction, powers):
        pwr, pwi = pq_r[powers, direction], pq_i[powers, direction]
        q_re = cq_r[direction] * pwr - cq_i[direction] * pwi
        q_im = -(cq_r[direction] * pwi + cq_i[direction] * pwr)
        q = jnp.stack([q_re, q_im])
        return jnp.transpose(q, (2, 0, 3, 1, 4)).reshape(no, 2 * P, L * jc)

    qf_tab = q_tab(0, np.arange(1, L + 1))
    qb_tab = q_tab(1, np.arange(L, 0, -1))
    tabs = jnp.stack([hf_tab, hb_tab, qf_tab, qb_tab], axis=1).astype(BF16)
    a8 = jnp.stack([pw_r[L, 0], pw_i[L, 0], pw_r[L, 1], pw_i[L, 1]])
    a8 = jnp.transpose(a8, (1, 0, 2))
    dsk = jnp.tile(d_skip.astype(f32).reshape(no, 1, LANES), (1, 1, S5_CHUNK))
    return klag, tabs, a8, dsk


def kernel(x, p, norm_scale, w_in, lru_conv_w, lru_conv_b, lru_w_r, lru_b_r, lru_w_i, lru_b_i, lru_lambda, na_q_gain, na_k_gain, na_rel_bias, ssm_a_re, ssm_a_im, ssm_log_dt, ssm_b_re, ssm_b_im, ssm_c_re, ssm_c_im, ssm_d, ssm_glu_w, ssm_glu_b, pool_w, pool_scale, w_branch, w_out, ple_proj, ple_gate):
    b, s, d = x.shape
    w = BRANCH_W
    assert b == SUBLANES and d == D_MODEL and s % (LRU_STEPS * 4) == 0 and s // GRID_W >= NA_ROWS
    depth = w_in.shape[0]
    rows = s * b

    norm = norm_scale.astype(F32)[:, None, :]
    w_in16 = w_in.astype(BF16)
    ones_bd = _block_diag(jnp.ones((MXU_DIM // NA_HEAD_DIM, NA_HEAD_DIM, NA_HEAD_DIM), BF16))
    q_gain = (jnp.tile(na_q_gain.astype(F32), (1, NA_HEADS)) * (NA_HEAD_DIM ** -0.5 * LOG2E))[:, None, :]
    k_gain = jnp.tile(na_k_gain.astype(F32), (1, NA_HEADS))[:, None, :]
    conv_w = lru_conv_w.astype(F32)
    conv_b = lru_conv_b.astype(F32)[:, None, :]
    gate_w = jnp.concatenate([_block_diag(lru_w_r), _block_diag(lru_w_i)], axis=-1).astype(BF16)
    gate_b = jnp.concatenate([lru_b_r, lru_b_i], axis=-1).astype(F32)[:, :, None, :]
    c8 = (LRU_C * jax.nn.softplus(-lru_lambda.astype(F32)))[:, :, None, :]
    bias_tab = jax.vmap(_attn_bias_tables)(na_rel_bias.astype(F32) * LOG2E)
    s5_mats = jax.vmap(_s5_matrices)(ssm_a_re, ssm_a_im, ssm_log_dt, ssm_b_re, ssm_b_im, ssm_c_re, ssm_c_im, ssm_d)
    glu_w = ssm_glu_w.astype(BF16)
    glu_b = ssm_glu_b.astype(F32)[:, None, :]
    pool_bd = _block_diag(pool_w).astype(BF16)
    pool_sc = pool_scale.astype(F32)[:, None, :]
    wbr = w_branch.astype(BF16)
    wout = w_out.astype(BF16)
    pproj = ple_proj.astype(BF16)
    pgate = ple_gate.astype(BF16)

    xt = x
    nchunks = s // S5_CHUNK
    for i in range(depth):
        ax, q, k, v, cx, dx = _inproj(xt, i, norm, w_in16, ones_bd, q_gain, k_gain, b)
        hf, hb = _lru(ax, i, conv_w, conv_b, gate_w, gate_b, c8)
        yb = _attn(q, k, v, i, bias_tab, b)
        yc = _s5(cx.reshape(nchunks, S5_CHUNK * b, w), i, s5_mats).reshape(rows, w)
        yd = _pool(dx, i, pool_bd, pool_sc, s)
        xt = _merge(xt, p, i, hf, hb, yb, yc, yd, norm, w_in16, wbr, wout, pgate, pproj, glu_w, glu_b, b,
                    out_batch_major=(i == depth - 1))
    return xt
```

```python
import functools

import jax
import jax.numpy as jnp
import numpy as np
from jax import lax
from jax.experimental import pallas as pl
from jax.experimental.pallas import tpu as pltpu

F32 = jnp.float32
BF16 = jnp.bfloat16

D_MODEL = 1024
BRANCH_W = 512
N_BRANCH = 4
NORM_EPS = 1e-6
LOG2E = 1.4426950408889634
GRID_W = 64
LRU_C = 8.0
LRU_CONV_W = 4
NA_HEADS = 8
NA_HEAD_DIM = 64
NA_ROWS = 8
NA_COLS = 16
SSM_GROUP = 16
SSM_GROUPS = 32
SSM_STATE = 64
POOL_WINDOWS = (2, 4, 8, 16)
POOL_GROUP = 128

SUBLANES = 8
LANES = 128
MXU_DIM = 256
W_TILES = BRANCH_W // LANES
S5_CHUNK = 8

COL_AX, COL_AG, COL_Q, COL_K, COL_V, COL_BG, COL_CX, COL_CG, COL_DX, COL_DG = range(10)
COL_MERGE = 10 * BRANCH_W // D_MODEL

PROJ_ROWS = 1024
MERGE_ROWS = 512
LRU_STEPS = 128
POOL_STEPS = 128
ATTN_ROWS = 8
S5_BLOCK_CHUNKS = 64
ELEM_ROWS = 256

VMEM_LIMIT = 56 * 1024 * 1024


def _params(*sem):
    return pltpu.CompilerParams(dimension_semantics=sem, vmem_limit_bytes=VMEM_LIMIT)


def _dot(a, b):
    return jnp.dot(a, b, preferred_element_type=F32)


def _sigmoid(z):
    return 0.5 * jnp.tanh(0.5 * z) + 0.5


def _silu(z):
    return z * _sigmoid(z)


def _gelu_tanh(y):
    return 0.5 * y * (1.0 + jnp.tanh(0.7978845608028654 * (y + 0.044715 * (y * y * y))))


def _rms_rows(x, g):
    ms = jnp.mean(x * x, axis=-1, keepdims=True)
    return x * lax.rsqrt(ms + NORM_EPS) * g


def _head_mean_square(v, ones_ref):
    v2 = (v * v).astype(BF16)
    n = ones_ref.shape[0]
    sums = [_dot(v2[:, c:c + n], ones_ref[...]) for c in range(0, v.shape[1], n)]
    return jnp.concatenate(sums, axis=-1) * (1.0 / NA_HEAD_DIM)


def _layer_spec(shape, layer, *tail):
    tail = tail or (0,) * len(shape)
    return pl.BlockSpec((None,) + tuple(shape), lambda *_: (layer,) + tuple(tail))


S5_TILE_GROUPS = LANES // SSM_GROUP
S5_GROUP_FLAT = S5_CHUNK * SSM_GROUP


def _block_transpose(xs):
    n = len(xs)
    lane = lax.broadcasted_iota(jnp.int32, (1, LANES), 1)
    cur = list(xs)
    width, stride = LANES // 2, n // 2
    while stride >= 1:
        low = (lane & width) == 0
        nxt = list(cur)
        for i in range(n):
            if i & stride:
                continue
            a, b = cur[i], cur[i + stride]
            nxt[i] = jnp.where(low, a, pltpu.roll(b, width, 1))
            nxt[i + stride] = jnp.where(low, pltpu.roll(a, LANES - width, 1), b)
        cur = nxt
        width //= 2
        stride //= 2
    return cur


def _load_time_major(src_ref, slab_ref):
    nb, steps, n = src_ref.shape
    tiles = n // LANES
    for b in range(nb):
        for l in range(tiles):
            slab_ref[l, pl.ds(b, steps, stride=nb), :] = src_ref[b, :, l * LANES:(l + 1) * LANES]
    return jnp.concatenate([slab_ref[l] for l in range(tiles)], axis=-1)


def _store_batch_major(val, dst_ref, slab_ref):
    nb, steps, n = dst_ref.shape
    tiles = n // LANES
    for l in range(tiles):
        slab_ref[l] = val[:, l * LANES:(l + 1) * LANES]
    for b in range(nb):
        for l in range(tiles):
            dst_ref[b, :, l * LANES:(l + 1) * LANES] = slab_ref[l, pl.ds(b, steps, stride=nb), :]


def _inproj_body(x_ref, ns_ref, wax_ref, wq_ref, wk_ref, wv_ref, wcx_ref, wdx_ref, ones_ref, qg_ref, kg_ref,
                 ax_ref, q_ref, k_ref, v_ref, cx_ref, dx_ref, stage_ref, *slabs):
    x = _load_time_major(x_ref, slabs[0]) if slabs else x_ref[...]
    hn = _rms_rows(x, ns_ref[...]).astype(BF16)
    steps = stage_ref.shape[1] // SUBLANES

    def to_batch_lanes(val, out_ref):
        for l in range(W_TILES):
            stage_ref[l] = val[:, l * LANES:(l + 1) * LANES]
        for b in range(SUBLANES):
            for l in range(W_TILES):
                c0 = b * BRANCH_W + l * LANES
                out_ref[:, c0:c0 + LANES] = stage_ref[l, pl.ds(b, steps, stride=SUBLANES), :].astype(BF16)

    ax_ref[...] = _dot(hn, wax_ref[...]).astype(BF16)
    q = _dot(hn, wq_ref[...])
    to_batch_lanes(q * lax.rsqrt(_head_mean_square(q, ones_ref) + NORM_EPS) * qg_ref[...], q_ref)
    k = _dot(hn, wk_ref[...])
    to_batch_lanes(k * lax.rsqrt(_head_mean_square(k, ones_ref) + NORM_EPS) * kg_ref[...], k_ref)
    to_batch_lanes(_dot(hn, wv_ref[...]), v_ref)
    cxv = _dot(hn, wcx_ref[...])
    nchunk = cxv.shape[0] // (S5_CHUNK * SUBLANES)
    for o in range(W_TILES):
        tile3 = cxv[:, o * LANES:(o + 1) * LANES].reshape(nchunk, S5_CHUNK * SUBLANES, LANES)
        per_t = [tile3[:, t * SUBLANES:(t + 1) * SUBLANES, :].reshape(nchunk * SUBLANES, LANES)
                 for t in range(S5_CHUNK)]
        for g, ug in enumerate(_block_transpose(per_t)):
            c0 = (o * S5_TILE_GROUPS + g) * S5_GROUP_FLAT
            cx_ref[:, c0:c0 + S5_GROUP_FLAT] = ug.astype(BF16)
    dx_ref[...] = _dot(hn, wdx_ref[...]).astype(BF16)


def _inproj(x, layer, norm_scale, w_in, ones_bd, q_gain, k_gain, batch):
    w = BRANCH_W
    tm = PROJ_ROWS
    batch_major = x.ndim == 3
    d = x.shape[-1]
    rows = x.shape[0] * x.shape[1] if batch_major else x.shape[0]
    s = rows // batch
    if batch_major:
        x_spec = pl.BlockSpec((batch, tm // batch, d), lambda i: (0, i, 0))
        slabs = [pltpu.VMEM((d // LANES, tm, LANES), F32)]
    else:
        x_spec = pl.BlockSpec((tm, d), lambda i: (i, 0))
        slabs = []
    wcol = lambda j: pl.BlockSpec((None, d, w), lambda i: (layer, 0, j))
    tm_spec = pl.BlockSpec((tm, w), lambda i: (i, 0))
    bl_spec = pl.BlockSpec((tm // batch, batch * w), lambda i: (i, 0))
    tm_shape = jax.ShapeDtypeStruct((rows, w), BF16)
    bl_shape = jax.ShapeDtypeStruct((s, batch * w), BF16)
    gf_spec = pl.BlockSpec((tm // S5_CHUNK, S5_CHUNK * w), lambda i: (i, 0))
    gf_shape = jax.ShapeDtypeStruct((rows // S5_CHUNK, S5_CHUNK * w), BF16)
    return pl.pallas_call(
        _inproj_body,
        grid=(rows // tm,),
        in_specs=[
            x_spec,
            _layer_spec((1, d), layer),
            wcol(COL_AX), wcol(COL_Q), wcol(COL_K), wcol(COL_V), wcol(COL_CX), wcol(COL_DX),
            pl.BlockSpec(ones_bd.shape, lambda i: (0, 0)),
            _layer_spec((1, w), layer), _layer_spec((1, w), layer),
        ],
        out_specs=[tm_spec, bl_spec, bl_spec, bl_spec, gf_spec, tm_spec],
        out_shape=[tm_shape, bl_shape, bl_shape, bl_shape, gf_shape, tm_shape],
        scratch_shapes=[pltpu.VMEM((W_TILES, tm, LANES), F32)] + slabs,
        compiler_params=_params("parallel"),
        name="inproj",
    )(x, norm_scale, w_in, w_in, w_in, w_in, w_in, w_in, ones_bd, q_gain, k_gain)


def _fill_extended(ext_ref, main_ref, prev_ref, next_ref, n_prev, n_next, is_first, is_last):
    r = main_ref.shape[0]
    prev = prev_ref[...].astype(F32)
    ext_ref[0:n_prev, :] = jnp.where(is_first, 0.0, prev[prev.shape[0] - n_prev:, :])
    ext_ref[n_prev:n_prev + r, :] = main_ref[...].astype(F32)
    nxt = next_ref[...].astype(F32)
    ext_ref[n_prev + r:n_prev + r + n_next, :] = jnp.where(is_last, 0.0, nxt[0:n_next, :])


def _lru_body(mf_ref, pf_ref, nf_ref, mb_ref, pb_ref, nb_ref, cw_ref, cb_ref, wg_ref, bg_ref, ch_ref,
              hf_ref, hb_ref, ext_ref, af_ref, bf_ref, ab_ref, bb_ref, carry_ref):
    i = pl.program_id(0)
    nt = pl.num_programs(0)
    w = BRANCH_W
    r = mf_ref.shape[0]
    steps = r // SUBLANES
    n_prev, n_next = 2 * SUBLANES, SUBLANES

    @pl.when(i == 0)
    def _():
        carry_ref[...] = jnp.zeros_like(carry_ref)

    def prepare(direction, main_ref, prev_ref, next_ref, chunk, a_ref, b_ref):
        _fill_extended(ext_ref, main_ref, prev_ref, next_ref, n_prev, n_next, chunk == 0, chunk == nt - 1)

        def sub(sb, _):
            r0 = pl.multiple_of(sb * ELEM_ROWS, ELEM_ROWS)
            xc = cb_ref[...]
            for tap in range(LRU_CONV_W):
                xc = xc + cw_ref[tap:tap + 1, :] * ext_ref[pl.ds(r0 + tap * SUBLANES, ELEM_ROWS), :]
            g = _dot(xc.astype(BF16), wg_ref[direction]) + bg_ref[direction]
            tr = jnp.tanh(g[:, 0:w]) + 1.0
            ti = jnp.tanh(g[:, w:2 * w]) + 1.0
            a = jnp.exp2(ch_ref[direction] * tr)
            a_ref[pl.ds(r0, ELEM_ROWS), :] = a
            om = 1.0 - a * a
            b_ref[pl.ds(r0, ELEM_ROWS), :] = om * lax.rsqrt(jnp.maximum(om, 1e-37)) * (ti * xc)
            return 0

        lax.fori_loop(0, r // ELEM_ROWS, sub, 0)

    prepare(0, mf_ref, pf_ref, nf_ref, i, af_ref, bf_ref)
    prepare(1, mb_ref, pb_ref, nb_ref, nt - 1 - i, ab_ref, bb_ref)

    def step(t, carry):
        hf, hb = carry
        rf = pl.ds(pl.multiple_of(t * SUBLANES, SUBLANES), SUBLANES)
        hf = af_ref[rf, :] * hf + bf_ref[rf, :]
        bf_ref[rf, :] = hf
        rb = pl.ds(pl.multiple_of((steps - 1 - t) * SUBLANES, SUBLANES), SUBLANES)
        hb = ab_ref[rb, :] * hb + bb_ref[rb, :]
        bb_ref[rb, :] = hb
        return hf, hb

    hf, hb = lax.fori_loop(0, steps, step, (carry_ref[0], carry_ref[1]), unroll=8)
    carry_ref[0] = hf
    carry_ref[1] = hb
    hf_ref[...] = bf_ref[...].astype(BF16)
    hb_ref[...] = bb_ref[...].astype(BF16)


def _lru(ax, layer, conv_w, conv_b, w_gate, b_gate, c8):
    rows, w = ax.shape
    r = LRU_STEPS * SUBLANES
    nt = rows // r
    hb = 2 * SUBLANES
    per = r // hb
    last_hb = rows // hb - 1

    def main(ci):
        return pl.BlockSpec((r, w), lambda i: (ci(i), 0))

    def prev(ci):
        return pl.BlockSpec((hb, w), lambda i: (jnp.maximum(ci(i) * per - 1, 0), 0))

    def nxt(ci):
        return pl.BlockSpec((hb, w), lambda i: (jnp.minimum((ci(i) + 1) * per, last_hb), 0))

    fwd = lambda i: i
    bwd = lambda i: nt - 1 - i
    out_shape = jax.ShapeDtypeStruct((rows, w), BF16)
    return pl.pallas_call(
        _lru_body,
        grid=(nt,),
        in_specs=[
            main(fwd), prev(fwd), nxt(fwd), main(bwd), prev(bwd), nxt(bwd),
            _layer_spec((LRU_CONV_W, w), layer),
            _layer_spec((1, w), layer),
            _layer_spec((2, w, 2 * w), layer),
            _layer_spec((2, 1, 2 * w), layer),
            _layer_spec((2, 1, w), layer),
        ],
        out_specs=[main(fwd), main(bwd)],
        out_shape=[out_shape, out_shape],
        scratch_shapes=[
            pltpu.VMEM((r + 3 * SUBLANES, w), F32),
            pltpu.VMEM((r, w), F32), pltpu.VMEM((r, w), F32),
            pltpu.VMEM((r, w), F32), pltpu.VMEM((r, w), F32),
            pltpu.VMEM((2, SUBLANES, w), F32),
        ],
        compiler_params=_params("arbitrary"),
        name="rglru",
    )(ax, ax, ax, ax, ax, ax, conv_w, conv_b, w_gate, b_gate, c8)


def _pool_body(m_ref, p_ref, n_ref, wp_ref, sc_ref, o_ref, ext_ref, pooled_ref, *, seq_len):
    i = pl.program_id(0)
    nt = pl.num_programs(0)
    r = m_ref.shape[0]
    steps = r // SUBLANES
    halo = max(POOL_WINDOWS) // 2
    hr = halo * SUBLANES
    _fill_extended(ext_ref, m_ref, p_ref, n_ref, hr, hr, i == 0, i == nt - 1)

    t_glob = i * steps + lax.broadcasted_iota(jnp.int32, (r, POOL_GROUP), 0) // SUBLANES
    for g, win in enumerate(POOL_WINDOWS):
        ls = slice(g * POOL_GROUP, (g + 1) * POOL_GROUP)
        e = ext_ref[:, ls]
        lo_t = -halo
        cur = e
        half = 1
        while half < win:
            n = cur.shape[0] - half * SUBLANES
            if half == 1:
                cur = cur[0:n, :] + cur[SUBLANES:SUBLANES + n, :]
                lo_t += 1
            else:
                sh = (half // 2) * SUBLANES
                cur = cur[0:n, :] + cur[2 * sh:2 * sh + n, :]
                lo_t += half // 2
            half *= 2
        off = (0 - lo_t) * SUBLANES
        wsum = cur[off:off + r, :]
        half_w = win // 2
        cnt = (jnp.clip(t_glob + half_w, 0, seq_len) - jnp.clip(t_glob - half_w, 0, seq_len)).astype(F32)
        pooled_ref[:, ls] = wsum / cnt - e[hr:hr + r, :]
    o_ref[...] = (_dot(pooled_ref[...].astype(BF16), wp_ref[...]) * sc_ref[...]).astype(BF16)


def _pool(dx, layer, w_pool_bd, scale, seq_len):
    rows, w = dx.shape
    r = POOL_STEPS * SUBLANES
    nt = rows // r
    hb = (max(POOL_WINDOWS) // 2) * SUBLANES
    per = r // hb
    last_hb = rows // hb - 1
    return pl.pallas_call(
        functools.partial(_pool_body, seq_len=seq_len),
        grid=(nt,),
        in_specs=[
            pl.BlockSpec((r, w), lambda i: (i, 0)),
            pl.BlockSpec((hb, w), lambda i: (jnp.maximum(i * per - 1, 0), 0)),
            pl.BlockSpec((hb, w), lambda i: (jnp.minimum((i + 1) * per, last_hb), 0)),
            _layer_spec((w, w), layer),
            _layer_spec((1, w), layer),
        ],
        out_specs=pl.BlockSpec((r, w), lambda i: (i, 0)),
        out_shape=jax.ShapeDtypeStruct((rows, w), BF16),
        scratch_shapes=[pltpu.VMEM((r + 2 * hb, w), F32), pltpu.VMEM((r, w), F32)],
        compiler_params=_params("parallel"),
        name="pool",
    )(dx, dx, dx, w_pool_bd, scale)


def _attn_body(q_ref, k_ref, v_ref, bias_ref, o_ref, *, grid_rows):
    g = pl.program_id(1)
    lane = lax.broadcasted_iota(jnp.int32, (GRID_W, LANES), 1)
    lo_half = lane < NA_HEAD_DIM
    nkeys = NA_ROWS * GRID_W

    def row_body(rr, _):
        r = g * ATTN_ROWS + rr
        rs = jnp.clip(r - NA_ROWS // 2, 0, grid_rows - NA_ROWS)
        d0 = rs - r + (NA_ROWS - 1)
        k0 = pl.multiple_of(rs * GRID_W, GRID_W)
        q0 = pl.multiple_of(rr * GRID_W, GRID_W)
        scores = []
        for hp in range(NA_HEADS // 2):
            ls = slice(hp * LANES, (hp + 1) * LANES)
            kp = k_ref[pl.ds(k0, nkeys), ls]
            qp = q_ref[pl.ds(q0, GRID_W), ls]
            zero = jnp.zeros_like(qp)
            qm = jnp.concatenate([jnp.where(lo_half, qp, zero), jnp.where(lo_half, zero, qp)], axis=0)
            sc = lax.dot_general(qm, kp, (((1,), (1,)), ((), ())), preferred_element_type=F32)
            bias = jnp.concatenate([bias_ref[hp, d0 + 2 * j] for j in range(NA_ROWS // 2)], axis=-1)
            scores.append(sc + bias)
        maxes = [jnp.max(sc, axis=-1, keepdims=True) for sc in scores]
        probs = [jnp.exp2(sc - m) for sc, m in zip(scores, maxes)]
        inv = [1.0 / jnp.sum(p, axis=-1, keepdims=True) for p in probs]
        for hp in range(NA_HEADS // 2):
            ls = slice(hp * LANES, (hp + 1) * LANES)
            vp = v_ref[pl.ds(k0, nkeys), ls]
            o = _dot(probs[hp].astype(BF16), vp) * inv[hp]
            o_ref[pl.ds(q0, GRID_W), ls] = jnp.where(lo_half, o[0:GRID_W], o[GRID_W:2 * GRID_W]).astype(BF16)
        return 0

    lax.fori_loop(0, ATTN_ROWS, row_body, 0, unroll=2)


def _attn(q, k, v, layer, bias_tab, batch):
    s, bw = q.shape
    w = bw // batch
    grid_rows = s // GRID_W
    ng = grid_rows // ATTN_ROWS
    qr = ATTN_ROWS * GRID_W
    q_spec = pl.BlockSpec((qr, w), lambda b, g: (g, b))
    kv_spec = pl.BlockSpec((s, w), lambda b, g: (0, b))
    return pl.pallas_call(
        functools.partial(_attn_body, grid_rows=grid_rows),
        grid=(batch, ng),
        in_specs=[q_spec, kv_spec, kv_spec, _layer_spec(bias_tab.shape[1:], layer)],
        out_specs=q_spec,
        out_shape=jax.ShapeDtypeStruct((s, bw), BF16),
        compiler_params=_params("parallel", "parallel"),
        name="natten",
    )(q, k, v, bias_tab)


def _s5_body(u_ref, g_ref, hf_ref, hb_ref, q_ref, a_ref, dsk_ref,
             y_ref, sinb_ref, sloc_ref, sin_ref, carry_ref):
    p = pl.program_id(1)
    j = pl.program_id(2)
    nblk = pl.num_programs(2)
    rb = u_ref.shape[0]
    nk = rb // SUBLANES
    npairs = S5_TILE_GROUPS // 2
    pf = 2 * S5_GROUP_FLAT
    re_t = lambda q: slice(2 * q * LANES, (2 * q + 1) * LANES)
    im_t = lambda q: slice((2 * q + 1) * LANES, (2 * q + 2) * LANES)
    pair_t = lambda q: slice(q * pf, (q + 1) * pf)
    coef_t = lambda q: slice(q * LANES, (q + 1) * LANES)

    @pl.when(j == 0)
    def _():
        carry_ref[...] = jnp.zeros_like(carry_ref)

    def sweep(direction, reverse):
        a_re = a_ref[2 * direction:2 * direction + 1, :]
        a_im = a_ref[2 * direction + 1:2 * direction + 2, :]

        def body(n, state):
            kk = (nk - 1 - n) if reverse else n
            rows = pl.ds(pl.multiple_of(kk * SUBLANES, SUBLANES), SUBLANES)
            new = []
            for q in range(npairs):
                sr, si = state[2 * q], state[2 * q + 1]
                sin_ref[rows, re_t(q)] = sr
                sin_ref[rows, im_t(q)] = si
                ar, ai = a_re[:, coef_t(q)], a_im[:, coef_t(q)]
                new.append(ar * sr - ai * si + sloc_ref[rows, re_t(q)])
                new.append(ar * si + ai * sr + sloc_ref[rows, im_t(q)])
            return tuple(new)

        init = tuple(carry_ref[:, t * LANES:(t + 1) * LANES] for t in range(2 * npairs))
        out = lax.fori_loop(0, nk, body, init, unroll=4)
        for t in range(2 * npairs):
            carry_ref[:, t * LANES:(t + 1) * LANES] = out[t]

    @pl.when(p == 0)
    def _():
        for q in range(npairs):
            sloc_ref[:, pair_t(q)] = _dot(u_ref[:, pair_t(q)], hb_ref[q])
        sweep(1, True)
        blk = nblk - 1 - j
        sinb_ref[pl.ds(pl.multiple_of(blk * rb, rb), rb), :] = sin_ref[...].astype(BF16)

    @pl.when(p == 1)
    def _():
        for q in range(npairs):
            sloc_ref[:, pair_t(q)] = _dot(u_ref[:, pair_t(q)], hf_ref[q])
        sweep(0, False)
        rows_b = pl.ds(pl.multiple_of(j * rb, rb), rb)
        for q in range(npairs):
            u = u_ref[:, pair_t(q)]
            states = jnp.concatenate([sin_ref[:, pair_t(q)].astype(BF16), sinb_ref[rows_b, pair_t(q)]], axis=-1)
            y = _dot(u, g_ref[q]) + _dot(states, q_ref[q]) + dsk_ref[q] * u.astype(F32)
            y_ref[:, pair_t(q)] = y.astype(BF16)


def _s5(cx, layer, mats):
    g_m, hf_m, hb_m, q_m, a8, dsk = mats
    rows, width = cx.shape
    rb = S5_BLOCK_CHUNKS * SUBLANES
    nblk = rows // rb
    tile_w = S5_TILE_GROUPS * S5_GROUP_FLAT
    npairs, pf = S5_TILE_GROUPS // 2, 2 * S5_GROUP_FLAT
    state_w = S5_TILE_GROUPS * 2 * SSM_STATE
    per_tile = lambda *shape: pl.BlockSpec((None, npairs) + shape, lambda o, p, j: (layer, o) + (0,) * len(shape))
    return pl.pallas_call(
        _s5_body,
        grid=(width // tile_w, 2, nblk),
        in_specs=[
            pl.BlockSpec((rb, tile_w), lambda o, p, j: (j + (1 - p) * (nblk - 1 - 2 * j), o)),
            per_tile(pf, pf), per_tile(pf, 2 * LANES), per_tile(pf, 2 * LANES), per_tile(4 * LANES, pf),
            pl.BlockSpec((None, None, 4, npairs * LANES), lambda o, p, j: (layer, o, 0, 0)),
            per_tile(1, pf),
        ],
        out_specs=pl.BlockSpec((rb, tile_w), lambda o, p, j: (p * j, o)),
        out_shape=jax.ShapeDtypeStruct((rows, width), BF16),
        scratch_shapes=[
            pltpu.VMEM((rows, state_w), BF16),
            pltpu.VMEM((rb, state_w), F32),
            pltpu.VMEM((rb, state_w), F32),
            pltpu.VMEM((SUBLANES, state_w), F32),
        ],
        compiler_params=_params("arbitrary", "arbitrary", "arbitrary"),
        name="s5",
    )(cx, g_m, hf_m, hb_m, q_m, a8, dsk)


def _merge_body(x_ref, p_ref, hf_ref, hb_ref, yb_ref, yc_ref, yd_ref, ns_ref,
                wag_ref, wbg_ref, wcg_ref, wdg_ref, wm0_ref, wm1_ref, wm2_ref, wm3_ref,
                wb0_ref, wb1_ref, wb2_ref, wb3_ref, wout_ref, pg_ref, pp_ref, gluw_ref, glub_ref,
                o_ref, stage_ref, pslab_ref, xslab_ref, *, x_batch_major, out_batch_major):
    x = _load_time_major(x_ref, xslab_ref) if x_batch_major else x_ref[...]
    hn = _rms_rows(x, ns_ref[...]).astype(BF16)

    def branch(y, wb_ref, wm_ref):
        return _dot(y.astype(BF16), wb_ref[...]) * _sigmoid(_dot(hn, wm_ref[...]))

    ya = (hf_ref[...].astype(F32) + hb_ref[...].astype(F32)) * _silu(_dot(hn, wag_ref[...]))
    merged = branch(ya, wb0_ref, wm0_ref)
    steps = stage_ref.shape[1] // SUBLANES
    for b in range(SUBLANES):
        for l in range(W_TILES):
            c0 = b * BRANCH_W + l * LANES
            stage_ref[l, pl.ds(b, steps, stride=SUBLANES), :] = yb_ref[:, c0:c0 + LANES].astype(F32)
    yb = jnp.concatenate([stage_ref[l] for l in range(W_TILES)], axis=-1)
    merged = merged + branch(yb * _silu(_dot(hn, wbg_ref[...])), wb1_ref, wm1_ref)
    nchunk = yc_ref.shape[0] // SUBLANES
    tiles = []
    for o in range(W_TILES):
        per_g = [yc_ref[:, (o * S5_TILE_GROUPS + g) * S5_GROUP_FLAT:(o * S5_TILE_GROUPS + g + 1) * S5_GROUP_FLAT]
                 .astype(F32) for g in range(S5_TILE_GROUPS)]
        per_t = [v.reshape(nchunk, 1, SUBLANES, LANES) for v in _block_transpose(per_g)]
        tiles.append(jnp.concatenate(per_t, axis=1).reshape(nchunk * S5_CHUNK * SUBLANES, LANES))
    yg = _gelu_tanh(jnp.concatenate(tiles, axis=-1))
    yc = yg * _sigmoid(_dot(yg.astype(BF16), gluw_ref[...]) + glub_ref[...])
    merged = merged + branch(yc * _silu(_dot(hn, wcg_ref[...])), wb2_ref, wm2_ref)
    merged = merged + branch(yd_ref[...].astype(F32) * _silu(_dot(hn, wdg_ref[...])), wb3_ref, wm3_ref)
    x1 = x + _dot(merged.astype(BF16), wout_ref[...])
    emb = _dot(_load_time_major(p_ref, pslab_ref).astype(BF16), pp_ref[...])
    out = x1 + _sigmoid(_dot(x1.astype(BF16), pg_ref[...])) * emb
    if out_batch_major:
        _store_batch_major(out, o_ref, xslab_ref)
    else:
        o_ref[...] = out


def _merge(x, p, layer, hf, hb, yb, yc, yd, norm_scale, w_in, wbr, wout, pgate, pproj, glu_w, glu_b, batch,
           out_batch_major):
    w = BRANCH_W
    tm = MERGE_ROWS
    pdim = p.shape[-1]
    once = pl.Buffered(1)
    x_batch_major = x.ndim == 3
    d = x.shape[-1]
    rows = x.shape[0] * x.shape[1] if x_batch_major else x.shape[0]
    steps = tm // batch
    bm_spec = pl.BlockSpec((batch, steps, d), lambda i: (0, i, 0))
    x_spec = bm_spec if x_batch_major else pl.BlockSpec((tm, d), lambda i: (i, 0))
    if out_batch_major:
        out_spec, out_shape = bm_spec, jax.ShapeDtypeStruct((batch, rows // batch, d), F32)
    else:
        out_spec, out_shape = pl.BlockSpec((tm, d), lambda i: (i, 0)), jax.ShapeDtypeStruct((rows, d), F32)

    def resident(shape, *tail):
        tail = tail or (0,) * len(shape)
        return pl.BlockSpec((None,) + tuple(shape), lambda i: (layer,) + tuple(tail), pipeline_mode=once)

    row_spec = lambda n: pl.BlockSpec((tm, n), lambda i: (i, 0))
    gate_cols = [resident((d, w), 0, j) for j in (COL_AG, COL_BG, COL_CG, COL_DG)]
    merge_cols = [resident((d, d), 0, COL_MERGE + n) for n in range(N_BRANCH)]
    return pl.pallas_call(
        functools.partial(_merge_body, x_batch_major=x_batch_major, out_batch_major=out_batch_major),
        grid=(rows // tm,),
        in_specs=[
            x_spec,
            pl.BlockSpec((None, batch, steps, pdim), lambda i: (layer, 0, i, 0)),
            row_spec(w), row_spec(w),
            pl.BlockSpec((tm // batch, batch * w), lambda i: (i, 0)),
            pl.BlockSpec((tm // S5_CHUNK, S5_CHUNK * w), lambda i: (i, 0)), row_spec(w),
            resident((1, d)),
            *gate_cols, *merge_cols,
            *[pl.BlockSpec((None, None, w, d), lambda i, n=n: (layer, n, 0, 0), pipeline_mode=once)
              for n in range(N_BRANCH)],
            resident((d, d)), resident((d, d)), resident((pdim, d)),
            resident((w, w)), resident((1, w)),
        ],
        out_specs=out_spec,
        out_shape=out_shape,
        scratch_shapes=[pltpu.VMEM((W_TILES, tm, LANES), F32),
                        pltpu.VMEM((pdim // LANES, tm, LANES), F32),
                        pltpu.VMEM((d // LANES, tm, LANES), F32)],
        compiler_params=_params("parallel"),
        name="merge",
    )(x, p, hf, hb, yb, yc, yd, norm_scale, *([w_in] * 8), *([wbr] * N_BRANCH), wout, pgate, pproj, glu_w, glu_b)


def _block_diag(blocks):
    n, r, c = blocks.shape[-3:]
    eye = jnp.eye(n, dtype=blocks.dtype)
    out = jnp.einsum('...nrc,nm->...nrmc', blocks, eye)
    return out.reshape(blocks.shape[:-3] + (n * r, n * c))


def _attn_bias_tables(rpb):
    qc = np.arange(GRID_W)[:, None]
    kc = np.arange(GRID_W)[None, :]
    ws = np.clip(qc - NA_COLS // 2, 0, GRID_W - NA_COLS)
    in_win = (kc >= ws) & (kc < ws + NA_COLS)
    dc = np.clip(kc - qc, -(NA_COLS - 1), NA_COLS - 1) + NA_COLS - 1
    onehot = ((dc[None] == np.arange(2 * NA_COLS - 1)[:, None, None]) & in_win[None]).astype(np.float32)
    tab = jnp.einsum('hrm,mqk->hrqk', rpb.astype(F32), jnp.asarray(onehot), precision=lax.Precision.HIGHEST)
    tab = tab + jnp.asarray(np.where(in_win, 0.0, -1e30).astype(np.float32))
    pairs = jnp.concatenate([tab[:, :-1], tab[:, 1:]], axis=-1)
    nr = pairs.shape[1]
    pairs = pairs.reshape(NA_HEADS // 2, 2, nr, GRID_W, 2 * GRID_W)
    return jnp.transpose(pairs, (0, 2, 1, 3, 4)).reshape(NA_HEADS // 2, nr, 2 * GRID_W, 2 * GRID_W)


def _s5_matrices(a_re, a_im, log_dt, b_re, b_im, c_re, c_im, d_skip):
    L, G, P, C = S5_CHUNK, SSM_GROUPS, SSM_STATE, SSM_GROUP
    f32 = F32
    hi = lax.Precision.HIGHEST
    lr = jnp.minimum(a_re.astype(f32), -1e-4)
    li = a_im.astype(f32)
    dt = jnp.exp(log_dt.astype(f32))[..., None]
    mag = jnp.exp(lr * dt)
    ab_r = mag * jnp.cos(li * dt)
    ab_i = mag * jnp.sin(li * dt)
    nr = ab_r - 1.0
    den = lr * lr + li * li
    fr = ((nr * lr + ab_i * li) / den)[..., None]
    fi = ((ab_i * lr - nr * li) / den)[..., None]
    br, bi = b_re.astype(f32), b_im.astype(f32)
    bb_r = fr * br - fi * bi
    bb_i = fr * bi + fi * br
    cr, ci = c_re.astype(f32), c_im.astype(f32)
    pr, pi = [jnp.ones_like(ab_r)], [jnp.zeros_like(ab_r)]
    for _ in range(L):
        pr.append(pr[-1] * ab_r - pi[-1] * ab_i)
        pi.append(pr[-2] * ab_i + pi[-1] * ab_r)
    pw_r = jnp.stack(pr)
    pw_i = jnp.stack(pi)
    m_r = pw_r[..., None] * bb_r[None] - pw_i[..., None] * bb_i[None]
    m_i = pw_r[..., None] * bb_i[None] + pw_i[..., None] * bb_r[None]
    k_lag = (jnp.einsum('dgop,ndgpi->ndgio', cr, m_r[:L], precision=hi)
             - jnp.einsum('dgop,ndgpi->ndgio', ci, m_i[:L], precision=hi))
    lag = np.arange(L)[None, :] - np.arange(L)[:, None]
    pick_f = (lag[None] == np.arange(L)[:, None, None]).astype(np.float32)
    pick_b = (-lag[None] == np.arange(L)[:, None, None]).astype(np.float32)
    g_mat = (jnp.einsum('ngio,nst->gsito', k_lag[:, 0], jnp.asarray(pick_f), precision=hi)
             + jnp.einsum('ngio,nst->gsito', k_lag[:, 1], jnp.asarray(pick_b), precision=hi)).reshape(G, L * C, L * C)

    npair = G // 2
    eye2 = jnp.eye(2, dtype=f32)
    g_pair = _block_diag(g_mat.reshape(npair, 2, L * C, L * C))

    def h_pair(direction, powers):
        both = jnp.stack([m_r[powers, direction], m_i[powers, direction]])
        h = jnp.transpose(both, (2, 1, 4, 0, 3)).reshape(npair, 2, L * C, 2, P)
        return jnp.einsum('qerip,ek->qerikp', h, eye2).reshape(npair, 2 * L * C, 4 * P)

    hf_pair = h_pair(0, np.arange(L - 1, -1, -1))
    hb_pair = h_pair(1, np.arange(L))

    def q_rows(direction, powers):
        pwr = pw_r[powers, direction]
        pwi = pw_i[powers, direction]
        c_r, c_i = cr[direction], ci[direction]
        q_re = jnp.einsum('gcp,tgp->gptc', c_r, pwr) - jnp.einsum('gcp,tgp->gptc', c_i, pwi)
        q_im = -(jnp.einsum('gcp,tgp->gptc', c_r, pwi) + jnp.einsum('gcp,tgp->gptc', c_i, pwr))
        return jnp.stack([q_re, q_im], axis=1).reshape(G, 2, P, L * C)

    q_all = jnp.stack([q_rows(0, np.arange(1, L + 1)), q_rows(1, np.arange(L, 0, -1))], axis=1)
    q_all = q_all.reshape(npair, 2, 2, 2, P, L * C)
    q_pair = jnp.einsum('qedipn,ek->qdiepkn', q_all, eye2).reshape(npair, 8 * P, 2 * L * C)
    a8 = jnp.stack([pw_r[L, 0], pw_i[L, 0], pw_r[L, 1], pw_i[L, 1]])
    a8 = jnp.transpose(a8.reshape(4, W_TILES, (S5_TILE_GROUPS // 2) * 2 * P), (1, 0, 2))
    dsk = jnp.tile(d_skip.astype(f32).reshape(G, 1, C), (1, 1, L)).reshape(npair, 1, 2 * L * C)
    bf = lambda m: m.astype(BF16)
    return bf(g_pair), bf(hf_pair), bf(hb_pair), bf(q_pair), a8, dsk


def kernel(x, p, norm_scale, w_in, lru_conv_w, lru_conv_b, lru_w_r, lru_b_r, lru_w_i, lru_b_i, lru_lambda, na_q_gain, na_k_gain, na_rel_bias, ssm_a_re, ssm_a_im, ssm_log_dt, ssm_b_re, ssm_b_im, ssm_c_re, ssm_c_im, ssm_d, ssm_glu_w, ssm_glu_b, pool_w, pool_scale, w_branch, w_out, ple_proj, ple_gate):
    b, s, d = x.shape
    w = BRANCH_W
    assert b == SUBLANES and d == D_MODEL and s % (LRU_STEPS * 4) == 0 and s // GRID_W >= NA_ROWS
    depth = w_in.shape[0]
    rows = s * b

    norm = norm_scale.astype(F32)[:, None, :]
    w_in16 = w_in.astype(BF16)
    ones_bd = _block_diag(jnp.ones((MXU_DIM // NA_HEAD_DIM, NA_HEAD_DIM, NA_HEAD_DIM), BF16))
    q_gain = (jnp.tile(na_q_gain.astype(F32), (1, NA_HEADS)) * (NA_HEAD_DIM ** -0.5 * LOG2E))[:, None, :]
    k_gain = jnp.tile(na_k_gain.astype(F32), (1, NA_HEADS))[:, None, :]
    conv_w = lru_conv_w.astype(F32)
    conv_b = lru_conv_b.astype(F32)[:, None, :]
    gate_w = (0.5 * jnp.concatenate([_block_diag(lru_w_r), _block_diag(lru_w_i)], axis=-1)).astype(BF16)
    gate_b = (0.5 * jnp.concatenate([lru_b_r, lru_b_i], axis=-1).astype(F32))[:, :, None, :]
    c8 = (-0.5 * LRU_C * LOG2E * jax.nn.softplus(-lru_lambda.astype(F32)))[:, :, None, :]
    bias_tab = jax.vmap(_attn_bias_tables)(na_rel_bias.astype(F32) * LOG2E)
    s5_mats = jax.vmap(_s5_matrices)(ssm_a_re, ssm_a_im, ssm_log_dt, ssm_b_re, ssm_b_im, ssm_c_re, ssm_c_im, ssm_d)
    glu_w = ssm_glu_w.astype(BF16)
    glu_b = ssm_glu_b.astype(F32)[:, None, :]
    pool_bd = _block_diag(pool_w).astype(BF16)
    pool_sc = pool_scale.astype(F32)[:, None, :]
    branch_scale = jnp.asarray([0.5] + [1.0] * (N_BRANCH - 1), F32)[None, :, None, None]
    wbr = (w_branch.astype(F32) * branch_scale).astype(BF16)
    wout = w_out.astype(BF16)
    pproj = ple_proj.astype(BF16)
    pgate = ple_gate.astype(BF16)

    xt = x
    for i in range(depth):
        ax, q, k, v, cx, dx = _inproj(xt, i, norm, w_in16, ones_bd, q_gain, k_gain, b)
        hf, hb = _lru(ax, i, conv_w, conv_b, gate_w, gate_b, c8)
        yb = _attn(q, k, v, i, bias_tab, b)
        yc = _s5(cx, i, s5_mats)
        yd = _pool(dx, i, pool_bd, pool_sc, s)
        xt = _merge(xt, p, i, hf, hb, yb, yc, yd, norm, w_in16, wbr, wout, pgate, pproj, glu_w, glu_b, b,
                    out_batch_major=(i == depth - 1))
    return xt
```

```python
import functools

import jax
import jax.numpy as jnp
import numpy as np
from jax import lax
from jax.experimental import pallas as pl
from jax.experimental.pallas import tpu as pltpu

F32 = jnp.float32
BF16 = jnp.bfloat16

D_MODEL = 1024
BRANCH_W = 512
N_BRANCH = 4
NORM_EPS = 1e-6
LOG2E = 1.4426950408889634
GRID_W = 64
LRU_C = 8.0
LRU_CONV_W = 4
LRU_HALO_STEPS = 2
NA_HEADS = 8
NA_HEAD_DIM = 64
NA_ROWS = 8
NA_COLS = 16
SSM_GROUP = 16
SSM_GROUPS = 32
SSM_STATE = 64
POOL_WINDOWS = (2, 4, 8, 16)
POOL_GROUP = 128

SUBLANES = 8
LANES = 128
MXU_DIM = 256
W_TILES = BRANCH_W // LANES
S5_CHUNK = 8

COL_AX, COL_AG, COL_Q, COL_K, COL_V, COL_BG, COL_CX, COL_CG, COL_DX, COL_DG = range(10)
COL_MERGE = 10 * BRANCH_W // D_MODEL

PROJ_ROWS = 1024
MERGE_ROWS = 512
LRU_STEPS = 128
POOL_STEPS = 128
ATTN_ROWS = 8
S5_BLOCK_CHUNKS = 64
ELEM_ROWS = 256

VMEM_LIMIT = 56 * 1024 * 1024


def _params(*sem):
    return pltpu.CompilerParams(dimension_semantics=sem, vmem_limit_bytes=VMEM_LIMIT)


def _dot(a, b):
    return jnp.dot(a, b, preferred_element_type=F32)


def _sigmoid(z):
    return 0.5 * jnp.tanh(0.5 * z) + 0.5


def _silu(z):
    return z * _sigmoid(z)


def _gelu_tanh(y):
    return 0.5 * y * (1.0 + jnp.tanh(0.7978845608028654 * (y + 0.044715 * (y * y * y))))


def _rms_rows(x, g):
    ms = jnp.mean(x * x, axis=-1, keepdims=True)
    return x * lax.rsqrt(ms + NORM_EPS) * g


def _head_mean_square(v, ones_ref):
    v2 = (v * v).astype(BF16)
    n = ones_ref.shape[0]
    sums = [_dot(v2[:, c:c + n], ones_ref[...]) for c in range(0, v.shape[1], n)]
    return jnp.concatenate(sums, axis=-1) * (1.0 / NA_HEAD_DIM)


def _layer_spec(shape, layer, *tail):
    tail = tail or (0,) * len(shape)
    return pl.BlockSpec((None,) + tuple(shape), lambda *_: (layer,) + tuple(tail))


S5_TILE_GROUPS = LANES // SSM_GROUP
S5_GROUP_FLAT = S5_CHUNK * SSM_GROUP


def _block_transpose(xs):
    n = len(xs)
    lane = lax.broadcasted_iota(jnp.int32, (1, LANES), 1)
    cur = list(xs)
    width, stride = LANES // 2, n // 2
    while stride >= 1:
        low = (lane & width) == 0
        nxt = list(cur)
        for i in range(n):
            if i & stride:
                continue
            a, b = cur[i], cur[i + stride]
            nxt[i] = jnp.where(low, a, pltpu.roll(b, width, 1))
            nxt[i + stride] = jnp.where(low, pltpu.roll(a, LANES - width, 1), b)
        cur = nxt
        width //= 2
        stride //= 2
    return cur


def _load_time_major(src_ref, slab_ref, s0=0, s1=None):
    nb, steps, n = src_ref.shape
    s1 = steps if s1 is None else s1
    tiles = n // LANES
    for b in range(nb):
        for l in range(tiles):
            slab_ref[l, pl.ds(b, s1 - s0, stride=nb), :] = src_ref[b, s0:s1, l * LANES:(l + 1) * LANES]
    return jnp.concatenate([slab_ref[l, 0:(s1 - s0) * nb, :] for l in range(tiles)], axis=-1)


def _store_batch_major(val, dst_ref, slab_ref):
    nb, steps, n = dst_ref.shape
    tiles = n // LANES
    for l in range(tiles):
        slab_ref[l] = val[:, l * LANES:(l + 1) * LANES]
    for b in range(nb):
        for l in range(tiles):
            dst_ref[b, :, l * LANES:(l + 1) * LANES] = slab_ref[l, pl.ds(b, steps, stride=nb), :]


def _inproj_body(x_ref, xp_ref, xn_ref, ns_ref, wax_ref, wq_ref, wk_ref, wv_ref, wcx_ref, wdx_ref, ones_ref,
                 qg_ref, kg_ref, cw_ref, cb_ref,
                 xc_ref, q_ref, k_ref, v_ref, cx_ref, dx_ref, stage_ref, *slabs):
    i = pl.program_id(0)
    halo = LRU_HALO_STEPS * SUBLANES
    if slabs:
        x = _load_time_major(x_ref, slabs[0])
        hs = xp_ref.shape[1]
        x_prev = _load_time_major(xp_ref, slabs[1], hs - LRU_HALO_STEPS, hs)
        x_next = _load_time_major(xn_ref, slabs[1], 0, LRU_HALO_STEPS)
    else:
        x = x_ref[...]
        x_prev = xp_ref[xp_ref.shape[0] - halo:, :]
        x_next = xn_ref[0:halo, :]
    hn = _rms_rows(x, ns_ref[...]).astype(BF16)
    steps = stage_ref.shape[1] // SUBLANES

    def to_batch_lanes(val, out_ref):
        for l in range(W_TILES):
            stage_ref[l] = val[:, l * LANES:(l + 1) * LANES]
        for b in range(SUBLANES):
            for l in range(W_TILES):
                c0 = b * BRANCH_W + l * LANES
                out_ref[:, c0:c0 + LANES] = stage_ref[l, pl.ds(b, steps, stride=SUBLANES), :].astype(BF16)

    hn_ext = jnp.concatenate([_rms_rows(x_prev, ns_ref[...]).astype(BF16), hn,
                              _rms_rows(x_next, ns_ref[...]).astype(BF16)], axis=0)
    ax = _dot(hn_ext, wax_ref[...])
    r = hn.shape[0]
    row = lax.broadcasted_iota(jnp.int32, (r + 2 * halo, 1), 0)
    outside = ((row < halo) & (i == 0)) | ((row >= r + halo) & (i == pl.num_programs(0) - 1))
    ax = jnp.where(outside, 0.0, ax)
    xc = cb_ref[...]
    for tap in range(LRU_CONV_W):
        xc = xc + cw_ref[tap:tap + 1, :] * ax[tap * SUBLANES:tap * SUBLANES + r, :]
    xc_ref[...] = xc.astype(BF16)
    q = _dot(hn, wq_ref[...])
    to_batch_lanes(q * lax.rsqrt(_head_mean_square(q, ones_ref) + NORM_EPS) * qg_ref[...], q_ref)
    k = _dot(hn, wk_ref[...])
    to_batch_lanes(k * lax.rsqrt(_head_mean_square(k, ones_ref) + NORM_EPS) * kg_ref[...], k_ref)
    to_batch_lanes(_dot(hn, wv_ref[...]), v_ref)
    cxv = _dot(hn, wcx_ref[...])
    nchunk = cxv.shape[0] // (S5_CHUNK * SUBLANES)
    for o in range(W_TILES):
        tile3 = cxv[:, o * LANES:(o + 1) * LANES].reshape(nchunk, S5_CHUNK * SUBLANES, LANES)
        per_t = [tile3[:, t * SUBLANES:(t + 1) * SUBLANES, :].reshape(nchunk * SUBLANES, LANES)
                 for t in range(S5_CHUNK)]
        for g, ug in enumerate(_block_transpose(per_t)):
            c0 = (o * S5_TILE_GROUPS + g) * S5_GROUP_FLAT
            cx_ref[:, c0:c0 + S5_GROUP_FLAT] = ug.astype(BF16)
    dx_ref[...] = _dot(hn, wdx_ref[...]).astype(BF16)


def _inproj(x, layer, norm_scale, w_in, ones_bd, q_gain, k_gain, conv_w, conv_b, batch):
    w = BRANCH_W
    tm = PROJ_ROWS
    batch_major = x.ndim == 3
    d = x.shape[-1]
    rows = x.shape[0] * x.shape[1] if batch_major else x.shape[0]
    s = rows // batch
    nt = rows // tm
    if batch_major:
        hs = SUBLANES
        per = tm // batch // hs
        x_spec = pl.BlockSpec((batch, tm // batch, d), lambda i: (0, i, 0))
        prev_spec = pl.BlockSpec((batch, hs, d), lambda i: (0, jnp.maximum(i * per - 1, 0), 0))
        next_spec = pl.BlockSpec((batch, hs, d), lambda i: (0, jnp.minimum((i + 1) * per, s // hs - 1), 0))
        slabs = [pltpu.VMEM((d // LANES, tm, LANES), F32),
                 pltpu.VMEM((d // LANES, LRU_HALO_STEPS * batch, LANES), F32)]
    else:
        hb = LRU_HALO_STEPS * batch
        per = tm // hb
        x_spec = pl.BlockSpec((tm, d), lambda i: (i, 0))
        prev_spec = pl.BlockSpec((hb, d), lambda i: (jnp.maximum(i * per - 1, 0), 0))
        next_spec = pl.BlockSpec((hb, d), lambda i: (jnp.minimum((i + 1) * per, rows // hb - 1), 0))
        slabs = []
    wcol = lambda j: pl.BlockSpec((None, d, w), lambda i: (layer, 0, j))
    tm_spec = pl.BlockSpec((tm, w), lambda i: (i, 0))
    bl_spec = pl.BlockSpec((tm // batch, batch * w), lambda i: (i, 0))
    tm_shape = jax.ShapeDtypeStruct((rows, w), BF16)
    bl_shape = jax.ShapeDtypeStruct((s, batch * w), BF16)
    gf_spec = pl.BlockSpec((tm // S5_CHUNK, S5_CHUNK * w), lambda i: (i, 0))
    gf_shape = jax.ShapeDtypeStruct((rows // S5_CHUNK, S5_CHUNK * w), BF16)
    return pl.pallas_call(
        _inproj_body,
        grid=(nt,),
        in_specs=[
            x_spec, prev_spec, next_spec,
            _layer_spec((1, d), layer),
            wcol(COL_AX), wcol(COL_Q), wcol(COL_K), wcol(COL_V), wcol(COL_CX), wcol(COL_DX),
            pl.BlockSpec(ones_bd.shape, lambda i: (0, 0)),
            _layer_spec((1, w), layer), _layer_spec((1, w), layer),
            _layer_spec((LRU_CONV_W, w), layer), _layer_spec((1, w), layer),
        ],
        out_specs=[tm_spec, bl_spec, bl_spec, bl_spec, gf_spec, tm_spec],
        out_shape=[tm_shape, bl_shape, bl_shape, bl_shape, gf_shape, tm_shape],
        scratch_shapes=[pltpu.VMEM((W_TILES, tm, LANES), F32)] + slabs,
        compiler_params=_params("parallel"),
        name="inproj",
    )(x, x, x, norm_scale, w_in, w_in, w_in, w_in, w_in, w_in, ones_bd, q_gain, k_gain, conv_w, conv_b)


def _fill_extended(ext_ref, main_ref, prev_ref, next_ref, n_prev, n_next, is_first, is_last):
    r = main_ref.shape[0]
    prev = prev_ref[...].astype(F32)
    ext_ref[0:n_prev, :] = jnp.where(is_first, 0.0, prev[prev.shape[0] - n_prev:, :])
    ext_ref[n_prev:n_prev + r, :] = main_ref[...].astype(F32)
    nxt = next_ref[...].astype(F32)
    ext_ref[n_prev + r:n_prev + r + n_next, :] = jnp.where(is_last, 0.0, nxt[0:n_next, :])


def _lru_body(xf_ref, xb_ref, wg_ref, bg_ref, ch_ref, hf_ref, hb_ref, af_ref, bf_ref, ab_ref, bb_ref, carry_ref):
    i = pl.program_id(0)
    w = BRANCH_W
    r = xf_ref.shape[0]
    steps = r // SUBLANES

    @pl.when(i == 0)
    def _():
        carry_ref[...] = jnp.zeros_like(carry_ref)

    def prepare(direction, x_ref, a_ref, b_ref):
        def sub(sb, _):
            rows = pl.ds(pl.multiple_of(sb * ELEM_ROWS, ELEM_ROWS), ELEM_ROWS)
            xcb = x_ref[rows, :]
            xc = xcb.astype(F32)
            g = _dot(xcb, wg_ref[direction]) + bg_ref[direction]
            tr = jnp.tanh(g[:, 0:w]) + 1.0
            ti = jnp.tanh(g[:, w:2 * w]) + 1.0
            a = jnp.exp2(ch_ref[direction] * tr)
            a_ref[rows, :] = a
            om = 1.0 - a * a
            b_ref[rows, :] = om * lax.rsqrt(jnp.maximum(om, 1e-37)) * (ti * xc)
            return 0

        lax.fori_loop(0, r // ELEM_ROWS, sub, 0, unroll=True)

    prepare(0, xf_ref, af_ref, bf_ref)
    prepare(1, xb_ref, ab_ref, bb_ref)

    def step(t, carry):
        hf, hb = carry
        rf = pl.ds(pl.multiple_of(t * SUBLANES, SUBLANES), SUBLANES)
        hf = af_ref[rf, :] * hf + bf_ref[rf, :]
        bf_ref[rf, :] = hf
        rb = pl.ds(pl.multiple_of((steps - 1 - t) * SUBLANES, SUBLANES), SUBLANES)
        hb = ab_ref[rb, :] * hb + bb_ref[rb, :]
        bb_ref[rb, :] = hb
        return hf, hb

    hf, hb = lax.fori_loop(0, steps, step, (carry_ref[0], carry_ref[1]), unroll=8)
    carry_ref[0] = hf
    carry_ref[1] = hb
    hf_ref[...] = bf_ref[...].astype(BF16)
    hb_ref[...] = bb_ref[...].astype(BF16)


def _lru(xc, layer, w_gate, b_gate, c8):
    rows, w = xc.shape
    r = LRU_STEPS * SUBLANES
    nt = rows // r
    fwd = pl.BlockSpec((r, w), lambda i: (i, 0))
    bwd = pl.BlockSpec((r, w), lambda i: (nt - 1 - i, 0))
    out_shape = jax.ShapeDtypeStruct((rows, w), BF16)
    return pl.pallas_call(
        _lru_body,
        grid=(nt,),
        in_specs=[
            fwd, bwd,
            _layer_spec((2, w, 2 * w), layer),
            _layer_spec((2, 1, 2 * w), layer),
            _layer_spec((2, 1, w), layer),
        ],
        out_specs=[fwd, bwd],
        out_shape=[out_shape, out_shape],
        scratch_shapes=[
            pltpu.VMEM((r, w), F32), pltpu.VMEM((r, w), F32),
            pltpu.VMEM((r, w), F32), pltpu.VMEM((r, w), F32),
            pltpu.VMEM((2, SUBLANES, w), F32),
        ],
        compiler_params=_params("arbitrary"),
        name="rglru",
    )(xc, xc, w_gate, b_gate, c8)


def _pool_body(m_ref, p_ref, n_ref, wp_ref, sc_ref, o_ref, ext_ref, pooled_ref, *, seq_len):
    i = pl.program_id(0)
    nt = pl.num_programs(0)
    r = m_ref.shape[0]
    steps = r // SUBLANES
    halo = max(POOL_WINDOWS) // 2
    hr = halo * SUBLANES
    _fill_extended(ext_ref, m_ref, p_ref, n_ref, hr, hr, i == 0, i == nt - 1)

    t_glob = i * steps + lax.broadcasted_iota(jnp.int32, (r, POOL_GROUP), 0) // SUBLANES
    for g, win in enumerate(POOL_WINDOWS):
        ls = slice(g * POOL_GROUP, (g + 1) * POOL_GROUP)
        e = ext_ref[:, ls]
        lo_t = -halo
        cur = e
        half = 1
        while half < win:
            n = cur.shape[0] - half * SUBLANES
            if half == 1:
                cur = cur[0:n, :] + cur[SUBLANES:SUBLANES + n, :]
                lo_t += 1
            else:
                sh = (half // 2) * SUBLANES
                cur = cur[0:n, :] + cur[2 * sh:2 * sh + n, :]
                lo_t += half // 2
            half *= 2
        off = (0 - lo_t) * SUBLANES
        wsum = cur[off:off + r, :]
        half_w = win // 2
        cnt = (jnp.clip(t_glob + half_w, 0, seq_len) - jnp.clip(t_glob - half_w, 0, seq_len)).astype(F32)
        pooled_ref[:, ls] = wsum / cnt - e[hr:hr + r, :]
    o_ref[...] = (_dot(pooled_ref[...].astype(BF16), wp_ref[...]) * sc_ref[...]).astype(BF16)


def _pool(dx, layer, w_pool_bd, scale, seq_len):
    rows, w = dx.shape
    r = POOL_STEPS * SUBLANES
    nt = rows // r
    hb = (max(POOL_WINDOWS) // 2) * SUBLANES
    per = r // hb
    last_hb = rows // hb - 1
    return pl.pallas_call(
        functools.partial(_pool_body, seq_len=seq_len),
        grid=(nt,),
        in_specs=[
            pl.BlockSpec((r, w), lambda i: (i, 0)),
            pl.BlockSpec((hb, w), lambda i: (jnp.maximum(i * per - 1, 0), 0)),
            pl.BlockSpec((hb, w), lambda i: (jnp.minimum((i + 1) * per, last_hb), 0)),
            _layer_spec((w, w), layer),
            _layer_spec((1, w), layer),
        ],
        out_specs=pl.BlockSpec((r, w), lambda i: (i, 0)),
        out_shape=jax.ShapeDtypeStruct((rows, w), BF16),
        scratch_shapes=[pltpu.VMEM((r + 2 * hb, w), F32), pltpu.VMEM((r, w), F32)],
        compiler_params=_params("parallel"),
        name="pool",
    )(dx, dx, dx, w_pool_bd, scale)


def _attn_body(q_ref, k_ref, v_ref, bias_ref, o_ref, *, grid_rows):
    g = pl.program_id(1)
    lane = lax.broadcasted_iota(jnp.int32, (GRID_W, LANES), 1)
    lo_half = lane < NA_HEAD_DIM
    nkeys = NA_ROWS * GRID_W

    def row_body(rr, _):
        r = g * ATTN_ROWS + rr
        rs = jnp.clip(r - NA_ROWS // 2, 0, grid_rows - NA_ROWS)
        d0 = rs - r + (NA_ROWS - 1)
        k0 = pl.multiple_of(rs * GRID_W, GRID_W)
        q0 = pl.multiple_of(rr * GRID_W, GRID_W)
        scores = []
        for hp in range(NA_HEADS // 2):
            ls = slice(hp * LANES, (hp + 1) * LANES)
            kp = k_ref[pl.ds(k0, nkeys), ls]
            qp = q_ref[pl.ds(q0, GRID_W), ls]
            zero = jnp.zeros_like(qp)
            qm = jnp.concatenate([jnp.where(lo_half, qp, zero), jnp.where(lo_half, zero, qp)], axis=0)
            sc = lax.dot_general(qm, kp, (((1,), (1,)), ((), ())), preferred_element_type=F32)
            bias = jnp.concatenate([bias_ref[hp, d0 + 2 * j] for j in range(NA_ROWS // 2)], axis=-1)
            scores.append(sc + bias)
        maxes = [jnp.max(sc, axis=-1, keepdims=True) for sc in scores]
        probs = [jnp.exp2(sc - m) for sc, m in zip(scores, maxes)]
        inv = [1.0 / jnp.sum(p, axis=-1, keepdims=True) for p in probs]
        for hp in range(NA_HEADS // 2):
            ls = slice(hp * LANES, (hp + 1) * LANES)
            vp = v_ref[pl.ds(k0, nkeys), ls]
            o = _dot(probs[hp].astype(BF16), vp) * inv[hp]
            o_ref[pl.ds(q0, GRID_W), ls] = jnp.where(lo_half, o[0:GRID_W], o[GRID_W:2 * GRID_W]).astype(BF16)
        return 0

    lax.fori_loop(0, ATTN_ROWS, row_body, 0, unroll=2)


def _attn(q, k, v, layer, bias_tab, batch):
    s, bw = q.shape
    w = bw // batch
    grid_rows = s // GRID_W
    ng = grid_rows // ATTN_ROWS
    qr = ATTN_ROWS * GRID_W
    q_spec = pl.BlockSpec((qr, w), lambda b, g: (g, b))
    kv_spec = pl.BlockSpec((s, w), lambda b, g: (0, b))
    return pl.pallas_call(
        functools.partial(_attn_body, grid_rows=grid_rows),
        grid=(batch, ng),
        in_specs=[q_spec, kv_spec, kv_spec, _layer_spec(bias_tab.shape[1:], layer)],
        out_specs=q_spec,
        out_shape=jax.ShapeDtypeStruct((s, bw), BF16),
        compiler_params=_params("parallel", "parallel"),
        name="natten",
    )(q, k, v, bias_tab)


def _s5_body(u_ref, g_ref, hf_ref, hb_ref, q_ref, a_ref, dsk_ref,
             y_ref, sinb_ref, sloc_ref, sin_ref, carry_ref):
    p = pl.program_id(1)
    j = pl.program_id(2)
    nblk = pl.num_programs(2)
    rb = u_ref.shape[0]
    nk = rb // SUBLANES
    npairs = S5_TILE_GROUPS // 2
    pf = 2 * S5_GROUP_FLAT
    re_t = lambda q: slice(2 * q * LANES, (2 * q + 1) * LANES)
    im_t = lambda q: slice((2 * q + 1) * LANES, (2 * q + 2) * LANES)
    pair_t = lambda q: slice(q * pf, (q + 1) * pf)
    coef_t = lambda q: slice(q * LANES, (q + 1) * LANES)

    @pl.when(j == 0)
    def _():
        carry_ref[...] = jnp.zeros_like(carry_ref)

    def sweep(direction, reverse):
        a_re = a_ref[2 * direction:2 * direction + 1, :]
        a_im = a_ref[2 * direction + 1:2 * direction + 2, :]

        def body(n, state):
            kk = (nk - 1 - n) if reverse else n
            rows = pl.ds(pl.multiple_of(kk * SUBLANES, SUBLANES), SUBLANES)
            new = []
            for q in range(npairs):
                sr, si = state[2 * q], state[2 * q + 1]
                sin_ref[rows, re_t(q)] = sr
                sin_ref[rows, im_t(q)] = si
                ar, ai = a_re[:, coef_t(q)], a_im[:, coef_t(q)]
                new.append(ar * sr - ai * si + sloc_ref[rows, re_t(q)])
                new.append(ar * si + ai * sr + sloc_ref[rows, im_t(q)])
            return tuple(new)

        init = tuple(carry_ref[:, t * LANES:(t + 1) * LANES] for t in range(2 * npairs))
        out = lax.fori_loop(0, nk, body, init, unroll=4)
        for t in range(2 * npairs):
            carry_ref[:, t * LANES:(t + 1) * LANES] = out[t]

    @pl.when(p == 0)
    def _():
        for q in range(npairs):
            sloc_ref[:, pair_t(q)] = _dot(u_ref[:, pair_t(q)], hb_ref[q])
        sweep(1, True)
        blk = nblk - 1 - j
        sinb_ref[pl.ds(pl.multiple_of(blk * rb, rb), rb), :] = sin_ref[...].astype(BF16)

    @pl.when(p == 1)
    def _():
        for q in range(npairs):
            sloc_ref[:, pair_t(q)] = _dot(u_ref[:, pair_t(q)], hf_ref[q])
        sweep(0, False)
        rows_b = pl.ds(pl.multiple_of(j * rb, rb), rb)
        for q in range(npairs):
            u = u_ref[:, pair_t(q)]
            states = jnp.concatenate([sin_ref[:, pair_t(q)].astype(BF16), sinb_ref[rows_b, pair_t(q)]], axis=-1)
            y = _dot(u, g_ref[q]) + _dot(states, q_ref[q]) + dsk_ref[q] * u.astype(F32)
            y_ref[:, pair_t(q)] = y.astype(BF16)


def _s5(cx, layer, mats):
    g_m, hf_m, hb_m, q_m, a8, dsk = mats
    rows, width = cx.shape
    rb = S5_BLOCK_CHUNKS * SUBLANES
    nblk = rows // rb
    tile_w = S5_TILE_GROUPS * S5_GROUP_FLAT
    npairs, pf = S5_TILE_GROUPS // 2, 2 * S5_GROUP_FLAT
    state_w = S5_TILE_GROUPS * 2 * SSM_STATE
    per_tile = lambda *shape: pl.BlockSpec((None, npairs) + shape, lambda o, p, j: (layer, o) + (0,) * len(shape))
    return pl.pallas_call(
        _s5_body,
        grid=(width // tile_w, 2, nblk),
        in_specs=[
            pl.BlockSpec((rb, tile_w), lambda o, p, j: (j + (1 - p) * (nblk - 1 - 2 * j), o)),
            per_tile(pf, pf), per_tile(pf, 2 * LANES), per_tile(pf, 2 * LANES), per_tile(4 * LANES, pf),
            pl.BlockSpec((None, None, 4, npairs * LANES), lambda o, p, j: (layer, o, 0, 0)),
            per_tile(1, pf),
        ],
        out_specs=pl.BlockSpec((rb, tile_w), lambda o, p, j: (p * j, o)),
        out_shape=jax.ShapeDtypeStruct((rows, width), BF16),
        scratch_shapes=[
            pltpu.VMEM((rows, state_w), BF16),
            pltpu.VMEM((rb, state_w), F32),
            pltpu.VMEM((rb, state_w), F32),
            pltpu.VMEM((SUBLANES, state_w), F32),
        ],
        compiler_params=_params("arbitrary", "arbitrary", "arbitrary"),
        name="s5",
    )(cx, g_m, hf_m, hb_m, q_m, a8, dsk)


def _merge_body(x_ref, p_ref, hf_ref, hb_ref, yb_ref, yc_ref, yd_ref, ns_ref,
                wag_ref, wbg_ref, wcg_ref, wdg_ref, wm0_ref, wm1_ref, wm2_ref, wm3_ref,
                wb0_ref, wb1_ref, wb2_ref, wb3_ref, wout_ref, pg_ref, pp_ref, gluw_ref, glub_ref,
                o_ref, stage_ref, pslab_ref, xslab_ref, *, x_batch_major, out_batch_major):
    x = _load_time_major(x_ref, xslab_ref) if x_batch_major else x_ref[...]
    hn = _rms_rows(x, ns_ref[...]).astype(BF16)

    def branch(y, wb_ref, wm_ref):
        return _dot(y.astype(BF16), wb_ref[...]) * _sigmoid(_dot(hn, wm_ref[...]))

    ya = (hf_ref[...].astype(F32) + hb_ref[...].astype(F32)) * _silu(_dot(hn, wag_ref[...]))
    merged = branch(ya, wb0_ref, wm0_ref)
    steps = stage_ref.shape[1] // SUBLANES
    for b in range(SUBLANES):
        for l in range(W_TILES):
            c0 = b * BRANCH_W + l * LANES
            stage_ref[l, pl.ds(b, steps, stride=SUBLANES), :] = yb_ref[:, c0:c0 + LANES].astype(F32)
    yb = jnp.concatenate([stage_ref[l] for l in range(W_TILES)], axis=-1)
    merged = merged + branch(yb * _silu(_dot(hn, wbg_ref[...])), wb1_ref, wm1_ref)
    nchunk = yc_ref.shape[0] // SUBLANES
    tiles = []
    for o in range(W_TILES):
        per_g = [yc_ref[:, (o * S5_TILE_GROUPS + g) * S5_GROUP_FLAT:(o * S5_TILE_GROUPS + g + 1) * S5_GROUP_FLAT]
                 .astype(F32) for g in range(S5_TILE_GROUPS)]
        per_t = [v.reshape(nchunk, 1, SUBLANES, LANES) for v in _block_transpose(per_g)]
        tiles.append(jnp.concatenate(per_t, axis=1).reshape(nchunk * S5_CHUNK * SUBLANES, LANES))
    yg = _gelu_tanh(jnp.concatenate(tiles, axis=-1))
    yc = yg * _sigmoid(_dot(yg.astype(BF16), gluw_ref[...]) + glub_ref[...])
    merged = merged + branch(yc * _silu(_dot(hn, wcg_ref[...])), wb2_ref, wm2_ref)
    merged = merged + branch(yd_ref[...].astype(F32) * _silu(_dot(hn, wdg_ref[...])), wb3_ref, wm3_ref)
    x1 = x + _dot(merged.astype(BF16), wout_ref[...])
    emb = _dot(_load_time_major(p_ref, pslab_ref).astype(BF16), pp_ref[...])
    out = x1 + _sigmoid(_dot(x1.astype(BF16), pg_ref[...])) * emb
    if out_batch_major:
        _store_batch_major(out, o_ref, xslab_ref)
    else:
        o_ref[...] = out


def _merge(x, p, layer, hf, hb, yb, yc, yd, norm_scale, w_in, wbr, wout, pgate, pproj, glu_w, glu_b, batch,
           out_batch_major):
    w = BRANCH_W
    tm = MERGE_ROWS
    pdim = p.shape[-1]
    once = pl.Buffered(1)
    x_batch_major = x.ndim == 3
    d = x.shape[-1]
    rows = x.shape[0] * x.shape[1] if x_batch_major else x.shape[0]
    steps = tm // batch
    bm_spec = pl.BlockSpec((batch, steps, d), lambda i: (0, i, 0))
    x_spec = bm_spec if x_batch_major else pl.BlockSpec((tm, d), lambda i: (i, 0))
    if out_batch_major:
        out_spec, out_shape = bm_spec, jax.ShapeDtypeStruct((batch, rows // batch, d), F32)
    else:
        out_spec, out_shape = pl.BlockSpec((tm, d), lambda i: (i, 0)), jax.ShapeDtypeStruct((rows, d), F32)

    def resident(shape, *tail):
        tail = tail or (0,) * len(shape)
        return pl.BlockSpec((None,) + tuple(shape), lambda i: (layer,) + tuple(tail), pipeline_mode=once)

    row_spec = lambda n: pl.BlockSpec((tm, n), lambda i: (i, 0))
    gate_cols = [resident((d, w), 0, j) for j in (COL_AG, COL_BG, COL_CG, COL_DG)]
    merge_cols = [resident((d, d), 0, COL_MERGE + n) for n in range(N_BRANCH)]
    return pl.pallas_call(
        functools.partial(_merge_body, x_batch_major=x_batch_major, out_batch_major=out_batch_major),
        grid=(rows // tm,),
        in_specs=[
            x_spec,
            pl.BlockSpec((None, batch, steps, pdim), lambda i: (layer, 0, i, 0)),
            row_spec(w), row_spec(w),
            pl.BlockSpec((tm // batch, batch * w), lambda i: (i, 0)),
            pl.BlockSpec((tm // S5_CHUNK, S5_CHUNK * w), lambda i: (i, 0)), row_spec(w),
            resident((1, d)),
            *gate_cols, *merge_cols,
            *[pl.BlockSpec((None, None, w, d), lambda i, n=n: (layer, n, 0, 0), pipeline_mode=once)
              for n in range(N_BRANCH)],
            resident((d, d)), resident((d, d)), resident((pdim, d)),
            resident((w, w)), resident((1, w)),
        ],
        out_specs=out_spec,
        out_shape=out_shape,
        scratch_shapes=[pltpu.VMEM((W_TILES, tm, LANES), F32),
                        pltpu.VMEM((pdim // LANES, tm, LANES), F32),
                        pltpu.VMEM((d // LANES, tm, LANES), F32)],
        compiler_params=_params("parallel"),
        name="merge",
    )(x, p, hf, hb, yb, yc, yd, norm_scale, *([w_in] * 8), *([wbr] * N_BRANCH), wout, pgate, pproj, glu_w, glu_b)


def _block_diag(blocks):
    n, r, c = blocks.shape[-3:]
    eye = jnp.eye(n, dtype=blocks.dtype)
    out = jnp.einsum('...nrc,nm->...nrmc', blocks, eye)
    return out.reshape(blocks.shape[:-3] + (n * r, n * c))


def _attn_bias_tables(rpb):
    qc = np.arange(GRID_W)[:, None]
    kc = np.arange(GRID_W)[None, :]
    ws = np.clip(qc - NA_COLS // 2, 0, GRID_W - NA_COLS)
    in_win = (kc >= ws) & (kc < ws + NA_COLS)
    dc = np.clip(kc - qc, -(NA_COLS - 1), NA_COLS - 1) + NA_COLS - 1
    onehot = ((dc[None] == np.arange(2 * NA_COLS - 1)[:, None, None]) & in_win[None]).astype(np.float32)
    tab = jnp.einsum('hrm,mqk->hrqk', rpb.astype(F32), jnp.asarray(onehot), precision=lax.Precision.HIGHEST)
    tab = tab + jnp.asarray(np.where(in_win, 0.0, -1e30).astype(np.float32))
    pairs = jnp.concatenate([tab[:, :-1], tab[:, 1:]], axis=-1)
    nr = pairs.shape[1]
    pairs = pairs.reshape(NA_HEADS // 2, 2, nr, GRID_W, 2 * GRID_W)
    return jnp.transpose(pairs, (0, 2, 1, 3, 4)).reshape(NA_HEADS // 2, nr, 2 * GRID_W, 2 * GRID_W)


def _s5_matrices(a_re, a_im, log_dt, b_re, b_im, c_re, c_im, d_skip):
    L, G, P, C = S5_CHUNK, SSM_GROUPS, SSM_STATE, SSM_GROUP
    f32 = F32
    hi = lax.Precision.HIGHEST
    lr = jnp.minimum(a_re.astype(f32), -1e-4)
    li = a_im.astype(f32)
    dt = jnp.exp(log_dt.astype(f32))[..., None]
    mag = jnp.exp(lr * dt)
    ab_r = mag * jnp.cos(li * dt)
    ab_i = mag * jnp.sin(li * dt)
    nr = ab_r - 1.0
    den = lr * lr + li * li
    fr = ((nr * lr + ab_i * li) / den)[..., None]
    fi = ((ab_i * lr - nr * li) / den)[..., None]
    br, bi = b_re.astype(f32), b_im.astype(f32)
    bb_r = fr * br - fi * bi
    bb_i = fr * bi + fi * br
    cr, ci = c_re.astype(f32), c_im.astype(f32)
    pr, pi = [jnp.ones_like(ab_r)], [jnp.zeros_like(ab_r)]
    for _ in range(L):
        pr.append(pr[-1] * ab_r - pi[-1] * ab_i)
        pi.append(pr[-2] * ab_i + pi[-1] * ab_r)
    pw_r = jnp.stack(pr)
    pw_i = jnp.stack(pi)
    m_r = pw_r[..., None] * bb_r[None] - pw_i[..., None] * bb_i[None]
    m_i = pw_r[..., None] * bb_i[None] + pw_i[..., None] * bb_r[None]
    k_lag = (jnp.einsum('dgop,ndgpi->ndgio', cr, m_r[:L], precision=hi)
             - jnp.einsum('dgop,ndgpi->ndgio', ci, m_i[:L], precision=hi))
    lag = np.arange(L)[None, :] - np.arange(L)[:, None]
    pick_f = (lag[None] == np.arange(L)[:, None, None]).astype(np.float32)
    pick_b = (-lag[None] == np.arange(L)[:, None, None]).astype(np.float32)
    g_mat = (jnp.einsum('ngio,nst->gsito', k_lag[:, 0], jnp.asarray(pick_f), precision=hi)
             + jnp.einsum('ngio,nst->gsito', k_lag[:, 1], jnp.asarray(pick_b), precision=hi)).reshape(G, L * C, L * C)

    npair = G // 2
    eye2 = jnp.eye(2, dtype=f32)
    g_pair = _block_diag(g_mat.reshape(npair, 2, L * C, L * C))

    def h_pair(direction, powers):
        both = jnp.stack([m_r[powers, direction], m_i[powers, direction]])
        h = jnp.transpose(both, (2, 1, 4, 0, 3)).reshape(npair, 2, L * C, 2, P)
        return jnp.einsum('qerip,ek->qerikp', h, eye2).reshape(npair, 2 * L * C, 4 * P)

    hf_pair = h_pair(0, np.arange(L - 1, -1, -1))
    hb_pair = h_pair(1, np.arange(L))

    def q_rows(direction, powers):
        pwr = pw_r[powers, direction]
        pwi = pw_i[powers, direction]
        c_r, c_i = cr[direction], ci[direction]
        q_re = jnp.einsum('gcp,tgp->gptc', c_r, pwr) - jnp.einsum('gcp,tgp->gptc', c_i, pwi)
        q_im = -(jnp.einsum('gcp,tgp->gptc', c_r, pwi) + jnp.einsum('gcp,tgp->gptc', c_i, pwr))
        return jnp.stack([q_re, q_im], axis=1).reshape(G, 2, P, L * C)

    q_all = jnp.stack([q_rows(0, np.arange(1, L + 1)), q_rows(1, np.arange(L, 0, -1))], axis=1)
    q_all = q_all.reshape(npair, 2, 2, 2, P, L * C)
    q_pair = jnp.einsum('qedipn,ek->qdiepkn', q_all, eye2).reshape(npair, 8 * P, 2 * L * C)
    a8 = jnp.stack([pw_r[L, 0], pw_i[L, 0], pw_r[L, 1], pw_i[L, 1]])
    a8 = jnp.transpose(a8.reshape(4, W_TILES, (S5_TILE_GROUPS // 2) * 2 * P), (1, 0, 2))
    dsk = jnp.tile(d_skip.astype(f32).reshape(G, 1, C), (1, 1, L)).reshape(npair, 1, 2 * L * C)
    bf = lambda m: m.astype(BF16)
    return bf(g_pair), bf(hf_pair), bf(hb_pair), bf(q_pair), a8, dsk


def kernel(x, p, norm_scale, w_in, lru_conv_w, lru_conv_b, lru_w_r, lru_b_r, lru_w_i, lru_b_i, lru_lambda, na_q_gain, na_k_gain, na_rel_bias, ssm_a_re, ssm_a_im, ssm_log_dt, ssm_b_re, ssm_b_im, ssm_c_re, ssm_c_im, ssm_d, ssm_glu_w, ssm_glu_b, pool_w, pool_scale, w_branch, w_out, ple_proj, ple_gate):
    b, s, d = x.shape
    w = BRANCH_W
    assert b == SUBLANES and d == D_MODEL and s % (LRU_STEPS * 4) == 0 and s // GRID_W >= NA_ROWS
    depth = w_in.shape[0]
    rows = s * b

    norm = norm_scale.astype(F32)[:, None, :]
    w_in16 = w_in.astype(BF16)
    ones_bd = _block_diag(jnp.ones((MXU_DIM // NA_HEAD_DIM, NA_HEAD_DIM, NA_HEAD_DIM), BF16))
    q_gain = (jnp.tile(na_q_gain.astype(F32), (1, NA_HEADS)) * (NA_HEAD_DIM ** -0.5 * LOG2E))[:, None, :]
    k_gain = jnp.tile(na_k_gain.astype(F32), (1, NA_HEADS))[:, None, :]
    conv_w = lru_conv_w.astype(F32)
    conv_b = lru_conv_b.astype(F32)[:, None, :]
    gate_w = (0.5 * jnp.concatenate([_block_diag(lru_w_r), _block_diag(lru_w_i)], axis=-1)).astype(BF16)
    gate_b = (0.5 * jnp.concatenate([lru_b_r, lru_b_i], axis=-1).astype(F32))[:, :, None, :]
    c8 = (-0.5 * LRU_C * LOG2E * jax.nn.softplus(-lru_lambda.astype(F32)))[:, :, None, :]
    bias_tab = jax.vmap(_attn_bias_tables)(na_rel_bias.astype(F32) * LOG2E)
    s5_mats = jax.vmap(_s5_matrices)(ssm_a_re, ssm_a_im, ssm_log_dt, ssm_b_re, ssm_b_im, ssm_c_re, ssm_c_im, ssm_d)
    glu_w = ssm_glu_w.astype(BF16)
    glu_b = ssm_glu_b.astype(F32)[:, None, :]
    pool_bd = _block_diag(pool_w).astype(BF16)
    pool_sc = pool_scale.astype(F32)[:, None, :]
    branch_scale = jnp.asarray([0.5] + [1.0] * (N_BRANCH - 1), F32)[None, :, None, None]
    wbr = (w_branch.astype(F32) * branch_scale).astype(BF16)
    wout = w_out.astype(BF16)
    pproj = ple_proj.astype(BF16)
    pgate = ple_gate.astype(BF16)

    xt = x
    for i in range(depth):
        xc, q, k, v, cx, dx = _inproj(xt, i, norm, w_in16, ones_bd, q_gain, k_gain, conv_w, conv_b, b)
        hf, hb = _lru(xc, i, gate_w, gate_b, c8)
        yb = _attn(q, k, v, i, bias_tab, b)
        yc = _s5(cx, i, s5_mats)
        yd = _pool(dx, i, pool_bd, pool_sc, s)
        xt = _merge(xt, p, i, hf, hb, yb, yc, yd, norm, w_in16, wbr, wout, pgate, pproj, glu_w, glu_b, b,
                    out_batch_major=(i == depth - 1))
    return xt
```

```python
import functools

import jax
import jax.numpy as jnp
import numpy as np
from jax import lax
from jax.experimental import pallas as pl
from jax.experimental.pallas import tpu as pltpu

F32 = jnp.float32
BF16 = jnp.bfloat16

D_MODEL = 1024
BRANCH_W = 512
N_BRANCH = 4
NORM_EPS = 1e-6
LOG2E = 1.4426950408889634
GRID_W = 64
LRU_C = 8.0
LRU_CONV_W = 4
LRU_HALO_STEPS = 2
NA_HEADS = 8
NA_HEAD_DIM = 64
NA_ROWS = 8
NA_COLS = 16
SSM_GROUP = 16
SSM_GROUPS = 32
SSM_STATE = 64
POOL_WINDOWS = (2, 4, 8, 16)
POOL_GROUP = 128

SUBLANES = 8
LANES = 128
MXU_DIM = 256
W_TILES = BRANCH_W // LANES
S5_CHUNK = 8

COL_AX, COL_AG, COL_Q, COL_K, COL_V, COL_BG, COL_CX, COL_CG, COL_DX, COL_DG = range(10)
COL_MERGE = 10 * BRANCH_W // D_MODEL

PROJ_ROWS = 1024
MERGE_ROWS = 512
LRU_STEPS = 256
POOL_STEPS = 256
ATTN_ROWS = 8
S5_BLOCK_CHUNKS = 128
ELEM_ROWS = 256

VMEM_LIMIT = 56 * 1024 * 1024


def _params(*sem):
    return pltpu.CompilerParams(dimension_semantics=sem, vmem_limit_bytes=VMEM_LIMIT)


def _dot(a, b):
    return jnp.dot(a, b, preferred_element_type=F32)


def _sigmoid(z):
    return 0.5 * jnp.tanh(0.5 * z) + 0.5


def _silu(z):
    return z * _sigmoid(z)


def _gelu_tanh(y):
    return 0.5 * y * (1.0 + jnp.tanh(0.7978845608028654 * (y + 0.044715 * (y * y * y))))


def _rms_rows(x, g):
    ms = jnp.mean(x * x, axis=-1, keepdims=True)
    return x * lax.rsqrt(ms + NORM_EPS) * g


def _head_mean_square(v, ones_ref):
    v2 = (v * v).astype(BF16)
    n = ones_ref.shape[0]
    sums = [_dot(v2[:, c:c + n], ones_ref[...]) for c in range(0, v.shape[1], n)]
    return jnp.concatenate(sums, axis=-1) * (1.0 / NA_HEAD_DIM)


def _layer_spec(shape, layer, *tail):
    tail = tail or (0,) * len(shape)
    return pl.BlockSpec((None,) + tuple(shape), lambda *_: (layer,) + tuple(tail))


S5_TILE_GROUPS = LANES // SSM_GROUP
S5_GROUP_FLAT = S5_CHUNK * SSM_GROUP


def _block_transpose(xs):
    n = len(xs)
    lane = lax.broadcasted_iota(jnp.int32, (1, LANES), 1)
    cur = list(xs)
    width, stride = LANES // 2, n // 2
    while stride >= 1:
        low = (lane & width) == 0
        nxt = list(cur)
        for i in range(n):
            if i & stride:
                continue
            a, b = cur[i], cur[i + stride]
            nxt[i] = jnp.where(low, a, pltpu.roll(b, width, 1))
            nxt[i + stride] = jnp.where(low, pltpu.roll(a, LANES - width, 1), b)
        cur = nxt
        width //= 2
        stride //= 2
    return cur


def _load_time_major(src_ref, slab_ref, s0=0, s1=None):
    nb, steps, n = src_ref.shape
    s1 = steps if s1 is None else s1
    tiles = n // LANES
    for b in range(nb):
        for l in range(tiles):
            slab_ref[l, pl.ds(b, s1 - s0, stride=nb), :] = src_ref[b, s0:s1, l * LANES:(l + 1) * LANES]
    return jnp.concatenate([slab_ref[l, 0:(s1 - s0) * nb, :] for l in range(tiles)], axis=-1)


def _store_batch_major(val, dst_ref, slab_ref):
    nb, steps, n = dst_ref.shape
    tiles = n // LANES
    for l in range(tiles):
        slab_ref[l] = val[:, l * LANES:(l + 1) * LANES]
    for b in range(nb):
        for l in range(tiles):
            dst_ref[b, :, l * LANES:(l + 1) * LANES] = slab_ref[l, pl.ds(b, steps, stride=nb), :]


def _inproj_body(x_ref, xp_ref, xn_ref, ns_ref, wax_ref, wq_ref, wk_ref, wv_ref, wcx_ref, wdx_ref, ones_ref,
                 qg_ref, kg_ref, cw_ref, cb_ref,
                 xc_ref, q_ref, k_ref, v_ref, cx_ref, dx_ref, stage_ref, *slabs):
    i = pl.program_id(0)
    halo = LRU_HALO_STEPS * SUBLANES
    if slabs:
        x = _load_time_major(x_ref, slabs[0])
        hs = xp_ref.shape[1]
        x_prev = _load_time_major(xp_ref, slabs[1], hs - LRU_HALO_STEPS, hs)
        x_next = _load_time_major(xn_ref, slabs[1], 0, LRU_HALO_STEPS)
    else:
        x = x_ref[...]
        x_prev = xp_ref[xp_ref.shape[0] - halo:, :]
        x_next = xn_ref[0:halo, :]
    hn = _rms_rows(x, ns_ref[...]).astype(BF16)
    steps = stage_ref.shape[1] // SUBLANES

    def to_batch_lanes(val, out_ref):
        for l in range(W_TILES):
            stage_ref[l] = val[:, l * LANES:(l + 1) * LANES]
        for b in range(SUBLANES):
            for l in range(W_TILES):
                c0 = b * BRANCH_W + l * LANES
                out_ref[:, c0:c0 + LANES] = stage_ref[l, pl.ds(b, steps, stride=SUBLANES), :].astype(BF16)

    hn_ext = jnp.concatenate([_rms_rows(x_prev, ns_ref[...]).astype(BF16), hn,
                              _rms_rows(x_next, ns_ref[...]).astype(BF16)], axis=0)
    ax = _dot(hn_ext, wax_ref[...])
    r = hn.shape[0]
    row = lax.broadcasted_iota(jnp.int32, (r + 2 * halo, 1), 0)
    outside = ((row < halo) & (i == 0)) | ((row >= r + halo) & (i == pl.num_programs(0) - 1))
    ax = jnp.where(outside, 0.0, ax)
    xc = cb_ref[...]
    for tap in range(LRU_CONV_W):
        xc = xc + cw_ref[tap:tap + 1, :] * ax[tap * SUBLANES:tap * SUBLANES + r, :]
    xc_ref[...] = xc.astype(BF16)
    q = _dot(hn, wq_ref[...])
    to_batch_lanes(q * lax.rsqrt(_head_mean_square(q, ones_ref) + NORM_EPS) * qg_ref[...], q_ref)
    k = _dot(hn, wk_ref[...])
    to_batch_lanes(k * lax.rsqrt(_head_mean_square(k, ones_ref) + NORM_EPS) * kg_ref[...], k_ref)
    to_batch_lanes(_dot(hn, wv_ref[...]), v_ref)
    cxv = _dot(hn, wcx_ref[...])
    nchunk = cxv.shape[0] // (S5_CHUNK * SUBLANES)
    for o in range(W_TILES):
        tile3 = cxv[:, o * LANES:(o + 1) * LANES].reshape(nchunk, S5_CHUNK * SUBLANES, LANES)
        per_t = [tile3[:, t * SUBLANES:(t + 1) * SUBLANES, :].reshape(nchunk * SUBLANES, LANES)
                 for t in range(S5_CHUNK)]
        for g, ug in enumerate(_block_transpose(per_t)):
            c0 = (o * S5_TILE_GROUPS + g) * S5_GROUP_FLAT
            cx_ref[:, c0:c0 + S5_GROUP_FLAT] = ug.astype(BF16)
    dx_ref[...] = _dot(hn, wdx_ref[...]).astype(BF16)


def _inproj(x, layer, norm_scale, w_in, ones_bd, q_gain, k_gain, conv_w, conv_b, batch):
    w = BRANCH_W
    tm = PROJ_ROWS
    batch_major = x.ndim == 3
    d = x.shape[-1]
    rows = x.shape[0] * x.shape[1] if batch_major else x.shape[0]
    s = rows // batch
    nt = rows // tm
    if batch_major:
        hs = SUBLANES
        per = tm // batch // hs
        x_spec = pl.BlockSpec((batch, tm // batch, d), lambda i: (0, i, 0))
        prev_spec = pl.BlockSpec((batch, hs, d), lambda i: (0, jnp.maximum(i * per - 1, 0), 0))
        next_spec = pl.BlockSpec((batch, hs, d), lambda i: (0, jnp.minimum((i + 1) * per, s // hs - 1), 0))
        slabs = [pltpu.VMEM((d // LANES, tm, LANES), F32),
                 pltpu.VMEM((d // LANES, LRU_HALO_STEPS * batch, LANES), F32)]
    else:
        hb = LRU_HALO_STEPS * batch
        per = tm // hb
        x_spec = pl.BlockSpec((tm, d), lambda i: (i, 0))
        prev_spec = pl.BlockSpec((hb, d), lambda i: (jnp.maximum(i * per - 1, 0), 0))
        next_spec = pl.BlockSpec((hb, d), lambda i: (jnp.minimum((i + 1) * per, rows // hb - 1), 0))
        slabs = []
    wcol = lambda j: pl.BlockSpec((None, d, w), lambda i: (layer, 0, j))
    tm_spec = pl.BlockSpec((tm, w), lambda i: (i, 0))
    bl_spec = pl.BlockSpec((tm // batch, batch * w), lambda i: (i, 0))
    tm_shape = jax.ShapeDtypeStruct((rows, w), BF16)
    bl_shape = jax.ShapeDtypeStruct((s, batch * w), BF16)
    gf_spec = pl.BlockSpec((tm // S5_CHUNK, S5_CHUNK * w), lambda i: (i, 0))
    gf_shape = jax.ShapeDtypeStruct((rows // S5_CHUNK, S5_CHUNK * w), BF16)
    return pl.pallas_call(
        _inproj_body,
        grid=(nt,),
        in_specs=[
            x_spec, prev_spec, next_spec,
            _layer_spec((1, d), layer),
            wcol(COL_AX), wcol(COL_Q), wcol(COL_K), wcol(COL_V), wcol(COL_CX), wcol(COL_DX),
            pl.BlockSpec(ones_bd.shape, lambda i: (0, 0)),
            _layer_spec((1, w), layer), _layer_spec((1, w), layer),
            _layer_spec((LRU_CONV_W, w), layer), _layer_spec((1, w), layer),
        ],
        out_specs=[tm_spec, bl_spec, bl_spec, bl_spec, gf_spec, tm_spec],
        out_shape=[tm_shape, bl_shape, bl_shape, bl_shape, gf_shape, tm_shape],
        scratch_shapes=[pltpu.VMEM((W_TILES, tm, LANES), F32)] + slabs,
        compiler_params=_params("parallel"),
        name="inproj",
    )(x, x, x, norm_scale, w_in, w_in, w_in, w_in, w_in, w_in, ones_bd, q_gain, k_gain, conv_w, conv_b)


def _fill_extended(ext_ref, main_ref, prev_ref, next_ref, n_prev, n_next, is_first, is_last):
    r = main_ref.shape[0]
    prev = prev_ref[...].astype(F32)
    ext_ref[0:n_prev, :] = jnp.where(is_first, 0.0, prev[prev.shape[0] - n_prev:, :])
    ext_ref[n_prev:n_prev + r, :] = main_ref[...].astype(F32)
    nxt = next_ref[...].astype(F32)
    ext_ref[n_prev + r:n_prev + r + n_next, :] = jnp.where(is_last, 0.0, nxt[0:n_next, :])


def _lru_body(xf_ref, xb_ref, wg_ref, bg_ref, ch_ref, hf_ref, hb_ref, af_ref, bf_ref, ab_ref, bb_ref, carry_ref):
    i = pl.program_id(0)
    w = BRANCH_W
    r = xf_ref.shape[0]
    steps = r // SUBLANES

    @pl.when(i == 0)
    def _():
        carry_ref[...] = jnp.zeros_like(carry_ref)

    def prepare(direction, x_ref, a_ref, b_ref):
        def sub(sb, _):
            rows = pl.ds(pl.multiple_of(sb * ELEM_ROWS, ELEM_ROWS), ELEM_ROWS)
            xcb = x_ref[rows, :]
            xc = xcb.astype(F32)
            g = _dot(xcb, wg_ref[direction]) + bg_ref[direction]
            tr = jnp.tanh(g[:, 0:w]) + 1.0
            ti = jnp.tanh(g[:, w:2 * w]) + 1.0
            a = jnp.exp2(ch_ref[direction] * tr)
            a_ref[rows, :] = a
            om = 1.0 - a * a
            b_ref[rows, :] = om * lax.rsqrt(jnp.maximum(om, 1e-37)) * (ti * xc)
            return 0

        lax.fori_loop(0, r // ELEM_ROWS, sub, 0, unroll=True)

    prepare(0, xf_ref, af_ref, bf_ref)
    prepare(1, xb_ref, ab_ref, bb_ref)

    def step(t, carry):
        hf, hb = carry
        rf = pl.ds(pl.multiple_of(t * SUBLANES, SUBLANES), SUBLANES)
        hf = af_ref[rf, :] * hf + bf_ref[rf, :]
        bf_ref[rf, :] = hf
        rb = pl.ds(pl.multiple_of((steps - 1 - t) * SUBLANES, SUBLANES), SUBLANES)
        hb = ab_ref[rb, :] * hb + bb_ref[rb, :]
        bb_ref[rb, :] = hb
        return hf, hb

    hf, hb = lax.fori_loop(0, steps, step, (carry_ref[0], carry_ref[1]), unroll=8)
    carry_ref[0] = hf
    carry_ref[1] = hb
    hf_ref[...] = bf_ref[...].astype(BF16)
    hb_ref[...] = bb_ref[...].astype(BF16)


def _lru(xc, layer, w_gate, b_gate, c8):
    rows, w = xc.shape
    r = LRU_STEPS * SUBLANES
    nt = rows // r
    fwd = pl.BlockSpec((r, w), lambda i: (i, 0))
    bwd = pl.BlockSpec((r, w), lambda i: (nt - 1 - i, 0))
    out_shape = jax.ShapeDtypeStruct((rows, w), BF16)
    return pl.pallas_call(
        _lru_body,
        grid=(nt,),
        in_specs=[
            fwd, bwd,
            _layer_spec((2, w, 2 * w), layer),
            _layer_spec((2, 1, 2 * w), layer),
            _layer_spec((2, 1, w), layer),
        ],
        out_specs=[fwd, bwd],
        out_shape=[out_shape, out_shape],
        scratch_shapes=[
            pltpu.VMEM((r, w), F32), pltpu.VMEM((r, w), F32),
            pltpu.VMEM((r, w), F32), pltpu.VMEM((r, w), F32),
            pltpu.VMEM((2, SUBLANES, w), F32),
        ],
        compiler_params=_params("arbitrary"),
        name="rglru",
    )(xc, xc, w_gate, b_gate, c8)


def _pool_body(m_ref, p_ref, n_ref, wp_ref, sc_ref, o_ref, ext_ref, pooled_ref, *, seq_len):
    i = pl.program_id(0)
    nt = pl.num_programs(0)
    r = m_ref.shape[0]
    steps = r // SUBLANES
    halo = max(POOL_WINDOWS) // 2
    hr = halo * SUBLANES
    _fill_extended(ext_ref, m_ref, p_ref, n_ref, hr, hr, i == 0, i == nt - 1)

    t_glob = i * steps + lax.broadcasted_iota(jnp.int32, (r, POOL_GROUP), 0) // SUBLANES
    for g, win in enumerate(POOL_WINDOWS):
        ls = slice(g * POOL_GROUP, (g + 1) * POOL_GROUP)
        e = ext_ref[:, ls]
        lo_t = -halo
        cur = e
        half = 1
        while half < win:
            n = cur.shape[0] - half * SUBLANES
            if half == 1:
                cur = cur[0:n, :] + cur[SUBLANES:SUBLANES + n, :]
                lo_t += 1
            else:
                sh = (half // 2) * SUBLANES
                cur = cur[0:n, :] + cur[2 * sh:2 * sh + n, :]
                lo_t += half // 2
            half *= 2
        off = (0 - lo_t) * SUBLANES
        wsum = cur[off:off + r, :]
        half_w = win // 2
        cnt = (jnp.clip(t_glob + half_w, 0, seq_len) - jnp.clip(t_glob - half_w, 0, seq_len)).astype(F32)
        pooled_ref[:, ls] = wsum / cnt - e[hr:hr + r, :]
    o_ref[...] = (_dot(pooled_ref[...].astype(BF16), wp_ref[...]) * sc_ref[...]).astype(BF16)


def _pool(dx, layer, w_pool_bd, scale, seq_len):
    rows, w = dx.shape
    r = POOL_STEPS * SUBLANES
    nt = rows // r
    hb = (max(POOL_WINDOWS) // 2) * SUBLANES
    per = r // hb
    last_hb = rows // hb - 1
    return pl.pallas_call(
        functools.partial(_pool_body, seq_len=seq_len),
        grid=(nt,),
        in_specs=[
            pl.BlockSpec((r, w), lambda i: (i, 0)),
            pl.BlockSpec((hb, w), lambda i: (jnp.maximum(i * per - 1, 0), 0)),
            pl.BlockSpec((hb, w), lambda i: (jnp.minimum((i + 1) * per, last_hb), 0)),
            _layer_spec((w, w), layer),
            _layer_spec((1, w), layer),
        ],
        out_specs=pl.BlockSpec((r, w), lambda i: (i, 0)),
        out_shape=jax.ShapeDtypeStruct((rows, w), BF16),
        scratch_shapes=[pltpu.VMEM((r + 2 * hb, w), F32), pltpu.VMEM((r, w), F32)],
        compiler_params=_params("parallel"),
        name="pool",
    )(dx, dx, dx, w_pool_bd, scale)


def _attn_body(q_ref, k_ref, v_ref, bias_ref, o_ref, *, grid_rows):
    g = pl.program_id(1)
    lane = lax.broadcasted_iota(jnp.int32, (GRID_W, LANES), 1)
    lo_half = lane < NA_HEAD_DIM
    nkeys = NA_ROWS * GRID_W

    def row_body(rr, _):
        r = g * ATTN_ROWS + rr
        rs = jnp.clip(r - NA_ROWS // 2, 0, grid_rows - NA_ROWS)
        d0 = rs - r + (NA_ROWS - 1)
        k0 = pl.multiple_of(rs * GRID_W, GRID_W)
        q0 = pl.multiple_of(rr * GRID_W, GRID_W)
        scores = []
        for hp in range(NA_HEADS // 2):
            ls = slice(hp * LANES, (hp + 1) * LANES)
            kp = k_ref[pl.ds(k0, nkeys), ls]
            qp = q_ref[pl.ds(q0, GRID_W), ls]
            zero = jnp.zeros_like(qp)
            qm = jnp.concatenate([jnp.where(lo_half, qp, zero), jnp.where(lo_half, zero, qp)], axis=0)
            sc = lax.dot_general(qm, kp, (((1,), (1,)), ((), ())), preferred_element_type=F32)
            bias = jnp.concatenate([bias_ref[hp, d0 + 2 * j] for j in range(NA_ROWS // 2)], axis=-1)
            scores.append(sc + bias)
        maxes = [jnp.max(sc, axis=-1, keepdims=True) for sc in scores]
        probs = [jnp.exp2(sc - m) for sc, m in zip(scores, maxes)]
        inv = [1.0 / jnp.sum(p, axis=-1, keepdims=True) for p in probs]
        for hp in range(NA_HEADS // 2):
            ls = slice(hp * LANES, (hp + 1) * LANES)
            vp = v_ref[pl.ds(k0, nkeys), ls]
            o = _dot(probs[hp].astype(BF16), vp) * inv[hp]
            o_ref[pl.ds(q0, GRID_W), ls] = jnp.where(lo_half, o[0:GRID_W], o[GRID_W:2 * GRID_W]).astype(BF16)
        return 0

    lax.fori_loop(0, ATTN_ROWS, row_body, 0, unroll=2)


def _attn(q, k, v, layer, bias_tab, batch):
    s, bw = q.shape
    w = bw // batch
    grid_rows = s // GRID_W
    ng = grid_rows // ATTN_ROWS
    qr = ATTN_ROWS * GRID_W
    q_spec = pl.BlockSpec((qr, w), lambda b, g: (g, b))
    kv_spec = pl.BlockSpec((s, w), lambda b, g: (0, b))
    return pl.pallas_call(
        functools.partial(_attn_body, grid_rows=grid_rows),
        grid=(batch, ng),
        in_specs=[q_spec, kv_spec, kv_spec, _layer_spec(bias_tab.shape[1:], layer)],
        out_specs=q_spec,
        out_shape=jax.ShapeDtypeStruct((s, bw), BF16),
        compiler_params=_params("parallel", "parallel"),
        name="natten",
    )(q, k, v, bias_tab)


def _s5_body(u_ref, g_ref, hf_ref, hb_ref, q_ref, a_ref, dsk_ref,
             y_ref, sinb_ref, sloc_ref, sin_ref, carry_ref):
    p = pl.program_id(1)
    j = pl.program_id(2)
    nblk = pl.num_programs(2)
    rb = u_ref.shape[0]
    nk = rb // SUBLANES
    npairs = S5_TILE_GROUPS // 2
    pf = 2 * S5_GROUP_FLAT
    re_t = lambda q: slice(2 * q * LANES, (2 * q + 1) * LANES)
    im_t = lambda q: slice((2 * q + 1) * LANES, (2 * q + 2) * LANES)
    pair_t = lambda q: slice(q * pf, (q + 1) * pf)
    coef_t = lambda q: slice(q * LANES, (q + 1) * LANES)

    @pl.when(j == 0)
    def _():
        carry_ref[...] = jnp.zeros_like(carry_ref)

    def sweep(direction, reverse):
        a_re = a_ref[2 * direction:2 * direction + 1, :]
        a_im = a_ref[2 * direction + 1:2 * direction + 2, :]

        def body(n, state):
            kk = (nk - 1 - n) if reverse else n
            rows = pl.ds(pl.multiple_of(kk * SUBLANES, SUBLANES), SUBLANES)
            new = []
            for q in range(npairs):
                sr, si = state[2 * q], state[2 * q + 1]
                sin_ref[rows, re_t(q)] = sr
                sin_ref[rows, im_t(q)] = si
                ar, ai = a_re[:, coef_t(q)], a_im[:, coef_t(q)]
                new.append(ar * sr - ai * si + sloc_ref[rows, re_t(q)])
                new.append(ar * si + ai * sr + sloc_ref[rows, im_t(q)])
            return tuple(new)

        init = tuple(carry_ref[:, t * LANES:(t + 1) * LANES] for t in range(2 * npairs))
        out = lax.fori_loop(0, nk, body, init, unroll=4)
        for t in range(2 * npairs):
            carry_ref[:, t * LANES:(t + 1) * LANES] = out[t]

    @pl.when(p == 0)
    def _():
        for q in range(npairs):
            sloc_ref[:, pair_t(q)] = _dot(u_ref[:, pair_t(q)], hb_ref[q])
        sweep(1, True)
        blk = nblk - 1 - j
        sinb_ref[pl.ds(pl.multiple_of(blk * rb, rb), rb), :] = sin_ref[...].astype(BF16)

    @pl.when(p == 1)
    def _():
        for q in range(npairs):
            sloc_ref[:, pair_t(q)] = _dot(u_ref[:, pair_t(q)], hf_ref[q])
        sweep(0, False)
        rows_b = pl.ds(pl.multiple_of(j * rb, rb), rb)
        for q in range(npairs):
            u = u_ref[:, pair_t(q)]
            states = jnp.concatenate([sin_ref[:, pair_t(q)].astype(BF16), sinb_ref[rows_b, pair_t(q)]], axis=-1)
            y = _dot(u, g_ref[q]) + _dot(states, q_ref[q]) + dsk_ref[q] * u.astype(F32)
            y_ref[:, pair_t(q)] = y.astype(BF16)


def _s5(cx, layer, mats):
    g_m, hf_m, hb_m, q_m, a8, dsk = mats
    rows, width = cx.shape
    rb = S5_BLOCK_CHUNKS * SUBLANES
    nblk = rows // rb
    tile_w = S5_TILE_GROUPS * S5_GROUP_FLAT
    npairs, pf = S5_TILE_GROUPS // 2, 2 * S5_GROUP_FLAT
    state_w = S5_TILE_GROUPS * 2 * SSM_STATE
    per_tile = lambda *shape: pl.BlockSpec((None, npairs) + shape, lambda o, p, j: (layer, o) + (0,) * len(shape))
    return pl.pallas_call(
        _s5_body,
        grid=(width // tile_w, 2, nblk),
        in_specs=[
            pl.BlockSpec((rb, tile_w), lambda o, p, j: (j + (1 - p) * (nblk - 1 - 2 * j), o)),
            per_tile(pf, pf), per_tile(pf, 2 * LANES), per_tile(pf, 2 * LANES), per_tile(4 * LANES, pf),
            pl.BlockSpec((None, None, 4, npairs * LANES), lambda o, p, j: (layer, o, 0, 0)),
            per_tile(1, pf),
        ],
        out_specs=pl.BlockSpec((rb, tile_w), lambda o, p, j: (p * j, o)),
        out_shape=jax.ShapeDtypeStruct((rows, width), BF16),
        scratch_shapes=[
            pltpu.VMEM((rows, state_w), BF16),
            pltpu.VMEM((rb, state_w), F32),
            pltpu.VMEM((rb, state_w), F32),
            pltpu.VMEM((SUBLANES, state_w), F32),
        ],
        compiler_params=_params("arbitrary", "arbitrary", "arbitrary"),
        name="s5",
    )(cx, g_m, hf_m, hb_m, q_m, a8, dsk)


def _merge_body(x_ref, p_ref, hf_ref, hb_ref, yb_ref, yc_ref, yd_ref, ns_ref,
                wag_ref, wbg_ref, wcg_ref, wdg_ref, wm0_ref, wm1_ref, wm2_ref, wm3_ref,
                wb0_ref, wb1_ref, wb2_ref, wb3_ref, wout_ref, pg_ref, pp_ref, gluw_ref, glub_ref,
                o_ref, stage_ref, pslab_ref, xslab_ref, *, x_batch_major, out_batch_major):
    x = _load_time_major(x_ref, xslab_ref) if x_batch_major else x_ref[...]
    hn = _rms_rows(x, ns_ref[...]).astype(BF16)

    def branch(y, wb_ref, wm_ref):
        return _dot(y.astype(BF16), wb_ref[...]) * _sigmoid(_dot(hn, wm_ref[...]))

    ya = (hf_ref[...].astype(F32) + hb_ref[...].astype(F32)) * _silu(_dot(hn, wag_ref[...]))
    merged = branch(ya, wb0_ref, wm0_ref)
    steps = stage_ref.shape[1] // SUBLANES
    for b in range(SUBLANES):
        for l in range(W_TILES):
            c0 = b * BRANCH_W + l * LANES
            stage_ref[l, pl.ds(b, steps, stride=SUBLANES), :] = yb_ref[:, c0:c0 + LANES].astype(F32)
    yb = jnp.concatenate([stage_ref[l] for l in range(W_TILES)], axis=-1)
    merged = merged + branch(yb * _silu(_dot(hn, wbg_ref[...])), wb1_ref, wm1_ref)
    nchunk = yc_ref.shape[0] // SUBLANES
    tiles = []
    for o in range(W_TILES):
        per_g = [yc_ref[:, (o * S5_TILE_GROUPS + g) * S5_GROUP_FLAT:(o * S5_TILE_GROUPS + g + 1) * S5_GROUP_FLAT]
                 .astype(F32) for g in range(S5_TILE_GROUPS)]
        per_t = [v.reshape(nchunk, 1, SUBLANES, LANES) for v in _block_transpose(per_g)]
        tiles.append(jnp.concatenate(per_t, axis=1).reshape(nchunk * S5_CHUNK * SUBLANES, LANES))
    yg = _gelu_tanh(jnp.concatenate(tiles, axis=-1))
    yc = yg * _sigmoid(_dot(yg.astype(BF16), gluw_ref[...]) + glub_ref[...])
    merged = merged + branch(yc * _silu(_dot(hn, wcg_ref[...])), wb2_ref, wm2_ref)
    merged = merged + branch(yd_ref[...].astype(F32) * _silu(_dot(hn, wdg_ref[...])), wb3_ref, wm3_ref)
    x1 = x + _dot(merged.astype(BF16), wout_ref[...])
    emb = _dot(_load_time_major(p_ref, pslab_ref).astype(BF16), pp_ref[...])
    out = x1 + _sigmoid(_dot(x1.astype(BF16), pg_ref[...])) * emb
    if out_batch_major:
        _store_batch_major(out, o_ref, xslab_ref)
    else:
        o_ref[...] = out


def _merge(x, p, layer, hf, hb, yb, yc, yd, norm_scale, w_in, wbr, wout, pgate, pproj, glu_w, glu_b, batch,
           out_batch_major):
    w = BRANCH_W
    tm = MERGE_ROWS
    pdim = p.shape[-1]
    once = pl.Buffered(1)
    x_batch_major = x.ndim == 3
    d = x.shape[-1]
    rows = x.shape[0] * x.shape[1] if x_batch_major else x.shape[0]
    steps = tm // batch
    bm_spec = pl.BlockSpec((batch, steps, d), lambda i: (0, i, 0))
    x_spec = bm_spec if x_batch_major else pl.BlockSpec((tm, d), lambda i: (i, 0))
    if out_batch_major:
        out_spec, out_shape = bm_spec, jax.ShapeDtypeStruct((batch, rows // batch, d), F32)
    else:
        out_spec, out_shape = pl.BlockSpec((tm, d), lambda i: (i, 0)), jax.ShapeDtypeStruct((rows, d), F32)

    def resident(shape, *tail):
        tail = tail or (0,) * len(shape)
        return pl.BlockSpec((None,) + tuple(shape), lambda i: (layer,) + tuple(tail), pipeline_mode=once)

    row_spec = lambda n: pl.BlockSpec((tm, n), lambda i: (i, 0))
    gate_cols = [resident((d, w), 0, j) for j in (COL_AG, COL_BG, COL_CG, COL_DG)]
    merge_cols = [resident((d, d), 0, COL_MERGE + n) for n in range(N_BRANCH)]
    return pl.pallas_call(
        functools.partial(_merge_body, x_batch_major=x_batch_major, out_batch_major=out_batch_major),
        grid=(rows // tm,),
        in_specs=[
            x_spec,
            pl.BlockSpec((None, batch, steps, pdim), lambda i: (layer, 0, i, 0)),
            row_spec(w), row_spec(w),
            pl.BlockSpec((tm // batch, batch * w), lambda i: (i, 0)),
            pl.BlockSpec((tm // S5_CHUNK, S5_CHUNK * w), lambda i: (i, 0)), row_spec(w),
            resident((1, d)),
            *gate_cols, *merge_cols,
            *[pl.BlockSpec((None, None, w, d), lambda i, n=n: (layer, n, 0, 0), pipeline_mode=once)
              for n in range(N_BRANCH)],
            resident((d, d)), resident((d, d)), resident((pdim, d)),
            resident((w, w)), resident((1, w)),
        ],
        out_specs=out_spec,
        out_shape=out_shape,
        scratch_shapes=[pltpu.VMEM((W_TILES, tm, LANES), F32),
                        pltpu.VMEM((pdim // LANES, tm, LANES), F32),
                        pltpu.VMEM((d // LANES, tm, LANES), F32)],
        compiler_params=_params("parallel"),
        name="merge",
    )(x, p, hf, hb, yb, yc, yd, norm_scale, *([w_in] * 8), *([wbr] * N_BRANCH), wout, pgate, pproj, glu_w, glu_b)


def _block_diag(blocks):
    n, r, c = blocks.shape[-3:]
    spread = np.tile(np.eye(c, dtype=np.float32), (1, n))
    keep = np.kron(np.eye(n, dtype=np.float32), np.ones((r, c), np.float32))
    rows = blocks.reshape(blocks.shape[:-3] + (n * r, c))
    full = jnp.einsum('...rc,cm->...rm', rows, jnp.asarray(spread, blocks.dtype), precision=lax.Precision.HIGHEST)
    return full * jnp.asarray(keep, blocks.dtype)


def _attn_bias_tables(rpb):
    qc = np.arange(GRID_W)[:, None]
    kc = np.arange(GRID_W)[None, :]
    ws = np.clip(qc - NA_COLS // 2, 0, GRID_W - NA_COLS)
    in_win = (kc >= ws) & (kc < ws + NA_COLS)
    dc = np.clip(kc - qc, -(NA_COLS - 1), NA_COLS - 1) + NA_COLS - 1
    onehot = ((dc[None] == np.arange(2 * NA_COLS - 1)[:, None, None]) & in_win[None]).astype(np.float32)
    tab = jnp.einsum('hrm,mqk->hrqk', rpb.astype(F32), jnp.asarray(onehot), precision=lax.Precision.HIGHEST)
    tab = tab + jnp.asarray(np.where(in_win, 0.0, -1e30).astype(np.float32))
    pairs = jnp.concatenate([tab[:, :-1], tab[:, 1:]], axis=-1)
    nr = pairs.shape[1]
    pairs = pairs.reshape(NA_HEADS // 2, 2, nr, GRID_W, 2 * GRID_W)
    return jnp.transpose(pairs, (0, 2, 1, 3, 4)).reshape(NA_HEADS // 2, nr, 2 * GRID_W, 2 * GRID_W)


def _s5_matrices(a_re, a_im, log_dt, b_re, b_im, c_re, c_im, d_skip):
    L, G, P, C = S5_CHUNK, SSM_GROUPS, SSM_STATE, SSM_GROUP
    f32 = F32
    hi = lax.Precision.HIGHEST
    lr = jnp.minimum(a_re.astype(f32), -1e-4)
    li = a_im.astype(f32)
    dt = jnp.exp(log_dt.astype(f32))[..., None]
    steps_n = jnp.arange(L + 1, dtype=f32)[:, None, None, None]
    pw_mag = jnp.exp(steps_n * (lr * dt))
    pw_r = pw_mag * jnp.cos(steps_n * (li * dt))
    pw_i = pw_mag * jnp.sin(steps_n * (li * dt))
    ab_r, ab_i = pw_r[1], pw_i[1]
    nr = ab_r - 1.0
    den = lr * lr + li * li
    fr = ((nr * lr + ab_i * li) / den)[..., None]
    fi = ((ab_i * lr - nr * li) / den)[..., None]
    br, bi = b_re.astype(f32), b_im.astype(f32)
    bb_r = fr * br - fi * bi
    bb_i = fr * bi + fi * br
    cr, ci = c_re.astype(f32), c_im.astype(f32)
    m_r = pw_r[..., None] * bb_r[None] - pw_i[..., None] * bb_i[None]
    m_i = pw_r[..., None] * bb_i[None] + pw_i[..., None] * bb_r[None]
    k_lag = (jnp.einsum('dgop,ndgpi->ndgio', cr, m_r[:L], precision=hi)
             - jnp.einsum('dgop,ndgpi->ndgio', ci, m_i[:L], precision=hi))
    lag = np.arange(L)[None, :] - np.arange(L)[:, None]
    pick_f = (lag[None] == np.arange(L)[:, None, None]).astype(np.float32)
    pick_b = (-lag[None] == np.arange(L)[:, None, None]).astype(np.float32)
    g_mat = (jnp.einsum('ngio,nst->gsito', k_lag[:, 0], jnp.asarray(pick_f), precision=hi)
             + jnp.einsum('ngio,nst->gsito', k_lag[:, 1], jnp.asarray(pick_b), precision=hi)).reshape(G, L * C, L * C)

    npair = G // 2
    eye2 = jnp.eye(2, dtype=f32)
    g_pair = _block_diag(g_mat.reshape(npair, 2, L * C, L * C))

    def h_pair(direction, powers):
        both = jnp.stack([m_r[powers, direction], m_i[powers, direction]])
        h = jnp.transpose(both, (2, 1, 4, 0, 3)).reshape(npair, 2, L * C, 2, P)
        return jnp.einsum('qerip,ek->qerikp', h, eye2).reshape(npair, 2 * L * C, 4 * P)

    hf_pair = h_pair(0, np.arange(L - 1, -1, -1))
    hb_pair = h_pair(1, np.arange(L))

    def q_rows(direction, powers):
        pwr = pw_r[powers, direction]
        pwi = pw_i[powers, direction]
        c_r, c_i = cr[direction], ci[direction]
        q_re = jnp.einsum('gcp,tgp->gptc', c_r, pwr) - jnp.einsum('gcp,tgp->gptc', c_i, pwi)
        q_im = -(jnp.einsum('gcp,tgp->gptc', c_r, pwi) + jnp.einsum('gcp,tgp->gptc', c_i, pwr))
        return jnp.stack([q_re, q_im], axis=1).reshape(G, 2, P, L * C)

    q_all = jnp.stack([q_rows(0, np.arange(1, L + 1)), q_rows(1, np.arange(L, 0, -1))], axis=1)
    q_all = q_all.reshape(npair, 2, 2, 2, P, L * C)
    q_pair = jnp.einsum('qedipn,ek->qdiepkn', q_all, eye2).reshape(npair, 8 * P, 2 * L * C)
    a8 = jnp.stack([pw_r[L, 0], pw_i[L, 0], pw_r[L, 1], pw_i[L, 1]])
    a8 = jnp.transpose(a8.reshape(4, W_TILES, (S5_TILE_GROUPS // 2) * 2 * P), (1, 0, 2))
    dsk = jnp.tile(d_skip.astype(f32).reshape(G, 1, C), (1, 1, L)).reshape(npair, 1, 2 * L * C)
    bf = lambda m: m.astype(BF16)
    return bf(g_pair), bf(hf_pair), bf(hb_pair), bf(q_pair), a8, dsk


def kernel(x, p, norm_scale, w_in, lru_conv_w, lru_conv_b, lru_w_r, lru_b_r, lru_w_i, lru_b_i, lru_lambda, na_q_gain, na_k_gain, na_rel_bias, ssm_a_re, ssm_a_im, ssm_log_dt, ssm_b_re, ssm_b_im, ssm_c_re, ssm_c_im, ssm_d, ssm_glu_w, ssm_glu_b, pool_w, pool_scale, w_branch, w_out, ple_proj, ple_gate):
    b, s, d = x.shape
    w = BRANCH_W
    assert b == SUBLANES and d == D_MODEL and s % (LRU_STEPS * 4) == 0 and s // GRID_W >= NA_ROWS
    depth = w_in.shape[0]
    rows = s * b

    norm = norm_scale.astype(F32)[:, None, :]
    w_in16 = w_in.astype(BF16)
    ones_bd = _block_diag(jnp.ones((MXU_DIM // NA_HEAD_DIM, NA_HEAD_DIM, NA_HEAD_DIM), BF16))
    q_gain = (jnp.tile(na_q_gain.astype(F32), (1, NA_HEADS)) * (NA_HEAD_DIM ** -0.5 * LOG2E))[:, None, :]
    k_gain = jnp.tile(na_k_gain.astype(F32), (1, NA_HEADS))[:, None, :]
    conv_w = lru_conv_w.astype(F32)
    conv_b = lru_conv_b.astype(F32)[:, None, :]
    gate_w = (0.5 * jnp.concatenate([_block_diag(lru_w_r), _block_diag(lru_w_i)], axis=-1)).astype(BF16)
    gate_b = (0.5 * jnp.concatenate([lru_b_r, lru_b_i], axis=-1).astype(F32))[:, :, None, :]
    c8 = (-0.5 * LRU_C * LOG2E * jax.nn.softplus(-lru_lambda.astype(F32)))[:, :, None, :]
    bias_tab = jax.vmap(_attn_bias_tables)(na_rel_bias.astype(F32) * LOG2E)
    s5_mats = jax.vmap(_s5_matrices)(ssm_a_re, ssm_a_im, ssm_log_dt, ssm_b_re, ssm_b_im, ssm_c_re, ssm_c_im, ssm_d)
    glu_w = ssm_glu_w.astype(BF16)
    glu_b = ssm_glu_b.astype(F32)[:, None, :]
    pool_bd = _block_diag(pool_w).astype(BF16)
    pool_sc = pool_scale.astype(F32)[:, None, :]
    branch_scale = jnp.asarray([0.5] + [1.0] * (N_BRANCH - 1), F32)[None, :, None, None]
    wbr = (w_branch.astype(F32) * branch_scale).astype(BF16)
    wout = w_out.astype(BF16)
    pproj = ple_proj.astype(BF16)
    pgate = ple_gate.astype(BF16)

    xt = x
    for i in range(depth):
        xc, q, k, v, cx, dx = _inproj(xt, i, norm, w_in16, ones_bd, q_gain, k_gain, conv_w, conv_b, b)
        hf, hb = _lru(xc, i, gate_w, gate_b, c8)
        yb = _attn(q, k, v, i, bias_tab, b)
        yc = _s5(cx, i, s5_mats)
        yd = _pool(dx, i, pool_bd, pool_sc, s)
        xt = _merge(xt, p, i, hf, hb, yb, yc, yd, norm, w_in16, wbr, wout, pgate, pproj, glu_w, glu_b, b,
                    out_batch_major=(i == depth - 1))
    return xt
```

```python
import functools

import jax
import jax.numpy as jnp
import numpy as np
from jax import lax
from jax.experimental import pallas as pl
from jax.experimental.pallas import tpu as pltpu

F32 = jnp.float32
BF16 = jnp.bfloat16

D_MODEL = 1024
BRANCH_W = 512
N_BRANCH = 4
NORM_EPS = 1e-6
LOG2E = 1.4426950408889634
GRID_W = 64
LRU_C = 8.0
LRU_CONV_W = 4
LRU_HALO_STEPS = 2
NA_HEADS = 8
NA_HEAD_DIM = 64
NA_ROWS = 8
NA_COLS = 16
SSM_GROUP = 16
SSM_GROUPS = 32
SSM_STATE = 64
POOL_WINDOWS = (2, 4, 8, 16)
POOL_GROUP = 128

SUBLANES = 8
LANES = 128
MXU_DIM = 256
W_TILES = BRANCH_W // LANES
S5_CHUNK = 8

COL_AX, COL_AG, COL_Q, COL_K, COL_V, COL_BG, COL_CX, COL_CG, COL_DX, COL_DG = range(10)
COL_MERGE = 10 * BRANCH_W // D_MODEL

PROJ_ROWS = 1024
MERGE_ROWS = 512
LRU_STEPS = 256
POOL_STEPS = 256
ATTN_ROWS = 8
S5_BLOCK_CHUNKS = 128
ELEM_ROWS = 256

VMEM_LIMIT = 56 * 1024 * 1024


def _params(*sem):
    return pltpu.CompilerParams(dimension_semantics=sem, vmem_limit_bytes=VMEM_LIMIT)


def _dot(a, b):
    return jnp.dot(a, b, preferred_element_type=F32)


def _sigmoid(z):
    return 0.5 * jnp.tanh(0.5 * z) + 0.5


def _silu(z):
    return z * _sigmoid(z)


def _gelu_tanh(y):
    return 0.5 * y * (1.0 + jnp.tanh(0.7978845608028654 * (y + 0.044715 * (y * y * y))))


def _rms_rows(x, g):
    ms = jnp.mean(x * x, axis=-1, keepdims=True)
    return x * lax.rsqrt(ms + NORM_EPS) * g


def _head_mean_square(v, ones_ref):
    v2 = (v * v).astype(BF16)
    n = ones_ref.shape[0]
    sums = [_dot(v2[:, c:c + n], ones_ref[...]) for c in range(0, v.shape[1], n)]
    return jnp.concatenate(sums, axis=-1) * (1.0 / NA_HEAD_DIM)


def _layer_spec(shape, layer, *tail):
    tail = tail or (0,) * len(shape)
    return pl.BlockSpec((None,) + tuple(shape), lambda *_: (layer,) + tuple(tail))


S5_TILE_GROUPS = LANES // SSM_GROUP
S5_GROUP_FLAT = S5_CHUNK * SSM_GROUP


def _block_transpose(xs):
    n = len(xs)
    lane = lax.broadcasted_iota(jnp.int32, (1, LANES), 1)
    cur = list(xs)
    width, stride = LANES // 2, n // 2
    while stride >= 1:
        low = (lane & width) == 0
        nxt = list(cur)
        for i in range(n):
            if i & stride:
                continue
            a, b = cur[i], cur[i + stride]
            nxt[i] = jnp.where(low, a, pltpu.roll(b, width, 1))
            nxt[i + stride] = jnp.where(low, pltpu.roll(a, LANES - width, 1), b)
        cur = nxt
        width //= 2
        stride //= 2
    return cur


def _load_time_major(src_ref, slab_ref, s0=0, s1=None):
    nb, steps, n = src_ref.shape
    s1 = steps if s1 is None else s1
    tiles = n // LANES
    for b in range(nb):
        for l in range(tiles):
            slab_ref[l, pl.ds(b, s1 - s0, stride=nb), :] = src_ref[b, s0:s1, l * LANES:(l + 1) * LANES]
    return jnp.concatenate([slab_ref[l, 0:(s1 - s0) * nb, :] for l in range(tiles)], axis=-1)


def _store_batch_major(val, dst_ref, slab_ref):
    nb, steps, n = dst_ref.shape
    tiles = n // LANES
    for l in range(tiles):
        slab_ref[l] = val[:, l * LANES:(l + 1) * LANES]
    for b in range(nb):
        for l in range(tiles):
            dst_ref[b, :, l * LANES:(l + 1) * LANES] = slab_ref[l, pl.ds(b, steps, stride=nb), :]


def _inproj_body(x_ref, xp_ref, xn_ref, ns_ref, wax_ref, wq_ref, wk_ref, wv_ref, wcx_ref, wdx_ref, ones_ref,
                 qg_ref, kg_ref, cw_ref, cb_ref,
                 xc_ref, q_ref, k_ref, v_ref, cx_ref, dx_ref, stage_ref, *slabs):
    i = pl.program_id(0)
    halo = LRU_HALO_STEPS * SUBLANES
    if slabs:
        x = _load_time_major(x_ref, slabs[0])
        hs = xp_ref.shape[1]
        x_prev = _load_time_major(xp_ref, slabs[1], hs - LRU_HALO_STEPS, hs)
        x_next = _load_time_major(xn_ref, slabs[1], 0, LRU_HALO_STEPS)
    else:
        x = x_ref[...]
        x_prev = xp_ref[xp_ref.shape[0] - halo:, :]
        x_next = xn_ref[0:halo, :]
    hn = _rms_rows(x, ns_ref[...]).astype(BF16)
    steps = stage_ref.shape[1] // SUBLANES

    def to_batch_lanes(val, out_ref):
        for l in range(W_TILES):
            stage_ref[l] = val[:, l * LANES:(l + 1) * LANES]
        for b in range(SUBLANES):
            for l in range(W_TILES):
                c0 = b * BRANCH_W + l * LANES
                out_ref[:, c0:c0 + LANES] = stage_ref[l, pl.ds(b, steps, stride=SUBLANES), :].astype(BF16)

    hn_ext = jnp.concatenate([_rms_rows(x_prev, ns_ref[...]).astype(BF16), hn,
                              _rms_rows(x_next, ns_ref[...]).astype(BF16)], axis=0)
    ax = _dot(hn_ext, wax_ref[...])
    r = hn.shape[0]
    row = lax.broadcasted_iota(jnp.int32, (r + 2 * halo, 1), 0)
    outside = ((row < halo) & (i == 0)) | ((row >= r + halo) & (i == pl.num_programs(0) - 1))
    ax = jnp.where(outside, 0.0, ax)
    xc = cb_ref[...]
    for tap in range(LRU_CONV_W):
        xc = xc + cw_ref[tap:tap + 1, :] * ax[tap * SUBLANES:tap * SUBLANES + r, :]
    xc_ref[...] = xc.astype(BF16)
    q = _dot(hn, wq_ref[...])
    to_batch_lanes(q * lax.rsqrt(_head_mean_square(q, ones_ref) + NORM_EPS) * qg_ref[...], q_ref)
    k = _dot(hn, wk_ref[...])
    to_batch_lanes(k * lax.rsqrt(_head_mean_square(k, ones_ref) + NORM_EPS) * kg_ref[...], k_ref)
    to_batch_lanes(_dot(hn, wv_ref[...]), v_ref)
    cxv = _dot(hn, wcx_ref[...])
    nchunk = cxv.shape[0] // (S5_CHUNK * SUBLANES)
    for o in range(W_TILES):
        tile3 = cxv[:, o * LANES:(o + 1) * LANES].reshape(nchunk, S5_CHUNK * SUBLANES, LANES)
        per_t = [tile3[:, t * SUBLANES:(t + 1) * SUBLANES, :].reshape(nchunk * SUBLANES, LANES)
                 for t in range(S5_CHUNK)]
        for g, ug in enumerate(_block_transpose(per_t)):
            c0 = (o * S5_TILE_GROUPS + g) * S5_GROUP_FLAT
            cx_ref[:, c0:c0 + S5_GROUP_FLAT] = ug.astype(BF16)
    dx_ref[...] = _dot(hn, wdx_ref[...]).astype(BF16)


def _inproj(x, layer, norm_scale, w_in, ones_bd, q_gain, k_gain, conv_w, conv_b, batch):
    w = BRANCH_W
    tm = PROJ_ROWS
    batch_major = x.ndim == 3
    d = x.shape[-1]
    rows = x.shape[0] * x.shape[1] if batch_major else x.shape[0]
    s = rows // batch
    nt = rows // tm
    if batch_major:
        hs = SUBLANES
        per = tm // batch // hs
        x_spec = pl.BlockSpec((batch, tm // batch, d), lambda i: (0, i, 0))
        prev_spec = pl.BlockSpec((batch, hs, d), lambda i: (0, jnp.maximum(i * per - 1, 0), 0))
        next_spec = pl.BlockSpec((batch, hs, d), lambda i: (0, jnp.minimum((i + 1) * per, s // hs - 1), 0))
        slabs = [pltpu.VMEM((d // LANES, tm, LANES), F32),
                 pltpu.VMEM((d // LANES, LRU_HALO_STEPS * batch, LANES), F32)]
    else:
        hb = LRU_HALO_STEPS * batch
        per = tm // hb
        x_spec = pl.BlockSpec((tm, d), lambda i: (i, 0))
        prev_spec = pl.BlockSpec((hb, d), lambda i: (jnp.maximum(i * per - 1, 0), 0))
        next_spec = pl.BlockSpec((hb, d), lambda i: (jnp.minimum((i + 1) * per, rows // hb - 1), 0))
        slabs = []
    wcol = lambda j: pl.BlockSpec((None, d, w), lambda i: (layer, 0, j))
    tm_spec = pl.BlockSpec((tm, w), lambda i: (i, 0))
    bl_spec = pl.BlockSpec((tm // batch, batch * w), lambda i: (i, 0))
    tm_shape = jax.ShapeDtypeStruct((rows, w), BF16)
    bl_shape = jax.ShapeDtypeStruct((s, batch * w), BF16)
    gf_spec = pl.BlockSpec((tm // S5_CHUNK, S5_CHUNK * w), lambda i: (i, 0))
    gf_shape = jax.ShapeDtypeStruct((rows // S5_CHUNK, S5_CHUNK * w), BF16)
    return pl.pallas_call(
        _inproj_body,
        grid=(nt,),
        in_specs=[
            x_spec, prev_spec, next_spec,
            _layer_spec((1, d), layer),
            wcol(COL_AX), wcol(COL_Q), wcol(COL_K), wcol(COL_V), wcol(COL_CX), wcol(COL_DX),
            pl.BlockSpec(ones_bd.shape, lambda i: (0, 0)),
            _layer_spec((1, w), layer), _layer_spec((1, w), layer),
            _layer_spec((LRU_CONV_W, w), layer), _layer_spec((1, w), layer),
        ],
        out_specs=[tm_spec, bl_spec, bl_spec, bl_spec, gf_spec, tm_spec],
        out_shape=[tm_shape, bl_shape, bl_shape, bl_shape, gf_shape, tm_shape],
        scratch_shapes=[pltpu.VMEM((W_TILES, tm, LANES), F32)] + slabs,
        compiler_params=_params("parallel"),
        name="inproj",
    )(x, x, x, norm_scale, w_in, w_in, w_in, w_in, w_in, w_in, ones_bd, q_gain, k_gain, conv_w, conv_b)


def _fill_extended(ext_ref, main_ref, prev_ref, next_ref, n_prev, n_next, is_first, is_last):
    r = main_ref.shape[0]
    prev = prev_ref[...].astype(F32)
    ext_ref[0:n_prev, :] = jnp.where(is_first, 0.0, prev[prev.shape[0] - n_prev:, :])
    ext_ref[n_prev:n_prev + r, :] = main_ref[...].astype(F32)
    nxt = next_ref[...].astype(F32)
    ext_ref[n_prev + r:n_prev + r + n_next, :] = jnp.where(is_last, 0.0, nxt[0:n_next, :])


def _lru_body(xf_ref, xb_ref, wg_ref, bg_ref, ch_ref, hf_ref, hb_ref, af_ref, bf_ref, ab_ref, bb_ref, carry_ref):
    i = pl.program_id(0)
    w = BRANCH_W
    r = xf_ref.shape[0]
    steps = r // SUBLANES

    @pl.when(i == 0)
    def _():
        carry_ref[...] = jnp.zeros_like(carry_ref)

    def prepare(direction, x_ref, a_ref, b_ref):
        def sub(sb, _):
            rows = pl.ds(pl.multiple_of(sb * ELEM_ROWS, ELEM_ROWS), ELEM_ROWS)
            xcb = x_ref[rows, :]
            xc = xcb.astype(F32)
            g = _dot(xcb, wg_ref[direction]) + bg_ref[direction]
            tr = jnp.tanh(g[:, 0:w]) + 1.0
            ti = jnp.tanh(g[:, w:2 * w]) + 1.0
            a = jnp.exp2(ch_ref[direction] * tr)
            a_ref[rows, :] = a
            om = 1.0 - a * a
            b_ref[rows, :] = om * lax.rsqrt(jnp.maximum(om, 1e-37)) * (ti * xc)
            return 0

        lax.fori_loop(0, r // ELEM_ROWS, sub, 0, unroll=True)

    prepare(0, xf_ref, af_ref, bf_ref)
    prepare(1, xb_ref, ab_ref, bb_ref)

    def step(t, carry):
        hf, hb = carry
        rf = pl.ds(pl.multiple_of(t * SUBLANES, SUBLANES), SUBLANES)
        hf = af_ref[rf, :] * hf + bf_ref[rf, :]
        bf_ref[rf, :] = hf
        rb = pl.ds(pl.multiple_of((steps - 1 - t) * SUBLANES, SUBLANES), SUBLANES)
        hb = ab_ref[rb, :] * hb + bb_ref[rb, :]
        bb_ref[rb, :] = hb
        return hf, hb

    hf, hb = lax.fori_loop(0, steps, step, (carry_ref[0], carry_ref[1]), unroll=8)
    carry_ref[0] = hf
    carry_ref[1] = hb
    hf_ref[...] = bf_ref[...].astype(BF16)
    hb_ref[...] = bb_ref[...].astype(BF16)


def _lru(xc, layer, w_gate, b_gate, c8):
    rows, w = xc.shape
    r = LRU_STEPS * SUBLANES
    nt = rows // r
    fwd = pl.BlockSpec((r, w), lambda i: (i, 0))
    bwd = pl.BlockSpec((r, w), lambda i: (nt - 1 - i, 0))
    out_shape = jax.ShapeDtypeStruct((rows, w), BF16)
    return pl.pallas_call(
        _lru_body,
        grid=(nt,),
        in_specs=[
            fwd, bwd,
            _layer_spec((2, w, 2 * w), layer),
            _layer_spec((2, 1, 2 * w), layer),
            _layer_spec((2, 1, w), layer),
        ],
        out_specs=[fwd, bwd],
        out_shape=[out_shape, out_shape],
        scratch_shapes=[
            pltpu.VMEM((r, w), F32), pltpu.VMEM((r, w), F32),
            pltpu.VMEM((r, w), F32), pltpu.VMEM((r, w), F32),
            pltpu.VMEM((2, SUBLANES, w), F32),
        ],
        compiler_params=_params("arbitrary"),
        name="rglru",
    )(xc, xc, w_gate, b_gate, c8)


def _pool_body(m_ref, p_ref, n_ref, wp_ref, sc_ref, o_ref, ext_ref, pooled_ref, *, seq_len):
    i = pl.program_id(0)
    nt = pl.num_programs(0)
    r = m_ref.shape[0]
    steps = r // SUBLANES
    halo = max(POOL_WINDOWS) // 2
    hr = halo * SUBLANES
    _fill_extended(ext_ref, m_ref, p_ref, n_ref, hr, hr, i == 0, i == nt - 1)

    t_glob = i * steps + lax.broadcasted_iota(jnp.int32, (r, POOL_GROUP), 0) // SUBLANES
    for g, win in enumerate(POOL_WINDOWS):
        ls = slice(g * POOL_GROUP, (g + 1) * POOL_GROUP)
        e = ext_ref[:, ls]
        lo_t = -halo
        cur = e
        half = 1
        while half < win:
            n = cur.shape[0] - half * SUBLANES
            if half == 1:
                cur = cur[0:n, :] + cur[SUBLANES:SUBLANES + n, :]
                lo_t += 1
            else:
                sh = (half // 2) * SUBLANES
                cur = cur[0:n, :] + cur[2 * sh:2 * sh + n, :]
                lo_t += half // 2
            half *= 2
        off = (0 - lo_t) * SUBLANES
        wsum = cur[off:off + r, :]
        half_w = win // 2
        cnt = (jnp.clip(t_glob + half_w, 0, seq_len) - jnp.clip(t_glob - half_w, 0, seq_len)).astype(F32)
        pooled_ref[:, ls] = wsum / cnt - e[hr:hr + r, :]
    o_ref[...] = (_dot(pooled_ref[...].astype(BF16), wp_ref[...]) * sc_ref[...]).astype(BF16)


def _pool(dx, layer, w_pool_bd, scale, seq_len):
    rows, w = dx.shape
    r = POOL_STEPS * SUBLANES
    nt = rows // r
    hb = (max(POOL_WINDOWS) // 2) * SUBLANES
    per = r // hb
    last_hb = rows // hb - 1
    return pl.pallas_call(
        functools.partial(_pool_body, seq_len=seq_len),
        grid=(nt,),
        in_specs=[
            pl.BlockSpec((r, w), lambda i: (i, 0)),
            pl.BlockSpec((hb, w), lambda i: (jnp.maximum(i * per - 1, 0), 0)),
            pl.BlockSpec((hb, w), lambda i: (jnp.minimum((i + 1) * per, last_hb), 0)),
            _layer_spec((w, w), layer),
            _layer_spec((1, w), layer),
        ],
        out_specs=pl.BlockSpec((r, w), lambda i: (i, 0)),
        out_shape=jax.ShapeDtypeStruct((rows, w), BF16),
        scratch_shapes=[pltpu.VMEM((r + 2 * hb, w), F32), pltpu.VMEM((r, w), F32)],
        compiler_params=_params("parallel"),
        name="pool",
    )(dx, dx, dx, w_pool_bd, scale)


def _attn_body(q_ref, k_ref, v_ref, bias_ref, o_ref, *, grid_rows):
    g = pl.program_id(1)
    lane = lax.broadcasted_iota(jnp.int32, (GRID_W, LANES), 1)
    lo_half = lane < NA_HEAD_DIM
    nkeys = NA_ROWS * GRID_W

    def row_body(rr, _):
        r = g * ATTN_ROWS + rr
        rs = jnp.clip(r - NA_ROWS // 2, 0, grid_rows - NA_ROWS)
        d0 = rs - r + (NA_ROWS - 1)
        k0 = pl.multiple_of(rs * GRID_W, GRID_W)
        q0 = pl.multiple_of(rr * GRID_W, GRID_W)
        scores = []
        for hp in range(NA_HEADS // 2):
            ls = slice(hp * LANES, (hp + 1) * LANES)
            kp = k_ref[pl.ds(k0, nkeys), ls]
            qp = q_ref[pl.ds(q0, GRID_W), ls]
            zero = jnp.zeros_like(qp)
            qm = jnp.concatenate([jnp.where(lo_half, qp, zero), jnp.where(lo_half, zero, qp)], axis=0)
            sc = lax.dot_general(qm, kp, (((1,), (1,)), ((), ())), preferred_element_type=F32)
            bias = jnp.concatenate([bias_ref[hp, d0 + 2 * j] for j in range(NA_ROWS // 2)], axis=-1)
            scores.append(sc + bias)
        maxes = [jnp.max(sc, axis=-1, keepdims=True) for sc in scores]
        probs = [jnp.exp2(sc - m) for sc, m in zip(scores, maxes)]
        inv = [1.0 / jnp.sum(p, axis=-1, keepdims=True) for p in probs]
        for hp in range(NA_HEADS // 2):
            ls = slice(hp * LANES, (hp + 1) * LANES)
            vp = v_ref[pl.ds(k0, nkeys), ls]
            o = _dot(probs[hp].astype(BF16), vp) * inv[hp]
            o_ref[pl.ds(q0, GRID_W), ls] = jnp.where(lo_half, o[0:GRID_W], o[GRID_W:2 * GRID_W]).astype(BF16)
        return 0

    lax.fori_loop(0, ATTN_ROWS, row_body, 0, unroll=2)


def _attn(q, k, v, layer, bias_tab, batch):
    s, bw = q.shape
    w = bw // batch
    grid_rows = s // GRID_W
    ng = grid_rows // ATTN_ROWS
    qr = ATTN_ROWS * GRID_W
    q_spec = pl.BlockSpec((qr, w), lambda b, g: (g, b))
    kv_spec = pl.BlockSpec((s, w), lambda b, g: (0, b))
    return pl.pallas_call(
        functools.partial(_attn_body, grid_rows=grid_rows),
        grid=(batch, ng),
        in_specs=[q_spec, kv_spec, kv_spec, _layer_spec(bias_tab.shape[1:], layer)],
        out_specs=q_spec,
        out_shape=jax.ShapeDtypeStruct((s, bw), BF16),
        compiler_params=_params("parallel", "parallel"),
        name="natten",
    )(q, k, v, bias_tab)


def _s5_assemble(gc_ref, hfc_ref, hbc_ref, qc_ref, g_ref, hf_ref, hb_ref, q_ref):
    half = LANES // 2
    lo = lax.broadcasted_iota(jnp.int32, (S5_GROUP_FLAT, LANES), 1) < half
    g_ref[...] = jnp.zeros_like(g_ref)
    q_ref[...] = jnp.zeros_like(q_ref)
    for q in range(S5_TILE_GROUPS // 2):
        for e in range(2):
            g = 2 * q + e
            rows = slice(e * S5_GROUP_FLAT, (e + 1) * S5_GROUP_FLAT)
            g_ref[q, rows, e * S5_GROUP_FLAT:(e + 1) * S5_GROUP_FLAT] = gc_ref[g]
            for src, dst in ((hfc_ref, hf_ref), (hbc_ref, hb_ref)):
                h = src[g].astype(F32)
                swapped = pltpu.roll(h, half, 1)
                if e == 0:
                    re_tile, im_tile = jnp.where(lo, h, 0.0), jnp.where(lo, swapped, 0.0)
                else:
                    re_tile, im_tile = jnp.where(lo, 0.0, swapped), jnp.where(lo, 0.0, h)
                dst[q, rows, 0:LANES] = re_tile.astype(BF16)
                dst[q, rows, LANES:2 * LANES] = im_tile.astype(BF16)
            for blk in range(4):
                q_ref[q, blk * LANES + e * half:blk * LANES + (e + 1) * half,
                      e * S5_GROUP_FLAT:(e + 1) * S5_GROUP_FLAT] = qc_ref[g, blk * half:(blk + 1) * half, :]


def _s5_body(u_ref, gc_ref, hfc_ref, hbc_ref, qc_ref, a_ref, dsk_ref,
             y_ref, g_ref, hf_ref, hb_ref, q_ref, sinb_ref, sloc_ref, sin_ref, carry_ref):
    p = pl.program_id(1)
    j = pl.program_id(2)
    nblk = pl.num_programs(2)

    @pl.when((j == 0) & (p == 0))
    def _():
        _s5_assemble(gc_ref, hfc_ref, hbc_ref, qc_ref, g_ref, hf_ref, hb_ref, q_ref)

    rb = u_ref.shape[0]
    nk = rb // SUBLANES
    npairs = S5_TILE_GROUPS // 2
    pf = 2 * S5_GROUP_FLAT
    re_t = lambda q: slice(2 * q * LANES, (2 * q + 1) * LANES)
    im_t = lambda q: slice((2 * q + 1) * LANES, (2 * q + 2) * LANES)
    pair_t = lambda q: slice(q * pf, (q + 1) * pf)
    coef_t = lambda q: slice(q * LANES, (q + 1) * LANES)

    @pl.when(j == 0)
    def _():
        carry_ref[...] = jnp.zeros_like(carry_ref)

    def sweep(direction, reverse):
        a_re = a_ref[2 * direction:2 * direction + 1, :]
        a_im = a_ref[2 * direction + 1:2 * direction + 2, :]

        def body(n, state):
            kk = (nk - 1 - n) if reverse else n
            rows = pl.ds(pl.multiple_of(kk * SUBLANES, SUBLANES), SUBLANES)
            new = []
            for q in range(npairs):
                sr, si = state[2 * q], state[2 * q + 1]
                sin_ref[rows, re_t(q)] = sr
                sin_ref[rows, im_t(q)] = si
                ar, ai = a_re[:, coef_t(q)], a_im[:, coef_t(q)]
                new.append(ar * sr - ai * si + sloc_ref[rows, re_t(q)])
                new.append(ar * si + ai * sr + sloc_ref[rows, im_t(q)])
            return tuple(new)

        init = tuple(carry_ref[:, t * LANES:(t + 1) * LANES] for t in range(2 * npairs))
        out = lax.fori_loop(0, nk, body, init, unroll=4)
        for t in range(2 * npairs):
            carry_ref[:, t * LANES:(t + 1) * LANES] = out[t]

    @pl.when(p == 0)
    def _():
        for q in range(npairs):
            sloc_ref[:, pair_t(q)] = _dot(u_ref[:, pair_t(q)], hb_ref[q])
        sweep(1, True)
        blk = nblk - 1 - j
        sinb_ref[pl.ds(pl.multiple_of(blk * rb, rb), rb), :] = sin_ref[...].astype(BF16)

    @pl.when(p == 1)
    def _():
        for q in range(npairs):
            sloc_ref[:, pair_t(q)] = _dot(u_ref[:, pair_t(q)], hf_ref[q])
        sweep(0, False)
        rows_b = pl.ds(pl.multiple_of(j * rb, rb), rb)
        for q in range(npairs):
            u = u_ref[:, pair_t(q)]
            states = jnp.concatenate([sin_ref[:, pair_t(q)].astype(BF16), sinb_ref[rows_b, pair_t(q)]], axis=-1)
            y = _dot(u, g_ref[q]) + _dot(states, q_ref[q]) + dsk_ref[q] * u.astype(F32)
            y_ref[:, pair_t(q)] = y.astype(BF16)


def _s5(cx, layer, mats):
    g_m, hf_m, hb_m, q_m, a8, dsk = mats
    rows, width = cx.shape
    rb = S5_BLOCK_CHUNKS * SUBLANES
    nblk = rows // rb
    tile_w = S5_TILE_GROUPS * S5_GROUP_FLAT
    npairs, pf = S5_TILE_GROUPS // 2, 2 * S5_GROUP_FLAT
    state_w = S5_TILE_GROUPS * 2 * SSM_STATE
    ng, gf = S5_TILE_GROUPS, S5_GROUP_FLAT
    per_tile = lambda n, *shape: pl.BlockSpec((None, n) + shape, lambda o, p, j: (layer, o) + (0,) * len(shape))
    return pl.pallas_call(
        _s5_body,
        grid=(width // tile_w, 2, nblk),
        in_specs=[
            pl.BlockSpec((rb, tile_w), lambda o, p, j: (j + (1 - p) * (nblk - 1 - 2 * j), o)),
            per_tile(ng, gf, gf), per_tile(ng, gf, LANES), per_tile(ng, gf, LANES), per_tile(ng, 2 * LANES, gf),
            pl.BlockSpec((None, None, 4, npairs * LANES), lambda o, p, j: (layer, o, 0, 0)),
            per_tile(npairs, 1, pf),
        ],
        out_specs=pl.BlockSpec((rb, tile_w), lambda o, p, j: (p * j, o)),
        out_shape=jax.ShapeDtypeStruct((rows, width), BF16),
        scratch_shapes=[
            pltpu.VMEM((npairs, pf, pf), BF16),
            pltpu.VMEM((npairs, pf, 2 * LANES), BF16),
            pltpu.VMEM((npairs, pf, 2 * LANES), BF16),
            pltpu.VMEM((npairs, 4 * LANES, pf), BF16),
            pltpu.VMEM((rows, state_w), BF16),
            pltpu.VMEM((rb, state_w), F32),
            pltpu.VMEM((rb, state_w), F32),
            pltpu.VMEM((SUBLANES, state_w), F32),
        ],
        compiler_params=_params("arbitrary", "arbitrary", "arbitrary"),
        name="s5",
    )(cx, g_m, hf_m, hb_m, q_m, a8, dsk)


def _merge_body(x_ref, p_ref, hf_ref, hb_ref, yb_ref, yc_ref, yd_ref, ns_ref,
                wag_ref, wbg_ref, wcg_ref, wdg_ref, wm0_ref, wm1_ref, wm2_ref, wm3_ref,
                wb0_ref, wb1_ref, wb2_ref, wb3_ref, wout_ref, pg_ref, pp_ref, gluw_ref, glub_ref,
                o_ref, stage_ref, pslab_ref, xslab_ref, *, x_batch_major, out_batch_major):
    x = _load_time_major(x_ref, xslab_ref) if x_batch_major else x_ref[...]
    hn = _rms_rows(x, ns_ref[...]).astype(BF16)

    def branch(y, wb_ref, wm_ref):
        return _dot(y.astype(BF16), wb_ref[...]) * _sigmoid(_dot(hn, wm_ref[...]))

    ya = (hf_ref[...].astype(F32) + hb_ref[...].astype(F32)) * _silu(_dot(hn, wag_ref[...]))
    merged = branch(ya, wb0_ref, wm0_ref)
    steps = stage_ref.shape[1] // SUBLANES
    for b in range(SUBLANES):
        for l in range(W_TILES):
            c0 = b * BRANCH_W + l * LANES
            stage_ref[l, pl.ds(b, steps, stride=SUBLANES), :] = yb_ref[:, c0:c0 + LANES].astype(F32)
    yb = jnp.concatenate([stage_ref[l] for l in range(W_TILES)], axis=-1)
    merged = merged + branch(yb * _silu(_dot(hn, wbg_ref[...])), wb1_ref, wm1_ref)
    nchunk = yc_ref.shape[0] // SUBLANES
    tiles = []
    for o in range(W_TILES):
        per_g = [yc_ref[:, (o * S5_TILE_GROUPS + g) * S5_GROUP_FLAT:(o * S5_TILE_GROUPS + g + 1) * S5_GROUP_FLAT]
                 .astype(F32) for g in range(S5_TILE_GROUPS)]
        per_t = [v.reshape(nchunk, 1, SUBLANES, LANES) for v in _block_transpose(per_g)]
        tiles.append(jnp.concatenate(per_t, axis=1).reshape(nchunk * S5_CHUNK * SUBLANES, LANES))
    yg = _gelu_tanh(jnp.concatenate(tiles, axis=-1))
    yc = yg * _sigmoid(_dot(yg.astype(BF16), gluw_ref[...]) + glub_ref[...])
    merged = merged + branch(yc * _silu(_dot(hn, wcg_ref[...])), wb2_ref, wm2_ref)
    merged = merged + branch(yd_ref[...].astype(F32) * _silu(_dot(hn, wdg_ref[...])), wb3_ref, wm3_ref)
    x1 = x + _dot(merged.astype(BF16), wout_ref[...])
    emb = _dot(_load_time_major(p_ref, pslab_ref).astype(BF16), pp_ref[...])
    out = x1 + _sigmoid(_dot(x1.astype(BF16), pg_ref[...])) * emb
    if out_batch_major:
        _store_batch_major(out, o_ref, xslab_ref)
    else:
        o_ref[...] = out


def _merge(x, p, layer, hf, hb, yb, yc, yd, norm_scale, w_in, wbr, wout, pgate, pproj, glu_w, glu_b, batch,
           out_batch_major):
    w = BRANCH_W
    tm = MERGE_ROWS
    pdim = p.shape[-1]
    once = pl.Buffered(1)
    x_batch_major = x.ndim == 3
    d = x.shape[-1]
    rows = x.shape[0] * x.shape[1] if x_batch_major else x.shape[0]
    steps = tm // batch
    bm_spec = pl.BlockSpec((batch, steps, d), lambda i: (0, i, 0))
    x_spec = bm_spec if x_batch_major else pl.BlockSpec((tm, d), lambda i: (i, 0))
    if out_batch_major:
        out_spec, out_shape = bm_spec, jax.ShapeDtypeStruct((batch, rows // batch, d), F32)
    else:
        out_spec, out_shape = pl.BlockSpec((tm, d), lambda i: (i, 0)), jax.ShapeDtypeStruct((rows, d), F32)

    def resident(shape, *tail):
        tail = tail or (0,) * len(shape)
        return pl.BlockSpec((None,) + tuple(shape), lambda i: (layer,) + tuple(tail), pipeline_mode=once)

    row_spec = lambda n: pl.BlockSpec((tm, n), lambda i: (i, 0))
    gate_cols = [resident((d, w), 0, j) for j in (COL_AG, COL_BG, COL_CG, COL_DG)]
    merge_cols = [resident((d, d), 0, COL_MERGE + n) for n in range(N_BRANCH)]
    return pl.pallas_call(
        functools.partial(_merge_body, x_batch_major=x_batch_major, out_batch_major=out_batch_major),
        grid=(rows // tm,),
        in_specs=[
            x_spec,
            pl.BlockSpec((None, batch, steps, pdim), lambda i: (layer, 0, i, 0)),
            row_spec(w), row_spec(w),
            pl.BlockSpec((tm // batch, batch * w), lambda i: (i, 0)),
            pl.BlockSpec((tm // S5_CHUNK, S5_CHUNK * w), lambda i: (i, 0)), row_spec(w),
            resident((1, d)),
            *gate_cols, *merge_cols,
            *[pl.BlockSpec((None, None, w, d), lambda i, n=n: (layer, n, 0, 0), pipeline_mode=once)
              for n in range(N_BRANCH)],
            resident((d, d)), resident((d, d)), resident((pdim, d)),
            resident((w, w)), resident((1, w)),
        ],
        out_specs=out_spec,
        out_shape=out_shape,
        scratch_shapes=[pltpu.VMEM((W_TILES, tm, LANES), F32),
                        pltpu.VMEM((pdim // LANES, tm, LANES), F32),
                        pltpu.VMEM((d // LANES, tm, LANES), F32)],
        compiler_params=_params("parallel"),
        name="merge",
    )(x, p, hf, hb, yb, yc, yd, norm_scale, *([w_in] * 8), *([wbr] * N_BRANCH), wout, pgate, pproj, glu_w, glu_b)


def _block_diag(blocks):
    n, r, c = blocks.shape[-3:]
    spread = np.tile(np.eye(c, dtype=np.float32), (1, n))
    keep = np.kron(np.eye(n, dtype=np.float32), np.ones((r, c), np.float32))
    rows = blocks.reshape(blocks.shape[:-3] + (n * r, c))
    full = jnp.einsum('...rc,cm->...rm', rows, jnp.asarray(spread, blocks.dtype), precision=lax.Precision.HIGHEST)
    return full * jnp.asarray(keep, blocks.dtype)


def _attn_bias_tables(rpb):
    qc = np.arange(GRID_W)[:, None]
    kc = np.arange(GRID_W)[None, :]
    ws = np.clip(qc - NA_COLS // 2, 0, GRID_W - NA_COLS)
    in_win = (kc >= ws) & (kc < ws + NA_COLS)
    dc = np.clip(kc - qc, -(NA_COLS - 1), NA_COLS - 1) + NA_COLS - 1
    onehot = ((dc[None] == np.arange(2 * NA_COLS - 1)[:, None, None]) & in_win[None]).astype(np.float32)
    tab = jnp.einsum('hrm,mqk->hrqk', rpb.astype(F32), jnp.asarray(onehot), precision=lax.Precision.HIGHEST)
    tab = tab + jnp.asarray(np.where(in_win, 0.0, -1e30).astype(np.float32))
    pairs = jnp.concatenate([tab[:, :-1], tab[:, 1:]], axis=-1)
    nr = pairs.shape[1]
    pairs = pairs.reshape(NA_HEADS // 2, 2, nr, GRID_W, 2 * GRID_W)
    return jnp.transpose(pairs, (0, 2, 1, 3, 4)).reshape(NA_HEADS // 2, nr, 2 * GRID_W, 2 * GRID_W)


def _s5_matrices(a_re, a_im, log_dt, b_re, b_im, c_re, c_im, d_skip):
    L, G, P, C = S5_CHUNK, SSM_GROUPS, SSM_STATE, SSM_GROUP
    f32 = F32
    hi = lax.Precision.HIGHEST
    lr = jnp.minimum(a_re.astype(f32), -1e-4)
    li = a_im.astype(f32)
    dt = jnp.exp(log_dt.astype(f32))[..., None]
    steps_n = jnp.arange(L + 1, dtype=f32)[:, None, None, None]
    pw_mag = jnp.exp(steps_n * (lr * dt))
    pw_r = pw_mag * jnp.cos(steps_n * (li * dt))
    pw_i = pw_mag * jnp.sin(steps_n * (li * dt))
    ab_r, ab_i = pw_r[1], pw_i[1]
    nr = ab_r - 1.0
    den = lr * lr + li * li
    fr = ((nr * lr + ab_i * li) / den)[..., None]
    fi = ((ab_i * lr - nr * li) / den)[..., None]
    br, bi = b_re.astype(f32), b_im.astype(f32)
    bb_r = fr * br - fi * bi
    bb_i = fr * bi + fi * br
    cr, ci = c_re.astype(f32), c_im.astype(f32)
    m_r = pw_r[..., None] * bb_r[None] - pw_i[..., None] * bb_i[None]
    m_i = pw_r[..., None] * bb_i[None] + pw_i[..., None] * bb_r[None]
    k_lag = (jnp.einsum('dgop,ndgpi->ndgio', cr, m_r[:L], precision=hi)
             - jnp.einsum('dgop,ndgpi->ndgio', ci, m_i[:L], precision=hi))
    lag = np.arange(L)[None, :] - np.arange(L)[:, None]
    pick_f = (lag[None] == np.arange(L)[:, None, None]).astype(np.float32)
    pick_b = (-lag[None] == np.arange(L)[:, None, None]).astype(np.float32)
    g_mat = (jnp.einsum('ngio,nst->gsito', k_lag[:, 0], jnp.asarray(pick_f), precision=hi)
             + jnp.einsum('ngio,nst->gsito', k_lag[:, 1], jnp.asarray(pick_b), precision=hi)).reshape(G, L * C, L * C)

    def h_mat(direction, powers):
        both = jnp.stack([m_r[powers, direction], m_i[powers, direction]])
        return jnp.transpose(both, (2, 1, 4, 0, 3)).reshape(G, L * C, 2 * P)

    hf_mat = h_mat(0, np.arange(L - 1, -1, -1))
    hb_mat = h_mat(1, np.arange(L))

    def q_rows(direction, powers):
        pwr = pw_r[powers, direction]
        pwi = pw_i[powers, direction]
        c_r, c_i = cr[direction], ci[direction]
        q_re = jnp.einsum('gcp,tgp->gptc', c_r, pwr) - jnp.einsum('gcp,tgp->gptc', c_i, pwi)
        q_im = -(jnp.einsum('gcp,tgp->gptc', c_r, pwi) + jnp.einsum('gcp,tgp->gptc', c_i, pwr))
        return jnp.stack([q_re, q_im], axis=1).reshape(G, 2 * P, L * C)

    q_mat = jnp.concatenate([q_rows(0, np.arange(1, L + 1)), q_rows(1, np.arange(L, 0, -1))], axis=1)
    npair = G // 2
    a8 = jnp.stack([pw_r[L, 0], pw_i[L, 0], pw_r[L, 1], pw_i[L, 1]])
    a8 = jnp.transpose(a8.reshape(4, W_TILES, (S5_TILE_GROUPS // 2) * 2 * P), (1, 0, 2))
    dsk = jnp.tile(d_skip.astype(f32).reshape(G, 1, C), (1, 1, L)).reshape(npair, 1, 2 * L * C)
    bf = lambda m: m.astype(BF16)
    return bf(g_mat), bf(hf_mat), bf(hb_mat), bf(q_mat), a8, dsk


def kernel(x, p, norm_scale, w_in, lru_conv_w, lru_conv_b, lru_w_r, lru_b_r, lru_w_i, lru_b_i, lru_lambda, na_q_gain, na_k_gain, na_rel_bias, ssm_a_re, ssm_a_im, ssm_log_dt, ssm_b_re, ssm_b_im, ssm_c_re, ssm_c_im, ssm_d, ssm_glu_w, ssm_glu_b, pool_w, pool_scale, w_branch, w_out, ple_proj, ple_gate):
    b, s, d = x.shape
    w = BRANCH_W
    assert b == SUBLANES and d == D_MODEL and s % (LRU_STEPS * 4) == 0 and s // GRID_W >= NA_ROWS
    depth = w_in.shape[0]
    rows = s * b

    norm = norm_scale.astype(F32)[:, None, :]
    w_in16 = w_in.astype(BF16)
    ones_bd = _block_diag(jnp.ones((MXU_DIM // NA_HEAD_DIM, NA_HEAD_DIM, NA_HEAD_DIM), BF16))
    q_gain = (jnp.tile(na_q_gain.astype(F32), (1, NA_HEADS)) * (NA_HEAD_DIM ** -0.5 * LOG2E))[:, None, :]
    k_gain = jnp.tile(na_k_gain.astype(F32), (1, NA_HEADS))[:, None, :]
    conv_w = lru_conv_w.astype(F32)
    conv_b = lru_conv_b.astype(F32)[:, None, :]
    gate_w = (0.5 * jnp.concatenate([_block_diag(lru_w_r), _block_diag(lru_w_i)], axis=-1)).astype(BF16)
    gate_b = (0.5 * jnp.concatenate([lru_b_r, lru_b_i], axis=-1).astype(F32))[:, :, None, :]
    c8 = (-0.5 * LRU_C * LOG2E * jax.nn.softplus(-lru_lambda.astype(F32)))[:, :, None, :]
    bias_tab = jax.vmap(_attn_bias_tables)(na_rel_bias.astype(F32) * LOG2E)
    s5_mats = jax.vmap(_s5_matrices)(ssm_a_re, ssm_a_im, ssm_log_dt, ssm_b_re, ssm_b_im, ssm_c_re, ssm_c_im, ssm_d)
    glu_w = ssm_glu_w.astype(BF16)
    glu_b = ssm_glu_b.astype(F32)[:, None, :]
    pool_bd = _block_diag(pool_w).astype(BF16)
    pool_sc = pool_scale.astype(F32)[:, None, :]
    branch_scale = jnp.asarray([0.5] + [1.0] * (N_BRANCH - 1), F32)[None, :, None, None]
    wbr = (w_branch.astype(F32) * branch_scale).astype(BF16)
    wout = w_out.astype(BF16)
    pproj = ple_proj.astype(BF16)
    pgate = ple_gate.astype(BF16)

    xt = x
    for i in range(depth):
        xc, q, k, v, cx, dx = _inproj(xt, i, norm, w_in16, ones_bd, q_gain, k_gain, conv_w, conv_b, b)
        hf, hb = _lru(xc, i, gate_w, gate_b, c8)
        yb = _attn(q, k, v, i, bias_tab, b)
        yc = _s5(cx, i, s5_mats)
        yd = _pool(dx, i, pool_bd, pool_sc, s)
        xt = _merge(xt, p, i, hf, hb, yb, yc, yd, norm, w_in16, wbr, wout, pgate, pproj, glu_w, glu_b, b,
                    out_batch_major=(i == depth - 1))
    return xt
```

```python
import functools

import jax
import jax.numpy as jnp
import numpy as np
from jax import lax
from jax.experimental import pallas as pl
from jax.experimental.pallas import tpu as pltpu

F32 = jnp.float32
BF16 = jnp.bfloat16

D_MODEL = 1024
BRANCH_W = 512
N_BRANCH = 4
NORM_EPS = 1e-6
LOG2E = 1.4426950408889634
GRID_W = 64
LRU_C = 8.0
LRU_CONV_W = 4
LRU_HALO_STEPS = 2
NA_HEADS = 8
NA_HEAD_DIM = 64
NA_ROWS = 8
NA_COLS = 16
SSM_GROUP = 16
SSM_GROUPS = 32
SSM_STATE = 64
POOL_WINDOWS = (2, 4, 8, 16)
POOL_GROUP = 128

SUBLANES = 8
LANES = 128
MXU_DIM = 256
W_TILES = BRANCH_W // LANES
S5_CHUNK = 8

COL_AX, COL_AG, COL_Q, COL_K, COL_V, COL_BG, COL_CX, COL_CG, COL_DX, COL_DG = range(10)
COL_MERGE = 10 * BRANCH_W // D_MODEL

PROJ_ROWS = 1024
MERGE_ROWS = 512
LRU_STEPS = 256
POOL_STEPS = 256
ATTN_ROWS = 16
S5_BLOCK_CHUNKS = 128
ELEM_ROWS = 256

VMEM_LIMIT = 56 * 1024 * 1024


def _params(*sem):
    return pltpu.CompilerParams(dimension_semantics=sem, vmem_limit_bytes=VMEM_LIMIT)


def _dot(a, b):
    return jnp.dot(a, b, preferred_element_type=F32)


def _sigmoid(z):
    return 0.5 * jnp.tanh(0.5 * z) + 0.5


def _silu(z):
    return z * _sigmoid(z)


def _gelu_tanh(y):
    return 0.5 * y * (1.0 + jnp.tanh(0.7978845608028654 * (y + 0.044715 * (y * y * y))))


def _rms_rows(x, g):
    ms = jnp.mean(x * x, axis=-1, keepdims=True)
    return x * lax.rsqrt(ms + NORM_EPS) * g


def _head_mean_square(v, ones_ref):
    v2 = (v * v).astype(BF16)
    n = ones_ref.shape[0]
    sums = [_dot(v2[:, c:c + n], ones_ref[...]) for c in range(0, v.shape[1], n)]
    return jnp.concatenate(sums, axis=-1) * (1.0 / NA_HEAD_DIM)


def _layer_spec(shape, layer, *tail):
    tail = tail or (0,) * len(shape)
    return pl.BlockSpec((None,) + tuple(shape), lambda *_: (layer,) + tuple(tail))


S5_TILE_GROUPS = LANES // SSM_GROUP
S5_GROUP_FLAT = S5_CHUNK * SSM_GROUP


def _block_transpose(xs):
    n = len(xs)
    lane = lax.broadcasted_iota(jnp.int32, (1, LANES), 1)
    cur = list(xs)
    width, stride = LANES // 2, n // 2
    while stride >= 1:
        low = (lane & width) == 0
        nxt = list(cur)
        for i in range(n):
            if i & stride:
                continue
            a, b = cur[i], cur[i + stride]
            nxt[i] = jnp.where(low, a, pltpu.roll(b, width, 1))
            nxt[i + stride] = jnp.where(low, pltpu.roll(a, LANES - width, 1), b)
        cur = nxt
        width //= 2
        stride //= 2
    return cur


def _load_time_major(src_ref, slab_ref, s0=0, s1=None):
    nb, steps, n = src_ref.shape
    s1 = steps if s1 is None else s1
    tiles = n // LANES
    for b in range(nb):
        for l in range(tiles):
            slab_ref[l, pl.ds(b, s1 - s0, stride=nb), :] = src_ref[b, s0:s1, l * LANES:(l + 1) * LANES]
    return jnp.concatenate([slab_ref[l, 0:(s1 - s0) * nb, :] for l in range(tiles)], axis=-1)


def _store_batch_major(val, dst_ref, slab_ref):
    nb, steps, n = dst_ref.shape
    tiles = n // LANES
    for l in range(tiles):
        slab_ref[l] = val[:, l * LANES:(l + 1) * LANES]
    for b in range(nb):
        for l in range(tiles):
            dst_ref[b, :, l * LANES:(l + 1) * LANES] = slab_ref[l, pl.ds(b, steps, stride=nb), :]


def _inproj_body(x_ref, xp_ref, xn_ref, ns_ref, wax_ref, wq_ref, wk_ref, wv_ref, wcx_ref, wdx_ref, ones_ref,
                 qg_ref, kg_ref, cw_ref, cb_ref,
                 xc_ref, q_ref, k_ref, v_ref, cx_ref, dx_ref, stage_ref, *slabs):
    i = pl.program_id(0)
    halo = LRU_HALO_STEPS * SUBLANES
    if slabs:
        x = _load_time_major(x_ref, slabs[0])
        hs = xp_ref.shape[1]
        x_prev = _load_time_major(xp_ref, slabs[1], hs - LRU_HALO_STEPS, hs)
        x_next = _load_time_major(xn_ref, slabs[1], 0, LRU_HALO_STEPS)
    else:
        x = x_ref[...]
        x_prev = xp_ref[xp_ref.shape[0] - halo:, :]
        x_next = xn_ref[0:halo, :]
    hn = _rms_rows(x, ns_ref[...]).astype(BF16)
    steps = stage_ref.shape[1] // SUBLANES

    def to_batch_lanes(val, out_ref):
        for l in range(W_TILES):
            stage_ref[l] = val[:, l * LANES:(l + 1) * LANES]
        for b in range(SUBLANES):
            for l in range(W_TILES):
                c0 = b * BRANCH_W + l * LANES
                out_ref[:, c0:c0 + LANES] = stage_ref[l, pl.ds(b, steps, stride=SUBLANES), :].astype(BF16)

    hn_ext = jnp.concatenate([_rms_rows(x_prev, ns_ref[...]).astype(BF16), hn,
                              _rms_rows(x_next, ns_ref[...]).astype(BF16)], axis=0)
    ax = _dot(hn_ext, wax_ref[...])
    r = hn.shape[0]
    row = lax.broadcasted_iota(jnp.int32, (r + 2 * halo, 1), 0)
    outside = ((row < halo) & (i == 0)) | ((row >= r + halo) & (i == pl.num_programs(0) - 1))
    ax = jnp.where(outside, 0.0, ax)
    xc = cb_ref[...]
    for tap in range(LRU_CONV_W):
        xc = xc + cw_ref[tap:tap + 1, :] * ax[tap * SUBLANES:tap * SUBLANES + r, :]
    xc_ref[...] = xc.astype(BF16)
    q = _dot(hn, wq_ref[...])
    to_batch_lanes(q * lax.rsqrt(_head_mean_square(q, ones_ref) + NORM_EPS) * qg_ref[...], q_ref)
    k = _dot(hn, wk_ref[...])
    to_batch_lanes(k * lax.rsqrt(_head_mean_square(k, ones_ref) + NORM_EPS) * kg_ref[...], k_ref)
    to_batch_lanes(_dot(hn, wv_ref[...]), v_ref)
    cxv = _dot(hn, wcx_ref[...])
    nchunk = cxv.shape[0] // (S5_CHUNK * SUBLANES)
    for o in range(W_TILES):
        tile3 = cxv[:, o * LANES:(o + 1) * LANES].reshape(nchunk, S5_CHUNK * SUBLANES, LANES)
        per_t = [tile3[:, t * SUBLANES:(t + 1) * SUBLANES, :].reshape(nchunk * SUBLANES, LANES)
                 for t in range(S5_CHUNK)]
        for g, ug in enumerate(_block_transpose(per_t)):
            c0 = (o * S5_TILE_GROUPS + g) * S5_GROUP_FLAT
            cx_ref[:, c0:c0 + S5_GROUP_FLAT] = ug.astype(BF16)
    dx_ref[...] = _dot(hn, wdx_ref[...]).astype(BF16)


def _inproj(x, layer, norm_scale, w_in, ones_bd, q_gain, k_gain, conv_w, conv_b, batch):
    w = BRANCH_W
    tm = PROJ_ROWS
    batch_major = x.ndim == 3
    d = x.shape[-1]
    rows = x.shape[0] * x.shape[1] if batch_major else x.shape[0]
    s = rows // batch
    nt = rows // tm
    if batch_major:
        hs = SUBLANES
        per = tm // batch // hs
        x_spec = pl.BlockSpec((batch, tm // batch, d), lambda i: (0, i, 0))
        prev_spec = pl.BlockSpec((batch, hs, d), lambda i: (0, jnp.maximum(i * per - 1, 0), 0))
        next_spec = pl.BlockSpec((batch, hs, d), lambda i: (0, jnp.minimum((i + 1) * per, s // hs - 1), 0))
        slabs = [pltpu.VMEM((d // LANES, tm, LANES), F32),
                 pltpu.VMEM((d // LANES, LRU_HALO_STEPS * batch, LANES), F32)]
    else:
        hb = LRU_HALO_STEPS * batch
        per = tm // hb
        x_spec = pl.BlockSpec((tm, d), lambda i: (i, 0))
        prev_spec = pl.BlockSpec((hb, d), lambda i: (jnp.maximum(i * per - 1, 0), 0))
        next_spec = pl.BlockSpec((hb, d), lambda i: (jnp.minimum((i + 1) * per, rows // hb - 1), 0))
        slabs = []
    wcol = lambda j: pl.BlockSpec((None, d, w), lambda i: (layer, 0, j))
    tm_spec = pl.BlockSpec((tm, w), lambda i: (i, 0))
    bl_spec = pl.BlockSpec((tm // batch, batch * w), lambda i: (i, 0))
    tm_shape = jax.ShapeDtypeStruct((rows, w), BF16)
    bl_shape = jax.ShapeDtypeStruct((s, batch * w), BF16)
    gf_spec = pl.BlockSpec((tm // S5_CHUNK, S5_CHUNK * w), lambda i: (i, 0))
    gf_shape = jax.ShapeDtypeStruct((rows // S5_CHUNK, S5_CHUNK * w), BF16)
    return pl.pallas_call(
        _inproj_body,
        grid=(nt,),
        in_specs=[
            x_spec, prev_spec, next_spec,
            _layer_spec((1, d), layer),
            wcol(COL_AX), wcol(COL_Q), wcol(COL_K), wcol(COL_V), wcol(COL_CX), wcol(COL_DX),
            pl.BlockSpec(ones_bd.shape, lambda i: (0, 0)),
            _layer_spec((1, w), layer), _layer_spec((1, w), layer),
            _layer_spec((LRU_CONV_W, w), layer), _layer_spec((1, w), layer),
        ],
        out_specs=[tm_spec, bl_spec, bl_spec, bl_spec, gf_spec, tm_spec],
        out_shape=[tm_shape, bl_shape, bl_shape, bl_shape, gf_shape, tm_shape],
        scratch_shapes=[pltpu.VMEM((W_TILES, tm, LANES), F32)] + slabs,
        compiler_params=_params("parallel"),
        name="inproj",
    )(x, x, x, norm_scale, w_in, w_in, w_in, w_in, w_in, w_in, ones_bd, q_gain, k_gain, conv_w, conv_b)


def _fill_extended(ext_ref, main_ref, prev_ref, next_ref, n_prev, n_next, is_first, is_last):
    r = main_ref.shape[0]
    prev = prev_ref[...].astype(F32)
    ext_ref[0:n_prev, :] = jnp.where(is_first, 0.0, prev[prev.shape[0] - n_prev:, :])
    ext_ref[n_prev:n_prev + r, :] = main_ref[...].astype(F32)
    nxt = next_ref[...].astype(F32)
    ext_ref[n_prev + r:n_prev + r + n_next, :] = jnp.where(is_last, 0.0, nxt[0:n_next, :])


def _lru_body(xf_ref, xb_ref, wg_ref, bg_ref, ch_ref, hf_ref, hb_ref, af_ref, bf_ref, ab_ref, bb_ref, carry_ref):
    i = pl.program_id(0)
    w = BRANCH_W
    r = xf_ref.shape[0]
    steps = r // SUBLANES

    @pl.when(i == 0)
    def _():
        carry_ref[...] = jnp.zeros_like(carry_ref)

    def prepare(direction, x_ref, a_ref, b_ref):
        def sub(sb, _):
            rows = pl.ds(pl.multiple_of(sb * ELEM_ROWS, ELEM_ROWS), ELEM_ROWS)
            xcb = x_ref[rows, :]
            xc = xcb.astype(F32)
            g = _dot(xcb, wg_ref[direction]) + bg_ref[direction]
            tr = jnp.tanh(g[:, 0:w]) + 1.0
            ti = jnp.tanh(g[:, w:2 * w]) + 1.0
            a = jnp.exp2(ch_ref[direction] * tr)
            a_ref[rows, :] = a
            om = 1.0 - a * a
            b_ref[rows, :] = om * lax.rsqrt(jnp.maximum(om, 1e-37)) * (ti * xc)
            return 0

        lax.fori_loop(0, r // ELEM_ROWS, sub, 0, unroll=True)

    prepare(0, xf_ref, af_ref, bf_ref)
    prepare(1, xb_ref, ab_ref, bb_ref)

    def step(t, carry):
        hf, hb = carry
        rf = pl.ds(pl.multiple_of(t * SUBLANES, SUBLANES), SUBLANES)
        hf = af_ref[rf, :] * hf + bf_ref[rf, :]
        bf_ref[rf, :] = hf
        rb = pl.ds(pl.multiple_of((steps - 1 - t) * SUBLANES, SUBLANES), SUBLANES)
        hb = ab_ref[rb, :] * hb + bb_ref[rb, :]
        bb_ref[rb, :] = hb
        return hf, hb

    hf, hb = lax.fori_loop(0, steps, step, (carry_ref[0], carry_ref[1]), unroll=8)
    carry_ref[0] = hf
    carry_ref[1] = hb
    hf_ref[...] = bf_ref[...].astype(BF16)
    hb_ref[...] = bb_ref[...].astype(BF16)


def _lru(xc, layer, w_gate, b_gate, c8):
    rows, w = xc.shape
    r = LRU_STEPS * SUBLANES
    nt = rows // r
    fwd = pl.BlockSpec((r, w), lambda i: (i, 0))
    bwd = pl.BlockSpec((r, w), lambda i: (nt - 1 - i, 0))
    out_shape = jax.ShapeDtypeStruct((rows, w), BF16)
    return pl.pallas_call(
        _lru_body,
        grid=(nt,),
        in_specs=[
            fwd, bwd,
            _layer_spec((2, w, 2 * w), layer),
            _layer_spec((2, 1, 2 * w), layer),
            _layer_spec((2, 1, w), layer),
        ],
        out_specs=[fwd, bwd],
        out_shape=[out_shape, out_shape],
        scratch_shapes=[
            pltpu.VMEM((r, w), F32), pltpu.VMEM((r, w), F32),
            pltpu.VMEM((r, w), F32), pltpu.VMEM((r, w), F32),
            pltpu.VMEM((2, SUBLANES, w), F32),
        ],
        compiler_params=_params("arbitrary"),
        name="rglru",
    )(xc, xc, w_gate, b_gate, c8)


def _pool_body(m_ref, p_ref, n_ref, wp_ref, sc_ref, o_ref, ext_ref, pooled_ref, *, seq_len):
    i = pl.program_id(0)
    nt = pl.num_programs(0)
    r = m_ref.shape[0]
    steps = r // SUBLANES
    halo = max(POOL_WINDOWS) // 2
    hr = halo * SUBLANES
    _fill_extended(ext_ref, m_ref, p_ref, n_ref, hr, hr, i == 0, i == nt - 1)

    t_glob = i * steps + lax.broadcasted_iota(jnp.int32, (r, POOL_GROUP), 0) // SUBLANES
    for g, win in enumerate(POOL_WINDOWS):
        ls = slice(g * POOL_GROUP, (g + 1) * POOL_GROUP)
        e = ext_ref[:, ls]
        lo_t = -halo
        cur = e
        half = 1
        while half < win:
            n = cur.shape[0] - half * SUBLANES
            if half == 1:
                cur = cur[0:n, :] + cur[SUBLANES:SUBLANES + n, :]
                lo_t += 1
            else:
                sh = (half // 2) * SUBLANES
                cur = cur[0:n, :] + cur[2 * sh:2 * sh + n, :]
                lo_t += half // 2
            half *= 2
        off = (0 - lo_t) * SUBLANES
        wsum = cur[off:off + r, :]
        half_w = win // 2
        cnt = (jnp.clip(t_glob + half_w, 0, seq_len) - jnp.clip(t_glob - half_w, 0, seq_len)).astype(F32)
        pooled_ref[:, ls] = wsum / cnt - e[hr:hr + r, :]
    o_ref[...] = (_dot(pooled_ref[...].astype(BF16), wp_ref[...]) * sc_ref[...]).astype(BF16)


def _pool(dx, layer, w_pool_bd, scale, seq_len):
    rows, w = dx.shape
    r = POOL_STEPS * SUBLANES
    nt = rows // r
    hb = (max(POOL_WINDOWS) // 2) * SUBLANES
    per = r // hb
    last_hb = rows // hb - 1
    return pl.pallas_call(
        functools.partial(_pool_body, seq_len=seq_len),
        grid=(nt,),
        in_specs=[
            pl.BlockSpec((r, w), lambda i: (i, 0)),
            pl.BlockSpec((hb, w), lambda i: (jnp.maximum(i * per - 1, 0), 0)),
            pl.BlockSpec((hb, w), lambda i: (jnp.minimum((i + 1) * per, last_hb), 0)),
            _layer_spec((w, w), layer),
            _layer_spec((1, w), layer),
        ],
        out_specs=pl.BlockSpec((r, w), lambda i: (i, 0)),
        out_shape=jax.ShapeDtypeStruct((rows, w), BF16),
        scratch_shapes=[pltpu.VMEM((r + 2 * hb, w), F32), pltpu.VMEM((r, w), F32)],
        compiler_params=_params("parallel"),
        name="pool",
    )(dx, dx, dx, w_pool_bd, scale)


def _attn_body(q_ref, k_ref, v_ref, bias_ref, o_ref, *, grid_rows):
    g = pl.program_id(1)
    lane = lax.broadcasted_iota(jnp.int32, (GRID_W, LANES), 1)
    lo_half = lane < NA_HEAD_DIM
    nkeys = NA_ROWS * GRID_W

    def row_body(rr, _):
        r = g * ATTN_ROWS + rr
        rs = jnp.clip(r - NA_ROWS // 2, 0, grid_rows - NA_ROWS)
        d0 = rs - r + (NA_ROWS - 1)
        k0 = pl.multiple_of(rs * GRID_W, GRID_W)
        q0 = pl.multiple_of(rr * GRID_W, GRID_W)
        scores = []
        for hp in range(NA_HEADS // 2):
            ls = slice(hp * LANES, (hp + 1) * LANES)
            kp = k_ref[pl.ds(k0, nkeys), ls]
            qp = q_ref[pl.ds(q0, GRID_W), ls]
            zero = jnp.zeros_like(qp)
            qm = jnp.concatenate([jnp.where(lo_half, qp, zero), jnp.where(lo_half, zero, qp)], axis=0)
            sc = lax.dot_general(qm, kp, (((1,), (1,)), ((), ())), preferred_element_type=F32)
            bias = jnp.concatenate([bias_ref[hp, d0 + 2 * j] for j in range(NA_ROWS // 2)], axis=-1)
            scores.append(sc + bias)
        maxes = [jnp.max(sc, axis=-1, keepdims=True) for sc in scores]
        probs = [jnp.exp2(sc - m) for sc, m in zip(scores, maxes)]
        inv = [1.0 / jnp.sum(p, axis=-1, keepdims=True) for p in probs]
        for hp in range(NA_HEADS // 2):
            ls = slice(hp * LANES, (hp + 1) * LANES)
            vp = v_ref[pl.ds(k0, nkeys), ls]
            o = _dot(probs[hp].astype(BF16), vp) * inv[hp]
            o_ref[pl.ds(q0, GRID_W), ls] = jnp.where(lo_half, o[0:GRID_W], o[GRID_W:2 * GRID_W]).astype(BF16)
        return 0

    lax.fori_loop(0, ATTN_ROWS, row_body, 0, unroll=2)


def _attn(q, k, v, layer, bias_tab, batch):
    s, bw = q.shape
    w = bw // batch
    grid_rows = s // GRID_W
    ng = grid_rows // ATTN_ROWS
    qr = ATTN_ROWS * GRID_W
    q_spec = pl.BlockSpec((qr, w), lambda b, g: (g, b))
    kv_spec = pl.BlockSpec((s, w), lambda b, g: (0, b))
    return pl.pallas_call(
        functools.partial(_attn_body, grid_rows=grid_rows),
        grid=(batch, ng),
        in_specs=[q_spec, kv_spec, kv_spec, _layer_spec(bias_tab.shape[1:], layer)],
        out_specs=q_spec,
        out_shape=jax.ShapeDtypeStruct((s, bw), BF16),
        compiler_params=_params("parallel", "parallel"),
        name="natten",
    )(q, k, v, bias_tab)


def _s5_assemble(gc_ref, hfc_ref, hbc_ref, qc_ref, g_ref, hf_ref, hb_ref, q_ref):
    half = LANES // 2
    lo = lax.broadcasted_iota(jnp.int32, (S5_GROUP_FLAT, LANES), 1) < half
    g_ref[...] = jnp.zeros_like(g_ref)
    q_ref[...] = jnp.zeros_like(q_ref)
    for q in range(S5_TILE_GROUPS // 2):
        for e in range(2):
            g = 2 * q + e
            rows = slice(e * S5_GROUP_FLAT, (e + 1) * S5_GROUP_FLAT)
            g_ref[q, rows, e * S5_GROUP_FLAT:(e + 1) * S5_GROUP_FLAT] = gc_ref[g]
            for src, dst in ((hfc_ref, hf_ref), (hbc_ref, hb_ref)):
                h = src[g].astype(F32)
                swapped = pltpu.roll(h, half, 1)
                if e == 0:
                    re_tile, im_tile = jnp.where(lo, h, 0.0), jnp.where(lo, swapped, 0.0)
                else:
                    re_tile, im_tile = jnp.where(lo, 0.0, swapped), jnp.where(lo, 0.0, h)
                dst[q, rows, 0:LANES] = re_tile.astype(BF16)
                dst[q, rows, LANES:2 * LANES] = im_tile.astype(BF16)
            for blk in range(4):
                q_ref[q, blk * LANES + e * half:blk * LANES + (e + 1) * half,
                      e * S5_GROUP_FLAT:(e + 1) * S5_GROUP_FLAT] = qc_ref[g, blk * half:(blk + 1) * half, :]


def _s5_body(u_ref, gc_ref, hfc_ref, hbc_ref, qc_ref, a_ref, dsk_ref,
             y_ref, g_ref, hf_ref, hb_ref, q_ref, sinb_ref, sloc_ref, sin_ref, carry_ref):
    p = pl.program_id(1)
    j = pl.program_id(2)
    nblk = pl.num_programs(2)

    @pl.when((j == 0) & (p == 0))
    def _():
        _s5_assemble(gc_ref, hfc_ref, hbc_ref, qc_ref, g_ref, hf_ref, hb_ref, q_ref)

    rb = u_ref.shape[0]
    nk = rb // SUBLANES
    npairs = S5_TILE_GROUPS // 2
    pf = 2 * S5_GROUP_FLAT
    re_t = lambda q: slice(2 * q * LANES, (2 * q + 1) * LANES)
    im_t = lambda q: slice((2 * q + 1) * LANES, (2 * q + 2) * LANES)
    pair_t = lambda q: slice(q * pf, (q + 1) * pf)
    coef_t = lambda q: slice(q * LANES, (q + 1) * LANES)

    @pl.when(j == 0)
    def _():
        carry_ref[...] = jnp.zeros_like(carry_ref)

    def sweep(direction, reverse):
        a_re = a_ref[2 * direction:2 * direction + 1, :]
        a_im = a_ref[2 * direction + 1:2 * direction + 2, :]

        def body(n, state):
            kk = (nk - 1 - n) if reverse else n
            rows = pl.ds(pl.multiple_of(kk * SUBLANES, SUBLANES), SUBLANES)
            new = []
            for q in range(npairs):
                sr, si = state[2 * q], state[2 * q + 1]
                sin_ref[rows, re_t(q)] = sr
                sin_ref[rows, im_t(q)] = si
                ar, ai = a_re[:, coef_t(q)], a_im[:, coef_t(q)]
                new.append(ar * sr - ai * si + sloc_ref[rows, re_t(q)])
                new.append(ar * si + ai * sr + sloc_ref[rows, im_t(q)])
            return tuple(new)

        init = tuple(carry_ref[:, t * LANES:(t + 1) * LANES] for t in range(2 * npairs))
        out = lax.fori_loop(0, nk, body, init, unroll=4)
        for t in range(2 * npairs):
            carry_ref[:, t * LANES:(t + 1) * LANES] = out[t]

    @pl.when(p == 0)
    def _():
        for q in range(npairs):
            sloc_ref[:, pair_t(q)] = _dot(u_ref[:, pair_t(q)], hb_ref[q])
        sweep(1, True)
        blk = nblk - 1 - j
        sinb_ref[pl.ds(pl.multiple_of(blk * rb, rb), rb), :] = sin_ref[...].astype(BF16)

    @pl.when(p == 1)
    def _():
        for q in range(npairs):
            sloc_ref[:, pair_t(q)] = _dot(u_ref[:, pair_t(q)], hf_ref[q])
        sweep(0, False)
        rows_b = pl.ds(pl.multiple_of(j * rb, rb), rb)
        for q in range(npairs):
            u = u_ref[:, pair_t(q)]
            states = jnp.concatenate([sin_ref[:, pair_t(q)].astype(BF16), sinb_ref[rows_b, pair_t(q)]], axis=-1)
            y = _dot(u, g_ref[q]) + _dot(states, q_ref[q]) + dsk_ref[q] * u.astype(F32)
            y_ref[:, pair_t(q)] = y.astype(BF16)


def _s5(cx, layer, mats):
    g_m, hf_m, hb_m, q_m, a8, dsk = mats
    rows, width = cx.shape
    rb = S5_BLOCK_CHUNKS * SUBLANES
    nblk = rows // rb
    tile_w = S5_TILE_GROUPS * S5_GROUP_FLAT
    npairs, pf = S5_TILE_GROUPS // 2, 2 * S5_GROUP_FLAT
    state_w = S5_TILE_GROUPS * 2 * SSM_STATE
    ng, gf = S5_TILE_GROUPS, S5_GROUP_FLAT
    per_tile = lambda n, *shape: pl.BlockSpec((None, n) + shape, lambda o, p, j: (layer, o) + (0,) * len(shape))
    return pl.pallas_call(
        _s5_body,
        grid=(width // tile_w, 2, nblk),
        in_specs=[
            pl.BlockSpec((rb, tile_w), lambda o, p, j: (j + (1 - p) * (nblk - 1 - 2 * j), o)),
            per_tile(ng, gf, gf), per_tile(ng, gf, LANES), per_tile(ng, gf, LANES), per_tile(ng, 2 * LANES, gf),
            pl.BlockSpec((None, None, 4, npairs * LANES), lambda o, p, j: (layer, o, 0, 0)),
            per_tile(npairs, 1, pf),
        ],
        out_specs=pl.BlockSpec((rb, tile_w), lambda o, p, j: (p * j, o)),
        out_shape=jax.ShapeDtypeStruct((rows, width), BF16),
        scratch_shapes=[
            pltpu.VMEM((npairs, pf, pf), BF16),
            pltpu.VMEM((npairs, pf, 2 * LANES), BF16),
            pltpu.VMEM((npairs, pf, 2 * LANES), BF16),
            pltpu.VMEM((npairs, 4 * LANES, pf), BF16),
            pltpu.VMEM((rows, state_w), BF16),
            pltpu.VMEM((rb, state_w), F32),
            pltpu.VMEM((rb, state_w), F32),
            pltpu.VMEM((SUBLANES, state_w), F32),
        ],
        compiler_params=_params("arbitrary", "arbitrary", "arbitrary"),
        name="s5",
    )(cx, g_m, hf_m, hb_m, q_m, a8, dsk)


def _merge_body(x_ref, p_ref, hf_ref, hb_ref, yb_ref, yc_ref, yd_ref, ns_ref,
                wag_ref, wbg_ref, wcg_ref, wdg_ref, wm0_ref, wm1_ref, wm2_ref, wm3_ref,
                wb0_ref, wb1_ref, wb2_ref, wb3_ref, wout_ref, pg_ref, pp_ref, gluw_ref, glub_ref,
                o_ref, stage_ref, pslab_ref, xslab_ref, *, x_batch_major, out_batch_major):
    x = _load_time_major(x_ref, xslab_ref) if x_batch_major else x_ref[...]
    hn = _rms_rows(x, ns_ref[...]).astype(BF16)

    def branch(y, wb_ref, wm_ref):
        return _dot(y.astype(BF16), wb_ref[...]) * _sigmoid(_dot(hn, wm_ref[...]))

    ya = (hf_ref[...].astype(F32) + hb_ref[...].astype(F32)) * _silu(_dot(hn, wag_ref[...]))
    merged = branch(ya, wb0_ref, wm0_ref)
    steps = stage_ref.shape[1] // SUBLANES
    for b in range(SUBLANES):
        for l in range(W_TILES):
            c0 = b * BRANCH_W + l * LANES
            stage_ref[l, pl.ds(b, steps, stride=SUBLANES), :] = yb_ref[:, c0:c0 + LANES].astype(F32)
    yb = jnp.concatenate([stage_ref[l] for l in range(W_TILES)], axis=-1)
    merged = merged + branch(yb * _silu(_dot(hn, wbg_ref[...])), wb1_ref, wm1_ref)
    nchunk = yc_ref.shape[0] // SUBLANES
    tiles = []
    for o in range(W_TILES):
        per_g = [yc_ref[:, (o * S5_TILE_GROUPS + g) * S5_GROUP_FLAT:(o * S5_TILE_GROUPS + g + 1) * S5_GROUP_FLAT]
                 .astype(F32) for g in range(S5_TILE_GROUPS)]
        per_t = [v.reshape(nchunk, 1, SUBLANES, LANES) for v in _block_transpose(per_g)]
        tiles.append(jnp.concatenate(per_t, axis=1).reshape(nchunk * S5_CHUNK * SUBLANES, LANES))
    yg = _gelu_tanh(jnp.concatenate(tiles, axis=-1))
    yc = yg * _sigmoid(_dot(yg.astype(BF16), gluw_ref[...]) + glub_ref[...])
    merged = merged + branch(yc * _silu(_dot(hn, wcg_ref[...])), wb2_ref, wm2_ref)
    merged = merged + branch(yd_ref[...].astype(F32) * _silu(_dot(hn, wdg_ref[...])), wb3_ref, wm3_ref)
    x1 = x + _dot(merged.astype(BF16), wout_ref[...])
    emb = _dot(_load_time_major(p_ref, pslab_ref).astype(BF16), pp_ref[...])
    out = x1 + _sigmoid(_dot(x1.astype(BF16), pg_ref[...])) * emb
    if out_batch_major:
        _store_batch_major(out, o_ref, xslab_ref)
    else:
        o_ref[...] = out


def _merge(x, p, layer, hf, hb, yb, yc, yd, norm_scale, w_in, wbr, wout, pgate, pproj, glu_w, glu_b, batch,
           out_batch_major):
    w = BRANCH_W
    tm = MERGE_ROWS
    pdim = p.shape[-1]
    once = pl.Buffered(1)
    x_batch_major = x.ndim == 3
    d = x.shape[-1]
    rows = x.shape[0] * x.shape[1] if x_batch_major else x.shape[0]
    steps = tm // batch
    bm_spec = pl.BlockSpec((batch, steps, d), lambda i: (0, i, 0))
    x_spec = bm_spec if x_batch_major else pl.BlockSpec((tm, d), lambda i: (i, 0))
    if out_batch_major:
        out_spec, out_shape = bm_spec, jax.ShapeDtypeStruct((batch, rows // batch, d), F32)
    else:
        out_spec, out_shape = pl.BlockSpec((tm, d), lambda i: (i, 0)), jax.ShapeDtypeStruct((rows, d), F32)

    def resident(shape, *tail):
        tail = tail or (0,) * len(shape)
        return pl.BlockSpec((None,) + tuple(shape), lambda i: (layer,) + tuple(tail), pipeline_mode=once)

    row_spec = lambda n: pl.BlockSpec((tm, n), lambda i: (i, 0))
    gate_cols = [resident((d, w), 0, j) for j in (COL_AG, COL_BG, COL_CG, COL_DG)]
    merge_cols = [resident((d, d), 0, COL_MERGE + n) for n in range(N_BRANCH)]
    return pl.pallas_call(
        functools.partial(_merge_body, x_batch_major=x_batch_major, out_batch_major=out_batch_major),
        grid=(rows // tm,),
        in_specs=[
            x_spec,
            pl.BlockSpec((None, batch, steps, pdim), lambda i: (layer, 0, i, 0)),
            row_spec(w), row_spec(w),
            pl.BlockSpec((tm // batch, batch * w), lambda i: (i, 0)),
            pl.BlockSpec((tm // S5_CHUNK, S5_CHUNK * w), lambda i: (i, 0)), row_spec(w),
            resident((1, d)),
            *gate_cols, *merge_cols,
            *[pl.BlockSpec((None, None, w, d), lambda i, n=n: (layer, n, 0, 0), pipeline_mode=once)
              for n in range(N_BRANCH)],
            resident((d, d)), resident((d, d)), resident((pdim, d)),
            resident((w, w)), resident((1, w)),
        ],
        out_specs=out_spec,
        out_shape=out_shape,
        scratch_shapes=[pltpu.VMEM((W_TILES, tm, LANES), F32),
                        pltpu.VMEM((pdim // LANES, tm, LANES), F32),
                        pltpu.VMEM((d // LANES, tm, LANES), F32)],
        compiler_params=_params("parallel"),
        name="merge",
    )(x, p, hf, hb, yb, yc, yd, norm_scale, *([w_in] * 8), *([wbr] * N_BRANCH), wout, pgate, pproj, glu_w, glu_b)


def _block_diag(blocks):
    n, r, c = blocks.shape[-3:]
    spread = np.tile(np.eye(c, dtype=np.float32), (1, n))
    keep = np.kron(np.eye(n, dtype=np.float32), np.ones((r, c), np.float32))
    rows = blocks.reshape(blocks.shape[:-3] + (n * r, c))
    full = jnp.einsum('...rc,cm->...rm', rows, jnp.asarray(spread, blocks.dtype), precision=lax.Precision.HIGHEST)
    return full * jnp.asarray(keep, blocks.dtype)


def _attn_bias_tables(rpb):
    qc = np.arange(GRID_W)[:, None]
    kc = np.arange(GRID_W)[None, :]
    ws = np.clip(qc - NA_COLS // 2, 0, GRID_W - NA_COLS)
    in_win = (kc >= ws) & (kc < ws + NA_COLS)
    dc = np.clip(kc - qc, -(NA_COLS - 1), NA_COLS - 1) + NA_COLS - 1
    onehot = ((dc[None] == np.arange(2 * NA_COLS - 1)[:, None, None]) & in_win[None]).astype(np.float32)
    tab = jnp.einsum('hrm,mqk->hrqk', rpb.astype(F32), jnp.asarray(onehot), precision=lax.Precision.HIGHEST)
    tab = tab + jnp.asarray(np.where(in_win, 0.0, -1e30).astype(np.float32))
    pairs = jnp.concatenate([tab[:, :-1], tab[:, 1:]], axis=-1)
    nr = pairs.shape[1]
    pairs = pairs.reshape(NA_HEADS // 2, 2, nr, GRID_W, 2 * GRID_W)
    return jnp.transpose(pairs, (0, 2, 1, 3, 4)).reshape(NA_HEADS // 2, nr, 2 * GRID_W, 2 * GRID_W)


def _s5_matrices(a_re, a_im, log_dt, b_re, b_im, c_re, c_im, d_skip):
    L, G, P, C = S5_CHUNK, SSM_GROUPS, SSM_STATE, SSM_GROUP
    f32 = F32
    hi = lax.Precision.HIGHEST
    lr = jnp.minimum(a_re.astype(f32), -1e-4)
    li = a_im.astype(f32)
    dt = jnp.exp(log_dt.astype(f32))[..., None]
    steps_n = jnp.arange(L + 1, dtype=f32)[:, None, None, None]
    pw_mag = jnp.exp(steps_n * (lr * dt))
    pw_r = pw_mag * jnp.cos(steps_n * (li * dt))
    pw_i = pw_mag * jnp.sin(steps_n * (li * dt))
    ab_r, ab_i = pw_r[1], pw_i[1]
    nr = ab_r - 1.0
    den = lr * lr + li * li
    fr = ((nr * lr + ab_i * li) / den)[..., None]
    fi = ((ab_i * lr - nr * li) / den)[..., None]
    br, bi = b_re.astype(f32), b_im.astype(f32)
    bb_r = fr * br - fi * bi
    bb_i = fr * bi + fi * br
    cr, ci = c_re.astype(f32), c_im.astype(f32)
    m_r = pw_r[..., None] * bb_r[None] - pw_i[..., None] * bb_i[None]
    m_i = pw_r[..., None] * bb_i[None] + pw_i[..., None] * bb_r[None]
    k_lag = (jnp.einsum('dgop,ndgpi->ndgio', cr, m_r[:L], precision=hi)
             - jnp.einsum('dgop,ndgpi->ndgio', ci, m_i[:L], precision=hi))
    lag = np.arange(L)[None, :] - np.arange(L)[:, None]
    pick = (lag[None] == np.arange(1 - L, L)[:, None, None]).astype(np.float32)
    k_signed = jnp.concatenate([k_lag[:0:-1, 1], k_lag[:1, 0] + k_lag[:1, 1], k_lag[1:, 0]])
    g_mat = jnp.einsum('ngio,nst->gsito', k_signed, jnp.asarray(pick), precision=hi).reshape(G, L * C, L * C)

    def h_mat(direction, powers):
        both = jnp.stack([m_r[powers, direction], m_i[powers, direction]])
        return jnp.transpose(both, (2, 1, 4, 0, 3)).reshape(G, L * C, 2 * P)

    hf_mat = h_mat(0, slice(L - 1, None, -1))
    hb_mat = h_mat(1, slice(0, L))

    def q_rows(direction, powers):
        pwr = pw_r[powers, direction]
        pwi = pw_i[powers, direction]
        c_r, c_i = cr[direction], ci[direction]
        q_re = jnp.einsum('gcp,tgp->gptc', c_r, pwr) - jnp.einsum('gcp,tgp->gptc', c_i, pwi)
        q_im = -(jnp.einsum('gcp,tgp->gptc', c_r, pwi) + jnp.einsum('gcp,tgp->gptc', c_i, pwr))
        return jnp.stack([q_re, q_im], axis=1).reshape(G, 2 * P, L * C)

    q_mat = jnp.concatenate([q_rows(0, slice(1, L + 1)), q_rows(1, slice(L, 0, -1))], axis=1)
    npair = G // 2
    a8 = jnp.stack([pw_r[L, 0], pw_i[L, 0], pw_r[L, 1], pw_i[L, 1]])
    a8 = jnp.transpose(a8.reshape(4, W_TILES, (S5_TILE_GROUPS // 2) * 2 * P), (1, 0, 2))
    dsk = jnp.tile(d_skip.astype(f32).reshape(G, 1, C), (1, 1, L)).reshape(npair, 1, 2 * L * C)
    bf = lambda m: m.astype(BF16)
    return bf(g_mat), bf(hf_mat), bf(hb_mat), bf(q_mat), a8, dsk


def kernel(x, p, norm_scale, w_in, lru_conv_w, lru_conv_b, lru_w_r, lru_b_r, lru_w_i, lru_b_i, lru_lambda, na_q_gain, na_k_gain, na_rel_bias, ssm_a_re, ssm_a_im, ssm_log_dt, ssm_b_re, ssm_b_im, ssm_c_re, ssm_c_im, ssm_d, ssm_glu_w, ssm_glu_b, pool_w, pool_scale, w_branch, w_out, ple_proj, ple_gate):
    b, s, d = x.shape
    w = BRANCH_W
    assert b == SUBLANES and d == D_MODEL and s % (LRU_STEPS * 4) == 0 and s // GRID_W >= NA_ROWS
    depth = w_in.shape[0]
    rows = s * b

    norm = norm_scale.astype(F32)[:, None, :]
    w_in16 = w_in.astype(BF16)
    ones_bd = _block_diag(jnp.ones((MXU_DIM // NA_HEAD_DIM, NA_HEAD_DIM, NA_HEAD_DIM), BF16))
    q_gain = (jnp.tile(na_q_gain.astype(F32), (1, NA_HEADS)) * (NA_HEAD_DIM ** -0.5 * LOG2E))[:, None, :]
    k_gain = jnp.tile(na_k_gain.astype(F32), (1, NA_HEADS))[:, None, :]
    conv_w = lru_conv_w.astype(F32)
    conv_b = lru_conv_b.astype(F32)[:, None, :]
    gate_w = (0.5 * jnp.concatenate([_block_diag(lru_w_r), _block_diag(lru_w_i)], axis=-1)).astype(BF16)
    gate_b = (0.5 * jnp.concatenate([lru_b_r, lru_b_i], axis=-1).astype(F32))[:, :, None, :]
    c8 = (-0.5 * LRU_C * LOG2E * jax.nn.softplus(-lru_lambda.astype(F32)))[:, :, None, :]
    bias_tab = jax.vmap(_attn_bias_tables)(na_rel_bias.astype(F32) * LOG2E)
    s5_mats = jax.vmap(_s5_matrices)(ssm_a_re, ssm_a_im, ssm_log_dt, ssm_b_re, ssm_b_im, ssm_c_re, ssm_c_im, ssm_d)
    glu_w = ssm_glu_w.astype(BF16)
    glu_b = ssm_glu_b.astype(F32)[:, None, :]
    pool_bd = _block_diag(pool_w).astype(BF16)
    pool_sc = pool_scale.astype(F32)[:, None, :]
    branch_scale = jnp.asarray([0.5] + [1.0] * (N_BRANCH - 1), F32)[None, :, None, None]
    wbr = (w_branch.astype(F32) * branch_scale).astype(BF16)
    wout = w_out.astype(BF16)
    pproj = ple_proj.astype(BF16)
    pgate = ple_gate.astype(BF16)

    xt = x
    for i in range(depth):
        xc, q, k, v, cx, dx = _inproj(xt, i, norm, w_in16, ones_bd, q_gain, k_gain, conv_w, conv_b, b)
        hf, hb = _lru(xc, i, gate_w, gate_b, c8)
        yb = _attn(q, k, v, i, bias_tab, b)
        yc = _s5(cx, i, s5_mats)
        yd = _pool(dx, i, pool_bd, pool_sc, s)
        xt = _merge(xt, p, i, hf, hb, yb, yc, yd, norm, w_in16, wbr, wout, pgate, pproj, glu_w, glu_b, b,
                    out_batch_major=(i == depth - 1))
    return xt
```

```python
import functools

import jax
import jax.numpy as jnp
import numpy as np
from jax import lax
from jax.experimental import pallas as pl
from jax.experimental.pallas import tpu as pltpu

F32 = jnp.float32
BF16 = jnp.bfloat16

D_MODEL = 1024
BRANCH_W = 512
N_BRANCH = 4
NORM_EPS = 1e-6
LOG2E = 1.4426950408889634
GRID_W = 64
LRU_C = 8.0
LRU_CONV_W = 4
LRU_HALO_STEPS = 2
NA_HEADS = 8
NA_HEAD_DIM = 64
NA_ROWS = 8
NA_COLS = 16
SSM_GROUP = 16
SSM_GROUPS = 32
SSM_STATE = 64
POOL_WINDOWS = (2, 4, 8, 16)
POOL_GROUP = 128

SUBLANES = 8
LANES = 128
MXU_DIM = 256
W_TILES = BRANCH_W // LANES
S5_CHUNK = 8

COL_AX, COL_AG, COL_Q, COL_K, COL_V, COL_BG, COL_CX, COL_CG, COL_DX, COL_DG = range(10)
COL_MERGE = 10 * BRANCH_W // D_MODEL

PROJ_ROWS = 1024
MERGE_ROWS = 512
LRU_STEPS = 256
POOL_STEPS = 256
ATTN_ROWS = 16
S5_BLOCK_CHUNKS = 128
ELEM_ROWS = 256

VMEM_LIMIT = 56 * 1024 * 1024


def _params(*sem):
    return pltpu.CompilerParams(dimension_semantics=sem, vmem_limit_bytes=VMEM_LIMIT)


def _dot(a, b):
    return jnp.dot(a, b, preferred_element_type=F32)


def _sigmoid(z):
    return 0.5 * jnp.tanh(0.5 * z) + 0.5


def _silu(z):
    return z * _sigmoid(z)


def _gelu_tanh(y):
    return 0.5 * y * (1.0 + jnp.tanh(0.7978845608028654 * (y + 0.044715 * (y * y * y))))


def _rms_rows(x, g):
    ms = jnp.mean(x * x, axis=-1, keepdims=True)
    return x * lax.rsqrt(ms + NORM_EPS) * g


def _head_mean_square(v, ones_ref):
    v2 = (v * v).astype(BF16)
    n = ones_ref.shape[0]
    sums = [_dot(v2[:, c:c + n], ones_ref[...]) for c in range(0, v.shape[1], n)]
    return jnp.concatenate(sums, axis=-1) * (1.0 / NA_HEAD_DIM)


def _layer_spec(shape, layer, *tail):
    tail = tail or (0,) * len(shape)
    return pl.BlockSpec((None,) + tuple(shape), lambda *_: (layer,) + tuple(tail))


S5_TILE_GROUPS = LANES // SSM_GROUP
S5_GROUP_FLAT = S5_CHUNK * SSM_GROUP


def _block_transpose(xs):
    n = len(xs)
    lane = lax.broadcasted_iota(jnp.int32, (1, LANES), 1)
    cur = list(xs)
    width, stride = LANES // 2, n // 2
    while stride >= 1:
        low = (lane & width) == 0
        nxt = list(cur)
        for i in range(n):
            if i & stride:
                continue
            a, b = cur[i], cur[i + stride]
            nxt[i] = jnp.where(low, a, pltpu.roll(b, width, 1))
            nxt[i + stride] = jnp.where(low, pltpu.roll(a, LANES - width, 1), b)
        cur = nxt
        width //= 2
        stride //= 2
    return cur


def _load_time_major(src_ref, slab_ref, s0=0, s1=None):
    nb, steps, n = src_ref.shape
    s1 = steps if s1 is None else s1
    tiles = n // LANES
    for b in range(nb):
        for l in range(tiles):
            slab_ref[l, pl.ds(b, s1 - s0, stride=nb), :] = src_ref[b, s0:s1, l * LANES:(l + 1) * LANES]
    return jnp.concatenate([slab_ref[l, 0:(s1 - s0) * nb, :] for l in range(tiles)], axis=-1)


def _store_batch_major(val, dst_ref, slab_ref):
    nb, steps, n = dst_ref.shape
    tiles = n // LANES
    for l in range(tiles):
        slab_ref[l] = val[:, l * LANES:(l + 1) * LANES]
    for b in range(nb):
        for l in range(tiles):
            dst_ref[b, :, l * LANES:(l + 1) * LANES] = slab_ref[l, pl.ds(b, steps, stride=nb), :]


def _inproj_body(x_ref, xp_ref, xn_ref, ns_ref, wax_ref, wq_ref, wk_ref, wv_ref, wcx_ref, wdx_ref, ones_ref,
                 qg_ref, kg_ref, cw_ref, cb_ref,
                 xc_ref, q_ref, k_ref, v_ref, cx_ref, dx_ref, stage_ref, *slabs):
    i = pl.program_id(0)
    halo = LRU_HALO_STEPS * SUBLANES
    if slabs:
        x = _load_time_major(x_ref, slabs[0])
        hs = xp_ref.shape[1]
        x_prev = _load_time_major(xp_ref, slabs[1], hs - LRU_HALO_STEPS, hs)
        x_next = _load_time_major(xn_ref, slabs[1], 0, LRU_HALO_STEPS)
    else:
        x = x_ref[...]
        x_prev = xp_ref[xp_ref.shape[0] - halo:, :]
        x_next = xn_ref[0:halo, :]
    hn = _rms_rows(x, ns_ref[...]).astype(BF16)
    steps = stage_ref.shape[1] // SUBLANES

    def to_batch_lanes(val, out_ref):
        for l in range(W_TILES):
            stage_ref[l] = val[:, l * LANES:(l + 1) * LANES]
        for b in range(SUBLANES):
            for l in range(W_TILES):
                c0 = b * BRANCH_W + l * LANES
                out_ref[:, c0:c0 + LANES] = stage_ref[l, pl.ds(b, steps, stride=SUBLANES), :].astype(BF16)

    hn_ext = jnp.concatenate([_rms_rows(x_prev, ns_ref[...]).astype(BF16), hn,
                              _rms_rows(x_next, ns_ref[...]).astype(BF16)], axis=0)
    ax = _dot(hn_ext, wax_ref[...])
    r = hn.shape[0]
    row = lax.broadcasted_iota(jnp.int32, (r + 2 * halo, 1), 0)
    outside = ((row < halo) & (i == 0)) | ((row >= r + halo) & (i == pl.num_programs(0) - 1))
    ax = jnp.where(outside, 0.0, ax)
    xc = cb_ref[...]
    for tap in range(LRU_CONV_W):
        xc = xc + cw_ref[tap:tap + 1, :] * ax[tap * SUBLANES:tap * SUBLANES + r, :]
    xc_ref[...] = xc.astype(BF16)
    q = _dot(hn, wq_ref[...])
    to_batch_lanes(q * lax.rsqrt(_head_mean_square(q, ones_ref) + NORM_EPS) * qg_ref[...], q_ref)
    k = _dot(hn, wk_ref[...])
    to_batch_lanes(k * lax.rsqrt(_head_mean_square(k, ones_ref) + NORM_EPS) * kg_ref[...], k_ref)
    to_batch_lanes(_dot(hn, wv_ref[...]), v_ref)
    cxv = _dot(hn, wcx_ref[...])
    nchunk = cxv.shape[0] // (S5_CHUNK * SUBLANES)
    for o in range(W_TILES):
        tile3 = cxv[:, o * LANES:(o + 1) * LANES].reshape(nchunk, S5_CHUNK * SUBLANES, LANES)
        per_t = [tile3[:, t * SUBLANES:(t + 1) * SUBLANES, :].reshape(nchunk * SUBLANES, LANES)
                 for t in range(S5_CHUNK)]
        for g, ug in enumerate(_block_transpose(per_t)):
            c0 = (o * S5_TILE_GROUPS + g) * S5_GROUP_FLAT
            cx_ref[:, c0:c0 + S5_GROUP_FLAT] = ug.astype(BF16)
    dx_ref[...] = _dot(hn, wdx_ref[...]).astype(BF16)


def _inproj(x, layer, norm_scale, w_in, ones_bd, q_gain, k_gain, conv_w, conv_b, batch):
    w = BRANCH_W
    tm = PROJ_ROWS
    batch_major = x.ndim == 3
    d = x.shape[-1]
    rows = x.shape[0] * x.shape[1] if batch_major else x.shape[0]
    s = rows // batch
    nt = rows // tm
    if batch_major:
        hs = SUBLANES
        per = tm // batch // hs
        x_spec = pl.BlockSpec((batch, tm // batch, d), lambda i: (0, i, 0))
        prev_spec = pl.BlockSpec((batch, hs, d), lambda i: (0, jnp.maximum(i * per - 1, 0), 0))
        next_spec = pl.BlockSpec((batch, hs, d), lambda i: (0, jnp.minimum((i + 1) * per, s // hs - 1), 0))
        slabs = [pltpu.VMEM((d // LANES, tm, LANES), F32),
                 pltpu.VMEM((d // LANES, LRU_HALO_STEPS * batch, LANES), F32)]
    else:
        hb = LRU_HALO_STEPS * batch
        per = tm // hb
        x_spec = pl.BlockSpec((tm, d), lambda i: (i, 0))
        prev_spec = pl.BlockSpec((hb, d), lambda i: (jnp.maximum(i * per - 1, 0), 0))
        next_spec = pl.BlockSpec((hb, d), lambda i: (jnp.minimum((i + 1) * per, rows // hb - 1), 0))
        slabs = []
    wcol = lambda j: pl.BlockSpec((None, d, w), lambda i: (layer, 0, j))
    tm_spec = pl.BlockSpec((tm, w), lambda i: (i, 0))
    bl_spec = pl.BlockSpec((tm // batch, batch * w), lambda i: (i, 0))
    tm_shape = jax.ShapeDtypeStruct((rows, w), BF16)
    bl_shape = jax.ShapeDtypeStruct((s, batch * w), BF16)
    gf_spec = pl.BlockSpec((tm // S5_CHUNK, S5_CHUNK * w), lambda i: (i, 0))
    gf_shape = jax.ShapeDtypeStruct((rows // S5_CHUNK, S5_CHUNK * w), BF16)
    return pl.pallas_call(
        _inproj_body,
        grid=(nt,),
        in_specs=[
            x_spec, prev_spec, next_spec,
            _layer_spec((1, d), layer),
            wcol(COL_AX), wcol(COL_Q), wcol(COL_K), wcol(COL_V), wcol(COL_CX), wcol(COL_DX),
            pl.BlockSpec(ones_bd.shape, lambda i: (0, 0)),
            _layer_spec((1, w), layer), _layer_spec((1, w), layer),
            _layer_spec((LRU_CONV_W, w), layer), _layer_spec((1, w), layer),
        ],
        out_specs=[tm_spec, bl_spec, bl_spec, bl_spec, gf_spec, tm_spec],
        out_shape=[tm_shape, bl_shape, bl_shape, bl_shape, gf_shape, tm_shape],
        scratch_shapes=[pltpu.VMEM((W_TILES, tm, LANES), F32)] + slabs,
        compiler_params=_params("parallel"),
        name="inproj",
    )(x, x, x, norm_scale, w_in, w_in, w_in, w_in, w_in, w_in, ones_bd, q_gain, k_gain, conv_w, conv_b)


def _fill_extended(ext_ref, main_ref, prev_ref, next_ref, n_prev, n_next, is_first, is_last):
    r = main_ref.shape[0]
    prev = prev_ref[...].astype(F32)
    ext_ref[0:n_prev, :] = jnp.where(is_first, 0.0, prev[prev.shape[0] - n_prev:, :])
    ext_ref[n_prev:n_prev + r, :] = main_ref[...].astype(F32)
    nxt = next_ref[...].astype(F32)
    ext_ref[n_prev + r:n_prev + r + n_next, :] = jnp.where(is_last, 0.0, nxt[0:n_next, :])


def _lru_body(xf_ref, xb_ref, wg_ref, bg_ref, ch_ref, hf_ref, hb_ref, af_ref, bf_ref, ab_ref, bb_ref, carry_ref):
    i = pl.program_id(0)
    w = BRANCH_W
    r = xf_ref.shape[0]
    steps = r // SUBLANES

    @pl.when(i == 0)
    def _():
        carry_ref[...] = jnp.zeros_like(carry_ref)

    def prepare(direction, x_ref, a_ref, b_ref):
        def sub(sb, _):
            rows = pl.ds(pl.multiple_of(sb * ELEM_ROWS, ELEM_ROWS), ELEM_ROWS)
            xcb = x_ref[rows, :]
            xc = xcb.astype(F32)
            g = _dot(xcb, wg_ref[direction]) + bg_ref[direction]
            tr = jnp.tanh(g[:, 0:w]) + 1.0
            ti = jnp.tanh(g[:, w:2 * w]) + 1.0
            a = jnp.exp2(ch_ref[direction] * tr)
            a_ref[rows, :] = a
            om = 1.0 - a * a
            b_ref[rows, :] = om * lax.rsqrt(jnp.maximum(om, 1e-37)) * (ti * xc)
            return 0

        lax.fori_loop(0, r // ELEM_ROWS, sub, 0, unroll=True)

    prepare(0, xf_ref, af_ref, bf_ref)
    prepare(1, xb_ref, ab_ref, bb_ref)

    def step(t, carry):
        hf, hb = carry
        rf = pl.ds(pl.multiple_of(t * SUBLANES, SUBLANES), SUBLANES)
        hf = af_ref[rf, :] * hf + bf_ref[rf, :]
        bf_ref[rf, :] = hf
        rb = pl.ds(pl.multiple_of((steps - 1 - t) * SUBLANES, SUBLANES), SUBLANES)
        hb = ab_ref[rb, :] * hb + bb_ref[rb, :]
        bb_ref[rb, :] = hb
        return hf, hb

    hf, hb = lax.fori_loop(0, steps, step, (carry_ref[0], carry_ref[1]), unroll=8)
    carry_ref[0] = hf
    carry_ref[1] = hb
    hf_ref[...] = bf_ref[...].astype(BF16)
    hb_ref[...] = bb_ref[...].astype(BF16)


def _lru(xc, layer, w_gate, b_gate, c8):
    rows, w = xc.shape
    r = LRU_STEPS * SUBLANES
    nt = rows // r
    fwd = pl.BlockSpec((r, w), lambda i: (i, 0))
    bwd = pl.BlockSpec((r, w), lambda i: (nt - 1 - i, 0))
    out_shape = jax.ShapeDtypeStruct((rows, w), BF16)
    return pl.pallas_call(
        _lru_body,
        grid=(nt,),
        in_specs=[
            fwd, bwd,
            _layer_spec((2, w, 2 * w), layer),
            _layer_spec((2, 1, 2 * w), layer),
            _layer_spec((2, 1, w), layer),
        ],
        out_specs=[fwd, bwd],
        out_shape=[out_shape, out_shape],
        scratch_shapes=[
            pltpu.VMEM((r, w), F32), pltpu.VMEM((r, w), F32),
            pltpu.VMEM((r, w), F32), pltpu.VMEM((r, w), F32),
            pltpu.VMEM((2, SUBLANES, w), F32),
        ],
        compiler_params=_params("arbitrary"),
        name="rglru",
    )(xc, xc, w_gate, b_gate, c8)


def _pool_body(m_ref, p_ref, n_ref, wp_ref, sc_ref, o_ref, ext_ref, pooled_ref, *, seq_len):
    i = pl.program_id(0)
    nt = pl.num_programs(0)
    r = m_ref.shape[0]
    steps = r // SUBLANES
    halo = max(POOL_WINDOWS) // 2
    hr = halo * SUBLANES
    _fill_extended(ext_ref, m_ref, p_ref, n_ref, hr, hr, i == 0, i == nt - 1)

    t_glob = i * steps + lax.broadcasted_iota(jnp.int32, (r, POOL_GROUP), 0) // SUBLANES
    for g, win in enumerate(POOL_WINDOWS):
        ls = slice(g * POOL_GROUP, (g + 1) * POOL_GROUP)
        e = ext_ref[:, ls]
        lo_t = -halo
        cur = e
        half = 1
        while half < win:
            n = cur.shape[0] - half * SUBLANES
            if half == 1:
                cur = cur[0:n, :] + cur[SUBLANES:SUBLANES + n, :]
                lo_t += 1
            else:
                sh = (half // 2) * SUBLANES
                cur = cur[0:n, :] + cur[2 * sh:2 * sh + n, :]
                lo_t += half // 2
            half *= 2
        off = (0 - lo_t) * SUBLANES
        wsum = cur[off:off + r, :]
        half_w = win // 2
        cnt = (jnp.clip(t_glob + half_w, 0, seq_len) - jnp.clip(t_glob - half_w, 0, seq_len)).astype(F32)
        pooled_ref[:, ls] = wsum / cnt - e[hr:hr + r, :]
    o_ref[...] = (_dot(pooled_ref[...].astype(BF16), wp_ref[...]) * sc_ref[...]).astype(BF16)


def _pool(dx, layer, w_pool_bd, scale, seq_len):
    rows, w = dx.shape
    r = POOL_STEPS * SUBLANES
    nt = rows // r
    hb = (max(POOL_WINDOWS) // 2) * SUBLANES
    per = r // hb
    last_hb = rows // hb - 1
    return pl.pallas_call(
        functools.partial(_pool_body, seq_len=seq_len),
        grid=(nt,),
        in_specs=[
            pl.BlockSpec((r, w), lambda i: (i, 0)),
            pl.BlockSpec((hb, w), lambda i: (jnp.maximum(i * per - 1, 0), 0)),
            pl.BlockSpec((hb, w), lambda i: (jnp.minimum((i + 1) * per, last_hb), 0)),
            _layer_spec((w, w), layer),
            _layer_spec((1, w), layer),
        ],
        out_specs=pl.BlockSpec((r, w), lambda i: (i, 0)),
        out_shape=jax.ShapeDtypeStruct((rows, w), BF16),
        scratch_shapes=[pltpu.VMEM((r + 2 * hb, w), F32), pltpu.VMEM((r, w), F32)],
        compiler_params=_params("parallel"),
        name="pool",
    )(dx, dx, dx, w_pool_bd, scale)


def _attn_body(q_ref, k_ref, v_ref, bias_ref, o_ref, *, grid_rows):
    g = pl.program_id(1)
    lane = lax.broadcasted_iota(jnp.int32, (GRID_W, LANES), 1)
    lo_half = lane < NA_HEAD_DIM
    nkeys = NA_ROWS * GRID_W

    def row_body(rr, _):
        r = g * ATTN_ROWS + rr
        rs = jnp.clip(r - NA_ROWS // 2, 0, grid_rows - NA_ROWS)
        d0 = rs - r + (NA_ROWS - 1)
        k0 = pl.multiple_of(rs * GRID_W, GRID_W)
        q0 = pl.multiple_of(rr * GRID_W, GRID_W)
        scores = []
        for hp in range(NA_HEADS // 2):
            ls = slice(hp * LANES, (hp + 1) * LANES)
            kp = k_ref[pl.ds(k0, nkeys), ls]
            qp = q_ref[pl.ds(q0, GRID_W), ls]
            zero = jnp.zeros_like(qp)
            qm = jnp.concatenate([jnp.where(lo_half, qp, zero), jnp.where(lo_half, zero, qp)], axis=0)
            sc = lax.dot_general(qm, kp, (((1,), (1,)), ((), ())), preferred_element_type=F32)
            bias = jnp.concatenate([bias_ref[hp, d0 + 2 * j] for j in range(NA_ROWS // 2)], axis=-1)
            scores.append(sc + bias)
        maxes = [jnp.max(sc, axis=-1, keepdims=True) for sc in scores]
        probs = [jnp.exp2(sc - m) for sc, m in zip(scores, maxes)]
        inv = [1.0 / jnp.sum(p, axis=-1, keepdims=True) for p in probs]
        for hp in range(NA_HEADS // 2):
            ls = slice(hp * LANES, (hp + 1) * LANES)
            vp = v_ref[pl.ds(k0, nkeys), ls]
            o = _dot(probs[hp].astype(BF16), vp) * inv[hp]
            o_ref[pl.ds(q0, GRID_W), ls] = jnp.where(lo_half, o[0:GRID_W], o[GRID_W:2 * GRID_W]).astype(BF16)
        return 0

    lax.fori_loop(0, ATTN_ROWS, row_body, 0, unroll=2)


def _attn(q, k, v, layer, bias_tab, batch):
    s, bw = q.shape
    w = bw // batch
    grid_rows = s // GRID_W
    ng = grid_rows // ATTN_ROWS
    qr = ATTN_ROWS * GRID_W
    q_spec = pl.BlockSpec((qr, w), lambda b, g: (g, b))
    kv_spec = pl.BlockSpec((s, w), lambda b, g: (0, b))
    return pl.pallas_call(
        functools.partial(_attn_body, grid_rows=grid_rows),
        grid=(batch, ng),
        in_specs=[q_spec, kv_spec, kv_spec, _layer_spec(bias_tab.shape[1:], layer)],
        out_specs=q_spec,
        out_shape=jax.ShapeDtypeStruct((s, bw), BF16),
        compiler_params=_params("parallel", "parallel"),
        name="natten",
    )(q, k, v, bias_tab)


def _s5_assemble(gc_ref, hfc_ref, hbc_ref, qc_ref, g_ref, hf_ref, hb_ref, q_ref):
    half = LANES // 2
    lo = lax.broadcasted_iota(jnp.int32, (S5_GROUP_FLAT, LANES), 1) < half
    g_ref[...] = jnp.zeros_like(g_ref)
    q_ref[...] = jnp.zeros_like(q_ref)
    for q in range(S5_TILE_GROUPS // 2):
        for e in range(2):
            g = 2 * q + e
            rows = slice(e * S5_GROUP_FLAT, (e + 1) * S5_GROUP_FLAT)
            g_ref[q, rows, e * S5_GROUP_FLAT:(e + 1) * S5_GROUP_FLAT] = gc_ref[g]
            for src, dst in ((hfc_ref, hf_ref), (hbc_ref, hb_ref)):
                h = src[g].astype(F32)
                swapped = pltpu.roll(h, half, 1)
                if e == 0:
                    re_tile, im_tile = jnp.where(lo, h, 0.0), jnp.where(lo, swapped, 0.0)
                else:
                    re_tile, im_tile = jnp.where(lo, 0.0, swapped), jnp.where(lo, 0.0, h)
                dst[q, rows, 0:LANES] = re_tile.astype(BF16)
                dst[q, rows, LANES:2 * LANES] = im_tile.astype(BF16)
            for blk in range(4):
                q_ref[q, blk * LANES + e * half:blk * LANES + (e + 1) * half,
                      e * S5_GROUP_FLAT:(e + 1) * S5_GROUP_FLAT] = qc_ref[g, blk * half:(blk + 1) * half, :]


def _s5_body(u_ref, gc_ref, hfc_ref, hbc_ref, qc_ref, a_ref, dsk_ref,
             y_ref, g_ref, hf_ref, hb_ref, q_ref, sinb_ref, sloc_ref, sin_ref, carry_ref):
    p = pl.program_id(1)
    j = pl.program_id(2)
    nblk = pl.num_programs(2)

    @pl.when((j == 0) & (p == 0))
    def _():
        _s5_assemble(gc_ref, hfc_ref, hbc_ref, qc_ref, g_ref, hf_ref, hb_ref, q_ref)

    rb = u_ref.shape[0]
    nk = rb // SUBLANES
    npairs = S5_TILE_GROUPS // 2
    pf = 2 * S5_GROUP_FLAT
    re_t = lambda q: slice(2 * q * LANES, (2 * q + 1) * LANES)
    im_t = lambda q: slice((2 * q + 1) * LANES, (2 * q + 2) * LANES)
    pair_t = lambda q: slice(q * pf, (q + 1) * pf)
    coef_t = lambda q: slice(q * LANES, (q + 1) * LANES)

    @pl.when(j == 0)
    def _():
        carry_ref[...] = jnp.zeros_like(carry_ref)

    def sweep(direction, reverse):
        a_re = a_ref[2 * direction:2 * direction + 1, :]
        a_im = a_ref[2 * direction + 1:2 * direction + 2, :]

        def body(n, state):
            kk = (nk - 1 - n) if reverse else n
            rows = pl.ds(pl.multiple_of(kk * SUBLANES, SUBLANES), SUBLANES)
            new = []
            for q in range(npairs):
                sr, si = state[2 * q], state[2 * q + 1]
                sin_ref[rows, re_t(q)] = sr
                sin_ref[rows, im_t(q)] = si
                ar, ai = a_re[:, coef_t(q)], a_im[:, coef_t(q)]
                new.append(ar * sr - ai * si + sloc_ref[rows, re_t(q)])
                new.append(ar * si + ai * sr + sloc_ref[rows, im_t(q)])
            return tuple(new)

        init = tuple(carry_ref[:, t * LANES:(t + 1) * LANES] for t in range(2 * npairs))
        out = lax.fori_loop(0, nk, body, init, unroll=4)
        for t in range(2 * npairs):
            carry_ref[:, t * LANES:(t + 1) * LANES] = out[t]

    @pl.when(p == 0)
    def _():
        for q in range(npairs):
            sloc_ref[:, pair_t(q)] = _dot(u_ref[:, pair_t(q)], hb_ref[q])
        sweep(1, True)
        blk = nblk - 1 - j
        sinb_ref[pl.ds(pl.multiple_of(blk * rb, rb), rb), :] = sin_ref[...].astype(BF16)

    @pl.when(p == 1)
    def _():
        for q in range(npairs):
            sloc_ref[:, pair_t(q)] = _dot(u_ref[:, pair_t(q)], hf_ref[q])
        sweep(0, False)
        rows_b = pl.ds(pl.multiple_of(j * rb, rb), rb)
        for q in range(npairs):
            u = u_ref[:, pair_t(q)]
            states = jnp.concatenate([sin_ref[:, pair_t(q)].astype(BF16), sinb_ref[rows_b, pair_t(q)]], axis=-1)
            y = _dot(u, g_ref[q]) + _dot(states, q_ref[q]) + dsk_ref[q] * u.astype(F32)
            y_ref[:, pair_t(q)] = y.astype(BF16)


def _s5(cx, layer, mats):
    g_m, hf_m, hb_m, q_m, a8, dsk = mats
    rows, width = cx.shape
    rb = S5_BLOCK_CHUNKS * SUBLANES
    nblk = rows // rb
    tile_w = S5_TILE_GROUPS * S5_GROUP_FLAT
    npairs, pf = S5_TILE_GROUPS // 2, 2 * S5_GROUP_FLAT
    state_w = S5_TILE_GROUPS * 2 * SSM_STATE
    ng, gf = S5_TILE_GROUPS, S5_GROUP_FLAT
    per_tile = lambda n, *shape: pl.BlockSpec((None, n) + shape, lambda o, p, j: (layer, o) + (0,) * len(shape))
    return pl.pallas_call(
        _s5_body,
        grid=(width // tile_w, 2, nblk),
        in_specs=[
            pl.BlockSpec((rb, tile_w), lambda o, p, j: (j + (1 - p) * (nblk - 1 - 2 * j), o)),
            per_tile(ng, gf, gf), per_tile(ng, gf, LANES), per_tile(ng, gf, LANES), per_tile(ng, 2 * LANES, gf),
            pl.BlockSpec((None, None, 4, npairs * LANES), lambda o, p, j: (layer, o, 0, 0)),
            per_tile(npairs, 1, pf),
        ],
        out_specs=pl.BlockSpec((rb, tile_w), lambda o, p, j: (p * j, o)),
        out_shape=jax.ShapeDtypeStruct((rows, width), BF16),
        scratch_shapes=[
            pltpu.VMEM((npairs, pf, pf), BF16),
            pltpu.VMEM((npairs, pf, 2 * LANES), BF16),
            pltpu.VMEM((npairs, pf, 2 * LANES), BF16),
            pltpu.VMEM((npairs, 4 * LANES, pf), BF16),
            pltpu.VMEM((rows, state_w), BF16),
            pltpu.VMEM((rb, state_w), F32),
            pltpu.VMEM((rb, state_w), F32),
            pltpu.VMEM((SUBLANES, state_w), F32),
        ],
        compiler_params=_params("arbitrary", "arbitrary", "arbitrary"),
        name="s5",
    )(cx, g_m, hf_m, hb_m, q_m, a8, dsk)


def _merge_body(x_ref, p_ref, hf_ref, hb_ref, yb_ref, yc_ref, yd_ref, ns_ref,
                wag_ref, wbg_ref, wcg_ref, wdg_ref, wm0_ref, wm1_ref, wm2_ref, wm3_ref,
                wb0_ref, wb1_ref, wb2_ref, wb3_ref, wout_ref, pg_ref, pp_ref, gluw_ref, glub_ref,
                o_ref, stage_ref, pslab_ref, xslab_ref, *, x_batch_major, out_batch_major):
    x = _load_time_major(x_ref, xslab_ref) if x_batch_major else x_ref[...]
    hn = _rms_rows(x, ns_ref[...]).astype(BF16)

    def branch(y, wb_ref, wm_ref):
        return _dot(y.astype(BF16), wb_ref[...]) * _sigmoid(_dot(hn, wm_ref[...]))

    ya = (hf_ref[...].astype(F32) + hb_ref[...].astype(F32)) * _silu(_dot(hn, wag_ref[...]))
    merged = branch(ya, wb0_ref, wm0_ref)
    steps = stage_ref.shape[1] // SUBLANES
    for b in range(SUBLANES):
        for l in range(W_TILES):
            c0 = b * BRANCH_W + l * LANES
            stage_ref[l, pl.ds(b, steps, stride=SUBLANES), :] = yb_ref[:, c0:c0 + LANES].astype(F32)
    yb = jnp.concatenate([stage_ref[l] for l in range(W_TILES)], axis=-1)
    merged = merged + branch(yb * _silu(_dot(hn, wbg_ref[...])), wb1_ref, wm1_ref)
    nchunk = yc_ref.shape[0] // SUBLANES
    tiles = []
    for o in range(W_TILES):
        per_g = [yc_ref[:, (o * S5_TILE_GROUPS + g) * S5_GROUP_FLAT:(o * S5_TILE_GROUPS + g + 1) * S5_GROUP_FLAT]
                 .astype(F32) for g in range(S5_TILE_GROUPS)]
        per_t = [v.reshape(nchunk, 1, SUBLANES, LANES) for v in _block_transpose(per_g)]
        tiles.append(jnp.concatenate(per_t, axis=1).reshape(nchunk * S5_CHUNK * SUBLANES, LANES))
    yg = _gelu_tanh(jnp.concatenate(tiles, axis=-1))
    yc = yg * _sigmoid(_dot(yg.astype(BF16), gluw_ref[...]) + glub_ref[...])
    merged = merged + branch(yc * _silu(_dot(hn, wcg_ref[...])), wb2_ref, wm2_ref)
    merged = merged + branch(yd_ref[...].astype(F32) * _silu(_dot(hn, wdg_ref[...])), wb3_ref, wm3_ref)
    x1 = x + _dot(merged.astype(BF16), wout_ref[...])
    emb = _dot(_load_time_major(p_ref, pslab_ref).astype(BF16), pp_ref[...])
    out = x1 + _sigmoid(_dot(x1.astype(BF16), pg_ref[...])) * emb
    if out_batch_major:
        _store_batch_major(out, o_ref, xslab_ref)
    else:
        o_ref[...] = out


def _merge(x, p, layer, hf, hb, yb, yc, yd, norm_scale, w_in, wbr, wout, pgate, pproj, glu_w, glu_b, batch,
           out_batch_major):
    w = BRANCH_W
    tm = MERGE_ROWS
    pdim = p.shape[-1]
    once = pl.Buffered(1)
    x_batch_major = x.ndim == 3
    d = x.shape[-1]
    rows = x.shape[0] * x.shape[1] if x_batch_major else x.shape[0]
    steps = tm // batch
    bm_spec = pl.BlockSpec((batch, steps, d), lambda i: (0, i, 0))
    x_spec = bm_spec if x_batch_major else pl.BlockSpec((tm, d), lambda i: (i, 0))
    if out_batch_major:
        out_spec, out_shape = bm_spec, jax.ShapeDtypeStruct((batch, rows // batch, d), F32)
    else:
        out_spec, out_shape = pl.BlockSpec((tm, d), lambda i: (i, 0)), jax.ShapeDtypeStruct((rows, d), F32)

    def resident(shape, *tail):
        tail = tail or (0,) * len(shape)
        return pl.BlockSpec((None,) + tuple(shape), lambda i: (layer,) + tuple(tail), pipeline_mode=once)

    row_spec = lambda n: pl.BlockSpec((tm, n), lambda i: (i, 0))
    gate_cols = [resident((d, w), 0, j) for j in (COL_AG, COL_BG, COL_CG, COL_DG)]
    merge_cols = [resident((d, d), 0, COL_MERGE + n) for n in range(N_BRANCH)]
    return pl.pallas_call(
        functools.partial(_merge_body, x_batch_major=x_batch_major, out_batch_major=out_batch_major),
        grid=(rows // tm,),
        in_specs=[
            x_spec,
            pl.BlockSpec((None, batch, steps, pdim), lambda i: (layer, 0, i, 0)),
            row_spec(w), row_spec(w),
            pl.BlockSpec((tm // batch, batch * w), lambda i: (i, 0)),
            pl.BlockSpec((tm // S5_CHUNK, S5_CHUNK * w), lambda i: (i, 0)), row_spec(w),
            resident((1, d)),
            *gate_cols, *merge_cols,
            *[pl.BlockSpec((None, None, w, d), lambda i, n=n: (layer, n, 0, 0), pipeline_mode=once)
              for n in range(N_BRANCH)],
            resident((d, d)), resident((d, d)), resident((pdim, d)),
            resident((w, w)), resident((1, w)),
        ],
        out_specs=out_spec,
        out_shape=out_shape,
        scratch_shapes=[pltpu.VMEM((W_TILES, tm, LANES), F32),
                        pltpu.VMEM((pdim // LANES, tm, LANES), F32),
                        pltpu.VMEM((d // LANES, tm, LANES), F32)],
        compiler_params=_params("parallel"),
        name="merge",
    )(x, p, hf, hb, yb, yc, yd, norm_scale, *([w_in] * 8), *([wbr] * N_BRANCH), wout, pgate, pproj, glu_w, glu_b)


def _block_diag(blocks):
    n, r, c = blocks.shape[-3:]
    spread = np.tile(np.eye(c, dtype=np.float32), (1, n))
    keep = np.kron(np.eye(n, dtype=np.float32), np.ones((r, c), np.float32))
    rows = blocks.reshape(blocks.shape[:-3] + (n * r, c))
    full = jnp.einsum('...rc,cm->...rm', rows, jnp.asarray(spread, blocks.dtype), precision=lax.Precision.HIGHEST)
    return full * jnp.asarray(keep, blocks.dtype)


def _attn_bias_tables(rpb):
    qc = np.arange(GRID_W)[:, None]
    kc = np.arange(GRID_W)[None, :]
    ws = np.clip(qc - NA_COLS // 2, 0, GRID_W - NA_COLS)
    in_win = (kc >= ws) & (kc < ws + NA_COLS)
    dc = np.clip(kc - qc, -(NA_COLS - 1), NA_COLS - 1) + NA_COLS - 1
    onehot = ((dc[None] == np.arange(2 * NA_COLS - 1)[:, None, None]) & in_win[None]).astype(np.float32)
    tab = jnp.einsum('hrm,mqk->hrqk', rpb.astype(F32), jnp.asarray(onehot), precision=lax.Precision.HIGHEST)
    tab = tab + jnp.asarray(np.where(in_win, 0.0, -1e30).astype(np.float32))
    pairs = jnp.concatenate([tab[:, :-1], tab[:, 1:]], axis=-1)
    nr = pairs.shape[1]
    pairs = pairs.reshape(NA_HEADS // 2, 2, nr, GRID_W, 2 * GRID_W)
    return jnp.transpose(pairs, (0, 2, 1, 3, 4)).reshape(NA_HEADS // 2, nr, 2 * GRID_W, 2 * GRID_W)


def _s5_matrices(a_re, a_im, log_dt, b_re, b_im, c_re, c_im, d_skip):
    L, G, P, C = S5_CHUNK, SSM_GROUPS, SSM_STATE, SSM_GROUP
    f32 = F32
    hi = lax.Precision.HIGHEST
    lr = jnp.minimum(a_re.astype(f32), -1e-4)
    li = a_im.astype(f32)
    dt = jnp.exp(log_dt.astype(f32))[..., None]
    steps_n = jnp.arange(L + 1, dtype=f32)[:, None, None, None]
    pw_mag = jnp.exp(steps_n * (lr * dt))
    pw_r = pw_mag * jnp.cos(steps_n * (li * dt))
    pw_i = pw_mag * jnp.sin(steps_n * (li * dt))
    ab_r, ab_i = pw_r[1], pw_i[1]
    nr = ab_r - 1.0
    den = lr * lr + li * li
    fr = ((nr * lr + ab_i * li) / den)[:, :, None, :]
    fi = ((ab_i * lr - nr * li) / den)[:, :, None, :]
    bt_r = jnp.swapaxes(b_re.astype(f32), -1, -2)
    bt_i = jnp.swapaxes(b_im.astype(f32), -1, -2)
    bb_r = fr * bt_r - fi * bt_i
    bb_i = fr * bt_i + fi * bt_r
    cr, ci = c_re.astype(f32), c_im.astype(f32)
    m_r = pw_r[:, :, :, None, :] * bb_r[None] - pw_i[:, :, :, None, :] * bb_i[None]
    m_i = pw_r[:, :, :, None, :] * bb_i[None] + pw_i[:, :, :, None, :] * bb_r[None]
    k_lag = (jnp.einsum('dgop,ndgip->ndgio', cr, m_r[:L], precision=hi)
             - jnp.einsum('dgop,ndgip->ndgio', ci, m_i[:L], precision=hi))
    k_signed = jnp.concatenate([k_lag[:0:-1, 1], k_lag[:1, 0] + k_lag[:1, 1], k_lag[1:, 0]])
    tile_c = jnp.asarray(np.tile(np.eye(C, dtype=np.float32), (1, L)))
    rep_t = jnp.asarray(np.repeat(np.eye(L, dtype=np.float32), C, axis=1))
    k_wide = jnp.einsum('ngio,ol->ngil', k_signed, tile_c, precision=hi)
    lag = np.arange(L)[None, :] - np.arange(L)[:, None]
    g_mat = 0.0
    for n in range(2 * L - 1):
        sel = np.repeat((lag == n - (L - 1)).astype(np.float32), C, axis=1)
        g_mat = g_mat + k_wide[n][:, None, :, :] * jnp.asarray(sel)[None, :, None, :]
    g_mat = g_mat.reshape(G, L * C, L * C)

    def h_mat(direction, powers):
        re = jnp.transpose(m_r[powers, direction], (1, 0, 2, 3)).reshape(G, L * C, P)
        im = jnp.transpose(m_i[powers, direction], (1, 0, 2, 3)).reshape(G, L * C, P)
        return jnp.concatenate([re, im], axis=-1)

    hf_mat = h_mat(0, slice(L - 1, None, -1))
    hb_mat = h_mat(1, slice(0, L))

    def q_rows(direction, powers):
        spread = lambda c: jnp.einsum('gcp,cl->gpl', c[direction], tile_c, precision=hi)
        along_t = lambda w: jnp.einsum('tgp,tl->gpl', w[powers, direction], rep_t, precision=hi)
        c_r, c_i, w_r, w_i = spread(cr), spread(ci), along_t(pw_r), along_t(pw_i)
        return jnp.concatenate([c_r * w_r - c_i * w_i, -(c_r * w_i + c_i * w_r)], axis=1)

    q_mat = jnp.concatenate([q_rows(0, slice(1, L + 1)), q_rows(1, slice(L, 0, -1))], axis=1)
    npair = G // 2
    a8 = jnp.stack([pw_r[L, 0], pw_i[L, 0], pw_r[L, 1], pw_i[L, 1]])
    a8 = jnp.transpose(a8.reshape(4, W_TILES, (S5_TILE_GROUPS // 2) * 2 * P), (1, 0, 2))
    dsk = jnp.tile(d_skip.astype(f32).reshape(G, 1, C), (1, 1, L)).reshape(npair, 1, 2 * L * C)
    bf = lambda m: m.astype(BF16)
    return bf(g_mat), bf(hf_mat), bf(hb_mat), bf(q_mat), a8, dsk


def kernel(x, p, norm_scale, w_in, lru_conv_w, lru_conv_b, lru_w_r, lru_b_r, lru_w_i, lru_b_i, lru_lambda, na_q_gain, na_k_gain, na_rel_bias, ssm_a_re, ssm_a_im, ssm_log_dt, ssm_b_re, ssm_b_im, ssm_c_re, ssm_c_im, ssm_d, ssm_glu_w, ssm_glu_b, pool_w, pool_scale, w_branch, w_out, ple_proj, ple_gate):
    b, s, d = x.shape
    w = BRANCH_W
    assert b == SUBLANES and d == D_MODEL and s % (LRU_STEPS * 4) == 0 and s // GRID_W >= NA_ROWS
    depth = w_in.shape[0]
    rows = s * b

    norm = norm_scale.astype(F32)[:, None, :]
    w_in16 = w_in.astype(BF16)
    ones_bd = _block_diag(jnp.ones((MXU_DIM // NA_HEAD_DIM, NA_HEAD_DIM, NA_HEAD_DIM), BF16))
    q_gain = (jnp.tile(na_q_gain.astype(F32), (1, NA_HEADS)) * (NA_HEAD_DIM ** -0.5 * LOG2E))[:, None, :]
    k_gain = jnp.tile(na_k_gain.astype(F32), (1, NA_HEADS))[:, None, :]
    conv_w = lru_conv_w.astype(F32)
    conv_b = lru_conv_b.astype(F32)[:, None, :]
    gate_w = (0.5 * jnp.concatenate([_block_diag(lru_w_r), _block_diag(lru_w_i)], axis=-1)).astype(BF16)
    gate_b = (0.5 * jnp.concatenate([lru_b_r, lru_b_i], axis=-1).astype(F32))[:, :, None, :]
    c8 = (-0.5 * LRU_C * LOG2E * jax.nn.softplus(-lru_lambda.astype(F32)))[:, :, None, :]
    bias_tab = jax.vmap(_attn_bias_tables)(na_rel_bias.astype(F32) * LOG2E)
    s5_mats = jax.vmap(_s5_matrices)(ssm_a_re, ssm_a_im, ssm_log_dt, ssm_b_re, ssm_b_im, ssm_c_re, ssm_c_im, ssm_d)
    glu_w = ssm_glu_w.astype(BF16)
    glu_b = ssm_glu_b.astype(F32)[:, None, :]
    pool_bd = _block_diag(pool_w).astype(BF16)
    pool_sc = pool_scale.astype(F32)[:, None, :]
    branch_scale = jnp.asarray([0.5] + [1.0] * (N_BRANCH - 1), F32)[None, :, None, None]
    wbr = (w_branch.astype(F32) * branch_scale).astype(BF16)
    wout = w_out.astype(BF16)
    pproj = ple_proj.astype(BF16)
    pgate = ple_gate.astype(BF16)

    xt = x
    for i in range(depth):
        xc, q, k, v, cx, dx = _inproj(xt, i, norm, w_in16, ones_bd, q_gain, k_gain, conv_w, conv_b, b)
        hf, hb = _lru(xc, i, gate_w, gate_b, c8)
        yb = _attn(q, k, v, i, bias_tab, b)
        yc = _s5(cx, i, s5_mats)
        yd = _pool(dx, i, pool_bd, pool_sc, s)
        xt = _merge(xt, p, i, hf, hb, yb, yc, yd, norm, w_in16, wbr, wout, pgate, pproj, glu_w, glu_b, b,
                    out_batch_major=(i == depth - 1))
    return xt
```

```python
import functools

import jax
import jax.numpy as jnp
import numpy as np
from jax import lax
from jax.experimental import pallas as pl
from jax.experimental.pallas import tpu as pltpu

F32 = jnp.float32
BF16 = jnp.bfloat16

D_MODEL = 1024
BRANCH_W = 512
N_BRANCH = 4
NORM_EPS = 1e-6
LOG2E = 1.4426950408889634
GRID_W = 64
LRU_C = 8.0
LRU_CONV_W = 4
LRU_HALO_STEPS = 2
NA_HEADS = 8
NA_HEAD_DIM = 64
NA_ROWS = 8
NA_COLS = 16
SSM_GROUP = 16
SSM_GROUPS = 32
SSM_STATE = 64
POOL_WINDOWS = (2, 4, 8, 16)
POOL_GROUP = 128

SUBLANES = 8
LANES = 128
MXU_DIM = 256
W_TILES = BRANCH_W // LANES
S5_CHUNK = 8

COL_AX, COL_AG, COL_Q, COL_K, COL_V, COL_BG, COL_CX, COL_CG, COL_DX, COL_DG = range(10)
COL_MERGE = 10 * BRANCH_W // D_MODEL

PROJ_ROWS = 1024
MERGE_ROWS = 512
LRU_STEPS = 256
POOL_STEPS = 256
ATTN_ROWS = 16
S5_BLOCK_CHUNKS = 128
ELEM_ROWS = 256

VMEM_LIMIT = 56 * 1024 * 1024


def _params(*sem):
    return pltpu.CompilerParams(dimension_semantics=sem, vmem_limit_bytes=VMEM_LIMIT)


def _dot(a, b):
    return jnp.dot(a, b, preferred_element_type=F32)


def _sigmoid(z):
    return 0.5 * jnp.tanh(0.5 * z) + 0.5


def _silu(z):
    return z * _sigmoid(z)


def _gelu_tanh(y):
    return 0.5 * y * (1.0 + jnp.tanh(0.7978845608028654 * (y + 0.044715 * (y * y * y))))


def _rms_rows(x, g):
    ms = jnp.mean(x * x, axis=-1, keepdims=True)
    return x * lax.rsqrt(ms + NORM_EPS) * g


def _head_mean_square(v, ones_ref):
    v2 = (v * v).astype(BF16)
    n = ones_ref.shape[0]
    sums = [_dot(v2[:, c:c + n], ones_ref[...]) for c in range(0, v.shape[1], n)]
    return jnp.concatenate(sums, axis=-1) * (1.0 / NA_HEAD_DIM)


def _layer_spec(shape, layer, *tail):
    tail = tail or (0,) * len(shape)
    return pl.BlockSpec((None,) + tuple(shape), lambda *_: (layer,) + tuple(tail))


S5_TILE_GROUPS = LANES // SSM_GROUP
S5_GROUP_FLAT = S5_CHUNK * SSM_GROUP


def _block_transpose(xs):
    n = len(xs)
    lane = lax.broadcasted_iota(jnp.int32, (1, LANES), 1)
    cur = list(xs)
    width, stride = LANES // 2, n // 2
    while stride >= 1:
        low = (lane & width) == 0
        nxt = list(cur)
        for i in range(n):
            if i & stride:
                continue
            a, b = cur[i], cur[i + stride]
            nxt[i] = jnp.where(low, a, pltpu.roll(b, width, 1))
            nxt[i + stride] = jnp.where(low, pltpu.roll(a, LANES - width, 1), b)
        cur = nxt
        width //= 2
        stride //= 2
    return cur


def _load_time_major(src_ref, slab_ref, s0=0, s1=None):
    nb, steps, n = src_ref.shape
    s1 = steps if s1 is None else s1
    tiles = n // LANES
    for b in range(nb):
        for l in range(tiles):
            slab_ref[l, pl.ds(b, s1 - s0, stride=nb), :] = src_ref[b, s0:s1, l * LANES:(l + 1) * LANES]
    return jnp.concatenate([slab_ref[l, 0:(s1 - s0) * nb, :] for l in range(tiles)], axis=-1)


def _store_batch_major(val, dst_ref, slab_ref):
    nb, steps, n = dst_ref.shape
    tiles = n // LANES
    for l in range(tiles):
        slab_ref[l] = val[:, l * LANES:(l + 1) * LANES]
    for b in range(nb):
        for l in range(tiles):
            dst_ref[b, :, l * LANES:(l + 1) * LANES] = slab_ref[l, pl.ds(b, steps, stride=nb), :]


def _inproj_body(x_ref, xp_ref, xn_ref, ns_ref, wax_ref, wq_ref, wk_ref, wv_ref, wcx_ref, wdx_ref, ones_ref,
                 qg_ref, kg_ref, cw_ref, cb_ref,
                 xc_ref, q_ref, k_ref, v_ref, cx_ref, dx_ref, stage_ref, *slabs):
    i = pl.program_id(0)
    halo = LRU_HALO_STEPS * SUBLANES
    if slabs:
        x = _load_time_major(x_ref, slabs[0])
        hs = xp_ref.shape[1]
        x_prev = _load_time_major(xp_ref, slabs[1], hs - LRU_HALO_STEPS, hs)
        x_next = _load_time_major(xn_ref, slabs[1], 0, LRU_HALO_STEPS)
    else:
        x = x_ref[...]
        x_prev = xp_ref[xp_ref.shape[0] - halo:, :]
        x_next = xn_ref[0:halo, :]
    hn = _rms_rows(x, ns_ref[...]).astype(BF16)
    steps = stage_ref.shape[1] // SUBLANES

    def to_batch_lanes(val, out_ref):
        for l in range(W_TILES):
            stage_ref[l] = val[:, l * LANES:(l + 1) * LANES]
        for b in range(SUBLANES):
            for l in range(W_TILES):
                c0 = b * BRANCH_W + l * LANES
                out_ref[:, c0:c0 + LANES] = stage_ref[l, pl.ds(b, steps, stride=SUBLANES), :].astype(BF16)

    hn_ext = jnp.concatenate([_rms_rows(x_prev, ns_ref[...]).astype(BF16), hn,
                              _rms_rows(x_next, ns_ref[...]).astype(BF16)], axis=0)
    ax = _dot(hn_ext, wax_ref[...])
    r = hn.shape[0]
    row = lax.broadcasted_iota(jnp.int32, (r + 2 * halo, 1), 0)
    outside = ((row < halo) & (i == 0)) | ((row >= r + halo) & (i == pl.num_programs(0) - 1))
    ax = jnp.where(outside, 0.0, ax)
    xc = cb_ref[...]
    for tap in range(LRU_CONV_W):
        xc = xc + cw_ref[tap:tap + 1, :] * ax[tap * SUBLANES:tap * SUBLANES + r, :]
    xc_ref[...] = xc.astype(BF16)
    q = _dot(hn, wq_ref[...])
    to_batch_lanes(q * lax.rsqrt(_head_mean_square(q, ones_ref) + NORM_EPS) * qg_ref[...], q_ref)
    k = _dot(hn, wk_ref[...])
    to_batch_lanes(k * lax.rsqrt(_head_mean_square(k, ones_ref) + NORM_EPS) * kg_ref[...], k_ref)
    to_batch_lanes(_dot(hn, wv_ref[...]), v_ref)
    cxv = _dot(hn, wcx_ref[...])
    nchunk = cxv.shape[0] // (S5_CHUNK * SUBLANES)
    for o in range(W_TILES):
        tile3 = cxv[:, o * LANES:(o + 1) * LANES].reshape(nchunk, S5_CHUNK * SUBLANES, LANES)
        per_t = [tile3[:, t * SUBLANES:(t + 1) * SUBLANES, :].reshape(nchunk * SUBLANES, LANES)
                 for t in range(S5_CHUNK)]
        for g, ug in enumerate(_block_transpose(per_t)):
            c0 = (o * S5_TILE_GROUPS + g) * S5_GROUP_FLAT
            cx_ref[:, c0:c0 + S5_GROUP_FLAT] = ug.astype(BF16)
    dx_ref[...] = _dot(hn, wdx_ref[...]).astype(BF16)


def _inproj(x, layer, norm_scale, w_in, ones_bd, q_gain, k_gain, conv_w, conv_b, batch):
    w = BRANCH_W
    tm = PROJ_ROWS
    batch_major = x.ndim == 3
    d = x.shape[-1]
    rows = x.shape[0] * x.shape[1] if batch_major else x.shape[0]
    s = rows // batch
    nt = rows // tm
    if batch_major:
        hs = SUBLANES
        per = tm // batch // hs
        x_spec = pl.BlockSpec((batch, tm // batch, d), lambda i: (0, i, 0))
        prev_spec = pl.BlockSpec((batch, hs, d), lambda i: (0, jnp.maximum(i * per - 1, 0), 0))
        next_spec = pl.BlockSpec((batch, hs, d), lambda i: (0, jnp.minimum((i + 1) * per, s // hs - 1), 0))
        slabs = [pltpu.VMEM((d // LANES, tm, LANES), F32),
                 pltpu.VMEM((d // LANES, LRU_HALO_STEPS * batch, LANES), F32)]
    else:
        hb = LRU_HALO_STEPS * batch
        per = tm // hb
        x_spec = pl.BlockSpec((tm, d), lambda i: (i, 0))
        prev_spec = pl.BlockSpec((hb, d), lambda i: (jnp.maximum(i * per - 1, 0), 0))
        next_spec = pl.BlockSpec((hb, d), lambda i: (jnp.minimum((i + 1) * per, rows // hb - 1), 0))
        slabs = []
    wcol = lambda j: pl.BlockSpec((None, d, w), lambda i: (layer, 0, j))
    tm_spec = pl.BlockSpec((tm, w), lambda i: (i, 0))
    bl_spec = pl.BlockSpec((tm // batch, batch * w), lambda i: (i, 0))
    tm_shape = jax.ShapeDtypeStruct((rows, w), BF16)
    bl_shape = jax.ShapeDtypeStruct((s, batch * w), BF16)
    gf_spec = pl.BlockSpec((tm // S5_CHUNK, S5_CHUNK * w), lambda i: (i, 0))
    gf_shape = jax.ShapeDtypeStruct((rows // S5_CHUNK, S5_CHUNK * w), BF16)
    return pl.pallas_call(
        _inproj_body,
        grid=(nt,),
        in_specs=[
            x_spec, prev_spec, next_spec,
            _layer_spec((1, d), layer),
            wcol(COL_AX), wcol(COL_Q), wcol(COL_K), wcol(COL_V), wcol(COL_CX), wcol(COL_DX),
            pl.BlockSpec(ones_bd.shape, lambda i: (0, 0)),
            _layer_spec((1, w), layer), _layer_spec((1, w), layer),
            _layer_spec((LRU_CONV_W, w), layer), _layer_spec((1, w), layer),
        ],
        out_specs=[tm_spec, bl_spec, bl_spec, bl_spec, gf_spec, tm_spec],
        out_shape=[tm_shape, bl_shape, bl_shape, bl_shape, gf_shape, tm_shape],
        scratch_shapes=[pltpu.VMEM((W_TILES, tm, LANES), F32)] + slabs,
        compiler_params=_params("parallel"),
        name="inproj",
    )(x, x, x, norm_scale, w_in, w_in, w_in, w_in, w_in, w_in, ones_bd, q_gain, k_gain, conv_w, conv_b)


def _fill_extended(ext_ref, main_ref, prev_ref, next_ref, n_prev, n_next, is_first, is_last):
    r = main_ref.shape[0]
    prev = prev_ref[...].astype(F32)
    ext_ref[0:n_prev, :] = jnp.where(is_first, 0.0, prev[prev.shape[0] - n_prev:, :])
    ext_ref[n_prev:n_prev + r, :] = main_ref[...].astype(F32)
    nxt = next_ref[...].astype(F32)
    ext_ref[n_prev + r:n_prev + r + n_next, :] = jnp.where(is_last, 0.0, nxt[0:n_next, :])


def _lru_body(xf_ref, xb_ref, wg_ref, bg_ref, ch_ref, hf_ref, hb_ref, af_ref, bf_ref, ab_ref, bb_ref, carry_ref):
    i = pl.program_id(0)
    w = BRANCH_W
    r = xf_ref.shape[0]
    steps = r // SUBLANES

    @pl.when(i == 0)
    def _():
        carry_ref[...] = jnp.zeros_like(carry_ref)

    def prepare(direction, x_ref, a_ref, b_ref):
        def sub(sb, _):
            rows = pl.ds(pl.multiple_of(sb * ELEM_ROWS, ELEM_ROWS), ELEM_ROWS)
            xcb = x_ref[rows, :]
            xc = xcb.astype(F32)
            g = _dot(xcb, wg_ref[direction]) + bg_ref[direction]
            tr = jnp.tanh(g[:, 0:w]) + 1.0
            ti = jnp.tanh(g[:, w:2 * w]) + 1.0
            a = jnp.exp2(ch_ref[direction] * tr)
            a_ref[rows, :] = a
            om = 1.0 - a * a
            b_ref[rows, :] = om * lax.rsqrt(jnp.maximum(om, 1e-37)) * (ti * xc)
            return 0

        lax.fori_loop(0, r // ELEM_ROWS, sub, 0, unroll=True)

    prepare(0, xf_ref, af_ref, bf_ref)
    prepare(1, xb_ref, ab_ref, bb_ref)

    def step(t, carry):
        hf, hb = carry
        rf = pl.ds(pl.multiple_of(t * SUBLANES, SUBLANES), SUBLANES)
        hf = af_ref[rf, :] * hf + bf_ref[rf, :]
        bf_ref[rf, :] = hf
        rb = pl.ds(pl.multiple_of((steps - 1 - t) * SUBLANES, SUBLANES), SUBLANES)
        hb = ab_ref[rb, :] * hb + bb_ref[rb, :]
        bb_ref[rb, :] = hb
        return hf, hb

    hf, hb = lax.fori_loop(0, steps, step, (carry_ref[0], carry_ref[1]), unroll=8)
    carry_ref[0] = hf
    carry_ref[1] = hb
    hf_ref[...] = bf_ref[...].astype(BF16)
    hb_ref[...] = bb_ref[...].astype(BF16)


def _lru(xc, layer, w_gate, b_gate, c8):
    rows, w = xc.shape
    r = LRU_STEPS * SUBLANES
    nt = rows // r
    fwd = pl.BlockSpec((r, w), lambda i: (i, 0))
    bwd = pl.BlockSpec((r, w), lambda i: (nt - 1 - i, 0))
    out_shape = jax.ShapeDtypeStruct((rows, w), BF16)
    return pl.pallas_call(
        _lru_body,
        grid=(nt,),
        in_specs=[
            fwd, bwd,
            _layer_spec((2, w, 2 * w), layer),
            _layer_spec((2, 1, 2 * w), layer),
            _layer_spec((2, 1, w), layer),
        ],
        out_specs=[fwd, bwd],
        out_shape=[out_shape, out_shape],
        scratch_shapes=[
            pltpu.VMEM((r, w), F32), pltpu.VMEM((r, w), F32),
            pltpu.VMEM((r, w), F32), pltpu.VMEM((r, w), F32),
            pltpu.VMEM((2, SUBLANES, w), F32),
        ],
        compiler_params=_params("arbitrary"),
        name="rglru",
    )(xc, xc, w_gate, b_gate, c8)


def _pool_body(m_ref, p_ref, n_ref, wp_ref, sc_ref, o_ref, ext_ref, pooled_ref, *, seq_len):
    i = pl.program_id(0)
    nt = pl.num_programs(0)
    r = m_ref.shape[0]
    steps = r // SUBLANES
    halo = max(POOL_WINDOWS) // 2
    hr = halo * SUBLANES
    _fill_extended(ext_ref, m_ref, p_ref, n_ref, hr, hr, i == 0, i == nt - 1)

    t_glob = i * steps + lax.broadcasted_iota(jnp.int32, (r, POOL_GROUP), 0) // SUBLANES
    for g, win in enumerate(POOL_WINDOWS):
        ls = slice(g * POOL_GROUP, (g + 1) * POOL_GROUP)
        e = ext_ref[:, ls]
        lo_t = -halo
        cur = e
        half = 1
        while half < win:
            n = cur.shape[0] - half * SUBLANES
            if half == 1:
                cur = cur[0:n, :] + cur[SUBLANES:SUBLANES + n, :]
                lo_t += 1
            else:
                sh = (half // 2) * SUBLANES
                cur = cur[0:n, :] + cur[2 * sh:2 * sh + n, :]
                lo_t += half // 2
            half *= 2
        off = (0 - lo_t) * SUBLANES
        wsum = cur[off:off + r, :]
        half_w = win // 2
        cnt = (jnp.clip(t_glob + half_w, 0, seq_len) - jnp.clip(t_glob - half_w, 0, seq_len)).astype(F32)
        pooled_ref[:, ls] = wsum / cnt - e[hr:hr + r, :]
    o_ref[...] = (_dot(pooled_ref[...].astype(BF16), wp_ref[...]) * sc_ref[...]).astype(BF16)


def _pool(dx, layer, w_pool_bd, scale, seq_len):
    rows, w = dx.shape
    r = POOL_STEPS * SUBLANES
    nt = rows // r
    hb = (max(POOL_WINDOWS) // 2) * SUBLANES
    per = r // hb
    last_hb = rows // hb - 1
    return pl.pallas_call(
        functools.partial(_pool_body, seq_len=seq_len),
        grid=(nt,),
        in_specs=[
            pl.BlockSpec((r, w), lambda i: (i, 0)),
            pl.BlockSpec((hb, w), lambda i: (jnp.maximum(i * per - 1, 0), 0)),
            pl.BlockSpec((hb, w), lambda i: (jnp.minimum((i + 1) * per, last_hb), 0)),
            _layer_spec((w, w), layer),
            _layer_spec((1, w), layer),
        ],
        out_specs=pl.BlockSpec((r, w), lambda i: (i, 0)),
        out_shape=jax.ShapeDtypeStruct((rows, w), BF16),
        scratch_shapes=[pltpu.VMEM((r + 2 * hb, w), F32), pltpu.VMEM((r, w), F32)],
        compiler_params=_params("parallel"),
        name="pool",
    )(dx, dx, dx, w_pool_bd, scale)


def _attn_body(q_ref, k_ref, v_ref, bias_ref, o_ref, *, grid_rows):
    g = pl.program_id(1)
    lane = lax.broadcasted_iota(jnp.int32, (GRID_W, LANES), 1)
    lo_half = lane < NA_HEAD_DIM
    nkeys = NA_ROWS * GRID_W

    def row_body(rr, _):
        r = g * ATTN_ROWS + rr
        rs = jnp.clip(r - NA_ROWS // 2, 0, grid_rows - NA_ROWS)
        d0 = rs - r + (NA_ROWS - 1)
        k0 = pl.multiple_of(rs * GRID_W, GRID_W)
        q0 = pl.multiple_of(rr * GRID_W, GRID_W)
        scores = []
        for hp in range(NA_HEADS // 2):
            ls = slice(hp * LANES, (hp + 1) * LANES)
            kp = k_ref[pl.ds(k0, nkeys), ls]
            qp = q_ref[pl.ds(q0, GRID_W), ls]
            zero = jnp.zeros_like(qp)
            qm = jnp.concatenate([jnp.where(lo_half, qp, zero), jnp.where(lo_half, zero, qp)], axis=0)
            sc = lax.dot_general(qm, kp, (((1,), (1,)), ((), ())), preferred_element_type=F32)
            bias = jnp.concatenate([bias_ref[hp, d0 + 2 * j] for j in range(NA_ROWS // 2)], axis=-1)
            scores.append(sc + bias)
        maxes = [jnp.max(sc, axis=-1, keepdims=True) for sc in scores]
        probs = [jnp.exp2(sc - m) for sc, m in zip(scores, maxes)]
        inv = [1.0 / jnp.sum(p, axis=-1, keepdims=True) for p in probs]
        for hp in range(NA_HEADS // 2):
            ls = slice(hp * LANES, (hp + 1) * LANES)
            vp = v_ref[pl.ds(k0, nkeys), ls]
            o = _dot(probs[hp].astype(BF16), vp) * inv[hp]
            o_ref[pl.ds(q0, GRID_W), ls] = jnp.where(lo_half, o[0:GRID_W], o[GRID_W:2 * GRID_W]).astype(BF16)
        return 0

    lax.fori_loop(0, ATTN_ROWS, row_body, 0, unroll=2)


def _attn(q, k, v, layer, bias_tab, batch):
    s, bw = q.shape
    w = bw // batch
    grid_rows = s // GRID_W
    ng = grid_rows // ATTN_ROWS
    qr = ATTN_ROWS * GRID_W
    q_spec = pl.BlockSpec((qr, w), lambda b, g: (g, b))
    kv_spec = pl.BlockSpec((s, w), lambda b, g: (0, b))
    return pl.pallas_call(
        functools.partial(_attn_body, grid_rows=grid_rows),
        grid=(batch, ng),
        in_specs=[q_spec, kv_spec, kv_spec, _layer_spec(bias_tab.shape[1:], layer)],
        out_specs=q_spec,
        out_shape=jax.ShapeDtypeStruct((s, bw), BF16),
        compiler_params=_params("parallel", "parallel"),
        name="natten",
    )(q, k, v, bias_tab)


def _s5_assemble(gc_ref, hfc_ref, hbc_ref, qc_ref, g_ref, hf_ref, hb_ref, q_ref):
    half = LANES // 2
    lo = lax.broadcasted_iota(jnp.int32, (S5_GROUP_FLAT, LANES), 1) < half
    g_ref[...] = jnp.zeros_like(g_ref)
    q_ref[...] = jnp.zeros_like(q_ref)
    for q in range(S5_TILE_GROUPS // 2):
        for e in range(2):
            g = 2 * q + e
            rows = slice(e * S5_GROUP_FLAT, (e + 1) * S5_GROUP_FLAT)
            g_ref[q, rows, e * S5_GROUP_FLAT:(e + 1) * S5_GROUP_FLAT] = gc_ref[g]
            for src, dst in ((hfc_ref, hf_ref), (hbc_ref, hb_ref)):
                h = src[g].astype(F32)
                swapped = pltpu.roll(h, half, 1)
                if e == 0:
                    re_tile, im_tile = jnp.where(lo, h, 0.0), jnp.where(lo, swapped, 0.0)
                else:
                    re_tile, im_tile = jnp.where(lo, 0.0, swapped), jnp.where(lo, 0.0, h)
                dst[q, rows, 0:LANES] = re_tile.astype(BF16)
                dst[q, rows, LANES:2 * LANES] = im_tile.astype(BF16)
            for blk in range(4):
                q_ref[q, blk * LANES + e * half:blk * LANES + (e + 1) * half,
                      e * S5_GROUP_FLAT:(e + 1) * S5_GROUP_FLAT] = qc_ref[g, blk * half:(blk + 1) * half, :]


def _s5_body(u_ref, gc_ref, hfc_ref, hbc_ref, qc_ref, a_ref, dsk_ref,
             y_ref, g_ref, hf_ref, hb_ref, q_ref, sinb_ref, sloc_ref, sin_ref, carry_ref):
    p = pl.program_id(1)
    j = pl.program_id(2)
    nblk = pl.num_programs(2)

    @pl.when((j == 0) & (p == 0))
    def _():
        _s5_assemble(gc_ref, hfc_ref, hbc_ref, qc_ref, g_ref, hf_ref, hb_ref, q_ref)

    rb = u_ref.shape[0]
    nk = rb // SUBLANES
    npairs = S5_TILE_GROUPS // 2
    pf = 2 * S5_GROUP_FLAT
    re_t = lambda q: slice(2 * q * LANES, (2 * q + 1) * LANES)
    im_t = lambda q: slice((2 * q + 1) * LANES, (2 * q + 2) * LANES)
    pair_t = lambda q: slice(q * pf, (q + 1) * pf)
    coef_t = lambda q: slice(q * LANES, (q + 1) * LANES)

    @pl.when(j == 0)
    def _():
        carry_ref[...] = jnp.zeros_like(carry_ref)

    def sweep(direction, reverse):
        a_re = a_ref[2 * direction:2 * direction + 1, :]
        a_im = a_ref[2 * direction + 1:2 * direction + 2, :]

        def body(n, state):
            kk = (nk - 1 - n) if reverse else n
            rows = pl.ds(pl.multiple_of(kk * SUBLANES, SUBLANES), SUBLANES)
            new = []
            for q in range(npairs):
                sr, si = state[2 * q], state[2 * q + 1]
                sin_ref[rows, re_t(q)] = sr
                sin_ref[rows, im_t(q)] = si
                ar, ai = a_re[:, coef_t(q)], a_im[:, coef_t(q)]
                new.append(ar * sr - ai * si + sloc_ref[rows, re_t(q)])
                new.append(ar * si + ai * sr + sloc_ref[rows, im_t(q)])
            return tuple(new)

        init = tuple(carry_ref[:, t * LANES:(t + 1) * LANES] for t in range(2 * npairs))
        out = lax.fori_loop(0, nk, body, init, unroll=4)
        for t in range(2 * npairs):
            carry_ref[:, t * LANES:(t + 1) * LANES] = out[t]

    @pl.when(p == 0)
    def _():
        for q in range(npairs):
            sloc_ref[:, pair_t(q)] = _dot(u_ref[:, pair_t(q)], hb_ref[q])
        sweep(1, True)
        blk = nblk - 1 - j
        sinb_ref[pl.ds(pl.multiple_of(blk * rb, rb), rb), :] = sin_ref[...].astype(BF16)

    @pl.when(p == 1)
    def _():
        for q in range(npairs):
            sloc_ref[:, pair_t(q)] = _dot(u_ref[:, pair_t(q)], hf_ref[q])
        sweep(0, False)
        rows_b = pl.ds(pl.multiple_of(j * rb, rb), rb)
        for q in range(npairs):
            u = u_ref[:, pair_t(q)]
            states = jnp.concatenate([sin_ref[:, pair_t(q)].astype(BF16), sinb_ref[rows_b, pair_t(q)]], axis=-1)
            y = _dot(u, g_ref[q]) + _dot(states, q_ref[q]) + dsk_ref[q] * u.astype(F32)
            y_ref[:, pair_t(q)] = y.astype(BF16)


def _s5(cx, layer, mats):
    g_m, hf_m, hb_m, q_m, a8, dsk = mats
    rows, width = cx.shape
    rb = S5_BLOCK_CHUNKS * SUBLANES
    nblk = rows // rb
    tile_w = S5_TILE_GROUPS * S5_GROUP_FLAT
    npairs, pf = S5_TILE_GROUPS // 2, 2 * S5_GROUP_FLAT
    state_w = S5_TILE_GROUPS * 2 * SSM_STATE
    ng, gf = S5_TILE_GROUPS, S5_GROUP_FLAT
    per_tile = lambda n, *shape: pl.BlockSpec((None, n) + shape, lambda o, p, j: (layer, o) + (0,) * len(shape))
    return pl.pallas_call(
        _s5_body,
        grid=(width // tile_w, 2, nblk),
        in_specs=[
            pl.BlockSpec((rb, tile_w), lambda o, p, j: (j + (1 - p) * (nblk - 1 - 2 * j), o)),
            per_tile(ng, gf, gf), per_tile(ng, gf, LANES), per_tile(ng, gf, LANES), per_tile(ng, 2 * LANES, gf),
            pl.BlockSpec((None, None, 4, npairs * LANES), lambda o, p, j: (layer, o, 0, 0)),
            per_tile(npairs, 1, pf),
        ],
        out_specs=pl.BlockSpec((rb, tile_w), lambda o, p, j: (p * j, o)),
        out_shape=jax.ShapeDtypeStruct((rows, width), BF16),
        scratch_shapes=[
            pltpu.VMEM((npairs, pf, pf), BF16),
            pltpu.VMEM((npairs, pf, 2 * LANES), BF16),
            pltpu.VMEM((npairs, pf, 2 * LANES), BF16),
            pltpu.VMEM((npairs, 4 * LANES, pf), BF16),
            pltpu.VMEM((rows, state_w), BF16),
            pltpu.VMEM((rb, state_w), F32),
            pltpu.VMEM((rb, state_w), F32),
            pltpu.VMEM((SUBLANES, state_w), F32),
        ],
        compiler_params=_params("arbitrary", "arbitrary", "arbitrary"),
        name="s5",
    )(cx, g_m, hf_m, hb_m, q_m, a8, dsk)


def _merge_body(x_ref, p_ref, hf_ref, hb_ref, yb_ref, yc_ref, yd_ref, ns_ref,
                wag_ref, wbg_ref, wcg_ref, wdg_ref, wm0_ref, wm1_ref, wm2_ref, wm3_ref,
                wb0_ref, wb1_ref, wb2_ref, wb3_ref, wout_ref, pg_ref, pp_ref, gluw_ref, glub_ref,
                o_ref, stage_ref, pslab_ref, xslab_ref, *, x_batch_major, out_batch_major):
    x = _load_time_major(x_ref, xslab_ref) if x_batch_major else x_ref[...]
    hn = _rms_rows(x, ns_ref[...]).astype(BF16)

    def branch(y, wb_ref, wm_ref):
        return _dot(y.astype(BF16), wb_ref[...]) * _sigmoid(_dot(hn, wm_ref[...]))

    ya = (hf_ref[...].astype(F32) + hb_ref[...].astype(F32)) * _silu(_dot(hn, wag_ref[...]))
    merged = branch(ya, wb0_ref, wm0_ref)
    steps = stage_ref.shape[1] // SUBLANES
    for b in range(SUBLANES):
        for l in range(W_TILES):
            c0 = b * BRANCH_W + l * LANES
            stage_ref[l, pl.ds(b, steps, stride=SUBLANES), :] = yb_ref[:, c0:c0 + LANES].astype(F32)
    yb = jnp.concatenate([stage_ref[l] for l in range(W_TILES)], axis=-1)
    merged = merged + branch(yb * _silu(_dot(hn, wbg_ref[...])), wb1_ref, wm1_ref)
    nchunk = yc_ref.shape[0] // SUBLANES
    tiles = []
    for o in range(W_TILES):
        per_g = [yc_ref[:, (o * S5_TILE_GROUPS + g) * S5_GROUP_FLAT:(o * S5_TILE_GROUPS + g + 1) * S5_GROUP_FLAT]
                 .astype(F32) for g in range(S5_TILE_GROUPS)]
        per_t = [v.reshape(nchunk, 1, SUBLANES, LANES) for v in _block_transpose(per_g)]
        tiles.append(jnp.concatenate(per_t, axis=1).reshape(nchunk * S5_CHUNK * SUBLANES, LANES))
    yg = _gelu_tanh(jnp.concatenate(tiles, axis=-1))
    yc = yg * _sigmoid(_dot(yg.astype(BF16), gluw_ref[...]) + glub_ref[...])
    merged = merged + branch(yc * _silu(_dot(hn, wcg_ref[...])), wb2_ref, wm2_ref)
    merged = merged + branch(yd_ref[...].astype(F32) * _silu(_dot(hn, wdg_ref[...])), wb3_ref, wm3_ref)
    x1 = x + _dot(merged.astype(BF16), wout_ref[...])
    emb = _dot(_load_time_major(p_ref, pslab_ref).astype(BF16), pp_ref[...])
    out = x1 + _sigmoid(_dot(x1.astype(BF16), pg_ref[...])) * emb
    if out_batch_major:
        _store_batch_major(out, o_ref, xslab_ref)
    else:
        o_ref[...] = out


def _merge(x, p, layer, hf, hb, yb, yc, yd, norm_scale, w_in, wbr, wout, pgate, pproj, glu_w, glu_b, batch,
           out_batch_major):
    w = BRANCH_W
    tm = MERGE_ROWS
    pdim = p.shape[-1]
    once = pl.Buffered(1)
    x_batch_major = x.ndim == 3
    d = x.shape[-1]
    rows = x.shape[0] * x.shape[1] if x_batch_major else x.shape[0]
    steps = tm // batch
    bm_spec = pl.BlockSpec((batch, steps, d), lambda i: (0, i, 0))
    x_spec = bm_spec if x_batch_major else pl.BlockSpec((tm, d), lambda i: (i, 0))
    if out_batch_major:
        out_spec, out_shape = bm_spec, jax.ShapeDtypeStruct((batch, rows // batch, d), F32)
    else:
        out_spec, out_shape = pl.BlockSpec((tm, d), lambda i: (i, 0)), jax.ShapeDtypeStruct((rows, d), F32)

    def resident(shape, *tail):
        tail = tail or (0,) * len(shape)
        return pl.BlockSpec((None,) + tuple(shape), lambda i: (layer,) + tuple(tail), pipeline_mode=once)

    row_spec = lambda n: pl.BlockSpec((tm, n), lambda i: (i, 0))
    gate_cols = [resident((d, w), 0, j) for j in (COL_AG, COL_BG, COL_CG, COL_DG)]
    merge_cols = [resident((d, d), 0, COL_MERGE + n) for n in range(N_BRANCH)]
    return pl.pallas_call(
        functools.partial(_merge_body, x_batch_major=x_batch_major, out_batch_major=out_batch_major),
        grid=(rows // tm,),
        in_specs=[
            x_spec,
            pl.BlockSpec((None, batch, steps, pdim), lambda i: (layer, 0, i, 0)),
            row_spec(w), row_spec(w),
            pl.BlockSpec((tm // batch, batch * w), lambda i: (i, 0)),
            pl.BlockSpec((tm // S5_CHUNK, S5_CHUNK * w), lambda i: (i, 0)), row_spec(w),
            resident((1, d)),
            *gate_cols, *merge_cols,
            *[pl.BlockSpec((None, None, w, d), lambda i, n=n: (layer, n, 0, 0), pipeline_mode=once)
              for n in range(N_BRANCH)],
            resident((d, d)), resident((d, d)), resident((pdim, d)),
            resident((w, w)), resident((1, w)),
        ],
        out_specs=out_spec,
        out_shape=out_shape,
        scratch_shapes=[pltpu.VMEM((W_TILES, tm, LANES), F32),
                        pltpu.VMEM((pdim // LANES, tm, LANES), F32),
                        pltpu.VMEM((d // LANES, tm, LANES), F32)],
        compiler_params=_params("parallel"),
        name="merge",
    )(x, p, hf, hb, yb, yc, yd, norm_scale, *([w_in] * 8), *([wbr] * N_BRANCH), wout, pgate, pproj, glu_w, glu_b)


def _block_diag(blocks):
    n, r, c = blocks.shape[-3:]
    spread = np.tile(np.eye(c, dtype=np.float32), (1, n))
    keep = np.kron(np.eye(n, dtype=np.float32), np.ones((r, c), np.float32))
    rows = blocks.reshape(blocks.shape[:-3] + (n * r, c))
    full = jnp.einsum('...rc,cm->...rm', rows, jnp.asarray(spread, blocks.dtype), precision=lax.Precision.HIGHEST)
    return full * jnp.asarray(keep, blocks.dtype)


def _attn_bias_tables(rpb):
    qc = np.arange(GRID_W)[:, None]
    kc = np.arange(GRID_W)[None, :]
    ws = np.clip(qc - NA_COLS // 2, 0, GRID_W - NA_COLS)
    in_win = (kc >= ws) & (kc < ws + NA_COLS)
    dc = np.clip(kc - qc, -(NA_COLS - 1), NA_COLS - 1) + NA_COLS - 1
    onehot = ((dc[None] == np.arange(2 * NA_COLS - 1)[:, None, None]) & in_win[None]).astype(np.float32)
    tab = jnp.einsum('hrm,mqk->hrqk', rpb.astype(F32), jnp.asarray(onehot), precision=lax.Precision.HIGHEST)
    tab = tab + jnp.asarray(np.where(in_win, 0.0, -1e30).astype(np.float32))
    pairs = jnp.concatenate([tab[:, :-1], tab[:, 1:]], axis=-1)
    nr = pairs.shape[1]
    pairs = pairs.reshape(NA_HEADS // 2, 2, nr, GRID_W, 2 * GRID_W)
    return jnp.transpose(pairs, (0, 2, 1, 3, 4)).reshape(NA_HEADS // 2, nr, 2 * GRID_W, 2 * GRID_W)


def _s5_matrices(a_re, a_im, log_dt, b_re, b_im, c_re, c_im, d_skip):
    L, G, P, C = S5_CHUNK, SSM_GROUPS, SSM_STATE, SSM_GROUP
    f32 = F32
    hi = lax.Precision.HIGHEST
    lr = jnp.minimum(a_re.astype(f32), -1e-4)
    li = a_im.astype(f32)
    dt = jnp.exp(log_dt.astype(f32))[..., None]
    steps_n = jnp.arange(L + 1, dtype=f32)[:, None, None, None]
    pw_mag = jnp.exp(steps_n * (lr * dt))
    pw_r = pw_mag * jnp.cos(steps_n * (li * dt))
    pw_i = pw_mag * jnp.sin(steps_n * (li * dt))
    ab_r, ab_i = pw_r[1], pw_i[1]
    nr = ab_r - 1.0
    den = lr * lr + li * li
    fr = ((nr * lr + ab_i * li) / den)[:, :, None, :]
    fi = ((ab_i * lr - nr * li) / den)[:, :, None, :]
    bt_r = jnp.swapaxes(b_re.astype(f32), -1, -2)
    bt_i = jnp.swapaxes(b_im.astype(f32), -1, -2)
    bb_r = fr * bt_r - fi * bt_i
    bb_i = fr * bt_i + fi * bt_r
    cr, ci = c_re.astype(f32), c_im.astype(f32)
    m_r = pw_r[:, :, :, None, :] * bb_r[None] - pw_i[:, :, :, None, :] * bb_i[None]
    m_i = pw_r[:, :, :, None, :] * bb_i[None] + pw_i[:, :, :, None, :] * bb_r[None]
    k_lag = (jnp.einsum('dgop,ndgip->ndgio', cr, m_r[:L], precision=hi)
             - jnp.einsum('dgop,ndgip->ndgio', ci, m_i[:L], precision=hi))
    k_signed = jnp.concatenate([k_lag[:0:-1, 1], k_lag[:1, 0] + k_lag[:1, 1], k_lag[1:, 0]])
    tile_c = jnp.asarray(np.tile(np.eye(C, dtype=np.float32), (1, L)))
    rep_t = jnp.asarray(np.repeat(np.eye(L, dtype=np.float32), C, axis=1))
    k_wide = jnp.einsum('ngio,ol->ngil', k_signed, tile_c, precision=hi)
    lag = (lax.broadcasted_iota(jnp.int32, (1, L, 1, L * C), 3) // C
           - lax.broadcasted_iota(jnp.int32, (1, L, 1, L * C), 1))
    g_mat = 0.0
    for n in range(2 * L - 1):
        g_mat = g_mat + jnp.where(lag == n - (L - 1), k_wide[n][:, None, :, :], 0.0)
    g_mat = g_mat.reshape(G, L * C, L * C)

    def h_mat(direction, powers):
        re = jnp.transpose(m_r[powers, direction], (1, 0, 2, 3)).reshape(G, L * C, P)
        im = jnp.transpose(m_i[powers, direction], (1, 0, 2, 3)).reshape(G, L * C, P)
        return jnp.concatenate([re, im], axis=-1)

    hf_mat = h_mat(0, slice(L - 1, None, -1))
    hb_mat = h_mat(1, slice(0, L))

    def q_rows(direction, powers):
        spread = lambda c: jnp.einsum('gcp,cl->gpl', c[direction], tile_c, precision=hi)
        along_t = lambda w: jnp.einsum('tgp,tl->gpl', w[powers, direction], rep_t, precision=hi)
        c_r, c_i, w_r, w_i = spread(cr), spread(ci), along_t(pw_r), along_t(pw_i)
        return jnp.concatenate([c_r * w_r - c_i * w_i, -(c_r * w_i + c_i * w_r)], axis=1)

    q_mat = jnp.concatenate([q_rows(0, slice(1, L + 1)), q_rows(1, slice(L, 0, -1))], axis=1)
    npair = G // 2
    a8 = jnp.stack([pw_r[L, 0], pw_i[L, 0], pw_r[L, 1], pw_i[L, 1]])
    a8 = jnp.transpose(a8.reshape(4, W_TILES, (S5_TILE_GROUPS // 2) * 2 * P), (1, 0, 2))
    dsk = jnp.tile(d_skip.astype(f32).reshape(G, 1, C), (1, 1, L)).reshape(npair, 1, 2 * L * C)
    bf = lambda m: m.astype(BF16)
    return bf(g_mat), bf(hf_mat), bf(hb_mat), bf(q_mat), a8, dsk


def kernel(x, p, norm_scale, w_in, lru_conv_w, lru_conv_b, lru_w_r, lru_b_r, lru_w_i, lru_b_i, lru_lambda, na_q_gain, na_k_gain, na_rel_bias, ssm_a_re, ssm_a_im, ssm_log_dt, ssm_b_re, ssm_b_im, ssm_c_re, ssm_c_im, ssm_d, ssm_glu_w, ssm_glu_b, pool_w, pool_scale, w_branch, w_out, ple_proj, ple_gate):
    b, s, d = x.shape
    w = BRANCH_W
    assert b == SUBLANES and d == D_MODEL and s % (LRU_STEPS * 4) == 0 and s // GRID_W >= NA_ROWS
    depth = w_in.shape[0]
    rows = s * b

    norm = norm_scale.astype(F32)[:, None, :]
    w_in16 = w_in.astype(BF16)
    ones_bd = _block_diag(jnp.ones((MXU_DIM // NA_HEAD_DIM, NA_HEAD_DIM, NA_HEAD_DIM), BF16))
    q_gain = (jnp.tile(na_q_gain.astype(F32), (1, NA_HEADS)) * (NA_HEAD_DIM ** -0.5 * LOG2E))[:, None, :]
    k_gain = jnp.tile(na_k_gain.astype(F32), (1, NA_HEADS))[:, None, :]
    conv_w = lru_conv_w.astype(F32)
    conv_b = lru_conv_b.astype(F32)[:, None, :]
    gate_w = (0.5 * jnp.concatenate([_block_diag(lru_w_r), _block_diag(lru_w_i)], axis=-1)).astype(BF16)
    gate_b = (0.5 * jnp.concatenate([lru_b_r, lru_b_i], axis=-1).astype(F32))[:, :, None, :]
    c8 = (-0.5 * LRU_C * LOG2E * jax.nn.softplus(-lru_lambda.astype(F32)))[:, :, None, :]
    bias_tab = jax.vmap(_attn_bias_tables)(na_rel_bias.astype(F32) * LOG2E)
    s5_mats = jax.vmap(_s5_matrices)(ssm_a_re, ssm_a_im, ssm_log_dt, ssm_b_re, ssm_b_im, ssm_c_re, ssm_c_im, ssm_d)
    glu_w = ssm_glu_w.astype(BF16)
    glu_b = ssm_glu_b.astype(F32)[:, None, :]
    pool_bd = _block_diag(pool_w).astype(BF16)
    pool_sc = pool_scale.astype(F32)[:, None, :]
    branch_scale = jnp.asarray([0.5] + [1.0] * (N_BRANCH - 1), F32)[None, :, None, None]
    wbr = (w_branch.astype(F32) * branch_scale).astype(BF16)
    wout = w_out.astype(BF16)
    pproj = ple_proj.astype(BF16)
    pgate = ple_gate.astype(BF16)

    xt = x
    for i in range(depth):
        xc, q, k, v, cx, dx = _inproj(xt, i, norm, w_in16, ones_bd, q_gain, k_gain, conv_w, conv_b, b)
        hf, hb = _lru(xc, i, gate_w, gate_b, c8)
        yb = _attn(q, k, v, i, bias_tab, b)
        yc = _s5(cx, i, s5_mats)
        yd = _pool(dx, i, pool_bd, pool_sc, s)
        xt = _merge(xt, p, i, hf, hb, yb, yc, yd, norm, w_in16, wbr, wout, pgate, pproj, glu_w, glu_b, b,
                    out_batch_major=(i == depth - 1))
    return xt
```

```python
import functools

import jax
import jax.numpy as jnp
import numpy as np
from jax import lax
from jax.experimental import pallas as pl
from jax.experimental.pallas import tpu as pltpu

F32 = jnp.float32
BF16 = jnp.bfloat16

D_MODEL = 1024
BRANCH_W = 512
N_BRANCH = 4
NORM_EPS = 1e-6
LOG2E = 1.4426950408889634
GRID_W = 64
LRU_C = 8.0
LRU_CONV_W = 4
LRU_HALO_STEPS = 2
NA_HEADS = 8
NA_HEAD_DIM = 64
NA_ROWS = 8
NA_COLS = 16
SSM_GROUP = 16
SSM_GROUPS = 32
SSM_STATE = 64
POOL_WINDOWS = (2, 4, 8, 16)
POOL_GROUP = 128

SUBLANES = 8
LANES = 128
MXU_DIM = 256
W_TILES = BRANCH_W // LANES
S5_CHUNK = 8

COL_AX, COL_AG, COL_Q, COL_K, COL_V, COL_BG, COL_CX, COL_CG, COL_DX, COL_DG = range(10)
COL_MERGE = 10 * BRANCH_W // D_MODEL

PROJ_ROWS = 1024
MERGE_ROWS = 512
LRU_STEPS = 256
POOL_STEPS = 256
ATTN_ROWS = 16
S5_BLOCK_CHUNKS = 128
ELEM_ROWS = 256

VMEM_LIMIT = 56 * 1024 * 1024


def _params(*sem):
    return pltpu.CompilerParams(dimension_semantics=sem, vmem_limit_bytes=VMEM_LIMIT)


def _dot(a, b):
    return jnp.dot(a, b, preferred_element_type=F32)


def _sigmoid(z):
    return 0.5 * jnp.tanh(0.5 * z) + 0.5


def _silu(z):
    return z * _sigmoid(z)


def _gelu_tanh(y):
    return 0.5 * y * (1.0 + jnp.tanh(0.7978845608028654 * (y + 0.044715 * (y * y * y))))


def _rms_rows(x, g):
    ms = jnp.mean(x * x, axis=-1, keepdims=True)
    return x * lax.rsqrt(ms + NORM_EPS) * g


def _head_mean_square(v, ones_ref):
    v2 = (v * v).astype(BF16)
    n = ones_ref.shape[0]
    sums = [_dot(v2[:, c:c + n], ones_ref[...]) for c in range(0, v.shape[1], n)]
    return jnp.concatenate(sums, axis=-1) * (1.0 / NA_HEAD_DIM)


def _layer_spec(shape, layer, *tail):
    tail = tail or (0,) * len(shape)
    return pl.BlockSpec((None,) + tuple(shape), lambda *_: (layer,) + tuple(tail))


S5_TILE_GROUPS = LANES // SSM_GROUP
S5_GROUP_FLAT = S5_CHUNK * SSM_GROUP


def _block_transpose(xs):
    n = len(xs)
    lane = lax.broadcasted_iota(jnp.int32, (1, LANES), 1)
    cur = list(xs)
    width, stride = LANES // 2, n // 2
    while stride >= 1:
        low = (lane & width) == 0
        nxt = list(cur)
        for i in range(n):
            if i & stride:
                continue
            a, b = cur[i], cur[i + stride]
            nxt[i] = jnp.where(low, a, pltpu.roll(b, width, 1))
            nxt[i + stride] = jnp.where(low, pltpu.roll(a, LANES - width, 1), b)
        cur = nxt
        width //= 2
        stride //= 2
    return cur


def _load_time_major(src_ref, slab_ref, s0=0, s1=None):
    nb, steps, n = src_ref.shape
    s1 = steps if s1 is None else s1
    tiles = n // LANES
    for b in range(nb):
        for l in range(tiles):
            slab_ref[l, pl.ds(b, s1 - s0, stride=nb), :] = src_ref[b, s0:s1, l * LANES:(l + 1) * LANES]
    return jnp.concatenate([slab_ref[l, 0:(s1 - s0) * nb, :] for l in range(tiles)], axis=-1)


def _store_batch_major(val, dst_ref, slab_ref):
    nb, steps, n = dst_ref.shape
    tiles = n // LANES
    for l in range(tiles):
        slab_ref[l] = val[:, l * LANES:(l + 1) * LANES]
    for b in range(nb):
        for l in range(tiles):
            dst_ref[b, :, l * LANES:(l + 1) * LANES] = slab_ref[l, pl.ds(b, steps, stride=nb), :]


def _inproj_body(x_ref, xp_ref, xn_ref, ns_ref, wax_ref, wq_ref, wk_ref, wv_ref, wcx_ref, wdx_ref, ones_ref,
                 qg_ref, kg_ref, cw_ref, cb_ref,
                 xc_ref, q_ref, k_ref, v_ref, cx_ref, dx_ref, stage_ref, *slabs):
    i = pl.program_id(0)
    halo = LRU_HALO_STEPS * SUBLANES
    if slabs:
        x = _load_time_major(x_ref, slabs[0])
        hs = xp_ref.shape[1]
        x_prev = _load_time_major(xp_ref, slabs[1], hs - LRU_HALO_STEPS, hs)
        x_next = _load_time_major(xn_ref, slabs[1], 0, LRU_HALO_STEPS)
    else:
        x = x_ref[...]
        x_prev = xp_ref[xp_ref.shape[0] - halo:, :]
        x_next = xn_ref[0:halo, :]
    hn = _rms_rows(x, ns_ref[...]).astype(BF16)
    steps = stage_ref.shape[1] // SUBLANES

    def to_batch_lanes(val, out_ref):
        for l in range(W_TILES):
            stage_ref[l] = val[:, l * LANES:(l + 1) * LANES]
        for b in range(SUBLANES):
            for l in range(W_TILES):
                c0 = b * BRANCH_W + l * LANES
                out_ref[:, c0:c0 + LANES] = stage_ref[l, pl.ds(b, steps, stride=SUBLANES), :].astype(BF16)

    hn_ext = jnp.concatenate([_rms_rows(x_prev, ns_ref[...]).astype(BF16), hn,
                              _rms_rows(x_next, ns_ref[...]).astype(BF16)], axis=0)
    ax = _dot(hn_ext, wax_ref[...])
    r = hn.shape[0]
    row = lax.broadcasted_iota(jnp.int32, (r + 2 * halo, 1), 0)
    outside = ((row < halo) & (i == 0)) | ((row >= r + halo) & (i == pl.num_programs(0) - 1))
    ax = jnp.where(outside, 0.0, ax)
    xc = cb_ref[...]
    for tap in range(LRU_CONV_W):
        xc = xc + cw_ref[tap:tap + 1, :] * ax[tap * SUBLANES:tap * SUBLANES + r, :]
    xc_ref[...] = xc.astype(BF16)
    q = _dot(hn, wq_ref[...])
    to_batch_lanes(q * lax.rsqrt(_head_mean_square(q, ones_ref) + NORM_EPS) * qg_ref[...], q_ref)
    k = _dot(hn, wk_ref[...])
    to_batch_lanes(k * lax.rsqrt(_head_mean_square(k, ones_ref) + NORM_EPS) * kg_ref[...], k_ref)
    to_batch_lanes(_dot(hn, wv_ref[...]), v_ref)
    cxv = _dot(hn, wcx_ref[...])
    nchunk = cxv.shape[0] // (S5_CHUNK * SUBLANES)
    for o in range(W_TILES):
        tile3 = cxv[:, o * LANES:(o + 1) * LANES].reshape(nchunk, S5_CHUNK * SUBLANES, LANES)
        per_t = [tile3[:, t * SUBLANES:(t + 1) * SUBLANES, :].reshape(nchunk * SUBLANES, LANES)
                 for t in range(S5_CHUNK)]
        for g, ug in enumerate(_block_transpose(per_t)):
            c0 = (o * S5_TILE_GROUPS + g) * S5_GROUP_FLAT
            cx_ref[:, c0:c0 + S5_GROUP_FLAT] = ug.astype(BF16)
    dx_ref[...] = _dot(hn, wdx_ref[...]).astype(BF16)


def _inproj(x, layer, norm_scale, w_in, ones_bd, q_gain, k_gain, conv_w, conv_b, batch):
    w = BRANCH_W
    tm = PROJ_ROWS
    batch_major = x.ndim == 3
    d = x.shape[-1]
    rows = x.shape[0] * x.shape[1] if batch_major else x.shape[0]
    s = rows // batch
    nt = rows // tm
    if batch_major:
        hs = SUBLANES
        per = tm // batch // hs
        x_spec = pl.BlockSpec((batch, tm // batch, d), lambda i: (0, i, 0))
        prev_spec = pl.BlockSpec((batch, hs, d), lambda i: (0, jnp.maximum(i * per - 1, 0), 0))
        next_spec = pl.BlockSpec((batch, hs, d), lambda i: (0, jnp.minimum((i + 1) * per, s // hs - 1), 0))
        slabs = [pltpu.VMEM((d // LANES, tm, LANES), F32),
                 pltpu.VMEM((d // LANES, LRU_HALO_STEPS * batch, LANES), F32)]
    else:
        hb = LRU_HALO_STEPS * batch
        per = tm // hb
        x_spec = pl.BlockSpec((tm, d), lambda i: (i, 0))
        prev_spec = pl.BlockSpec((hb, d), lambda i: (jnp.maximum(i * per - 1, 0), 0))
        next_spec = pl.BlockSpec((hb, d), lambda i: (jnp.minimum((i + 1) * per, rows // hb - 1), 0))
        slabs = []
    wcol = lambda j: pl.BlockSpec((None, d, w), lambda i: (layer, 0, j))
    tm_spec = pl.BlockSpec((tm, w), lambda i: (i, 0))
    bl_spec = pl.BlockSpec((tm // batch, batch * w), lambda i: (i, 0))
    tm_shape = jax.ShapeDtypeStruct((rows, w), BF16)
    bl_shape = jax.ShapeDtypeStruct((s, batch * w), BF16)
    gf_spec = pl.BlockSpec((tm // S5_CHUNK, S5_CHUNK * w), lambda i: (i, 0))
    gf_shape = jax.ShapeDtypeStruct((rows // S5_CHUNK, S5_CHUNK * w), BF16)
    return pl.pallas_call(
        _inproj_body,
        grid=(nt,),
        in_specs=[
            x_spec, prev_spec, next_spec,
            _layer_spec((1, d), layer),
            wcol(COL_AX), wcol(COL_Q), wcol(COL_K), wcol(COL_V), wcol(COL_CX), wcol(COL_DX),
            pl.BlockSpec(ones_bd.shape, lambda i: (0, 0)),
            _layer_spec((1, w), layer), _layer_spec((1, w), layer),
            _layer_spec((LRU_CONV_W, w), layer), _layer_spec((1, w), layer),
        ],
        out_specs=[tm_spec, bl_spec, bl_spec, bl_spec, gf_spec, tm_spec],
        out_shape=[tm_shape, bl_shape, bl_shape, bl_shape, gf_shape, tm_shape],
        scratch_shapes=[pltpu.VMEM((W_TILES, tm, LANES), F32)] + slabs,
        compiler_params=_params("parallel"),
        name="inproj",
    )(x, x, x, norm_scale, w_in, w_in, w_in, w_in, w_in, w_in, ones_bd, q_gain, k_gain, conv_w, conv_b)


def _fill_extended(ext_ref, main_ref, prev_ref, next_ref, n_prev, n_next, is_first, is_last):
    r = main_ref.shape[0]
    prev = prev_ref[...].astype(F32)
    ext_ref[0:n_prev, :] = jnp.where(is_first, 0.0, prev[prev.shape[0] - n_prev:, :])
    ext_ref[n_prev:n_prev + r, :] = main_ref[...].astype(F32)
    nxt = next_ref[...].astype(F32)
    ext_ref[n_prev + r:n_prev + r + n_next, :] = jnp.where(is_last, 0.0, nxt[0:n_next, :])


def _lru_body(xf_ref, xb_ref, wg_ref, bg_ref, ch_ref, hf_ref, hb_ref, af_ref, bf_ref, ab_ref, bb_ref, carry_ref):
    i = pl.program_id(0)
    w = BRANCH_W
    r = xf_ref.shape[0]
    steps = r // SUBLANES

    @pl.when(i == 0)
    def _():
        carry_ref[...] = jnp.zeros_like(carry_ref)

    def prepare(direction, x_ref, a_ref, b_ref):
        def sub(sb, _):
            rows = pl.ds(pl.multiple_of(sb * ELEM_ROWS, ELEM_ROWS), ELEM_ROWS)
            xcb = x_ref[rows, :]
            xc = xcb.astype(F32)
            g = _dot(xcb, wg_ref[direction]) + bg_ref[direction]
            tr = jnp.tanh(g[:, 0:w]) + 1.0
            ti = jnp.tanh(g[:, w:2 * w]) + 1.0
            a = jnp.exp2(ch_ref[direction] * tr)
            a_ref[rows, :] = a
            om = 1.0 - a * a
            b_ref[rows, :] = om * lax.rsqrt(jnp.maximum(om, 1e-37)) * (ti * xc)
            return 0

        lax.fori_loop(0, r // ELEM_ROWS, sub, 0, unroll=True)

    prepare(0, xf_ref, af_ref, bf_ref)
    prepare(1, xb_ref, ab_ref, bb_ref)

    def step(t, carry):
        hf, hb = carry
        rf = pl.ds(pl.multiple_of(t * SUBLANES, SUBLANES), SUBLANES)
        hf = af_ref[rf, :] * hf + bf_ref[rf, :]
        bf_ref[rf, :] = hf
        rb = pl.ds(pl.multiple_of((steps - 1 - t) * SUBLANES, SUBLANES), SUBLANES)
        hb = ab_ref[rb, :] * hb + bb_ref[rb, :]
        bb_ref[rb, :] = hb
        return hf, hb

    hf, hb = lax.fori_loop(0, steps, step, (carry_ref[0], carry_ref[1]), unroll=8)
    carry_ref[0] = hf
    carry_ref[1] = hb
    hf_ref[...] = bf_ref[...].astype(BF16)
    hb_ref[...] = bb_ref[...].astype(BF16)


def _lru(xc, layer, w_gate, b_gate, c8):
    rows, w = xc.shape
    r = LRU_STEPS * SUBLANES
    nt = rows // r
    fwd = pl.BlockSpec((r, w), lambda i: (i, 0))
    bwd = pl.BlockSpec((r, w), lambda i: (nt - 1 - i, 0))
    out_shape = jax.ShapeDtypeStruct((rows, w), BF16)
    return pl.pallas_call(
        _lru_body,
        grid=(nt,),
        in_specs=[
            fwd, bwd,
            _layer_spec((2, w, 2 * w), layer),
            _layer_spec((2, 1, 2 * w), layer),
            _layer_spec((2, 1, w), layer),
        ],
        out_specs=[fwd, bwd],
        out_shape=[out_shape, out_shape],
        scratch_shapes=[
            pltpu.VMEM((r, w), F32), pltpu.VMEM((r, w), F32),
            pltpu.VMEM((r, w), F32), pltpu.VMEM((r, w), F32),
            pltpu.VMEM((2, SUBLANES, w), F32),
        ],
        compiler_params=_params("arbitrary"),
        name="rglru",
    )(xc, xc, w_gate, b_gate, c8)


def _pool_body(m_ref, p_ref, n_ref, wp_ref, sc_ref, o_ref, ext_ref, pooled_ref, *, seq_len):
    i = pl.program_id(0)
    nt = pl.num_programs(0)
    r = m_ref.shape[0]
    steps = r // SUBLANES
    halo = max(POOL_WINDOWS) // 2
    hr = halo * SUBLANES
    _fill_extended(ext_ref, m_ref, p_ref, n_ref, hr, hr, i == 0, i == nt - 1)

    def pooled_minus_self(clipped):
        if clipped:
            t_glob = i * steps + lax.broadcasted_iota(jnp.int32, (r, POOL_GROUP), 0) // SUBLANES
        for g, win in enumerate(POOL_WINDOWS):
            ls = slice(g * POOL_GROUP, (g + 1) * POOL_GROUP)
            e = ext_ref[:, ls]
            lo_t = -halo
            cur = e
            half = 1
            while half < win:
                n = cur.shape[0] - half * SUBLANES
                if half == 1:
                    cur = cur[0:n, :] + cur[SUBLANES:SUBLANES + n, :]
                    lo_t += 1
                else:
                    sh = (half // 2) * SUBLANES
                    cur = cur[0:n, :] + cur[2 * sh:2 * sh + n, :]
                    lo_t += half // 2
                half *= 2
            off = (0 - lo_t) * SUBLANES
            wsum = cur[off:off + r, :]
            half_w = win // 2
            if clipped:
                cnt = (jnp.clip(t_glob + half_w, 0, seq_len) - jnp.clip(t_glob - half_w, 0, seq_len)).astype(F32)
                mean = wsum / cnt
            else:
                mean = wsum * (1.0 / win)
            pooled_ref[:, ls] = mean - e[hr:hr + r, :]

    at_edge = (i == 0) | (i == nt - 1)

    @pl.when(at_edge)
    def _():
        pooled_minus_self(True)

    @pl.when(jnp.logical_not(at_edge))
    def _():
        pooled_minus_self(False)

    o_ref[...] = (_dot(pooled_ref[...].astype(BF16), wp_ref[...]) * sc_ref[...]).astype(BF16)


def _pool(dx, layer, w_pool_bd, scale, seq_len):
    rows, w = dx.shape
    r = POOL_STEPS * SUBLANES
    nt = rows // r
    assert POOL_STEPS >= max(POOL_WINDOWS) // 2
    hb = (max(POOL_WINDOWS) // 2) * SUBLANES
    per = r // hb
    last_hb = rows // hb - 1
    return pl.pallas_call(
        functools.partial(_pool_body, seq_len=seq_len),
        grid=(nt,),
        in_specs=[
            pl.BlockSpec((r, w), lambda i: (i, 0)),
            pl.BlockSpec((hb, w), lambda i: (jnp.maximum(i * per - 1, 0), 0)),
            pl.BlockSpec((hb, w), lambda i: (jnp.minimum((i + 1) * per, last_hb), 0)),
            _layer_spec((w, w), layer),
            _layer_spec((1, w), layer),
        ],
        out_specs=pl.BlockSpec((r, w), lambda i: (i, 0)),
        out_shape=jax.ShapeDtypeStruct((rows, w), BF16),
        scratch_shapes=[pltpu.VMEM((r + 2 * hb, w), F32), pltpu.VMEM((r, w), F32)],
        compiler_params=_params("parallel"),
        name="pool",
    )(dx, dx, dx, w_pool_bd, scale)


def _attn_body(q_ref, k_ref, v_ref, bias_ref, o_ref, *, grid_rows):
    g = pl.program_id(1)
    lane = lax.broadcasted_iota(jnp.int32, (GRID_W, LANES), 1)
    lo_half = lane < NA_HEAD_DIM
    nkeys = NA_ROWS * GRID_W

    def row_body(rr, _):
        r = g * ATTN_ROWS + rr
        rs = jnp.clip(r - NA_ROWS // 2, 0, grid_rows - NA_ROWS)
        d0 = rs - r + (NA_ROWS - 1)
        k0 = pl.multiple_of(rs * GRID_W, GRID_W)
        q0 = pl.multiple_of(rr * GRID_W, GRID_W)
        scores = []
        for hp in range(NA_HEADS // 2):
            ls = slice(hp * LANES, (hp + 1) * LANES)
            kp = k_ref[pl.ds(k0, nkeys), ls]
            qp = q_ref[pl.ds(q0, GRID_W), ls]
            zero = jnp.zeros_like(qp)
            qm = jnp.concatenate([jnp.where(lo_half, qp, zero), jnp.where(lo_half, zero, qp)], axis=0)
            sc = lax.dot_general(qm, kp, (((1,), (1,)), ((), ())), preferred_element_type=F32)
            bias = jnp.concatenate([bias_ref[hp, d0 + 2 * j] for j in range(NA_ROWS // 2)], axis=-1)
            scores.append(sc + bias)
        maxes = [jnp.max(sc, axis=-1, keepdims=True) for sc in scores]
        probs = [jnp.exp2(sc - m) for sc, m in zip(scores, maxes)]
        inv = [1.0 / jnp.sum(p, axis=-1, keepdims=True) for p in probs]
        for hp in range(NA_HEADS // 2):
            ls = slice(hp * LANES, (hp + 1) * LANES)
            vp = v_ref[pl.ds(k0, nkeys), ls]
            o = _dot(probs[hp].astype(BF16), vp) * inv[hp]
            o_ref[pl.ds(q0, GRID_W), ls] = jnp.where(lo_half, o[0:GRID_W], o[GRID_W:2 * GRID_W]).astype(BF16)
        return 0

    lax.fori_loop(0, ATTN_ROWS, row_body, 0, unroll=2)


def _attn(q, k, v, layer, bias_tab, batch):
    s, bw = q.shape
    w = bw // batch
    grid_rows = s // GRID_W
    ng = grid_rows // ATTN_ROWS
    qr = ATTN_ROWS * GRID_W
    q_spec = pl.BlockSpec((qr, w), lambda b, g: (g, b))
    kv_spec = pl.BlockSpec((s, w), lambda b, g: (0, b))
    return pl.pallas_call(
        functools.partial(_attn_body, grid_rows=grid_rows),
        grid=(batch, ng),
        in_specs=[q_spec, kv_spec, kv_spec, _layer_spec(bias_tab.shape[1:], layer)],
        out_specs=q_spec,
        out_shape=jax.ShapeDtypeStruct((s, bw), BF16),
        compiler_params=_params("parallel", "parallel"),
        name="natten",
    )(q, k, v, bias_tab)


def _s5_assemble(gc_ref, hfc_ref, hbc_ref, qc_ref, g_ref, hf_ref, hb_ref, q_ref):
    half = LANES // 2
    lo = lax.broadcasted_iota(jnp.int32, (S5_GROUP_FLAT, LANES), 1) < half
    g_ref[...] = jnp.zeros_like(g_ref)
    q_ref[...] = jnp.zeros_like(q_ref)
    for q in range(S5_TILE_GROUPS // 2):
        for e in range(2):
            g = 2 * q + e
            rows = slice(e * S5_GROUP_FLAT, (e + 1) * S5_GROUP_FLAT)
            g_ref[q, rows, e * S5_GROUP_FLAT:(e + 1) * S5_GROUP_FLAT] = gc_ref[g]
            for src, dst in ((hfc_ref, hf_ref), (hbc_ref, hb_ref)):
                h = src[g].astype(F32)
                swapped = pltpu.roll(h, half, 1)
                if e == 0:
                    re_tile, im_tile = jnp.where(lo, h, 0.0), jnp.where(lo, swapped, 0.0)
                else:
                    re_tile, im_tile = jnp.where(lo, 0.0, swapped), jnp.where(lo, 0.0, h)
                dst[q, rows, 0:LANES] = re_tile.astype(BF16)
                dst[q, rows, LANES:2 * LANES] = im_tile.astype(BF16)
            for blk in range(4):
                q_ref[q, blk * LANES + e * half:blk * LANES + (e + 1) * half,
                      e * S5_GROUP_FLAT:(e + 1) * S5_GROUP_FLAT] = qc_ref[g, blk * half:(blk + 1) * half, :]


def _s5_body(u_ref, gc_ref, hfc_ref, hbc_ref, qc_ref, a_ref, dsk_ref,
             y_ref, g_ref, hf_ref, hb_ref, q_ref, sinb_ref, sloc_ref, sin_ref, carry_ref):
    p = pl.program_id(1)
    j = pl.program_id(2)
    nblk = pl.num_programs(2)

    @pl.when((j == 0) & (p == 0))
    def _():
        _s5_assemble(gc_ref, hfc_ref, hbc_ref, qc_ref, g_ref, hf_ref, hb_ref, q_ref)

    rb = u_ref.shape[0]
    nk = rb // SUBLANES
    npairs = S5_TILE_GROUPS // 2
    pf = 2 * S5_GROUP_FLAT
    re_t = lambda q: slice(2 * q * LANES, (2 * q + 1) * LANES)
    im_t = lambda q: slice((2 * q + 1) * LANES, (2 * q + 2) * LANES)
    pair_t = lambda q: slice(q * pf, (q + 1) * pf)
    coef_t = lambda q: slice(q * LANES, (q + 1) * LANES)

    @pl.when(j == 0)
    def _():
        carry_ref[...] = jnp.zeros_like(carry_ref)

    def sweep(direction, reverse):
        a_re = a_ref[2 * direction:2 * direction + 1, :]
        a_im = a_ref[2 * direction + 1:2 * direction + 2, :]

        def body(n, state):
            kk = (nk - 1 - n) if reverse else n
            rows = pl.ds(pl.multiple_of(kk * SUBLANES, SUBLANES), SUBLANES)
            new = []
            for q in range(npairs):
                sr, si = state[2 * q], state[2 * q + 1]
                sin_ref[rows, re_t(q)] = sr
                sin_ref[rows, im_t(q)] = si
                ar, ai = a_re[:, coef_t(q)], a_im[:, coef_t(q)]
                new.append(ar * sr - ai * si + sloc_ref[rows, re_t(q)])
                new.append(ar * si + ai * sr + sloc_ref[rows, im_t(q)])
            return tuple(new)

        init = tuple(carry_ref[:, t * LANES:(t + 1) * LANES] for t in range(2 * npairs))
        out = lax.fori_loop(0, nk, body, init, unroll=4)
        for t in range(2 * npairs):
            carry_ref[:, t * LANES:(t + 1) * LANES] = out[t]

    @pl.when(p == 0)
    def _():
        for q in range(npairs):
            sloc_ref[:, pair_t(q)] = _dot(u_ref[:, pair_t(q)], hb_ref[q])
        sweep(1, True)
        blk = nblk - 1 - j
        sinb_ref[pl.ds(pl.multiple_of(blk * rb, rb), rb), :] = sin_ref[...].astype(BF16)

    @pl.when(p == 1)
    def _():
        for q in range(npairs):
            sloc_ref[:, pair_t(q)] = _dot(u_ref[:, pair_t(q)], hf_ref[q])
        sweep(0, False)
        rows_b = pl.ds(pl.multiple_of(j * rb, rb), rb)
        for q in range(npairs):
            u = u_ref[:, pair_t(q)]
            states = jnp.concatenate([sin_ref[:, pair_t(q)].astype(BF16), sinb_ref[rows_b, pair_t(q)]], axis=-1)
            y = _dot(u, g_ref[q]) + _dot(states, q_ref[q]) + dsk_ref[q] * u.astype(F32)
            y_ref[:, pair_t(q)] = y.astype(BF16)


def _s5(cx, layer, mats):
    g_m, hf_m, hb_m, q_m, a8, dsk = mats
    rows, width = cx.shape
    rb = S5_BLOCK_CHUNKS * SUBLANES
    nblk = rows // rb
    tile_w = S5_TILE_GROUPS * S5_GROUP_FLAT
    npairs, pf = S5_TILE_GROUPS // 2, 2 * S5_GROUP_FLAT
    state_w = S5_TILE_GROUPS * 2 * SSM_STATE
    ng, gf = S5_TILE_GROUPS, S5_GROUP_FLAT
    per_tile = lambda n, *shape: pl.BlockSpec((None, n) + shape, lambda o, p, j: (layer, o) + (0,) * len(shape))
    return pl.pallas_call(
        _s5_body,
        grid=(width // tile_w, 2, nblk),
        in_specs=[
            pl.BlockSpec((rb, tile_w), lambda o, p, j: (j + (1 - p) * (nblk - 1 - 2 * j), o)),
            per_tile(ng, gf, gf), per_tile(ng, gf, LANES), per_tile(ng, gf, LANES), per_tile(ng, 2 * LANES, gf),
            pl.BlockSpec((None, None, 4, npairs * LANES), lambda o, p, j: (layer, o, 0, 0)),
            per_tile(npairs, 1, pf),
        ],
        out_specs=pl.BlockSpec((rb, tile_w), lambda o, p, j: (p * j, o)),
        out_shape=jax.ShapeDtypeStruct((rows, width), BF16),
        scratch_shapes=[
            pltpu.VMEM((npairs, pf, pf), BF16),
            pltpu.VMEM((npairs, pf, 2 * LANES), BF16),
            pltpu.VMEM((npairs, pf, 2 * LANES), BF16),
            pltpu.VMEM((npairs, 4 * LANES, pf), BF16),
            pltpu.VMEM((rows, state_w), BF16),
            pltpu.VMEM((rb, state_w), F32),
            pltpu.VMEM((rb, state_w), F32),
            pltpu.VMEM((SUBLANES, state_w), F32),
        ],
        compiler_params=_params("arbitrary", "arbitrary", "arbitrary"),
        name="s5",
    )(cx, g_m, hf_m, hb_m, q_m, a8, dsk)


def _merge_body(x_ref, p_ref, hf_ref, hb_ref, yb_ref, yc_ref, yd_ref, ns_ref,
                wag_ref, wbg_ref, wcg_ref, wdg_ref, wm0_ref, wm1_ref, wm2_ref, wm3_ref,
                wb0_ref, wb1_ref, wb2_ref, wb3_ref, wout_ref, pg_ref, pp_ref, gluw_ref, glub_ref,
                o_ref, stage_ref, pslab_ref, xslab_ref, *, x_batch_major, out_batch_major):
    x = _load_time_major(x_ref, xslab_ref) if x_batch_major else x_ref[...]
    hn = _rms_rows(x, ns_ref[...]).astype(BF16)

    def branch(y, wb_ref, wm_ref):
        return _dot(y.astype(BF16), wb_ref[...]) * _sigmoid(_dot(hn, wm_ref[...]))

    ya = (hf_ref[...].astype(F32) + hb_ref[...].astype(F32)) * _silu(_dot(hn, wag_ref[...]))
    merged = branch(ya, wb0_ref, wm0_ref)
    steps = stage_ref.shape[1] // SUBLANES
    for b in range(SUBLANES):
        for l in range(W_TILES):
            c0 = b * BRANCH_W + l * LANES
            stage_ref[l, pl.ds(b, steps, stride=SUBLANES), :] = yb_ref[:, c0:c0 + LANES].astype(F32)
    yb = jnp.concatenate([stage_ref[l] for l in range(W_TILES)], axis=-1)
    merged = merged + branch(yb * _silu(_dot(hn, wbg_ref[...])), wb1_ref, wm1_ref)
    nchunk = yc_ref.shape[0] // SUBLANES
    tiles = []
    for o in range(W_TILES):
        per_g = [yc_ref[:, (o * S5_TILE_GROUPS + g) * S5_GROUP_FLAT:(o * S5_TILE_GROUPS + g + 1) * S5_GROUP_FLAT]
                 .astype(F32) for g in range(S5_TILE_GROUPS)]
        per_t = [v.reshape(nchunk, 1, SUBLANES, LANES) for v in _block_transpose(per_g)]
        tiles.append(jnp.concatenate(per_t, axis=1).reshape(nchunk * S5_CHUNK * SUBLANES, LANES))
    yg = _gelu_tanh(jnp.concatenate(tiles, axis=-1))
    yc = yg * _sigmoid(_dot(yg.astype(BF16), gluw_ref[...]) + glub_ref[...])
    merged = merged + branch(yc * _silu(_dot(hn, wcg_ref[...])), wb2_ref, wm2_ref)
    merged = merged + branch(yd_ref[...].astype(F32) * _silu(_dot(hn, wdg_ref[...])), wb3_ref, wm3_ref)
    x1 = x + _dot(merged.astype(BF16), wout_ref[...])
    emb = _dot(_load_time_major(p_ref, pslab_ref).astype(BF16), pp_ref[...])
    out = x1 + _sigmoid(_dot(x1.astype(BF16), pg_ref[...])) * emb
    if out_batch_major:
        _store_batch_major(out, o_ref, xslab_ref)
    else:
        o_ref[...] = out


def _merge(x, p, layer, hf, hb, yb, yc, yd, norm_scale, w_in, wbr, wout, pgate, pproj, glu_w, glu_b, batch,
           out_batch_major):
    w = BRANCH_W
    tm = MERGE_ROWS
    pdim = p.shape[-1]
    once = pl.Buffered(1)
    x_batch_major = x.ndim == 3
    d = x.shape[-1]
    rows = x.shape[0] * x.shape[1] if x_batch_major else x.shape[0]
    steps = tm // batch
    bm_spec = pl.BlockSpec((batch, steps, d), lambda i: (0, i, 0))
    x_spec = bm_spec if x_batch_major else pl.BlockSpec((tm, d), lambda i: (i, 0))
    if out_batch_major:
        out_spec, out_shape = bm_spec, jax.ShapeDtypeStruct((batch, rows // batch, d), F32)
    else:
        out_spec, out_shape = pl.BlockSpec((tm, d), lambda i: (i, 0)), jax.ShapeDtypeStruct((rows, d), F32)

    def resident(shape, *tail):
        tail = tail or (0,) * len(shape)
        return pl.BlockSpec((None,) + tuple(shape), lambda i: (layer,) + tuple(tail), pipeline_mode=once)

    row_spec = lambda n: pl.BlockSpec((tm, n), lambda i: (i, 0))
    gate_cols = [resident((d, w), 0, j) for j in (COL_AG, COL_BG, COL_CG, COL_DG)]
    merge_cols = [resident((d, d), 0, COL_MERGE + n) for n in range(N_BRANCH)]
    return pl.pallas_call(
        functools.partial(_merge_body, x_batch_major=x_batch_major, out_batch_major=out_batch_major),
        grid=(rows // tm,),
        in_specs=[
            x_spec,
            pl.BlockSpec((None, batch, steps, pdim), lambda i: (layer, 0, i, 0)),
            row_spec(w), row_spec(w),
            pl.BlockSpec((tm // batch, batch * w), lambda i: (i, 0)),
            pl.BlockSpec((tm // S5_CHUNK, S5_CHUNK * w), lambda i: (i, 0)), row_spec(w),
            resident((1, d)),
            *gate_cols, *merge_cols,
            *[pl.BlockSpec((None, None, w, d), lambda i, n=n: (layer, n, 0, 0), pipeline_mode=once)
              for n in range(N_BRANCH)],
            resident((d, d)), resident((d, d)), resident((pdim, d)),
            resident((w, w)), resident((1, w)),
        ],
        out_specs=out_spec,
        out_shape=out_shape,
        scratch_shapes=[pltpu.VMEM((W_TILES, tm, LANES), F32),
                        pltpu.VMEM((pdim // LANES, tm, LANES), F32),
                        pltpu.VMEM((d // LANES, tm, LANES), F32)],
        compiler_params=_params("parallel"),
        name="merge",
    )(x, p, hf, hb, yb, yc, yd, norm_scale, *([w_in] * 8), *([wbr] * N_BRANCH), wout, pgate, pproj, glu_w, glu_b)


def _block_diag(blocks):
    n, r, c = blocks.shape[-3:]
    spread = np.tile(np.eye(c, dtype=np.float32), (1, n))
    keep = np.kron(np.eye(n, dtype=np.float32), np.ones((r, c), np.float32))
    rows = blocks.reshape(blocks.shape[:-3] + (n * r, c))
    full = jnp.einsum('...rc,cm->...rm', rows, jnp.asarray(spread, blocks.dtype), precision=lax.Precision.HIGHEST)
    return full * jnp.asarray(keep, blocks.dtype)


def _attn_bias_tables(rpb):
    qc = np.arange(GRID_W)[:, None]
    kc = np.arange(GRID_W)[None, :]
    ws = np.clip(qc - NA_COLS // 2, 0, GRID_W - NA_COLS)
    in_win = (kc >= ws) & (kc < ws + NA_COLS)
    dc = np.clip(kc - qc, -(NA_COLS - 1), NA_COLS - 1) + NA_COLS - 1
    onehot = ((dc[None] == np.arange(2 * NA_COLS - 1)[:, None, None]) & in_win[None]).astype(np.float32)
    tab = jnp.einsum('hrm,mqk->hrqk', rpb.astype(F32), jnp.asarray(onehot), precision=lax.Precision.HIGHEST)
    tab = tab + jnp.asarray(np.where(in_win, 0.0, -1e30).astype(np.float32))
    pairs = jnp.concatenate([tab[:, :-1], tab[:, 1:]], axis=-1)
    nr = pairs.shape[1]
    pairs = pairs.reshape(NA_HEADS // 2, 2, nr, GRID_W, 2 * GRID_W)
    return jnp.transpose(pairs, (0, 2, 1, 3, 4)).reshape(NA_HEADS // 2, nr, 2 * GRID_W, 2 * GRID_W)


def _s5_matrices(a_re, a_im, log_dt, b_re, b_im, c_re, c_im, d_skip):
    L, G, P, C = S5_CHUNK, SSM_GROUPS, SSM_STATE, SSM_GROUP
    f32 = F32
    hi = lax.Precision.HIGHEST
    lr = jnp.minimum(a_re.astype(f32), -1e-4)
    li = a_im.astype(f32)
    dt = jnp.exp(log_dt.astype(f32))[..., None]
    steps_n = jnp.arange(L + 1, dtype=f32)[:, None, None, None]
    pw_mag = jnp.exp(steps_n * (lr * dt))
    pw_r = pw_mag * jnp.cos(steps_n * (li * dt))
    pw_i = pw_mag * jnp.sin(steps_n * (li * dt))
    ab_r, ab_i = pw_r[1], pw_i[1]
    nr = ab_r - 1.0
    den = lr * lr + li * li
    fr = ((nr * lr + ab_i * li) / den)[:, :, None, :]
    fi = ((ab_i * lr - nr * li) / den)[:, :, None, :]
    bt_r = jnp.swapaxes(b_re.astype(f32), -1, -2)
    bt_i = jnp.swapaxes(b_im.astype(f32), -1, -2)
    bb_r = fr * bt_r - fi * bt_i
    bb_i = fr * bt_i + fi * bt_r
    cr, ci = c_re.astype(f32), c_im.astype(f32)
    m_r = pw_r[:, :, :, None, :] * bb_r[None] - pw_i[:, :, :, None, :] * bb_i[None]
    m_i = pw_r[:, :, :, None, :] * bb_i[None] + pw_i[:, :, :, None, :] * bb_r[None]
    k_lag = (jnp.einsum('dgop,ndgip->ndgio', cr, m_r[:L], precision=hi)
             - jnp.einsum('dgop,ndgip->ndgio', ci, m_i[:L], precision=hi))
    k_signed = jnp.concatenate([k_lag[:0:-1, 1], k_lag[:1, 0] + k_lag[:1, 1], k_lag[1:, 0]])
    tile_c = jnp.asarray(np.tile(np.eye(C, dtype=np.float32), (1, L)))
    rep_t = jnp.asarray(np.repeat(np.eye(L, dtype=np.float32), C, axis=1))
    k_wide = jnp.einsum('ngio,ol->ngil', k_signed, tile_c, precision=hi)
    lag = (lax.broadcasted_iota(jnp.int32, (1, L, 1, L * C), 3) // C
           - lax.broadcasted_iota(jnp.int32, (1, L, 1, L * C), 1))
    g_mat = 0.0
    for n in range(2 * L - 1):
        g_mat = g_mat + jnp.where(lag == n - (L - 1), k_wide[n][:, None, :, :], 0.0)
    g_mat = g_mat.reshape(G, L * C, L * C)

    def h_mat(direction, powers):
        re = jnp.transpose(m_r[powers, direction], (1, 0, 2, 3)).reshape(G, L * C, P)
        im = jnp.transpose(m_i[powers, direction], (1, 0, 2, 3)).reshape(G, L * C, P)
        return jnp.concatenate([re, im], axis=-1)

    hf_mat = h_mat(0, slice(L - 1, None, -1))
    hb_mat = h_mat(1, slice(0, L))

    def q_rows(direction, powers):
        spread = lambda c: jnp.einsum('gcp,cl->gpl', c[direction], tile_c, precision=hi)
        along_t = lambda w: jnp.einsum('tgp,tl->gpl', w[powers, direction], rep_t, precision=hi)
        c_r, c_i, w_r, w_i = spread(cr), spread(ci), along_t(pw_r), along_t(pw_i)
        return jnp.concatenate([c_r * w_r - c_i * w_i, -(c_r * w_i + c_i * w_r)], axis=1)

    q_mat = jnp.concatenate([q_rows(0, slice(1, L + 1)), q_rows(1, slice(L, 0, -1))], axis=1)
    npair = G // 2
    a8 = jnp.stack([pw_r[L, 0], pw_i[L, 0], pw_r[L, 1], pw_i[L, 1]])
    a8 = jnp.transpose(a8.reshape(4, W_TILES, (S5_TILE_GROUPS // 2) * 2 * P), (1, 0, 2))
    dsk = jnp.tile(d_skip.astype(f32).reshape(G, 1, C), (1, 1, L)).reshape(npair, 1, 2 * L * C)
    bf = lambda m: m.astype(BF16)
    return bf(g_mat), bf(hf_mat), bf(hb_mat), bf(q_mat), a8, dsk


def kernel(x, p, norm_scale, w_in, lru_conv_w, lru_conv_b, lru_w_r, lru_b_r, lru_w_i, lru_b_i, lru_lambda, na_q_gain, na_k_gain, na_rel_bias, ssm_a_re, ssm_a_im, ssm_log_dt, ssm_b_re, ssm_b_im, ssm_c_re, ssm_c_im, ssm_d, ssm_glu_w, ssm_glu_b, pool_w, pool_scale, w_branch, w_out, ple_proj, ple_gate):
    b, s, d = x.shape
    w = BRANCH_W
    assert b == SUBLANES and d == D_MODEL and s % (LRU_STEPS * 4) == 0 and s // GRID_W >= NA_ROWS
    depth = w_in.shape[0]
    rows = s * b

    norm = norm_scale.astype(F32)[:, None, :]
    w_in16 = w_in.astype(BF16)
    ones_bd = _block_diag(jnp.ones((MXU_DIM // NA_HEAD_DIM, NA_HEAD_DIM, NA_HEAD_DIM), BF16))
    q_gain = (jnp.tile(na_q_gain.astype(F32), (1, NA_HEADS)) * (NA_HEAD_DIM ** -0.5 * LOG2E))[:, None, :]
    k_gain = jnp.tile(na_k_gain.astype(F32), (1, NA_HEADS))[:, None, :]
    conv_w = lru_conv_w.astype(F32)
    conv_b = lru_conv_b.astype(F32)[:, None, :]
    gate_w = (0.5 * jnp.concatenate([_block_diag(lru_w_r), _block_diag(lru_w_i)], axis=-1)).astype(BF16)
    gate_b = (0.5 * jnp.concatenate([lru_b_r, lru_b_i], axis=-1).astype(F32))[:, :, None, :]
    c8 = (-0.5 * LRU_C * LOG2E * jax.nn.softplus(-lru_lambda.astype(F32)))[:, :, None, :]
    bias_tab = jax.vmap(_attn_bias_tables)(na_rel_bias.astype(F32) * LOG2E)
    s5_mats = jax.vmap(_s5_matrices)(ssm_a_re, ssm_a_im, ssm_log_dt, ssm_b_re, ssm_b_im, ssm_c_re, ssm_c_im, ssm_d)
    glu_w = ssm_glu_w.astype(BF16)
    glu_b = ssm_glu_b.astype(F32)[:, None, :]
    pool_bd = _block_diag(pool_w).astype(BF16)
    pool_sc = pool_scale.astype(F32)[:, None, :]
    branch_scale = jnp.asarray([0.5] + [1.0] * (N_BRANCH - 1), F32)[None, :, None, None]
    wbr = (w_branch.astype(F32) * branch_scale).astype(BF16)
    wout = w_out.astype(BF16)
    pproj = ple_proj.astype(BF16)
    pgate = ple_gate.astype(BF16)

    xt = x
    for i in range(depth):
        xc, q, k, v, cx, dx = _inproj(xt, i, norm, w_in16, ones_bd, q_gain, k_gain, conv_w, conv_b, b)
        hf, hb = _lru(xc, i, gate_w, gate_b, c8)
        yb = _attn(q, k, v, i, bias_tab, b)
        yc = _s5(cx, i, s5_mats)
        yd = _pool(dx, i, pool_bd, pool_sc, s)
        xt = _merge(xt, p, i, hf, hb, yb, yc, yd, norm, w_in16, wbr, wout, pgate, pproj, glu_w, glu_b, b,
                    out_batch_major=(i == depth - 1))
    return xt
```

```python
import functools

import jax
import jax.numpy as jnp
import numpy as np
from jax import lax
from jax.experimental import pallas as pl
from jax.experimental.pallas import tpu as pltpu

F32 = jnp.float32
BF16 = jnp.bfloat16

D_MODEL = 1024
BRANCH_W = 512
N_BRANCH = 4
NORM_EPS = 1e-6
LOG2E = 1.4426950408889634
GRID_W = 64
LRU_C = 8.0
LRU_CONV_W = 4
LRU_HALO_STEPS = 2
NA_HEADS = 8
NA_HEAD_DIM = 64
NA_ROWS = 8
NA_COLS = 16
SSM_GROUP = 16
SSM_GROUPS = 32
SSM_STATE = 64
POOL_WINDOWS = (2, 4, 8, 16)
POOL_GROUP = 128

SUBLANES = 8
LANES = 128
MXU_DIM = 256
W_TILES = BRANCH_W // LANES
S5_CHUNK = 8

COL_AX, COL_AG, COL_Q, COL_K, COL_V, COL_BG, COL_CX, COL_CG, COL_DX, COL_DG = range(10)
COL_MERGE = 10 * BRANCH_W // D_MODEL

PROJ_ROWS = 1024
MERGE_ROWS = 512
LRU_STEPS = 256
POOL_STEPS = 256
ATTN_ROWS = 16
S5_BLOCK_CHUNKS = 128
ELEM_ROWS = 256

VMEM_LIMIT = 56 * 1024 * 1024


def _params(*sem):
    return pltpu.CompilerParams(dimension_semantics=sem, vmem_limit_bytes=VMEM_LIMIT)


def _dot(a, b):
    return jnp.dot(a, b, preferred_element_type=F32)


def _sigmoid(z):
    return 0.5 * jnp.tanh(0.5 * z) + 0.5


def _silu(z):
    return z * _sigmoid(z)


def _gelu_tanh(y):
    return 0.5 * y * (1.0 + jnp.tanh(0.7978845608028654 * (y + 0.044715 * (y * y * y))))


def _rms_rows(x, g):
    ms = jnp.mean(x * x, axis=-1, keepdims=True)
    return x * lax.rsqrt(ms + NORM_EPS) * g


def _head_mean_square(v, ones_ref):
    v2 = (v * v).astype(BF16)
    n = ones_ref.shape[0]
    sums = [_dot(v2[:, c:c + n], ones_ref[...]) for c in range(0, v.shape[1], n)]
    return jnp.concatenate(sums, axis=-1) * (1.0 / NA_HEAD_DIM)


def _layer_spec(shape, layer, *tail):
    tail = tail or (0,) * len(shape)
    return pl.BlockSpec((None,) + tuple(shape), lambda *_: (layer,) + tuple(tail))


S5_TILE_GROUPS = LANES // SSM_GROUP
S5_GROUP_FLAT = S5_CHUNK * SSM_GROUP


def _block_transpose(xs):
    n = len(xs)
    lane = lax.broadcasted_iota(jnp.int32, (1, LANES), 1)
    cur = list(xs)
    width, stride = LANES // 2, n // 2
    while stride >= 1:
        low = (lane & width) == 0
        nxt = list(cur)
        for i in range(n):
            if i & stride:
                continue
            a, b = cur[i], cur[i + stride]
            nxt[i] = jnp.where(low, a, pltpu.roll(b, width, 1))
            nxt[i + stride] = jnp.where(low, pltpu.roll(a, LANES - width, 1), b)
        cur = nxt
        width //= 2
        stride //= 2
    return cur


def _load_time_major(src_ref, slab_ref, s0=0, s1=None):
    nb, steps, n = src_ref.shape
    s1 = steps if s1 is None else s1
    tiles = n // LANES
    for b in range(nb):
        for l in range(tiles):
            slab_ref[l, pl.ds(b, s1 - s0, stride=nb), :] = src_ref[b, s0:s1, l * LANES:(l + 1) * LANES]
    return jnp.concatenate([slab_ref[l, 0:(s1 - s0) * nb, :] for l in range(tiles)], axis=-1)


def _store_batch_major(val, dst_ref, slab_ref):
    nb, steps, n = dst_ref.shape
    tiles = n // LANES
    for l in range(tiles):
        slab_ref[l] = val[:, l * LANES:(l + 1) * LANES]
    for b in range(nb):
        for l in range(tiles):
            dst_ref[b, :, l * LANES:(l + 1) * LANES] = slab_ref[l, pl.ds(b, steps, stride=nb), :]


def _inproj_body(x_ref, xp_ref, xn_ref, ns_ref, wax_ref, wq_ref, wk_ref, wv_ref, wcx_ref, wdx_ref, ones_ref,
                 qg_ref, kg_ref, cw_ref, cb_ref,
                 xc_ref, q_ref, k_ref, v_ref, cx_ref, dx_ref, stage_ref, *slabs):
    i = pl.program_id(0)
    halo = LRU_HALO_STEPS * SUBLANES
    if slabs:
        x = _load_time_major(x_ref, slabs[0])
        hs = xp_ref.shape[1]
        x_prev = _load_time_major(xp_ref, slabs[1], hs - LRU_HALO_STEPS, hs)
        x_next = _load_time_major(xn_ref, slabs[1], 0, LRU_HALO_STEPS)
    else:
        x = x_ref[...]
        x_prev = xp_ref[xp_ref.shape[0] - halo:, :]
        x_next = xn_ref[0:halo, :]
    hn = _rms_rows(x, ns_ref[...]).astype(BF16)
    steps = stage_ref.shape[1] // SUBLANES

    def to_batch_lanes(val, out_ref):
        for l in range(W_TILES):
            stage_ref[l] = val[:, l * LANES:(l + 1) * LANES]
        for b in range(SUBLANES):
            for l in range(W_TILES):
                c0 = b * BRANCH_W + l * LANES
                out_ref[:, c0:c0 + LANES] = stage_ref[l, pl.ds(b, steps, stride=SUBLANES), :].astype(BF16)

    hn_ext = jnp.concatenate([_rms_rows(x_prev, ns_ref[...]).astype(BF16), hn,
                              _rms_rows(x_next, ns_ref[...]).astype(BF16)], axis=0)
    ax = _dot(hn_ext, wax_ref[...])
    r = hn.shape[0]
    row = lax.broadcasted_iota(jnp.int32, (r + 2 * halo, 1), 0)
    outside = ((row < halo) & (i == 0)) | ((row >= r + halo) & (i == pl.num_programs(0) - 1))
    ax = jnp.where(outside, 0.0, ax)
    xc = cb_ref[...]
    for tap in range(LRU_CONV_W):
        xc = xc + cw_ref[tap:tap + 1, :] * ax[tap * SUBLANES:tap * SUBLANES + r, :]
    xc_ref[...] = xc.astype(BF16)
    q = _dot(hn, wq_ref[...])
    to_batch_lanes(q * lax.rsqrt(_head_mean_square(q, ones_ref) + NORM_EPS) * qg_ref[...], q_ref)
    k = _dot(hn, wk_ref[...])
    to_batch_lanes(k * lax.rsqrt(_head_mean_square(k, ones_ref) + NORM_EPS) * kg_ref[...], k_ref)
    to_batch_lanes(_dot(hn, wv_ref[...]), v_ref)
    cxv = _dot(hn, wcx_ref[...])
    nchunk = cxv.shape[0] // (S5_CHUNK * SUBLANES)
    for o in range(W_TILES):
        tile3 = cxv[:, o * LANES:(o + 1) * LANES].reshape(nchunk, S5_CHUNK * SUBLANES, LANES)
        per_t = [tile3[:, t * SUBLANES:(t + 1) * SUBLANES, :].reshape(nchunk * SUBLANES, LANES)
                 for t in range(S5_CHUNK)]
        for g, ug in enumerate(_block_transpose(per_t)):
            c0 = (o * S5_TILE_GROUPS + g) * S5_GROUP_FLAT
            cx_ref[:, c0:c0 + S5_GROUP_FLAT] = ug.astype(BF16)
    dx_ref[...] = _dot(hn, wdx_ref[...]).astype(BF16)


def _inproj(x, layer, norm_scale, w_in, ones_bd, q_gain, k_gain, conv_w, conv_b, batch):
    w = BRANCH_W
    tm = PROJ_ROWS
    batch_major = x.ndim == 3
    d = x.shape[-1]
    rows = x.shape[0] * x.shape[1] if batch_major else x.shape[0]
    s = rows // batch
    nt = rows // tm
    if batch_major:
        hs = SUBLANES
        per = tm // batch // hs
        x_spec = pl.BlockSpec((batch, tm // batch, d), lambda i: (0, i, 0))
        prev_spec = pl.BlockSpec((batch, hs, d), lambda i: (0, jnp.maximum(i * per - 1, 0), 0))
        next_spec = pl.BlockSpec((batch, hs, d), lambda i: (0, jnp.minimum((i + 1) * per, s // hs - 1), 0))
        slabs = [pltpu.VMEM((d // LANES, tm, LANES), F32),
                 pltpu.VMEM((d // LANES, LRU_HALO_STEPS * batch, LANES), F32)]
    else:
        hb = LRU_HALO_STEPS * batch
        per = tm // hb
        x_spec = pl.BlockSpec((tm, d), lambda i: (i, 0))
        prev_spec = pl.BlockSpec((hb, d), lambda i: (jnp.maximum(i * per - 1, 0), 0))
        next_spec = pl.BlockSpec((hb, d), lambda i: (jnp.minimum((i + 1) * per, rows // hb - 1), 0))
        slabs = []
    wcol = lambda j: pl.BlockSpec((None, d, w), lambda i: (layer, 0, j))
    tm_spec = pl.BlockSpec((tm, w), lambda i: (i, 0))
    bl_spec = pl.BlockSpec((tm // batch, batch * w), lambda i: (i, 0))
    tm_shape = jax.ShapeDtypeStruct((rows, w), BF16)
    bl_shape = jax.ShapeDtypeStruct((s, batch * w), BF16)
    gf_spec = pl.BlockSpec((tm // S5_CHUNK, S5_CHUNK * w), lambda i: (i, 0))
    gf_shape = jax.ShapeDtypeStruct((rows // S5_CHUNK, S5_CHUNK * w), BF16)
    return pl.pallas_call(
        _inproj_body,
        grid=(nt,),
        in_specs=[
            x_spec, prev_spec, next_spec,
            _layer_spec((1, d), layer),
            wcol(COL_AX), wcol(COL_Q), wcol(COL_K), wcol(COL_V), wcol(COL_CX), wcol(COL_DX),
            pl.BlockSpec(ones_bd.shape, lambda i: (0, 0)),
            _layer_spec((1, w), layer), _layer_spec((1, w), layer),
            _layer_spec((LRU_CONV_W, w), layer), _layer_spec((1, w), layer),
        ],
        out_specs=[tm_spec, bl_spec, bl_spec, bl_spec, gf_spec, tm_spec],
        out_shape=[tm_shape, bl_shape, bl_shape, bl_shape, gf_shape, tm_shape],
        scratch_shapes=[pltpu.VMEM((W_TILES, tm, LANES), F32)] + slabs,
        compiler_params=_params("parallel"),
        name="inproj",
    )(x, x, x, norm_scale, w_in, w_in, w_in, w_in, w_in, w_in, ones_bd, q_gain, k_gain, conv_w, conv_b)


def _fill_extended(ext_ref, main_ref, prev_ref, next_ref, n_prev, n_next, is_first, is_last):
    r = main_ref.shape[0]
    prev = prev_ref[...].astype(F32)
    ext_ref[0:n_prev, :] = jnp.where(is_first, 0.0, prev[prev.shape[0] - n_prev:, :])
    ext_ref[n_prev:n_prev + r, :] = main_ref[...].astype(F32)
    nxt = next_ref[...].astype(F32)
    ext_ref[n_prev + r:n_prev + r + n_next, :] = jnp.where(is_last, 0.0, nxt[0:n_next, :])


def _lru_body(xf_ref, xb_ref, wg_ref, bg_ref, ch_ref, hf_ref, hb_ref, af_ref, bf_ref, ab_ref, bb_ref, carry_ref):
    i = pl.program_id(0)
    w = BRANCH_W
    r = xf_ref.shape[0]
    steps = r // SUBLANES

    @pl.when(i == 0)
    def _():
        carry_ref[...] = jnp.zeros_like(carry_ref)

    def prepare(direction, x_ref, a_ref, b_ref):
        def sub(sb, _):
            rows = pl.ds(pl.multiple_of(sb * ELEM_ROWS, ELEM_ROWS), ELEM_ROWS)
            xcb = x_ref[rows, :]
            xc = xcb.astype(F32)
            g = _dot(xcb, wg_ref[direction]) + bg_ref[direction]
            tr = jnp.tanh(g[:, 0:w]) + 1.0
            ti = jnp.tanh(g[:, w:2 * w]) + 1.0
            a = jnp.exp2(ch_ref[direction] * tr)
            a_ref[rows, :] = a
            om = 1.0 - a * a
            b_ref[rows, :] = om * lax.rsqrt(jnp.maximum(om, 1e-37)) * (ti * xc)
            return 0

        lax.fori_loop(0, r // ELEM_ROWS, sub, 0, unroll=True)

    prepare(0, xf_ref, af_ref, bf_ref)
    prepare(1, xb_ref, ab_ref, bb_ref)

    def step(t, carry):
        hf, hb = carry
        rf = pl.ds(pl.multiple_of(t * SUBLANES, SUBLANES), SUBLANES)
        hf = af_ref[rf, :] * hf + bf_ref[rf, :]
        bf_ref[rf, :] = hf
        rb = pl.ds(pl.multiple_of((steps - 1 - t) * SUBLANES, SUBLANES), SUBLANES)
        hb = ab_ref[rb, :] * hb + bb_ref[rb, :]
        bb_ref[rb, :] = hb
        return hf, hb

    hf, hb = lax.fori_loop(0, steps, step, (carry_ref[0], carry_ref[1]), unroll=8)
    carry_ref[0] = hf
    carry_ref[1] = hb
    hf_ref[...] = bf_ref[...].astype(BF16)
    hb_ref[...] = bb_ref[...].astype(BF16)


def _lru(xc, layer, w_gate, b_gate, c8):
    rows, w = xc.shape
    r = LRU_STEPS * SUBLANES
    nt = rows // r
    fwd = pl.BlockSpec((r, w), lambda i: (i, 0))
    bwd = pl.BlockSpec((r, w), lambda i: (nt - 1 - i, 0))
    out_shape = jax.ShapeDtypeStruct((rows, w), BF16)
    return pl.pallas_call(
        _lru_body,
        grid=(nt,),
        in_specs=[
            fwd, bwd,
            _layer_spec((2, w, 2 * w), layer),
            _layer_spec((2, 1, 2 * w), layer),
            _layer_spec((2, 1, w), layer),
        ],
        out_specs=[fwd, bwd],
        out_shape=[out_shape, out_shape],
        scratch_shapes=[
            pltpu.VMEM((r, w), F32), pltpu.VMEM((r, w), F32),
            pltpu.VMEM((r, w), F32), pltpu.VMEM((r, w), F32),
            pltpu.VMEM((2, SUBLANES, w), F32),
        ],
        compiler_params=_params("arbitrary"),
        name="rglru",
    )(xc, xc, w_gate, b_gate, c8)


def _pool_body(m_ref, p_ref, n_ref, wp_ref, sc_ref, o_ref, ext_ref, pooled_ref, *, seq_len):
    i = pl.program_id(0)
    nt = pl.num_programs(0)
    r = m_ref.shape[0]
    steps = r // SUBLANES
    halo = max(POOL_WINDOWS) // 2
    hr = halo * SUBLANES
    _fill_extended(ext_ref, m_ref, p_ref, n_ref, hr, hr, i == 0, i == nt - 1)

    def pooled_minus_self(clipped):
        if clipped:
            t_glob = i * steps + lax.broadcasted_iota(jnp.int32, (r, POOL_GROUP), 0) // SUBLANES
        for g, win in enumerate(POOL_WINDOWS):
            ls = slice(g * POOL_GROUP, (g + 1) * POOL_GROUP)
            e = ext_ref[:, ls]
            lo_t = -halo
            cur = e
            half = 1
            while half < win:
                n = cur.shape[0] - half * SUBLANES
                if half == 1:
                    cur = cur[0:n, :] + cur[SUBLANES:SUBLANES + n, :]
                    lo_t += 1
                else:
                    sh = (half // 2) * SUBLANES
                    cur = cur[0:n, :] + cur[2 * sh:2 * sh + n, :]
                    lo_t += half // 2
                half *= 2
            off = (0 - lo_t) * SUBLANES
            wsum = cur[off:off + r, :]
            half_w = win // 2
            if clipped:
                cnt = (jnp.clip(t_glob + half_w, 0, seq_len) - jnp.clip(t_glob - half_w, 0, seq_len)).astype(F32)
                mean = wsum / cnt
            else:
                mean = wsum * (1.0 / win)
            pooled_ref[:, ls] = mean - e[hr:hr + r, :]

    at_edge = (i == 0) | (i == nt - 1)

    @pl.when(at_edge)
    def _():
        pooled_minus_self(True)

    @pl.when(jnp.logical_not(at_edge))
    def _():
        pooled_minus_self(False)

    o_ref[...] = (_dot(pooled_ref[...].astype(BF16), wp_ref[...]) * sc_ref[...]).astype(BF16)


def _pool(dx, layer, w_pool_bd, scale, seq_len):
    rows, w = dx.shape
    r = POOL_STEPS * SUBLANES
    nt = rows // r
    assert POOL_STEPS >= max(POOL_WINDOWS) // 2
    hb = (max(POOL_WINDOWS) // 2) * SUBLANES
    per = r // hb
    last_hb = rows // hb - 1
    return pl.pallas_call(
        functools.partial(_pool_body, seq_len=seq_len),
        grid=(nt,),
        in_specs=[
            pl.BlockSpec((r, w), lambda i: (i, 0)),
            pl.BlockSpec((hb, w), lambda i: (jnp.maximum(i * per - 1, 0), 0)),
            pl.BlockSpec((hb, w), lambda i: (jnp.minimum((i + 1) * per, last_hb), 0)),
            _layer_spec((w, w), layer),
            _layer_spec((1, w), layer),
        ],
        out_specs=pl.BlockSpec((r, w), lambda i: (i, 0)),
        out_shape=jax.ShapeDtypeStruct((rows, w), BF16),
        scratch_shapes=[pltpu.VMEM((r + 2 * hb, w), F32), pltpu.VMEM((r, w), F32)],
        compiler_params=_params("parallel"),
        name="pool",
    )(dx, dx, dx, w_pool_bd, scale)


def _attn_body(q_ref, k_ref, v_ref, bias_ref, o_ref, *, grid_rows):
    g = pl.program_id(1)
    lane = lax.broadcasted_iota(jnp.int32, (GRID_W, LANES), 1)
    lo_half = lane < NA_HEAD_DIM
    nkeys = NA_ROWS * GRID_W

    def row_body(rr, _):
        r = g * ATTN_ROWS + rr
        rs = jnp.clip(r - NA_ROWS // 2, 0, grid_rows - NA_ROWS)
        d0 = rs - r + (NA_ROWS - 1)
        k0 = pl.multiple_of(rs * GRID_W, GRID_W)
        q0 = pl.multiple_of(rr * GRID_W, GRID_W)
        scores = []
        for hp in range(NA_HEADS // 2):
            ls = slice(hp * LANES, (hp + 1) * LANES)
            kp = k_ref[pl.ds(k0, nkeys), ls]
            qp = q_ref[pl.ds(q0, GRID_W), ls]
            zero = jnp.zeros_like(qp)
            qm = jnp.concatenate([jnp.where(lo_half, qp, zero), jnp.where(lo_half, zero, qp)], axis=0)
            sc = lax.dot_general(qm, kp, (((1,), (1,)), ((), ())), preferred_element_type=F32)
            bias = jnp.concatenate([bias_ref[hp, d0 + 2 * j] for j in range(NA_ROWS // 2)], axis=-1)
            scores.append(sc + bias)
        maxes = [jnp.max(sc, axis=-1, keepdims=True) for sc in scores]
        probs = [jnp.exp2(sc - m) for sc, m in zip(scores, maxes)]
        inv = [1.0 / jnp.sum(p, axis=-1, keepdims=True) for p in probs]
        for hp in range(NA_HEADS // 2):
            ls = slice(hp * LANES, (hp + 1) * LANES)
            vp = v_ref[pl.ds(k0, nkeys), ls]
            o = _dot(probs[hp].astype(BF16), vp) * inv[hp]
            o_ref[pl.ds(q0, GRID_W), ls] = jnp.where(lo_half, o[0:GRID_W], o[GRID_W:2 * GRID_W]).astype(BF16)
        return 0

    lax.fori_loop(0, ATTN_ROWS, row_body, 0, unroll=2)


def _attn(q, k, v, layer, bias_tab, batch):
    s, bw = q.shape
    w = bw // batch
    grid_rows = s // GRID_W
    ng = grid_rows // ATTN_ROWS
    qr = ATTN_ROWS * GRID_W
    q_spec = pl.BlockSpec((qr, w), lambda b, g: (g, b))
    kv_spec = pl.BlockSpec((s, w), lambda b, g: (0, b))
    return pl.pallas_call(
        functools.partial(_attn_body, grid_rows=grid_rows),
        grid=(batch, ng),
        in_specs=[q_spec, kv_spec, kv_spec, _layer_spec(bias_tab.shape[1:], layer)],
        out_specs=q_spec,
        out_shape=jax.ShapeDtypeStruct((s, bw), BF16),
        compiler_params=_params("parallel", "parallel"),
        name="natten",
    )(q, k, v, bias_tab)


def _s5_assemble(gc_ref, hfc_ref, hbc_ref, qc_ref, g_ref, hf_ref, hb_ref, q_ref):
    half = LANES // 2
    lo = lax.broadcasted_iota(jnp.int32, (S5_GROUP_FLAT, LANES), 1) < half
    g_ref[...] = jnp.zeros_like(g_ref)
    q_ref[...] = jnp.zeros_like(q_ref)
    for q in range(S5_TILE_GROUPS // 2):
        for e in range(2):
            g = 2 * q + e
            rows = slice(e * S5_GROUP_FLAT, (e + 1) * S5_GROUP_FLAT)
            g_ref[q, rows, e * S5_GROUP_FLAT:(e + 1) * S5_GROUP_FLAT] = gc_ref[g]
            for src, dst in ((hfc_ref, hf_ref), (hbc_ref, hb_ref)):
                h = src[g].astype(F32)
                swapped = pltpu.roll(h, half, 1)
                if e == 0:
                    re_tile, im_tile = jnp.where(lo, h, 0.0), jnp.where(lo, swapped, 0.0)
                else:
                    re_tile, im_tile = jnp.where(lo, 0.0, swapped), jnp.where(lo, 0.0, h)
                dst[q, rows, 0:LANES] = re_tile.astype(BF16)
                dst[q, rows, LANES:2 * LANES] = im_tile.astype(BF16)
            for blk in range(4):
                q_ref[q, blk * LANES + e * half:blk * LANES + (e + 1) * half,
                      e * S5_GROUP_FLAT:(e + 1) * S5_GROUP_FLAT] = qc_ref[g, blk * half:(blk + 1) * half, :]


def _s5_body(u_ref, gc_ref, hfc_ref, hbc_ref, qc_ref, a_ref, dsk_ref,
             y_ref, g_ref, hf_ref, hb_ref, q_ref, sinb_ref, sloc_ref, sin_ref, carry_ref):
    p = pl.program_id(1)
    j = pl.program_id(2)
    nblk = pl.num_programs(2)

    @pl.when((j == 0) & (p == 0))
    def _():
        _s5_assemble(gc_ref, hfc_ref, hbc_ref, qc_ref, g_ref, hf_ref, hb_ref, q_ref)

    rb = u_ref.shape[0]
    nk = rb // SUBLANES
    npairs = S5_TILE_GROUPS // 2
    pf = 2 * S5_GROUP_FLAT
    re_t = lambda q: slice(2 * q * LANES, (2 * q + 1) * LANES)
    im_t = lambda q: slice((2 * q + 1) * LANES, (2 * q + 2) * LANES)
    pair_t = lambda q: slice(q * pf, (q + 1) * pf)
    coef_t = lambda q: slice(q * LANES, (q + 1) * LANES)

    @pl.when(j == 0)
    def _():
        carry_ref[...] = jnp.zeros_like(carry_ref)

    def sweep(direction, reverse):
        a_re = a_ref[2 * direction:2 * direction + 1, :]
        a_im = a_ref[2 * direction + 1:2 * direction + 2, :]

        def body(n, state):
            kk = (nk - 1 - n) if reverse else n
            rows = pl.ds(pl.multiple_of(kk * SUBLANES, SUBLANES), SUBLANES)
            new = []
            for q in range(npairs):
                sr, si = state[2 * q], state[2 * q + 1]
                sin_ref[rows, re_t(q)] = sr
                sin_ref[rows, im_t(q)] = si
                ar, ai = a_re[:, coef_t(q)], a_im[:, coef_t(q)]
                new.append(ar * sr - ai * si + sloc_ref[rows, re_t(q)])
                new.append(ar * si + ai * sr + sloc_ref[rows, im_t(q)])
            return tuple(new)

        init = tuple(carry_ref[:, t * LANES:(t + 1) * LANES] for t in range(2 * npairs))
        out = lax.fori_loop(0, nk, body, init, unroll=4)
        for t in range(2 * npairs):
            carry_ref[:, t * LANES:(t + 1) * LANES] = out[t]

    @pl.when(p == 0)
    def _():
        for q in range(npairs):
            sloc_ref[:, pair_t(q)] = _dot(u_ref[:, pair_t(q)], hb_ref[q])
        sweep(1, True)
        blk = nblk - 1 - j
        sinb_ref[pl.ds(pl.multiple_of(blk * rb, rb), rb), :] = sin_ref[...].astype(BF16)

    @pl.when(p == 1)
    def _():
        for q in range(npairs):
            sloc_ref[:, pair_t(q)] = _dot(u_ref[:, pair_t(q)], hf_ref[q])
        sweep(0, False)
        rows_b = pl.ds(pl.multiple_of(j * rb, rb), rb)
        for q in range(npairs):
            u = u_ref[:, pair_t(q)]
            states = jnp.concatenate([sin_ref[:, pair_t(q)].astype(BF16), sinb_ref[rows_b, pair_t(q)]], axis=-1)
            y = _dot(u, g_ref[q]) + _dot(states, q_ref[q]) + dsk_ref[q] * u.astype(F32)
            y_ref[:, pair_t(q)] = y.astype(BF16)


def _s5(cx, layer, mats):
    g_m, hf_m, hb_m, q_m, a8, dsk = mats
    rows, width = cx.shape
    rb = S5_BLOCK_CHUNKS * SUBLANES
    nblk = rows // rb
    tile_w = S5_TILE_GROUPS * S5_GROUP_FLAT
    npairs, pf = S5_TILE_GROUPS // 2, 2 * S5_GROUP_FLAT
    state_w = S5_TILE_GROUPS * 2 * SSM_STATE
    ng, gf = S5_TILE_GROUPS, S5_GROUP_FLAT
    per_tile = lambda n, *shape: pl.BlockSpec((None, n) + shape, lambda o, p, j: (layer, o) + (0,) * len(shape))
    return pl.pallas_call(
        _s5_body,
        grid=(width // tile_w, 2, nblk),
        in_specs=[
            pl.BlockSpec((rb, tile_w), lambda o, p, j: (j + (1 - p) * (nblk - 1 - 2 * j), o)),
            per_tile(ng, gf, gf), per_tile(ng, gf, LANES), per_tile(ng, gf, LANES), per_tile(ng, 2 * LANES, gf),
            pl.BlockSpec((None, None, 4, npairs * LANES), lambda o, p, j: (layer, o, 0, 0)),
            per_tile(npairs, 1, pf),
        ],
        out_specs=pl.BlockSpec((rb, tile_w), lambda o, p, j: (p * j, o)),
        out_shape=jax.ShapeDtypeStruct((rows, width), BF16),
        scratch_shapes=[
            pltpu.VMEM((npairs, pf, pf), BF16),
            pltpu.VMEM((npairs, pf, 2 * LANES), BF16),
            pltpu.VMEM((npairs, pf, 2 * LANES), BF16),
            pltpu.VMEM((npairs, 4 * LANES, pf), BF16),
            pltpu.VMEM((rows, state_w), BF16),
            pltpu.VMEM((rb, state_w), F32),
            pltpu.VMEM((rb, state_w), F32),
            pltpu.VMEM((SUBLANES, state_w), F32),
        ],
        compiler_params=_params("arbitrary", "arbitrary", "arbitrary"),
        name="s5",
    )(cx, g_m, hf_m, hb_m, q_m, a8, dsk)


def _merge_body(x_ref, p_ref, hf_ref, hb_ref, yb_ref, yc_ref, yd_ref, ns_ref,
                wag_ref, wbg_ref, wcg_ref, wdg_ref, wm0_ref, wm1_ref, wm2_ref, wm3_ref,
                wb0_ref, wb1_ref, wb2_ref, wb3_ref, wout_ref, pg_ref, pp_ref, gluw_ref, glub_ref,
                o_ref, stage_ref, pslab_ref, xslab_ref, *, x_batch_major, out_batch_major):
    x = _load_time_major(x_ref, xslab_ref) if x_batch_major else x_ref[...]
    hn = _rms_rows(x, ns_ref[...]).astype(BF16)

    def branch(y, wb_ref, wm_ref):
        return _dot(y.astype(BF16), wb_ref[...]) * _sigmoid(_dot(hn, wm_ref[...]))

    ya = (hf_ref[...].astype(F32) + hb_ref[...].astype(F32)) * _silu(_dot(hn, wag_ref[...]))
    merged = branch(ya, wb0_ref, wm0_ref)
    steps = stage_ref.shape[1] // SUBLANES
    for b in range(SUBLANES):
        for l in range(W_TILES):
            c0 = b * BRANCH_W + l * LANES
            stage_ref[l, pl.ds(b, steps, stride=SUBLANES), :] = yb_ref[:, c0:c0 + LANES].astype(F32)
    yb = jnp.concatenate([stage_ref[l] for l in range(W_TILES)], axis=-1)
    merged = merged + branch(yb * _silu(_dot(hn, wbg_ref[...])), wb1_ref, wm1_ref)
    nchunk = yc_ref.shape[0] // SUBLANES
    tiles = []
    for o in range(W_TILES):
        per_g = [yc_ref[:, (o * S5_TILE_GROUPS + g) * S5_GROUP_FLAT:(o * S5_TILE_GROUPS + g + 1) * S5_GROUP_FLAT]
                 .astype(F32) for g in range(S5_TILE_GROUPS)]
        per_t = [v.reshape(nchunk, 1, SUBLANES, LANES) for v in _block_transpose(per_g)]
        tiles.append(jnp.concatenate(per_t, axis=1).reshape(nchunk * S5_CHUNK * SUBLANES, LANES))
    yg = _gelu_tanh(jnp.concatenate(tiles, axis=-1))
    yc = yg * _sigmoid(_dot(yg.astype(BF16), gluw_ref[...]) + glub_ref[...])
    merged = merged + branch(yc * _silu(_dot(hn, wcg_ref[...])), wb2_ref, wm2_ref)
    merged = merged + branch(yd_ref[...].astype(F32) * _silu(_dot(hn, wdg_ref[...])), wb3_ref, wm3_ref)
    x1 = x + _dot(merged.astype(BF16), wout_ref[...])
    emb = _dot(_load_time_major(p_ref, pslab_ref).astype(BF16), pp_ref[...])
    out = x1 + _sigmoid(_dot(x1.astype(BF16), pg_ref[...])) * emb
    if out_batch_major:
        _store_batch_major(out, o_ref, xslab_ref)
    else:
        o_ref[...] = out


def _merge(x, p, layer, hf, hb, yb, yc, yd, norm_scale, w_in, wbr, wout, pgate, pproj, glu_w, glu_b, batch,
           out_batch_major):
    w = BRANCH_W
    tm = MERGE_ROWS
    pdim = p.shape[-1]
    once = pl.Buffered(1)
    x_batch_major = x.ndim == 3
    d = x.shape[-1]
    rows = x.shape[0] * x.shape[1] if x_batch_major else x.shape[0]
    steps = tm // batch
    bm_spec = pl.BlockSpec((batch, steps, d), lambda i: (0, i, 0))
    x_spec = bm_spec if x_batch_major else pl.BlockSpec((tm, d), lambda i: (i, 0))
    if out_batch_major:
        out_spec, out_shape = bm_spec, jax.ShapeDtypeStruct((batch, rows // batch, d), F32)
    else:
        out_spec, out_shape = pl.BlockSpec((tm, d), lambda i: (i, 0)), jax.ShapeDtypeStruct((rows, d), F32)

    def resident(shape, *tail):
        tail = tail or (0,) * len(shape)
        return pl.BlockSpec((None,) + tuple(shape), lambda i: (layer,) + tuple(tail), pipeline_mode=once)

    row_spec = lambda n: pl.BlockSpec((tm, n), lambda i: (i, 0))
    gate_cols = [resident((d, w), 0, j) for j in (COL_AG, COL_BG, COL_CG, COL_DG)]
    merge_cols = [resident((d, d), 0, COL_MERGE + n) for n in range(N_BRANCH)]
    return pl.pallas_call(
        functools.partial(_merge_body, x_batch_major=x_batch_major, out_batch_major=out_batch_major),
        grid=(rows // tm,),
        in_specs=[
            x_spec,
            pl.BlockSpec((None, batch, steps, pdim), lambda i: (layer, 0, i, 0)),
            row_spec(w), row_spec(w),
            pl.BlockSpec((tm // batch, batch * w), lambda i: (i, 0)),
            pl.BlockSpec((tm // S5_CHUNK, S5_CHUNK * w), lambda i: (i, 0)), row_spec(w),
            resident((1, d)),
            *gate_cols, *merge_cols,
            *[pl.BlockSpec((None, None, w, d), lambda i, n=n: (layer, n, 0, 0), pipeline_mode=once)
              for n in range(N_BRANCH)],
            resident((d, d)), resident((d, d)), resident((pdim, d)),
            resident((w, w)), resident((1, w)),
        ],
        out_specs=out_spec,
        out_shape=out_shape,
        scratch_shapes=[pltpu.VMEM((W_TILES, tm, LANES), F32),
                        pltpu.VMEM((pdim // LANES, tm, LANES), F32),
                        pltpu.VMEM((d // LANES, tm, LANES), F32)],
        compiler_params=_params("parallel"),
        name="merge",
    )(x, p, hf, hb, yb, yc, yd, norm_scale, *([w_in] * 8), *([wbr] * N_BRANCH), wout, pgate, pproj, glu_w, glu_b)


def _block_diag(blocks):
    n, r, c = blocks.shape[-3:]
    spread = np.tile(np.eye(c, dtype=np.float32), (1, n))
    keep = np.kron(np.eye(n, dtype=np.float32), np.ones((r, c), np.float32))
    rows = blocks.reshape(blocks.shape[:-3] + (n * r, c))
    full = jnp.einsum('...rc,cm->...rm', rows, jnp.asarray(spread, blocks.dtype), precision=lax.Precision.HIGHEST)
    return full * jnp.asarray(keep, blocks.dtype)


def _attn_bias_tables(rpb):
    qc = np.arange(GRID_W)[:, None]
    kc = np.arange(GRID_W)[None, :]
    ws = np.clip(qc - NA_COLS // 2, 0, GRID_W - NA_COLS)
    in_win = (kc >= ws) & (kc < ws + NA_COLS)
    dc = np.clip(kc - qc, -(NA_COLS - 1), NA_COLS - 1) + NA_COLS - 1
    nrel = 2 * NA_COLS - 1
    onehot = ((dc[None] == np.arange(nrel)[:, None, None]) & in_win[None]).astype(np.float32)
    onehot2 = np.zeros((2, nrel, GRID_W, 2, GRID_W), np.float32)
    onehot2[0, :, :, 0, :] = onehot
    onehot2[1, :, :, 1, :] = onehot
    onehot2 = onehot2.reshape(2 * nrel, GRID_W, 2 * GRID_W)
    rows2 = jnp.concatenate([rpb.astype(F32)[:, :-1], rpb.astype(F32)[:, 1:]], axis=-1)
    tab = jnp.einsum('hmx,xql->hmql', rows2, jnp.asarray(onehot2), precision=lax.Precision.HIGHEST)
    tab = tab + jnp.asarray(np.tile(np.where(in_win, 0.0, -1e30).astype(np.float32), (1, 2)))
    nr = tab.shape[1]
    tab = tab.reshape(NA_HEADS // 2, 2, nr, GRID_W, 2 * GRID_W)
    return jnp.transpose(tab, (0, 2, 1, 3, 4)).reshape(NA_HEADS // 2, nr, 2 * GRID_W, 2 * GRID_W)


def _s5_matrices(a_re, a_im, log_dt, b_re, b_im, c_re, c_im, d_skip):
    L, G, P, C = S5_CHUNK, SSM_GROUPS, SSM_STATE, SSM_GROUP
    f32 = F32
    hi = lax.Precision.HIGHEST
    lr = jnp.minimum(a_re.astype(f32), -1e-4)
    li = a_im.astype(f32)
    dt = jnp.exp(log_dt.astype(f32))[..., None]
    steps_n = jnp.arange(L + 1, dtype=f32)[:, None, None, None]
    pw_mag = jnp.exp(steps_n * (lr * dt))
    pw_r = pw_mag * jnp.cos(steps_n * (li * dt))
    pw_i = pw_mag * jnp.sin(steps_n * (li * dt))
    ab_r, ab_i = pw_r[1], pw_i[1]
    nr = ab_r - 1.0
    den = lr * lr + li * li
    fr = ((nr * lr + ab_i * li) / den)[:, :, None, :]
    fi = ((ab_i * lr - nr * li) / den)[:, :, None, :]
    bt_r = jnp.swapaxes(b_re.astype(f32), -1, -2)
    bt_i = jnp.swapaxes(b_im.astype(f32), -1, -2)
    bb_r = fr * bt_r - fi * bt_i
    bb_i = fr * bt_i + fi * bt_r
    cr, ci = c_re.astype(f32), c_im.astype(f32)
    m_r = pw_r[:, :, :, None, :] * bb_r[None] - pw_i[:, :, :, None, :] * bb_i[None]
    m_i = pw_r[:, :, :, None, :] * bb_i[None] + pw_i[:, :, :, None, :] * bb_r[None]
    k_lag = (jnp.einsum('dgop,ndgip->ndgio', cr, m_r[:L], precision=hi)
             - jnp.einsum('dgop,ndgip->ndgio', ci, m_i[:L], precision=hi))
    k_signed = jnp.concatenate([k_lag[:0:-1, 1], k_lag[:1, 0] + k_lag[:1, 1], k_lag[1:, 0]])
    tile_c = jnp.asarray(np.tile(np.eye(C, dtype=np.float32), (1, L)))
    rep_t = jnp.asarray(np.repeat(np.eye(L, dtype=np.float32), C, axis=1))
    k_wide = jnp.einsum('ngio,ol->ngil', k_signed, tile_c, precision=hi)
    lag = (lax.broadcasted_iota(jnp.int32, (1, L, 1, L * C), 3) // C
           - lax.broadcasted_iota(jnp.int32, (1, L, 1, L * C), 1))
    g_mat = 0.0
    for n in range(2 * L - 1):
        g_mat = g_mat + jnp.where(lag == n - (L - 1), k_wide[n][:, None, :, :], 0.0)
    g_mat = g_mat.reshape(G, L * C, L * C)

    def h_mat(direction, powers):
        re = jnp.transpose(m_r[powers, direction], (1, 0, 2, 3)).reshape(G, L * C, P)
        im = jnp.transpose(m_i[powers, direction], (1, 0, 2, 3)).reshape(G, L * C, P)
        return jnp.concatenate([re, im], axis=-1)

    hf_mat = h_mat(0, slice(L - 1, None, -1))
    hb_mat = h_mat(1, slice(0, L))

    def q_rows(direction, powers):
        spread = lambda c: jnp.einsum('gcp,cl->gpl', c[direction], tile_c, precision=hi)
        along_t = lambda w: jnp.einsum('tgp,tl->gpl', w[powers, direction], rep_t, precision=hi)
        c_r, c_i, w_r, w_i = spread(cr), spread(ci), along_t(pw_r), along_t(pw_i)
        return jnp.concatenate([c_r * w_r - c_i * w_i, -(c_r * w_i + c_i * w_r)], axis=1)

    q_mat = jnp.concatenate([q_rows(0, slice(1, L + 1)), q_rows(1, slice(L, 0, -1))], axis=1)
    npair = G // 2
    a8 = jnp.stack([pw_r[L, 0], pw_i[L, 0], pw_r[L, 1], pw_i[L, 1]])
    a8 = jnp.transpose(a8.reshape(4, W_TILES, (S5_TILE_GROUPS // 2) * 2 * P), (1, 0, 2))
    dsk = jnp.tile(d_skip.astype(f32).reshape(G, 1, C), (1, 1, L)).reshape(npair, 1, 2 * L * C)
    bf = lambda m: m.astype(BF16)
    return bf(g_mat), bf(hf_mat), bf(hb_mat), bf(q_mat), a8, dsk


def kernel(x, p, norm_scale, w_in, lru_conv_w, lru_conv_b, lru_w_r, lru_b_r, lru_w_i, lru_b_i, lru_lambda, na_q_gain, na_k_gain, na_rel_bias, ssm_a_re, ssm_a_im, ssm_log_dt, ssm_b_re, ssm_b_im, ssm_c_re, ssm_c_im, ssm_d, ssm_glu_w, ssm_glu_b, pool_w, pool_scale, w_branch, w_out, ple_proj, ple_gate):
    b, s, d = x.shape
    assert b == SUBLANES and d == D_MODEL and s // GRID_W >= NA_ROWS
    for rows_per_step in (PROJ_ROWS, MERGE_ROWS, LRU_STEPS * b, POOL_STEPS * b, S5_BLOCK_CHUNKS * S5_CHUNK * b,
                          ATTN_ROWS * GRID_W * b):
        assert (s * b) % rows_per_step == 0
    depth = w_in.shape[0]

    norm = norm_scale.astype(F32)[:, None, :]
    w_in16 = w_in.astype(BF16)
    ones_bd = _block_diag(jnp.ones((MXU_DIM // NA_HEAD_DIM, NA_HEAD_DIM, NA_HEAD_DIM), BF16))
    q_gain = (jnp.tile(na_q_gain.astype(F32), (1, NA_HEADS)) * (NA_HEAD_DIM ** -0.5 * LOG2E))[:, None, :]
    k_gain = jnp.tile(na_k_gain.astype(F32), (1, NA_HEADS))[:, None, :]
    conv_w = lru_conv_w.astype(F32)
    conv_b = lru_conv_b.astype(F32)[:, None, :]
    gate_w = (0.5 * jnp.concatenate([_block_diag(lru_w_r), _block_diag(lru_w_i)], axis=-1)).astype(BF16)
    gate_b = (0.5 * jnp.concatenate([lru_b_r, lru_b_i], axis=-1).astype(F32))[:, :, None, :]
    c8 = (-0.5 * LRU_C * LOG2E * jax.nn.softplus(-lru_lambda.astype(F32)))[:, :, None, :]
    bias_tab = jax.vmap(_attn_bias_tables)(na_rel_bias.astype(F32) * LOG2E)
    s5_mats = jax.vmap(_s5_matrices)(ssm_a_re, ssm_a_im, ssm_log_dt, ssm_b_re, ssm_b_im, ssm_c_re, ssm_c_im, ssm_d)
    glu_w = ssm_glu_w.astype(BF16)
    glu_b = ssm_glu_b.astype(F32)[:, None, :]
    pool_bd = _block_diag(pool_w).astype(BF16)
    pool_sc = pool_scale.astype(F32)[:, None, :]
    branch_scale = jnp.asarray([0.5] + [1.0] * (N_BRANCH - 1), F32)[None, :, None, None]
    wbr = (w_branch.astype(F32) * branch_scale).astype(BF16)
    wout = w_out.astype(BF16)
    pproj = ple_proj.astype(BF16)
    pgate = ple_gate.astype(BF16)

    xt = x
    for i in range(depth):
        xc, q, k, v, cx, dx = _inproj(xt, i, norm, w_in16, ones_bd, q_gain, k_gain, conv_w, conv_b, b)
        hf, hb = _lru(xc, i, gate_w, gate_b, c8)
        yb = _attn(q, k, v, i, bias_tab, b)
        yc = _s5(cx, i, s5_mats)
        yd = _pool(dx, i, pool_bd, pool_sc, s)
        xt = _merge(xt, p, i, hf, hb, yb, yc, yd, norm, w_in16, wbr, wout, pgate, pproj, glu_w, glu_b, b,
                    out_batch_major=(i == depth - 1))
    return xt
```

```python
import functools

import jax
import jax.numpy as jnp
import numpy as np
from jax import lax
from jax.experimental import pallas as pl
from jax.experimental.pallas import tpu as pltpu

F32 = jnp.float32
BF16 = jnp.bfloat16

D_MODEL = 1024
BRANCH_W = 512
N_BRANCH = 4
NORM_EPS = 1e-6
LOG2E = 1.4426950408889634
GRID_W = 64
LRU_C = 8.0
LRU_CONV_W = 4
LRU_HALO_STEPS = 2
NA_HEADS = 8
NA_HEAD_DIM = 64
NA_ROWS = 8
NA_COLS = 16
SSM_GROUP = 16
SSM_GROUPS = 32
SSM_STATE = 64
POOL_WINDOWS = (2, 4, 8, 16)
POOL_GROUP = 128

SUBLANES = 8
LANES = 128
MXU_DIM = 256
W_TILES = BRANCH_W // LANES
S5_CHUNK = 8

COL_AX, COL_AG, COL_Q, COL_K, COL_V, COL_BG, COL_CX, COL_CG, COL_DX, COL_DG = range(10)
COL_MERGE = 10 * BRANCH_W // D_MODEL

PROJ_ROWS = 1024
MERGE_ROWS = 512
LRU_STEPS = 256
POOL_STEPS = 256
ATTN_ROWS = 16
S5_BLOCK_CHUNKS = 256
ELEM_ROWS = 256

VMEM_LIMIT = 56 * 1024 * 1024


def _params(*sem):
    return pltpu.CompilerParams(dimension_semantics=sem, vmem_limit_bytes=VMEM_LIMIT)


def _dot(a, b):
    return jnp.dot(a, b, preferred_element_type=F32)


def _sigmoid(z):
    return 0.5 * jnp.tanh(0.5 * z) + 0.5


def _silu(z):
    return z * _sigmoid(z)


def _gelu_tanh(y):
    return 0.5 * y * (1.0 + jnp.tanh(0.7978845608028654 * (y + 0.044715 * (y * y * y))))


def _rms_rows(x, g):
    ms = jnp.mean(x * x, axis=-1, keepdims=True)
    return x * lax.rsqrt(ms + NORM_EPS) * g


def _head_mean_square(v, ones_ref):
    v2 = (v * v).astype(BF16)
    n = ones_ref.shape[0]
    sums = [_dot(v2[:, c:c + n], ones_ref[...]) for c in range(0, v.shape[1], n)]
    return jnp.concatenate(sums, axis=-1) * (1.0 / NA_HEAD_DIM)


def _layer_spec(shape, layer, *tail):
    tail = tail or (0,) * len(shape)
    return pl.BlockSpec((None,) + tuple(shape), lambda *_: (layer,) + tuple(tail))


S5_TILE_GROUPS = LANES // SSM_GROUP
S5_GROUP_FLAT = S5_CHUNK * SSM_GROUP


def _block_transpose(xs):
    n = len(xs)
    lane = lax.broadcasted_iota(jnp.int32, (1, LANES), 1)
    cur = list(xs)
    width, stride = LANES // 2, n // 2
    while stride >= 1:
        low = (lane & width) == 0
        nxt = list(cur)
        for i in range(n):
            if i & stride:
                continue
            a, b = cur[i], cur[i + stride]
            nxt[i] = jnp.where(low, a, pltpu.roll(b, width, 1))
            nxt[i + stride] = jnp.where(low, pltpu.roll(a, LANES - width, 1), b)
        cur = nxt
        width //= 2
        stride //= 2
    return cur


def _load_time_major(src_ref, slab_ref, s0=0, s1=None):
    nb, steps, n = src_ref.shape
    s1 = steps if s1 is None else s1
    tiles = n // LANES
    for b in range(nb):
        for l in range(tiles):
            slab_ref[l, pl.ds(b, s1 - s0, stride=nb), :] = src_ref[b, s0:s1, l * LANES:(l + 1) * LANES]
    return jnp.concatenate([slab_ref[l, 0:(s1 - s0) * nb, :] for l in range(tiles)], axis=-1)


def _store_batch_major(val, dst_ref, slab_ref):
    nb, steps, n = dst_ref.shape
    tiles = n // LANES
    for l in range(tiles):
        slab_ref[l] = val[:, l * LANES:(l + 1) * LANES]
    for b in range(nb):
        for l in range(tiles):
            dst_ref[b, :, l * LANES:(l + 1) * LANES] = slab_ref[l, pl.ds(b, steps, stride=nb), :]


def _inproj_body(x_ref, xp_ref, xn_ref, ns_ref, wax_ref, wq_ref, wk_ref, wv_ref, wcx_ref, wdx_ref, ones_ref,
                 qg_ref, kg_ref, cw_ref, cb_ref,
                 xc_ref, q_ref, k_ref, v_ref, cx_ref, dx_ref, stage_ref, *slabs):
    i = pl.program_id(0)
    halo = LRU_HALO_STEPS * SUBLANES
    if slabs:
        x = _load_time_major(x_ref, slabs[0])
        hs = xp_ref.shape[1]
        x_prev = _load_time_major(xp_ref, slabs[1], hs - LRU_HALO_STEPS, hs)
        x_next = _load_time_major(xn_ref, slabs[1], 0, LRU_HALO_STEPS)
    else:
        x = x_ref[...]
        x_prev = xp_ref[xp_ref.shape[0] - halo:, :]
        x_next = xn_ref[0:halo, :]
    hn = _rms_rows(x, ns_ref[...]).astype(BF16)
    steps = stage_ref.shape[1] // SUBLANES

    def to_batch_lanes(val, out_ref):
        for l in range(W_TILES):
            stage_ref[l] = val[:, l * LANES:(l + 1) * LANES]
        for b in range(SUBLANES):
            for l in range(W_TILES):
                c0 = b * BRANCH_W + l * LANES
                out_ref[:, c0:c0 + LANES] = stage_ref[l, pl.ds(b, steps, stride=SUBLANES), :].astype(BF16)

    hn_ext = jnp.concatenate([_rms_rows(x_prev, ns_ref[...]).astype(BF16), hn,
                              _rms_rows(x_next, ns_ref[...]).astype(BF16)], axis=0)
    ax = _dot(hn_ext, wax_ref[...])
    r = hn.shape[0]
    row = lax.broadcasted_iota(jnp.int32, (r + 2 * halo, 1), 0)
    outside = ((row < halo) & (i == 0)) | ((row >= r + halo) & (i == pl.num_programs(0) - 1))
    ax = jnp.where(outside, 0.0, ax)
    xc = cb_ref[...]
    for tap in range(LRU_CONV_W):
        xc = xc + cw_ref[tap:tap + 1, :] * ax[tap * SUBLANES:tap * SUBLANES + r, :]
    xc_ref[...] = xc.astype(BF16)
    q = _dot(hn, wq_ref[...])
    to_batch_lanes(q * lax.rsqrt(_head_mean_square(q, ones_ref) + NORM_EPS) * qg_ref[...], q_ref)
    k = _dot(hn, wk_ref[...])
    to_batch_lanes(k * lax.rsqrt(_head_mean_square(k, ones_ref) + NORM_EPS) * kg_ref[...], k_ref)
    to_batch_lanes(_dot(hn, wv_ref[...]), v_ref)
    cxv = _dot(hn, wcx_ref[...])
    nchunk = cxv.shape[0] // (S5_CHUNK * SUBLANES)
    for o in range(W_TILES):
        tile3 = cxv[:, o * LANES:(o + 1) * LANES].reshape(nchunk, S5_CHUNK * SUBLANES, LANES)
        per_t = [tile3[:, t * SUBLANES:(t + 1) * SUBLANES, :].reshape(nchunk * SUBLANES, LANES)
                 for t in range(S5_CHUNK)]
        for g, ug in enumerate(_block_transpose(per_t)):
            c0 = (o * S5_TILE_GROUPS + g) * S5_GROUP_FLAT
            cx_ref[:, c0:c0 + S5_GROUP_FLAT] = ug.astype(BF16)
    dx_ref[...] = _dot(hn, wdx_ref[...]).astype(BF16)


def _inproj(x, layer, norm_scale, w_in, ones_bd, q_gain, k_gain, conv_w, conv_b, batch):
    w = BRANCH_W
    tm = PROJ_ROWS
    batch_major = x.ndim == 3
    d = x.shape[-1]
    rows = x.shape[0] * x.shape[1] if batch_major else x.shape[0]
    s = rows // batch
    nt = rows // tm
    if batch_major:
        hs = SUBLANES
        per = tm // batch // hs
        x_spec = pl.BlockSpec((batch, tm // batch, d), lambda i: (0, i, 0))
        prev_spec = pl.BlockSpec((batch, hs, d), lambda i: (0, jnp.maximum(i * per - 1, 0), 0))
        next_spec = pl.BlockSpec((batch, hs, d), lambda i: (0, jnp.minimum((i + 1) * per, s // hs - 1), 0))
        slabs = [pltpu.VMEM((d // LANES, tm, LANES), F32),
                 pltpu.VMEM((d // LANES, LRU_HALO_STEPS * batch, LANES), F32)]
    else:
        hb = LRU_HALO_STEPS * batch
        per = tm // hb
        x_spec = pl.BlockSpec((tm, d), lambda i: (i, 0))
        prev_spec = pl.BlockSpec((hb, d), lambda i: (jnp.maximum(i * per - 1, 0), 0))
        next_spec = pl.BlockSpec((hb, d), lambda i: (jnp.minimum((i + 1) * per, rows // hb - 1), 0))
        slabs = []
    wcol = lambda j: pl.BlockSpec((None, d, w), lambda i: (layer, 0, j))
    tm_spec = pl.BlockSpec((tm, w), lambda i: (i, 0))
    bl_spec = pl.BlockSpec((tm // batch, batch * w), lambda i: (i, 0))
    tm_shape = jax.ShapeDtypeStruct((rows, w), BF16)
    bl_shape = jax.ShapeDtypeStruct((s, batch * w), BF16)
    gf_spec = pl.BlockSpec((tm // S5_CHUNK, S5_CHUNK * w), lambda i: (i, 0))
    gf_shape = jax.ShapeDtypeStruct((rows // S5_CHUNK, S5_CHUNK * w), BF16)
    return pl.pallas_call(
        _inproj_body,
        grid=(nt,),
        in_specs=[
            x_spec, prev_spec, next_spec,
            _layer_spec((1, d), layer),
            wcol(COL_AX), wcol(COL_Q), wcol(COL_K), wcol(COL_V), wcol(COL_CX), wcol(COL_DX),
            pl.BlockSpec(ones_bd.shape, lambda i: (0, 0)),
            _layer_spec((1, w), layer), _layer_spec((1, w), layer),
            _layer_spec((LRU_CONV_W, w), layer), _layer_spec((1, w), layer),
        ],
        out_specs=[tm_spec, bl_spec, bl_spec, bl_spec, gf_spec, tm_spec],
        out_shape=[tm_shape, bl_shape, bl_shape, bl_shape, gf_shape, tm_shape],
        scratch_shapes=[pltpu.VMEM((W_TILES, tm, LANES), F32)] + slabs,
        compiler_params=_params("parallel"),
        name="inproj",
    )(x, x, x, norm_scale, w_in, w_in, w_in, w_in, w_in, w_in, ones_bd, q_gain, k_gain, conv_w, conv_b)


def _fill_extended(ext_ref, main_ref, prev_ref, next_ref, n_prev, n_next, is_first, is_last):
    r = main_ref.shape[0]
    prev = prev_ref[...].astype(F32)
    ext_ref[0:n_prev, :] = jnp.where(is_first, 0.0, prev[prev.shape[0] - n_prev:, :])
    ext_ref[n_prev:n_prev + r, :] = main_ref[...].astype(F32)
    nxt = next_ref[...].astype(F32)
    ext_ref[n_prev + r:n_prev + r + n_next, :] = jnp.where(is_last, 0.0, nxt[0:n_next, :])


def _lru_body(xf_ref, xb_ref, wg_ref, bg_ref, ch_ref, hf_ref, hb_ref, af_ref, bf_ref, ab_ref, bb_ref, carry_ref):
    i = pl.program_id(0)
    w = BRANCH_W
    r = xf_ref.shape[0]
    steps = r // SUBLANES

    @pl.when(i == 0)
    def _():
        carry_ref[...] = jnp.zeros_like(carry_ref)

    def prepare(direction, x_ref, a_ref, b_ref):
        def sub(sb, _):
            rows = pl.ds(pl.multiple_of(sb * ELEM_ROWS, ELEM_ROWS), ELEM_ROWS)
            xcb = x_ref[rows, :]
            xc = xcb.astype(F32)
            g = _dot(xcb, wg_ref[direction]) + bg_ref[direction]
            tr = jnp.tanh(g[:, 0:w]) + 1.0
            ti = jnp.tanh(g[:, w:2 * w]) + 1.0
            a = jnp.exp2(ch_ref[direction] * tr)
            a_ref[rows, :] = a
            om = 1.0 - a * a
            b_ref[rows, :] = om * lax.rsqrt(jnp.maximum(om, 1e-37)) * (ti * xc)
            return 0

        lax.fori_loop(0, r // ELEM_ROWS, sub, 0, unroll=True)

    prepare(0, xf_ref, af_ref, bf_ref)
    prepare(1, xb_ref, ab_ref, bb_ref)

    def step(t, carry):
        hf, hb = carry
        rf = pl.ds(pl.multiple_of(t * SUBLANES, SUBLANES), SUBLANES)
        hf = af_ref[rf, :] * hf + bf_ref[rf, :]
        bf_ref[rf, :] = hf
        rb = pl.ds(pl.multiple_of((steps - 1 - t) * SUBLANES, SUBLANES), SUBLANES)
        hb = ab_ref[rb, :] * hb + bb_ref[rb, :]
        bb_ref[rb, :] = hb
        return hf, hb

    hf, hb = lax.fori_loop(0, steps, step, (carry_ref[0], carry_ref[1]), unroll=8)
    carry_ref[0] = hf
    carry_ref[1] = hb
    hf_ref[...] = bf_ref[...].astype(BF16)
    hb_ref[...] = bb_ref[...].astype(BF16)


def _lru(xc, layer, w_gate, b_gate, c8):
    rows, w = xc.shape
    r = LRU_STEPS * SUBLANES
    nt = rows // r
    fwd = pl.BlockSpec((r, w), lambda i: (i, 0))
    bwd = pl.BlockSpec((r, w), lambda i: (nt - 1 - i, 0))
    out_shape = jax.ShapeDtypeStruct((rows, w), BF16)
    return pl.pallas_call(
        _lru_body,
        grid=(nt,),
        in_specs=[
            fwd, bwd,
            _layer_spec((2, w, 2 * w), layer),
            _layer_spec((2, 1, 2 * w), layer),
            _layer_spec((2, 1, w), layer),
        ],
        out_specs=[fwd, bwd],
        out_shape=[out_shape, out_shape],
        scratch_shapes=[
            pltpu.VMEM((r, w), F32), pltpu.VMEM((r, w), F32),
            pltpu.VMEM((r, w), F32), pltpu.VMEM((r, w), F32),
            pltpu.VMEM((2, SUBLANES, w), F32),
        ],
        compiler_params=_params("arbitrary"),
        name="rglru",
    )(xc, xc, w_gate, b_gate, c8)


def _pool_body(m_ref, p_ref, n_ref, wp_ref, sc_ref, o_ref, ext_ref, pooled_ref, *, seq_len):
    i = pl.program_id(0)
    nt = pl.num_programs(0)
    r = m_ref.shape[0]
    steps = r // SUBLANES
    halo = max(POOL_WINDOWS) // 2
    hr = halo * SUBLANES
    _fill_extended(ext_ref, m_ref, p_ref, n_ref, hr, hr, i == 0, i == nt - 1)

    def pooled_minus_self(clipped):
        if clipped:
            t_glob = i * steps + lax.broadcasted_iota(jnp.int32, (r, POOL_GROUP), 0) // SUBLANES
        for g, win in enumerate(POOL_WINDOWS):
            ls = slice(g * POOL_GROUP, (g + 1) * POOL_GROUP)
            e = ext_ref[:, ls]
            lo_t = -halo
            cur = e
            half = 1
            while half < win:
                n = cur.shape[0] - half * SUBLANES
                if half == 1:
                    cur = cur[0:n, :] + cur[SUBLANES:SUBLANES + n, :]
                    lo_t += 1
                else:
                    sh = (half // 2) * SUBLANES
                    cur = cur[0:n, :] + cur[2 * sh:2 * sh + n, :]
                    lo_t += half // 2
                half *= 2
            off = (0 - lo_t) * SUBLANES
            wsum = cur[off:off + r, :]
            half_w = win // 2
            if clipped:
                cnt = (jnp.clip(t_glob + half_w, 0, seq_len) - jnp.clip(t_glob - half_w, 0, seq_len)).astype(F32)
                mean = wsum / cnt
            else:
                mean = wsum * (1.0 / win)
            pooled_ref[:, ls] = mean - e[hr:hr + r, :]

    at_edge = (i == 0) | (i == nt - 1)

    @pl.when(at_edge)
    def _():
        pooled_minus_self(True)

    @pl.when(jnp.logical_not(at_edge))
    def _():
        pooled_minus_self(False)

    o_ref[...] = (_dot(pooled_ref[...].astype(BF16), wp_ref[...]) * sc_ref[...]).astype(BF16)


def _pool(dx, layer, w_pool_bd, scale, seq_len):
    rows, w = dx.shape
    r = POOL_STEPS * SUBLANES
    nt = rows // r
    assert POOL_STEPS >= max(POOL_WINDOWS) // 2
    hb = (max(POOL_WINDOWS) // 2) * SUBLANES
    per = r // hb
    last_hb = rows // hb - 1
    return pl.pallas_call(
        functools.partial(_pool_body, seq_len=seq_len),
        grid=(nt,),
        in_specs=[
            pl.BlockSpec((r, w), lambda i: (i, 0)),
            pl.BlockSpec((hb, w), lambda i: (jnp.maximum(i * per - 1, 0), 0)),
            pl.BlockSpec((hb, w), lambda i: (jnp.minimum((i + 1) * per, last_hb), 0)),
            _layer_spec((w, w), layer),
            _layer_spec((1, w), layer),
        ],
        out_specs=pl.BlockSpec((r, w), lambda i: (i, 0)),
        out_shape=jax.ShapeDtypeStruct((rows, w), BF16),
        scratch_shapes=[pltpu.VMEM((r + 2 * hb, w), F32), pltpu.VMEM((r, w), F32)],
        compiler_params=_params("parallel"),
        name="pool",
    )(dx, dx, dx, w_pool_bd, scale)


def _attn_body(q_ref, k_ref, v_ref, bias_ref, o_ref, *, grid_rows):
    g = pl.program_id(1)
    lane = lax.broadcasted_iota(jnp.int32, (GRID_W, LANES), 1)
    lo_half = lane < NA_HEAD_DIM
    nkeys = NA_ROWS * GRID_W

    def row_body(rr, _):
        r = g * ATTN_ROWS + rr
        rs = jnp.clip(r - NA_ROWS // 2, 0, grid_rows - NA_ROWS)
        d0 = rs - r + (NA_ROWS - 1)
        k0 = pl.multiple_of(rs * GRID_W, GRID_W)
        q0 = pl.multiple_of(rr * GRID_W, GRID_W)
        scores = []
        for hp in range(NA_HEADS // 2):
            ls = slice(hp * LANES, (hp + 1) * LANES)
            kp = k_ref[pl.ds(k0, nkeys), ls]
            qp = q_ref[pl.ds(q0, GRID_W), ls]
            zero = jnp.zeros_like(qp)
            qm = jnp.concatenate([jnp.where(lo_half, qp, zero), jnp.where(lo_half, zero, qp)], axis=0)
            sc = lax.dot_general(qm, kp, (((1,), (1,)), ((), ())), preferred_element_type=F32)
            bias = jnp.concatenate([bias_ref[hp, d0 + 2 * j] for j in range(NA_ROWS // 2)], axis=-1)
            scores.append(sc + bias)
        maxes = [jnp.max(sc, axis=-1, keepdims=True) for sc in scores]
        probs = [jnp.exp2(sc - m) for sc, m in zip(scores, maxes)]
        inv = [1.0 / jnp.sum(p, axis=-1, keepdims=True) for p in probs]
        for hp in range(NA_HEADS // 2):
            ls = slice(hp * LANES, (hp + 1) * LANES)
            vp = v_ref[pl.ds(k0, nkeys), ls]
            o = _dot(probs[hp].astype(BF16), vp) * inv[hp]
            o_ref[pl.ds(q0, GRID_W), ls] = jnp.where(lo_half, o[0:GRID_W], o[GRID_W:2 * GRID_W]).astype(BF16)
        return 0

    lax.fori_loop(0, ATTN_ROWS, row_body, 0, unroll=2)


def _attn(q, k, v, layer, bias_tab, batch):
    s, bw = q.shape
    w = bw // batch
    grid_rows = s // GRID_W
    ng = grid_rows // ATTN_ROWS
    qr = ATTN_ROWS * GRID_W
    q_spec = pl.BlockSpec((qr, w), lambda b, g: (g, b))
    kv_spec = pl.BlockSpec((s, w), lambda b, g: (0, b))
    return pl.pallas_call(
        functools.partial(_attn_body, grid_rows=grid_rows),
        grid=(batch, ng),
        in_specs=[q_spec, kv_spec, kv_spec, _layer_spec(bias_tab.shape[1:], layer)],
        out_specs=q_spec,
        out_shape=jax.ShapeDtypeStruct((s, bw), BF16),
        compiler_params=_params("parallel", "parallel"),
        name="natten",
    )(q, k, v, bias_tab)


def _s5_assemble(gc_ref, hfc_ref, hbc_ref, qc_ref, g_ref, hf_ref, hb_ref, q_ref):
    half = LANES // 2
    lo = lax.broadcasted_iota(jnp.int32, (S5_GROUP_FLAT, LANES), 1) < half
    g_ref[...] = jnp.zeros_like(g_ref)
    q_ref[...] = jnp.zeros_like(q_ref)
    for q in range(S5_TILE_GROUPS // 2):
        for e in range(2):
            g = 2 * q + e
            rows = slice(e * S5_GROUP_FLAT, (e + 1) * S5_GROUP_FLAT)
            g_ref[q, rows, e * S5_GROUP_FLAT:(e + 1) * S5_GROUP_FLAT] = gc_ref[g]
            for src, dst in ((hfc_ref, hf_ref), (hbc_ref, hb_ref)):
                h = src[g].astype(F32)
                swapped = pltpu.roll(h, half, 1)
                if e == 0:
                    re_tile, im_tile = jnp.where(lo, h, 0.0), jnp.where(lo, swapped, 0.0)
                else:
                    re_tile, im_tile = jnp.where(lo, 0.0, swapped), jnp.where(lo, 0.0, h)
                dst[q, rows, 0:LANES] = re_tile.astype(BF16)
                dst[q, rows, LANES:2 * LANES] = im_tile.astype(BF16)
            for blk in range(4):
                q_ref[q, blk * LANES + e * half:blk * LANES + (e + 1) * half,
                      e * S5_GROUP_FLAT:(e + 1) * S5_GROUP_FLAT] = qc_ref[g, blk * half:(blk + 1) * half, :]


def _s5_body(u_ref, gc_ref, hfc_ref, hbc_ref, qc_ref, a_ref, dsk_ref,
             y_ref, g_ref, hf_ref, hb_ref, q_ref, sinb_ref, sloc_ref, sin_ref, carry_ref):
    p = pl.program_id(1)
    j = pl.program_id(2)
    nblk = pl.num_programs(2)

    @pl.when((j == 0) & (p == 0))
    def _():
        _s5_assemble(gc_ref, hfc_ref, hbc_ref, qc_ref, g_ref, hf_ref, hb_ref, q_ref)

    rb = u_ref.shape[0]
    nk = rb // SUBLANES
    npairs = S5_TILE_GROUPS // 2
    pf = 2 * S5_GROUP_FLAT
    re_t = lambda q: slice(2 * q * LANES, (2 * q + 1) * LANES)
    im_t = lambda q: slice((2 * q + 1) * LANES, (2 * q + 2) * LANES)
    pair_t = lambda q: slice(q * pf, (q + 1) * pf)
    coef_t = lambda q: slice(q * LANES, (q + 1) * LANES)

    @pl.when(j == 0)
    def _():
        carry_ref[...] = jnp.zeros_like(carry_ref)

    def sweep(direction, reverse):
        a_re = a_ref[2 * direction:2 * direction + 1, :]
        a_im = a_ref[2 * direction + 1:2 * direction + 2, :]

        def body(n, state):
            kk = (nk - 1 - n) if reverse else n
            rows = pl.ds(pl.multiple_of(kk * SUBLANES, SUBLANES), SUBLANES)
            new = []
            for q in range(npairs):
                sr, si = state[2 * q], state[2 * q + 1]
                sin_ref[rows, re_t(q)] = sr
                sin_ref[rows, im_t(q)] = si
                ar, ai = a_re[:, coef_t(q)], a_im[:, coef_t(q)]
                new.append(ar * sr - ai * si + sloc_ref[rows, re_t(q)])
                new.append(ar * si + ai * sr + sloc_ref[rows, im_t(q)])
            return tuple(new)

        init = tuple(carry_ref[:, t * LANES:(t + 1) * LANES] for t in range(2 * npairs))
        out = lax.fori_loop(0, nk, body, init, unroll=4)
        for t in range(2 * npairs):
            carry_ref[:, t * LANES:(t + 1) * LANES] = out[t]

    @pl.when(p == 0)
    def _():
        for q in range(npairs):
            sloc_ref[:, pair_t(q)] = _dot(u_ref[:, pair_t(q)], hb_ref[q])
        sweep(1, True)
        blk = nblk - 1 - j
        sinb_ref[pl.ds(pl.multiple_of(blk * rb, rb), rb), :] = sin_ref[...].astype(BF16)

    @pl.when(p == 1)
    def _():
        for q in range(npairs):
            sloc_ref[:, pair_t(q)] = _dot(u_ref[:, pair_t(q)], hf_ref[q])
        sweep(0, False)
        rows_b = pl.ds(pl.multiple_of(j * rb, rb), rb)
        for q in range(npairs):
            u = u_ref[:, pair_t(q)]
            states = jnp.concatenate([sin_ref[:, pair_t(q)].astype(BF16), sinb_ref[rows_b, pair_t(q)]], axis=-1)
            y = _dot(u, g_ref[q]) + _dot(states, q_ref[q]) + dsk_ref[q] * u.astype(F32)
            y_ref[:, pair_t(q)] = y.astype(BF16)


def _s5(cx, layer, mats):
    g_m, hf_m, hb_m, q_m, a8, dsk = mats
    rows, width = cx.shape
    rb = S5_BLOCK_CHUNKS * SUBLANES
    nblk = rows // rb
    tile_w = S5_TILE_GROUPS * S5_GROUP_FLAT
    npairs, pf = S5_TILE_GROUPS // 2, 2 * S5_GROUP_FLAT
    state_w = S5_TILE_GROUPS * 2 * SSM_STATE
    ng, gf = S5_TILE_GROUPS, S5_GROUP_FLAT
    per_tile = lambda n, *shape: pl.BlockSpec((None, n) + shape, lambda o, p, j: (layer, o) + (0,) * len(shape))
    return pl.pallas_call(
        _s5_body,
        grid=(width // tile_w, 2, nblk),
        in_specs=[
            pl.BlockSpec((rb, tile_w), lambda o, p, j: (j + (1 - p) * (nblk - 1 - 2 * j), o)),
            per_tile(ng, gf, gf), per_tile(ng, gf, LANES), per_tile(ng, gf, LANES), per_tile(ng, 2 * LANES, gf),
            pl.BlockSpec((None, None, 4, npairs * LANES), lambda o, p, j: (layer, o, 0, 0)),
            per_tile(npairs, 1, pf),
        ],
        out_specs=pl.BlockSpec((rb, tile_w), lambda o, p, j: (p * j, o)),
        out_shape=jax.ShapeDtypeStruct((rows, width), BF16),
        scratch_shapes=[
            pltpu.VMEM((npairs, pf, pf), BF16),
            pltpu.VMEM((npairs, pf, 2 * LANES), BF16),
            pltpu.VMEM((npairs, pf, 2 * LANES), BF16),
            pltpu.VMEM((npairs, 4 * LANES, pf), BF16),
            pltpu.VMEM((rows, state_w), BF16),
            pltpu.VMEM((rb, state_w), F32),
            pltpu.VMEM((rb, state_w), F32),
            pltpu.VMEM((SUBLANES, state_w), F32),
        ],
        compiler_params=_params("arbitrary", "arbitrary", "arbitrary"),
        name="s5",
    )(cx, g_m, hf_m, hb_m, q_m, a8, dsk)


def _merge_body(x_ref, p_ref, hf_ref, hb_ref, yb_ref, yc_ref, yd_ref, ns_ref,
                wag_ref, wbg_ref, wcg_ref, wdg_ref, wm0_ref, wm1_ref, wm2_ref, wm3_ref,
                wb0_ref, wb1_ref, wb2_ref, wb3_ref, wout_ref, pg_ref, pp_ref, gluw_ref, glub_ref,
                o_ref, stage_ref, pslab_ref, xslab_ref, *, x_batch_major, out_batch_major):
    x = _load_time_major(x_ref, xslab_ref) if x_batch_major else x_ref[...]
    hn = _rms_rows(x, ns_ref[...]).astype(BF16)

    def branch(y, wb_ref, wm_ref):
        return _dot(y.astype(BF16), wb_ref[...]) * _sigmoid(_dot(hn, wm_ref[...]))

    ya = (hf_ref[...].astype(F32) + hb_ref[...].astype(F32)) * _silu(_dot(hn, wag_ref[...]))
    merged = branch(ya, wb0_ref, wm0_ref)
    steps = stage_ref.shape[1] // SUBLANES
    for b in range(SUBLANES):
        for l in range(W_TILES):
            c0 = b * BRANCH_W + l * LANES
            stage_ref[l, pl.ds(b, steps, stride=SUBLANES), :] = yb_ref[:, c0:c0 + LANES].astype(F32)
    yb = jnp.concatenate([stage_ref[l] for l in range(W_TILES)], axis=-1)
    merged = merged + branch(yb * _silu(_dot(hn, wbg_ref[...])), wb1_ref, wm1_ref)
    nchunk = yc_ref.shape[0] // SUBLANES
    tiles = []
    for o in range(W_TILES):
        per_g = [yc_ref[:, (o * S5_TILE_GROUPS + g) * S5_GROUP_FLAT:(o * S5_TILE_GROUPS + g + 1) * S5_GROUP_FLAT]
                 .astype(F32) for g in range(S5_TILE_GROUPS)]
        per_t = [v.reshape(nchunk, 1, SUBLANES, LANES) for v in _block_transpose(per_g)]
        tiles.append(jnp.concatenate(per_t, axis=1).reshape(nchunk * S5_CHUNK * SUBLANES, LANES))
    yg = _gelu_tanh(jnp.concatenate(tiles, axis=-1))
    yc = yg * _sigmoid(_dot(yg.astype(BF16), gluw_ref[...]) + glub_ref[...])
    merged = merged + branch(yc * _silu(_dot(hn, wcg_ref[...])), wb2_ref, wm2_ref)
    merged = merged + branch(yd_ref[...].astype(F32) * _silu(_dot(hn, wdg_ref[...])), wb3_ref, wm3_ref)
    x1 = x + _dot(merged.astype(BF16), wout_ref[...])
    emb = _dot(_load_time_major(p_ref, pslab_ref).astype(BF16), pp_ref[...])
    out = x1 + _sigmoid(_dot(x1.astype(BF16), pg_ref[...])) * emb
    if out_batch_major:
        _store_batch_major(out, o_ref, xslab_ref)
    else:
        o_ref[...] = out


def _merge(x, p, layer, hf, hb, yb, yc, yd, norm_scale, w_in, wbr, wout, pgate, pproj, glu_w, glu_b, batch,
           out_batch_major):
    w = BRANCH_W
    tm = MERGE_ROWS
    pdim = p.shape[-1]
    once = pl.Buffered(1)
    x_batch_major = x.ndim == 3
    d = x.shape[-1]
    rows = x.shape[0] * x.shape[1] if x_batch_major else x.shape[0]
    steps = tm // batch
    bm_spec = pl.BlockSpec((batch, steps, d), lambda i: (0, i, 0))
    x_spec = bm_spec if x_batch_major else pl.BlockSpec((tm, d), lambda i: (i, 0))
    if out_batch_major:
        out_spec, out_shape = bm_spec, jax.ShapeDtypeStruct((batch, rows // batch, d), F32)
    else:
        out_spec, out_shape = pl.BlockSpec((tm, d), lambda i: (i, 0)), jax.ShapeDtypeStruct((rows, d), F32)

    def resident(shape, *tail):
        tail = tail or (0,) * len(shape)
        return pl.BlockSpec((None,) + tuple(shape), lambda i: (layer,) + tuple(tail), pipeline_mode=once)

    row_spec = lambda n: pl.BlockSpec((tm, n), lambda i: (i, 0))
    gate_cols = [resident((d, w), 0, j) for j in (COL_AG, COL_BG, COL_CG, COL_DG)]
    merge_cols = [resident((d, d), 0, COL_MERGE + n) for n in range(N_BRANCH)]
    return pl.pallas_call(
        functools.partial(_merge_body, x_batch_major=x_batch_major, out_batch_major=out_batch_major),
        grid=(rows // tm,),
        in_specs=[
            x_spec,
            pl.BlockSpec((None, batch, steps, pdim), lambda i: (layer, 0, i, 0)),
            row_spec(w), row_spec(w),
            pl.BlockSpec((tm // batch, batch * w), lambda i: (i, 0)),
            pl.BlockSpec((tm // S5_CHUNK, S5_CHUNK * w), lambda i: (i, 0)), row_spec(w),
            resident((1, d)),
            *gate_cols, *merge_cols,
            *[pl.BlockSpec((None, None, w, d), lambda i, n=n: (layer, n, 0, 0), pipeline_mode=once)
              for n in range(N_BRANCH)],
            resident((d, d)), resident((d, d)), resident((pdim, d)),
            resident((w, w)), resident((1, w)),
        ],
        out_specs=out_spec,
        out_shape=out_shape,
        scratch_shapes=[pltpu.VMEM((W_TILES, tm, LANES), F32),
                        pltpu.VMEM((pdim // LANES, tm, LANES), F32),
                        pltpu.VMEM((d // LANES, tm, LANES), F32)],
        compiler_params=_params("parallel"),
        name="merge",
    )(x, p, hf, hb, yb, yc, yd, norm_scale, *([w_in] * 8), *([wbr] * N_BRANCH), wout, pgate, pproj, glu_w, glu_b)


def _block_diag(blocks):
    n, r, c = blocks.shape[-3:]
    spread = np.tile(np.eye(c, dtype=np.float32), (1, n))
    keep = np.kron(np.eye(n, dtype=np.float32), np.ones((r, c), np.float32))
    rows = blocks.reshape(blocks.shape[:-3] + (n * r, c))
    full = jnp.einsum('...rc,cm->...rm', rows, jnp.asarray(spread, blocks.dtype), precision=lax.Precision.HIGHEST)
    return full * jnp.asarray(keep, blocks.dtype)


def _attn_bias_tables(rpb):
    qc = np.arange(GRID_W)[:, None]
    kc = np.arange(GRID_W)[None, :]
    ws = np.clip(qc - NA_COLS // 2, 0, GRID_W - NA_COLS)
    in_win = (kc >= ws) & (kc < ws + NA_COLS)
    dc = np.clip(kc - qc, -(NA_COLS - 1), NA_COLS - 1) + NA_COLS - 1
    nrel = 2 * NA_COLS - 1
    onehot = ((dc[None] == np.arange(nrel)[:, None, None]) & in_win[None]).astype(np.float32)
    onehot2 = np.zeros((2, nrel, GRID_W, 2, GRID_W), np.float32)
    onehot2[0, :, :, 0, :] = onehot
    onehot2[1, :, :, 1, :] = onehot
    onehot2 = onehot2.reshape(2 * nrel, GRID_W, 2 * GRID_W)
    rows2 = jnp.concatenate([rpb.astype(F32)[:, :-1], rpb.astype(F32)[:, 1:]], axis=-1)
    tab = jnp.einsum('hmx,xql->hmql', rows2, jnp.asarray(onehot2), precision=lax.Precision.HIGHEST)
    tab = tab + jnp.asarray(np.tile(np.where(in_win, 0.0, -1e30).astype(np.float32), (1, 2)))
    nr = tab.shape[1]
    tab = tab.reshape(NA_HEADS // 2, 2, nr, GRID_W, 2 * GRID_W)
    return jnp.transpose(tab, (0, 2, 1, 3, 4)).reshape(NA_HEADS // 2, nr, 2 * GRID_W, 2 * GRID_W)


def _s5_matrices(a_re, a_im, log_dt, b_re, b_im, c_re, c_im, d_skip):
    L, G, P, C = S5_CHUNK, SSM_GROUPS, SSM_STATE, SSM_GROUP
    f32 = F32
    hi = lax.Precision.HIGHEST
    lr = jnp.minimum(a_re.astype(f32), -1e-4)
    li = a_im.astype(f32)
    dt = jnp.exp(log_dt.astype(f32))[..., None]
    steps_n = jnp.arange(L + 1, dtype=f32)[:, None, None, None]
    pw_mag = jnp.exp(steps_n * (lr * dt))
    pw_r = pw_mag * jnp.cos(steps_n * (li * dt))
    pw_i = pw_mag * jnp.sin(steps_n * (li * dt))
    ab_r, ab_i = pw_r[1], pw_i[1]
    nr = ab_r - 1.0
    den = lr * lr + li * li
    fr = ((nr * lr + ab_i * li) / den)[:, :, None, :]
    fi = ((ab_i * lr - nr * li) / den)[:, :, None, :]
    bt_r = jnp.swapaxes(b_re.astype(f32), -1, -2)
    bt_i = jnp.swapaxes(b_im.astype(f32), -1, -2)
    bb_r = fr * bt_r - fi * bt_i
    bb_i = fr * bt_i + fi * bt_r
    cr, ci = c_re.astype(f32), c_im.astype(f32)
    m_r = pw_r[:, :, :, None, :] * bb_r[None] - pw_i[:, :, :, None, :] * bb_i[None]
    m_i = pw_r[:, :, :, None, :] * bb_i[None] + pw_i[:, :, :, None, :] * bb_r[None]
    k_lag = (jnp.einsum('dgop,ndgip->ndgio', cr, m_r[:L], precision=hi)
             - jnp.einsum('dgop,ndgip->ndgio', ci, m_i[:L], precision=hi))
    k_signed = jnp.concatenate([k_lag[:0:-1, 1], k_lag[:1, 0] + k_lag[:1, 1], k_lag[1:, 0]])
    tile_c = jnp.asarray(np.tile(np.eye(C, dtype=np.float32), (1, L)))
    rep_t = jnp.asarray(np.repeat(np.eye(L, dtype=np.float32), C, axis=1))
    k_wide = jnp.einsum('ngio,ol->ngil', k_signed, tile_c, precision=hi)
    lag = (lax.broadcasted_iota(jnp.int32, (1, L, 1, L * C), 3) // C
           - lax.broadcasted_iota(jnp.int32, (1, L, 1, L * C), 1))
    g_mat = 0.0
    for n in range(2 * L - 1):
        g_mat = g_mat + jnp.where(lag == n - (L - 1), k_wide[n][:, None, :, :], 0.0)
    g_mat = g_mat.reshape(G, L * C, L * C)

    def h_mat(direction, powers):
        re = jnp.transpose(m_r[powers, direction], (1, 0, 2, 3)).reshape(G, L * C, P)
        im = jnp.transpose(m_i[powers, direction], (1, 0, 2, 3)).reshape(G, L * C, P)
        return jnp.concatenate([re, im], axis=-1)

    hf_mat = h_mat(0, slice(L - 1, None, -1))
    hb_mat = h_mat(1, slice(0, L))

    def q_rows(direction, powers):
        spread = lambda c: jnp.einsum('gcp,cl->gpl', c[direction], tile_c, precision=hi)
        along_t = lambda w: jnp.einsum('tgp,tl->gpl', w[powers, direction], rep_t, precision=hi)
        c_r, c_i, w_r, w_i = spread(cr), spread(ci), along_t(pw_r), along_t(pw_i)
        return jnp.concatenate([c_r * w_r - c_i * w_i, -(c_r * w_i + c_i * w_r)], axis=1)

    q_mat = jnp.concatenate([q_rows(0, slice(1, L + 1)), q_rows(1, slice(L, 0, -1))], axis=1)
    npair = G // 2
    a8 = jnp.stack([pw_r[L, 0], pw_i[L, 0], pw_r[L, 1], pw_i[L, 1]])
    a8 = jnp.transpose(a8.reshape(4, W_TILES, (S5_TILE_GROUPS // 2) * 2 * P), (1, 0, 2))
    dsk = jnp.tile(d_skip.astype(f32).reshape(G, 1, C), (1, 1, L)).reshape(npair, 1, 2 * L * C)
    bf = lambda m: m.astype(BF16)
    return bf(g_mat), bf(hf_mat), bf(hb_mat), bf(q_mat), a8, dsk


def kernel(x, p, norm_scale, w_in, lru_conv_w, lru_conv_b, lru_w_r, lru_b_r, lru_w_i, lru_b_i, lru_lambda, na_q_gain, na_k_gain, na_rel_bias, ssm_a_re, ssm_a_im, ssm_log_dt, ssm_b_re, ssm_b_im, ssm_c_re, ssm_c_im, ssm_d, ssm_glu_w, ssm_glu_b, pool_w, pool_scale, w_branch, w_out, ple_proj, ple_gate):
    b, s, d = x.shape
    assert b == SUBLANES and d == D_MODEL and s // GRID_W >= NA_ROWS
    for rows_per_step in (PROJ_ROWS, MERGE_ROWS, LRU_STEPS * b, POOL_STEPS * b, S5_BLOCK_CHUNKS * S5_CHUNK * b,
                          ATTN_ROWS * GRID_W * b):
        assert (s * b) % rows_per_step == 0
    depth = w_in.shape[0]

    norm = norm_scale.astype(F32)[:, None, :]
    w_in16 = w_in.astype(BF16)
    ones_bd = _block_diag(jnp.ones((MXU_DIM // NA_HEAD_DIM, NA_HEAD_DIM, NA_HEAD_DIM), BF16))
    q_gain = (jnp.tile(na_q_gain.astype(F32), (1, NA_HEADS)) * (NA_HEAD_DIM ** -0.5 * LOG2E))[:, None, :]
    k_gain = jnp.tile(na_k_gain.astype(F32), (1, NA_HEADS))[:, None, :]
    conv_w = lru_conv_w.astype(F32)
    conv_b = lru_conv_b.astype(F32)[:, None, :]
    gate_w = (0.5 * jnp.concatenate([_block_diag(lru_w_r), _block_diag(lru_w_i)], axis=-1)).astype(BF16)
    gate_b = (0.5 * jnp.concatenate([lru_b_r, lru_b_i], axis=-1).astype(F32))[:, :, None, :]
    c8 = (-0.5 * LRU_C * LOG2E * jax.nn.softplus(-lru_lambda.astype(F32)))[:, :, None, :]
    bias_tab = jax.vmap(_attn_bias_tables)(na_rel_bias.astype(F32) * LOG2E)
    s5_mats = jax.vmap(_s5_matrices)(ssm_a_re, ssm_a_im, ssm_log_dt, ssm_b_re, ssm_b_im, ssm_c_re, ssm_c_im, ssm_d)
    glu_w = ssm_glu_w.astype(BF16)
    glu_b = ssm_glu_b.astype(F32)[:, None, :]
    pool_bd = _block_diag(pool_w).astype(BF16)
    pool_sc = pool_scale.astype(F32)[:, None, :]
    branch_scale = jnp.asarray([0.5] + [1.0] * (N_BRANCH - 1), F32)[None, :, None, None]
    wbr = (w_branch.astype(F32) * branch_scale).astype(BF16)
    wout = w_out.astype(BF16)
    pproj = ple_proj.astype(BF16)
    pgate = ple_gate.astype(BF16)

    xt = x
    for i in range(depth):
        xc, q, k, v, cx, dx = _inproj(xt, i, norm, w_in16, ones_bd, q_gain, k_gain, conv_w, conv_b, b)
        hf, hb = _lru(xc, i, gate_w, gate_b, c8)
        yb = _attn(q, k, v, i, bias_tab, b)
        yc = _s5(cx, i, s5_mats)
        yd = _pool(dx, i, pool_bd, pool_sc, s)
        xt = _merge(xt, p, i, hf, hb, yb, yc, yd, norm, w_in16, wbr, wout, pgate, pproj, glu_w, glu_b, b,
                    out_batch_major=(i == depth - 1))
    return xt
```

```python
import functools

import jax
import jax.numpy as jnp
import numpy as np
from jax import lax
from jax.experimental import pallas as pl
from jax.experimental.pallas import tpu as pltpu

F32 = jnp.float32
BF16 = jnp.bfloat16

D_MODEL = 1024
BRANCH_W = 512
N_BRANCH = 4
NORM_EPS = 1e-6
LOG2E = 1.4426950408889634
GRID_W = 64
LRU_C = 8.0
LRU_CONV_W = 4
LRU_HALO_STEPS = 2
NA_HEADS = 8
NA_HEAD_DIM = 64
NA_ROWS = 8
NA_COLS = 16
SSM_GROUP = 16
SSM_GROUPS = 32
SSM_STATE = 64
POOL_WINDOWS = (2, 4, 8, 16)
POOL_GROUP = 128

SUBLANES = 8
LANES = 128
MXU_DIM = 256
W_TILES = BRANCH_W // LANES
S5_CHUNK = 8

COL_AX, COL_AG, COL_Q, COL_K, COL_V, COL_BG, COL_CX, COL_CG, COL_DX, COL_DG = range(10)
COL_MERGE = 10 * BRANCH_W // D_MODEL

PROJ_ROWS = 1024
MERGE_ROWS = 512
LRU_STEPS = 256
POOL_STEPS = 256
ATTN_ROWS = 16
S5_BLOCK_CHUNKS = 256
ELEM_ROWS = 256

VMEM_LIMIT = 56 * 1024 * 1024


def _params(*sem):
    return pltpu.CompilerParams(dimension_semantics=sem, vmem_limit_bytes=VMEM_LIMIT)


def _dot(a, b):
    return jnp.dot(a, b, preferred_element_type=F32)


def _sigmoid(z):
    return 0.5 * jnp.tanh(0.5 * z) + 0.5


def _silu(z):
    return z * _sigmoid(z)


def _gelu_tanh(y):
    return 0.5 * y * (1.0 + jnp.tanh(0.7978845608028654 * (y + 0.044715 * (y * y * y))))


def _rms_rows(x, g):
    ms = jnp.mean(x * x, axis=-1, keepdims=True)
    return x * lax.rsqrt(ms + NORM_EPS) * g


def _head_mean_square(v, ones_ref):
    v2 = (v * v).astype(BF16)
    n = ones_ref.shape[0]
    sums = [_dot(v2[:, c:c + n], ones_ref[...]) for c in range(0, v.shape[1], n)]
    return jnp.concatenate(sums, axis=-1) * (1.0 / NA_HEAD_DIM)


def _layer_spec(shape, layer, *tail):
    tail = tail or (0,) * len(shape)
    return pl.BlockSpec((None,) + tuple(shape), lambda *_: (layer,) + tuple(tail))


S5_TILE_GROUPS = LANES // SSM_GROUP
S5_GROUP_FLAT = S5_CHUNK * SSM_GROUP


def _block_transpose(xs):
    n = len(xs)
    lane = lax.broadcasted_iota(jnp.int32, (1, LANES), 1)
    cur = list(xs)
    width, stride = LANES // 2, n // 2
    while stride >= 1:
        low = (lane & width) == 0
        nxt = list(cur)
        for i in range(n):
            if i & stride:
                continue
            a, b = cur[i], cur[i + stride]
            nxt[i] = jnp.where(low, a, pltpu.roll(b, width, 1))
            nxt[i + stride] = jnp.where(low, pltpu.roll(a, LANES - width, 1), b)
        cur = nxt
        width //= 2
        stride //= 2
    return cur


def _load_time_major(src_ref, slab_ref, s0=0, s1=None):
    nb, steps, n = src_ref.shape
    s1 = steps if s1 is None else s1
    tiles = n // LANES
    for b in range(nb):
        for l in range(tiles):
            slab_ref[l, pl.ds(b, s1 - s0, stride=nb), :] = src_ref[b, s0:s1, l * LANES:(l + 1) * LANES]
    return jnp.concatenate([slab_ref[l, 0:(s1 - s0) * nb, :] for l in range(tiles)], axis=-1)


def _store_batch_major(val, dst_ref, slab_ref):
    nb, steps, n = dst_ref.shape
    tiles = n // LANES
    for l in range(tiles):
        slab_ref[l] = val[:, l * LANES:(l + 1) * LANES]
    for b in range(nb):
        for l in range(tiles):
            dst_ref[b, :, l * LANES:(l + 1) * LANES] = slab_ref[l, pl.ds(b, steps, stride=nb), :]


def _inproj_body(x_ref, xp_ref, xn_ref, ns_ref, wax_ref, wq_ref, wk_ref, wv_ref, wcx_ref, wdx_ref, ones_ref,
                 qg_ref, kg_ref, cw_ref, cb_ref,
                 xc_ref, q_ref, k_ref, v_ref, cx_ref, dx_ref, stage_ref, *slabs):
    i = pl.program_id(0)
    halo = LRU_HALO_STEPS * SUBLANES
    if slabs:
        x = _load_time_major(x_ref, slabs[0])
        hs = xp_ref.shape[1]
        x_prev = _load_time_major(xp_ref, slabs[1], hs - LRU_HALO_STEPS, hs)
        x_next = _load_time_major(xn_ref, slabs[1], 0, LRU_HALO_STEPS)
    else:
        x = x_ref[...]
        x_prev = xp_ref[xp_ref.shape[0] - halo:, :]
        x_next = xn_ref[0:halo, :]
    hn = _rms_rows(x, ns_ref[...]).astype(BF16)
    steps = stage_ref.shape[1] // SUBLANES

    def to_batch_lanes(val, out_ref):
        for l in range(W_TILES):
            stage_ref[l] = val[:, l * LANES:(l + 1) * LANES]
        for b in range(SUBLANES):
            for l in range(W_TILES):
                c0 = b * BRANCH_W + l * LANES
                out_ref[:, c0:c0 + LANES] = stage_ref[l, pl.ds(b, steps, stride=SUBLANES), :].astype(BF16)

    hn_ext = jnp.concatenate([_rms_rows(x_prev, ns_ref[...]).astype(BF16), hn,
                              _rms_rows(x_next, ns_ref[...]).astype(BF16)], axis=0)
    ax = _dot(hn_ext, wax_ref[...])
    r = hn.shape[0]
    row = lax.broadcasted_iota(jnp.int32, (r + 2 * halo, 1), 0)
    outside = ((row < halo) & (i == 0)) | ((row >= r + halo) & (i == pl.num_programs(0) - 1))
    ax = jnp.where(outside, 0.0, ax)
    xc = cb_ref[...]
    for tap in range(LRU_CONV_W):
        xc = xc + cw_ref[tap:tap + 1, :] * ax[tap * SUBLANES:tap * SUBLANES + r, :]
    xc_ref[...] = xc.astype(BF16)
    q = _dot(hn, wq_ref[...])
    to_batch_lanes(q * lax.rsqrt(_head_mean_square(q, ones_ref) + NORM_EPS) * qg_ref[...], q_ref)
    k = _dot(hn, wk_ref[...])
    to_batch_lanes(k * lax.rsqrt(_head_mean_square(k, ones_ref) + NORM_EPS) * kg_ref[...], k_ref)
    to_batch_lanes(_dot(hn, wv_ref[...]), v_ref)
    cxv = _dot(hn, wcx_ref[...])
    nchunk = cxv.shape[0] // (S5_CHUNK * SUBLANES)
    for o in range(W_TILES):
        tile3 = cxv[:, o * LANES:(o + 1) * LANES].reshape(nchunk, S5_CHUNK * SUBLANES, LANES)
        per_t = [tile3[:, t * SUBLANES:(t + 1) * SUBLANES, :].reshape(nchunk * SUBLANES, LANES)
                 for t in range(S5_CHUNK)]
        for g, ug in enumerate(_block_transpose(per_t)):
            c0 = (o * S5_TILE_GROUPS + g) * S5_GROUP_FLAT
            cx_ref[:, c0:c0 + S5_GROUP_FLAT] = ug.astype(BF16)
    dx_ref[...] = _dot(hn, wdx_ref[...]).astype(BF16)


def _inproj(x, layer, norm_scale, w_in, ones_bd, q_gain, k_gain, conv_w, conv_b, batch):
    w = BRANCH_W
    tm = PROJ_ROWS
    batch_major = x.ndim == 3
    d = x.shape[-1]
    rows = x.shape[0] * x.shape[1] if batch_major else x.shape[0]
    s = rows // batch
    nt = rows // tm
    if batch_major:
        hs = SUBLANES
        per = tm // batch // hs
        x_spec = pl.BlockSpec((batch, tm // batch, d), lambda i: (0, i, 0))
        prev_spec = pl.BlockSpec((batch, hs, d), lambda i: (0, jnp.maximum(i * per - 1, 0), 0))
        next_spec = pl.BlockSpec((batch, hs, d), lambda i: (0, jnp.minimum((i + 1) * per, s // hs - 1), 0))
        slabs = [pltpu.VMEM((d // LANES, tm, LANES), F32),
                 pltpu.VMEM((d // LANES, LRU_HALO_STEPS * batch, LANES), F32)]
    else:
        hb = LRU_HALO_STEPS * batch
        per = tm // hb
        x_spec = pl.BlockSpec((tm, d), lambda i: (i, 0))
        prev_spec = pl.BlockSpec((hb, d), lambda i: (jnp.maximum(i * per - 1, 0), 0))
        next_spec = pl.BlockSpec((hb, d), lambda i: (jnp.minimum((i + 1) * per, rows // hb - 1), 0))
        slabs = []
    wcol = lambda j: pl.BlockSpec((None, d, w), lambda i: (layer, 0, j))
    tm_spec = pl.BlockSpec((tm, w), lambda i: (i, 0))
    bl_spec = pl.BlockSpec((tm // batch, batch * w), lambda i: (i, 0))
    tm_shape = jax.ShapeDtypeStruct((rows, w), BF16)
    bl_shape = jax.ShapeDtypeStruct((s, batch * w), BF16)
    gf_spec = pl.BlockSpec((tm // S5_CHUNK, S5_CHUNK * w), lambda i: (i, 0))
    gf_shape = jax.ShapeDtypeStruct((rows // S5_CHUNK, S5_CHUNK * w), BF16)
    return pl.pallas_call(
        _inproj_body,
        grid=(nt,),
        in_specs=[
            x_spec, prev_spec, next_spec,
            _layer_spec((1, d), layer),
            wcol(COL_AX), wcol(COL_Q), wcol(COL_K), wcol(COL_V), wcol(COL_CX), wcol(COL_DX),
            pl.BlockSpec(ones_bd.shape, lambda i: (0, 0)),
            _layer_spec((1, w), layer), _layer_spec((1, w), layer),
            _layer_spec((LRU_CONV_W, w), layer), _layer_spec((1, w), layer),
        ],
        out_specs=[tm_spec, bl_spec, bl_spec, bl_spec, gf_spec, tm_spec],
        out_shape=[tm_shape, bl_shape, bl_shape, bl_shape, gf_shape, tm_shape],
        scratch_shapes=[pltpu.VMEM((W_TILES, tm, LANES), F32)] + slabs,
        compiler_params=_params("parallel"),
        name="inproj",
    )(x, x, x, norm_scale, w_in, w_in, w_in, w_in, w_in, w_in, ones_bd, q_gain, k_gain, conv_w, conv_b)


def _fill_extended(ext_ref, main_ref, prev_ref, next_ref, n_prev, n_next, is_first, is_last):
    r = main_ref.shape[0]
    prev = prev_ref[...].astype(F32)
    ext_ref[0:n_prev, :] = jnp.where(is_first, 0.0, prev[prev.shape[0] - n_prev:, :])
    ext_ref[n_prev:n_prev + r, :] = main_ref[...].astype(F32)
    nxt = next_ref[...].astype(F32)
    ext_ref[n_prev + r:n_prev + r + n_next, :] = jnp.where(is_last, 0.0, nxt[0:n_next, :])


def _lru_body(xf_ref, xb_ref, wg_ref, bg_ref, ch_ref, hf_ref, hb_ref, af_ref, bf_ref, ab_ref, bb_ref, carry_ref):
    i = pl.program_id(0)
    w = BRANCH_W
    r = xf_ref.shape[0]
    steps = r // SUBLANES

    @pl.when(i == 0)
    def _():
        carry_ref[...] = jnp.zeros_like(carry_ref)

    def prepare(direction, x_ref, a_ref, b_ref):
        def sub(sb, _):
            rows = pl.ds(pl.multiple_of(sb * ELEM_ROWS, ELEM_ROWS), ELEM_ROWS)
            xcb = x_ref[rows, :]
            xc = xcb.astype(F32)
            g = _dot(xcb, wg_ref[direction]) + bg_ref[direction]
            tr = jnp.tanh(g[:, 0:w]) + 1.0
            ti = jnp.tanh(g[:, w:2 * w]) + 1.0
            a = jnp.exp2(ch_ref[direction] * tr)
            a_ref[rows, :] = a
            om = 1.0 - a * a
            b_ref[rows, :] = om * lax.rsqrt(jnp.maximum(om, 1e-37)) * (ti * xc)
            return 0

        lax.fori_loop(0, r // ELEM_ROWS, sub, 0, unroll=True)

    prepare(0, xf_ref, af_ref, bf_ref)
    prepare(1, xb_ref, ab_ref, bb_ref)

    def step(t, carry):
        hf, hb = carry
        rf = pl.ds(pl.multiple_of(t * SUBLANES, SUBLANES), SUBLANES)
        hf = af_ref[rf, :] * hf + bf_ref[rf, :]
        bf_ref[rf, :] = hf
        rb = pl.ds(pl.multiple_of((steps - 1 - t) * SUBLANES, SUBLANES), SUBLANES)
        hb = ab_ref[rb, :] * hb + bb_ref[rb, :]
        bb_ref[rb, :] = hb
        return hf, hb

    hf, hb = lax.fori_loop(0, steps, step, (carry_ref[0], carry_ref[1]), unroll=8)
    carry_ref[0] = hf
    carry_ref[1] = hb
    hf_ref[...] = bf_ref[...].astype(BF16)
    hb_ref[...] = bb_ref[...].astype(BF16)


def _lru(xc, layer, w_gate, b_gate, c8):
    rows, w = xc.shape
    r = LRU_STEPS * SUBLANES
    nt = rows // r
    fwd = pl.BlockSpec((r, w), lambda i: (i, 0))
    bwd = pl.BlockSpec((r, w), lambda i: (nt - 1 - i, 0))
    out_shape = jax.ShapeDtypeStruct((rows, w), BF16)
    return pl.pallas_call(
        _lru_body,
        grid=(nt,),
        in_specs=[
            fwd, bwd,
            _layer_spec((2, w, 2 * w), layer),
            _layer_spec((2, 1, 2 * w), layer),
            _layer_spec((2, 1, w), layer),
        ],
        out_specs=[fwd, bwd],
        out_shape=[out_shape, out_shape],
        scratch_shapes=[
            pltpu.VMEM((r, w), F32), pltpu.VMEM((r, w), F32),
            pltpu.VMEM((r, w), F32), pltpu.VMEM((r, w), F32),
            pltpu.VMEM((2, SUBLANES, w), F32),
        ],
        compiler_params=_params("arbitrary"),
        name="rglru",
    )(xc, xc, w_gate, b_gate, c8)


def _pool_body(m_ref, p_ref, n_ref, wp_ref, sc_ref, o_ref, ext_ref, pooled_ref, *, seq_len):
    i = pl.program_id(0)
    nt = pl.num_programs(0)
    r = m_ref.shape[0]
    steps = r // SUBLANES
    halo = max(POOL_WINDOWS) // 2
    hr = halo * SUBLANES
    _fill_extended(ext_ref, m_ref, p_ref, n_ref, hr, hr, i == 0, i == nt - 1)

    def pooled_minus_self(clipped):
        if clipped:
            t_glob = i * steps + lax.broadcasted_iota(jnp.int32, (r, POOL_GROUP), 0) // SUBLANES
        for g, win in enumerate(POOL_WINDOWS):
            ls = slice(g * POOL_GROUP, (g + 1) * POOL_GROUP)
            e = ext_ref[:, ls]
            lo_t = -halo
            cur = e
            half = 1
            while half < win:
                n = cur.shape[0] - half * SUBLANES
                if half == 1:
                    cur = cur[0:n, :] + cur[SUBLANES:SUBLANES + n, :]
                    lo_t += 1
                else:
                    sh = (half // 2) * SUBLANES
                    cur = cur[0:n, :] + cur[2 * sh:2 * sh + n, :]
                    lo_t += half // 2
                half *= 2
            off = (0 - lo_t) * SUBLANES
            wsum = cur[off:off + r, :]
            half_w = win // 2
            if clipped:
                cnt = (jnp.clip(t_glob + half_w, 0, seq_len) - jnp.clip(t_glob - half_w, 0, seq_len)).astype(F32)
                mean = wsum / cnt
            else:
                mean = wsum * (1.0 / win)
            pooled_ref[:, ls] = mean - e[hr:hr + r, :]

    at_edge = (i == 0) | (i == nt - 1)

    @pl.when(at_edge)
    def _():
        pooled_minus_self(True)

    @pl.when(jnp.logical_not(at_edge))
    def _():
        pooled_minus_self(False)

    o_ref[...] = (_dot(pooled_ref[...].astype(BF16), wp_ref[...]) * sc_ref[...]).astype(BF16)


def _pool(dx, layer, w_pool_bd, scale, seq_len):
    rows, w = dx.shape
    r = POOL_STEPS * SUBLANES
    nt = rows // r
    assert POOL_STEPS >= max(POOL_WINDOWS) // 2
    hb = (max(POOL_WINDOWS) // 2) * SUBLANES
    per = r // hb
    last_hb = rows // hb - 1
    return pl.pallas_call(
        functools.partial(_pool_body, seq_len=seq_len),
        grid=(nt,),
        in_specs=[
            pl.BlockSpec((r, w), lambda i: (i, 0)),
            pl.BlockSpec((hb, w), lambda i: (jnp.maximum(i * per - 1, 0), 0)),
            pl.BlockSpec((hb, w), lambda i: (jnp.minimum((i + 1) * per, last_hb), 0)),
            _layer_spec((w, w), layer),
            _layer_spec((1, w), layer),
        ],
        out_specs=pl.BlockSpec((r, w), lambda i: (i, 0)),
        out_shape=jax.ShapeDtypeStruct((rows, w), BF16),
        scratch_shapes=[pltpu.VMEM((r + 2 * hb, w), F32), pltpu.VMEM((r, w), F32)],
        compiler_params=_params("parallel"),
        name="pool",
    )(dx, dx, dx, w_pool_bd, scale)


def _attn_body(q_ref, k_ref, v_ref, bias_ref, o_ref, *, grid_rows):
    g = pl.program_id(1)
    lane = lax.broadcasted_iota(jnp.int32, (GRID_W, LANES), 1)
    lo_half = lane < NA_HEAD_DIM
    nkeys = NA_ROWS * GRID_W

    def row_body(rr, _):
        r = g * ATTN_ROWS + rr
        rs = jnp.clip(r - NA_ROWS // 2, 0, grid_rows - NA_ROWS)
        d0 = rs - r + (NA_ROWS - 1)
        k0 = pl.multiple_of(rs * GRID_W, GRID_W)
        q0 = pl.multiple_of(rr * GRID_W, GRID_W)
        scores = []
        for hp in range(NA_HEADS // 2):
            ls = slice(hp * LANES, (hp + 1) * LANES)
            kp = k_ref[pl.ds(k0, nkeys), ls]
            qp = q_ref[pl.ds(q0, GRID_W), ls]
            zero = jnp.zeros_like(qp)
            qm = jnp.concatenate([jnp.where(lo_half, qp, zero), jnp.where(lo_half, zero, qp)], axis=0)
            sc = lax.dot_general(qm, kp, (((1,), (1,)), ((), ())), preferred_element_type=F32)
            bias = jnp.concatenate([bias_ref[hp, d0 + 2 * j] for j in range(NA_ROWS // 2)], axis=-1)
            scores.append(sc + bias)
        maxes = [jnp.max(sc, axis=-1, keepdims=True) for sc in scores]
        probs = [jnp.exp2(sc - m) for sc, m in zip(scores, maxes)]
        inv = [1.0 / jnp.sum(p, axis=-1, keepdims=True) for p in probs]
        for hp in range(NA_HEADS // 2):
            ls = slice(hp * LANES, (hp + 1) * LANES)
            vp = v_ref[pl.ds(k0, nkeys), ls]
            o = _dot(probs[hp].astype(BF16), vp) * inv[hp]
            o_ref[pl.ds(q0, GRID_W), ls] = jnp.where(lo_half, o[0:GRID_W], o[GRID_W:2 * GRID_W]).astype(BF16)
        return 0

    lax.fori_loop(0, ATTN_ROWS, row_body, 0, unroll=4)


def _attn(q, k, v, layer, bias_tab, batch):
    s, bw = q.shape
    w = bw // batch
    grid_rows = s // GRID_W
    ng = grid_rows // ATTN_ROWS
    qr = ATTN_ROWS * GRID_W
    q_spec = pl.BlockSpec((qr, w), lambda b, g: (g, b))
    kv_spec = pl.BlockSpec((s, w), lambda b, g: (0, b))
    return pl.pallas_call(
        functools.partial(_attn_body, grid_rows=grid_rows),
        grid=(batch, ng),
        in_specs=[q_spec, kv_spec, kv_spec, _layer_spec(bias_tab.shape[1:], layer)],
        out_specs=q_spec,
        out_shape=jax.ShapeDtypeStruct((s, bw), BF16),
        compiler_params=_params("parallel", "parallel"),
        name="natten",
    )(q, k, v, bias_tab)


def _s5_assemble(gc_ref, hfc_ref, hbc_ref, qc_ref, g_ref, hf_ref, hb_ref, q_ref):
    half = LANES // 2
    lo = lax.broadcasted_iota(jnp.int32, (S5_GROUP_FLAT, LANES), 1) < half
    g_ref[...] = jnp.zeros_like(g_ref)
    q_ref[...] = jnp.zeros_like(q_ref)
    for q in range(S5_TILE_GROUPS // 2):
        for e in range(2):
            g = 2 * q + e
            rows = slice(e * S5_GROUP_FLAT, (e + 1) * S5_GROUP_FLAT)
            g_ref[q, rows, e * S5_GROUP_FLAT:(e + 1) * S5_GROUP_FLAT] = gc_ref[g]
            for src, dst in ((hfc_ref, hf_ref), (hbc_ref, hb_ref)):
                h = src[g].astype(F32)
                swapped = pltpu.roll(h, half, 1)
                if e == 0:
                    re_tile, im_tile = jnp.where(lo, h, 0.0), jnp.where(lo, swapped, 0.0)
                else:
                    re_tile, im_tile = jnp.where(lo, 0.0, swapped), jnp.where(lo, 0.0, h)
                dst[q, rows, 0:LANES] = re_tile.astype(BF16)
                dst[q, rows, LANES:2 * LANES] = im_tile.astype(BF16)
            for blk in range(4):
                q_ref[q, blk * LANES + e * half:blk * LANES + (e + 1) * half,
                      e * S5_GROUP_FLAT:(e + 1) * S5_GROUP_FLAT] = qc_ref[g, blk * half:(blk + 1) * half, :]


def _s5_body(u_ref, gc_ref, hfc_ref, hbc_ref, qc_ref, a_ref, dsk_ref,
             y_ref, g_ref, hf_ref, hb_ref, q_ref, sinb_ref, sloc_ref, sin_ref, carry_ref):
    p = pl.program_id(1)
    j = pl.program_id(2)
    nblk = pl.num_programs(2)

    @pl.when((j == 0) & (p == 0))
    def _():
        _s5_assemble(gc_ref, hfc_ref, hbc_ref, qc_ref, g_ref, hf_ref, hb_ref, q_ref)

    rb = u_ref.shape[0]
    nk = rb // SUBLANES
    npairs = S5_TILE_GROUPS // 2
    pf = 2 * S5_GROUP_FLAT
    re_t = lambda q: slice(2 * q * LANES, (2 * q + 1) * LANES)
    im_t = lambda q: slice((2 * q + 1) * LANES, (2 * q + 2) * LANES)
    pair_t = lambda q: slice(q * pf, (q + 1) * pf)
    coef_t = lambda q: slice(q * LANES, (q + 1) * LANES)

    @pl.when(j == 0)
    def _():
        carry_ref[...] = jnp.zeros_like(carry_ref)

    def sweep(direction, reverse):
        a_re = a_ref[2 * direction:2 * direction + 1, :]
        a_im = a_ref[2 * direction + 1:2 * direction + 2, :]

        def body(n, state):
            kk = (nk - 1 - n) if reverse else n
            rows = pl.ds(pl.multiple_of(kk * SUBLANES, SUBLANES), SUBLANES)
            new = []
            for q in range(npairs):
                sr, si = state[2 * q], state[2 * q + 1]
                sin_ref[rows, re_t(q)] = sr
                sin_ref[rows, im_t(q)] = si
                ar, ai = a_re[:, coef_t(q)], a_im[:, coef_t(q)]
                new.append(ar * sr - ai * si + sloc_ref[rows, re_t(q)])
                new.append(ar * si + ai * sr + sloc_ref[rows, im_t(q)])
            return tuple(new)

        init = tuple(carry_ref[:, t * LANES:(t + 1) * LANES] for t in range(2 * npairs))
        out = lax.fori_loop(0, nk, body, init, unroll=4)
        for t in range(2 * npairs):
            carry_ref[:, t * LANES:(t + 1) * LANES] = out[t]

    @pl.when(p == 0)
    def _():
        for q in range(npairs):
            sloc_ref[:, pair_t(q)] = _dot(u_ref[:, pair_t(q)], hb_ref[q])
        sweep(1, True)
        blk = nblk - 1 - j
        sinb_ref[pl.ds(pl.multiple_of(blk * rb, rb), rb), :] = sin_ref[...].astype(BF16)

    @pl.when(p == 1)
    def _():
        for q in range(npairs):
            sloc_ref[:, pair_t(q)] = _dot(u_ref[:, pair_t(q)], hf_ref[q])
        sweep(0, False)
        rows_b = pl.ds(pl.multiple_of(j * rb, rb), rb)
        for q in range(npairs):
            u = u_ref[:, pair_t(q)]
            states = jnp.concatenate([sin_ref[:, pair_t(q)].astype(BF16), sinb_ref[rows_b, pair_t(q)]], axis=-1)
            y = _dot(u, g_ref[q]) + _dot(states, q_ref[q]) + dsk_ref[q] * u.astype(F32)
            y_ref[:, pair_t(q)] = y.astype(BF16)


def _s5(cx, layer, mats):
    g_m, hf_m, hb_m, q_m, a8, dsk = mats
    rows, width = cx.shape
    rb = S5_BLOCK_CHUNKS * SUBLANES
    nblk = rows // rb
    tile_w = S5_TILE_GROUPS * S5_GROUP_FLAT
    npairs, pf = S5_TILE_GROUPS // 2, 2 * S5_GROUP_FLAT
    state_w = S5_TILE_GROUPS * 2 * SSM_STATE
    ng, gf = S5_TILE_GROUPS, S5_GROUP_FLAT
    per_tile = lambda n, *shape: pl.BlockSpec((None, n) + shape, lambda o, p, j: (layer, o) + (0,) * len(shape))
    return pl.pallas_call(
        _s5_body,
        grid=(width // tile_w, 2, nblk),
        in_specs=[
            pl.BlockSpec((rb, tile_w), lambda o, p, j: (j + (1 - p) * (nblk - 1 - 2 * j), o)),
            per_tile(ng, gf, gf), per_tile(ng, gf, LANES), per_tile(ng, gf, LANES), per_tile(ng, 2 * LANES, gf),
            pl.BlockSpec((None, None, 4, npairs * LANES), lambda o, p, j: (layer, o, 0, 0)),
            per_tile(npairs, 1, pf),
        ],
        out_specs=pl.BlockSpec((rb, tile_w), lambda o, p, j: (p * j, o)),
        out_shape=jax.ShapeDtypeStruct((rows, width), BF16),
        scratch_shapes=[
            pltpu.VMEM((npairs, pf, pf), BF16),
            pltpu.VMEM((npairs, pf, 2 * LANES), BF16),
            pltpu.VMEM((npairs, pf, 2 * LANES), BF16),
            pltpu.VMEM((npairs, 4 * LANES, pf), BF16),
            pltpu.VMEM((rows, state_w), BF16),
            pltpu.VMEM((rb, state_w), F32),
            pltpu.VMEM((rb, state_w), F32),
            pltpu.VMEM((SUBLANES, state_w), F32),
        ],
        compiler_params=_params("arbitrary", "arbitrary", "arbitrary"),
        name="s5",
    )(cx, g_m, hf_m, hb_m, q_m, a8, dsk)


def _merge_body(x_ref, p_ref, hf_ref, hb_ref, yb_ref, yc_ref, yd_ref, ns_ref,
                wag_ref, wbg_ref, wcg_ref, wdg_ref, wm0_ref, wm1_ref, wm2_ref, wm3_ref,
                wb0_ref, wb1_ref, wb2_ref, wb3_ref, wout_ref, pg_ref, pp_ref, gluw_ref, glub_ref,
                o_ref, stage_ref, pslab_ref, xslab_ref, *, x_batch_major, out_batch_major):
    x = _load_time_major(x_ref, xslab_ref) if x_batch_major else x_ref[...]
    hn = _rms_rows(x, ns_ref[...]).astype(BF16)

    def branch(y, wb_ref, wm_ref):
        return _dot(y.astype(BF16), wb_ref[...]) * _sigmoid(_dot(hn, wm_ref[...]))

    ya = (hf_ref[...].astype(F32) + hb_ref[...].astype(F32)) * _silu(_dot(hn, wag_ref[...]))
    merged = branch(ya, wb0_ref, wm0_ref)
    steps = stage_ref.shape[1] // SUBLANES
    for b in range(SUBLANES):
        for l in range(W_TILES):
            c0 = b * BRANCH_W + l * LANES
            stage_ref[l, pl.ds(b, steps, stride=SUBLANES), :] = yb_ref[:, c0:c0 + LANES].astype(F32)
    yb = jnp.concatenate([stage_ref[l] for l in range(W_TILES)], axis=-1)
    merged = merged + branch(yb * _silu(_dot(hn, wbg_ref[...])), wb1_ref, wm1_ref)
    nchunk = yc_ref.shape[0] // SUBLANES
    tiles = []
    for o in range(W_TILES):
        per_g = [yc_ref[:, (o * S5_TILE_GROUPS + g) * S5_GROUP_FLAT:(o * S5_TILE_GROUPS + g + 1) * S5_GROUP_FLAT]
                 .astype(F32) for g in range(S5_TILE_GROUPS)]
        per_t = [v.reshape(nchunk, 1, SUBLANES, LANES) for v in _block_transpose(per_g)]
        tiles.append(jnp.concatenate(per_t, axis=1).reshape(nchunk * S5_CHUNK * SUBLANES, LANES))
    yg = _gelu_tanh(jnp.concatenate(tiles, axis=-1))
    yc = yg * _sigmoid(_dot(yg.astype(BF16), gluw_ref[...]) + glub_ref[...])
    merged = merged + branch(yc * _silu(_dot(hn, wcg_ref[...])), wb2_ref, wm2_ref)
    merged = merged + branch(yd_ref[...].astype(F32) * _silu(_dot(hn, wdg_ref[...])), wb3_ref, wm3_ref)
    x1 = x + _dot(merged.astype(BF16), wout_ref[...])
    emb = _dot(_load_time_major(p_ref, pslab_ref).astype(BF16), pp_ref[...])
    out = x1 + _sigmoid(_dot(x1.astype(BF16), pg_ref[...])) * emb
    if out_batch_major:
        _store_batch_major(out, o_ref, xslab_ref)
    else:
        o_ref[...] = out


def _merge(x, p, layer, hf, hb, yb, yc, yd, norm_scale, w_in, wbr, wout, pgate, pproj, glu_w, glu_b, batch,
           out_batch_major):
    w = BRANCH_W
    tm = MERGE_ROWS
    pdim = p.shape[-1]
    once = pl.Buffered(1)
    x_batch_major = x.ndim == 3
    d = x.shape[-1]
    rows = x.shape[0] * x.shape[1] if x_batch_major else x.shape[0]
    steps = tm // batch
    bm_spec = pl.BlockSpec((batch, steps, d), lambda i: (0, i, 0))
    x_spec = bm_spec if x_batch_major else pl.BlockSpec((tm, d), lambda i: (i, 0))
    if out_batch_major:
        out_spec, out_shape = bm_spec, jax.ShapeDtypeStruct((batch, rows // batch, d), F32)
    else:
        out_spec, out_shape = pl.BlockSpec((tm, d), lambda i: (i, 0)), jax.ShapeDtypeStruct((rows, d), F32)

    def resident(shape, *tail):
        tail = tail or (0,) * len(shape)
        return pl.BlockSpec((None,) + tuple(shape), lambda i: (layer,) + tuple(tail), pipeline_mode=once)

    row_spec = lambda n: pl.BlockSpec((tm, n), lambda i: (i, 0))
    gate_cols = [resident((d, w), 0, j) for j in (COL_AG, COL_BG, COL_CG, COL_DG)]
    merge_cols = [resident((d, d), 0, COL_MERGE + n) for n in range(N_BRANCH)]
    return pl.pallas_call(
        functools.partial(_merge_body, x_batch_major=x_batch_major, out_batch_major=out_batch_major),
        grid=(rows // tm,),
        in_specs=[
            x_spec,
            pl.BlockSpec((None, batch, steps, pdim), lambda i: (layer, 0, i, 0)),
            row_spec(w), row_spec(w),
            pl.BlockSpec((tm // batch, batch * w), lambda i: (i, 0)),
            pl.BlockSpec((tm // S5_CHUNK, S5_CHUNK * w), lambda i: (i, 0)), row_spec(w),
            resident((1, d)),
            *gate_cols, *merge_cols,
            *[pl.BlockSpec((None, None, w, d), lambda i, n=n: (layer, n, 0, 0), pipeline_mode=once)
              for n in range(N_BRANCH)],
            resident((d, d)), resident((d, d)), resident((pdim, d)),
            resident((w, w)), resident((1, w)),
        ],
        out_specs=out_spec,
        out_shape=out_shape,
        scratch_shapes=[pltpu.VMEM((W_TILES, tm, LANES), F32),
                        pltpu.VMEM((pdim // LANES, tm, LANES), F32),
                        pltpu.VMEM((d // LANES, tm, LANES), F32)],
        compiler_params=_params("parallel"),
        name="merge",
    )(x, p, hf, hb, yb, yc, yd, norm_scale, *([w_in] * 8), *([wbr] * N_BRANCH), wout, pgate, pproj, glu_w, glu_b)


def _block_diag(blocks):
    n, r, c = blocks.shape[-3:]
    spread = np.tile(np.eye(c, dtype=np.float32), (1, n))
    keep = np.kron(np.eye(n, dtype=np.float32), np.ones((r, c), np.float32))
    rows = blocks.reshape(blocks.shape[:-3] + (n * r, c))
    full = jnp.einsum('...rc,cm->...rm', rows, jnp.asarray(spread, blocks.dtype), precision=lax.Precision.HIGHEST)
    return full * jnp.asarray(keep, blocks.dtype)


def _attn_bias_tables(rpb):
    qc = np.arange(GRID_W)[:, None]
    kc = np.arange(GRID_W)[None, :]
    ws = np.clip(qc - NA_COLS // 2, 0, GRID_W - NA_COLS)
    in_win = (kc >= ws) & (kc < ws + NA_COLS)
    dc = np.clip(kc - qc, -(NA_COLS - 1), NA_COLS - 1) + NA_COLS - 1
    nrel = 2 * NA_COLS - 1
    onehot = ((dc[None] == np.arange(nrel)[:, None, None]) & in_win[None]).astype(np.float32)
    onehot2 = np.zeros((2, nrel, GRID_W, 2, GRID_W), np.float32)
    onehot2[0, :, :, 0, :] = onehot
    onehot2[1, :, :, 1, :] = onehot
    onehot2 = onehot2.reshape(2 * nrel, GRID_W, 2 * GRID_W)
    rows2 = jnp.concatenate([rpb.astype(F32)[:, :-1], rpb.astype(F32)[:, 1:]], axis=-1)
    tab = jnp.einsum('hmx,xql->hmql', rows2, jnp.asarray(onehot2), precision=lax.Precision.HIGHEST)
    tab = tab + jnp.asarray(np.tile(np.where(in_win, 0.0, -1e30).astype(np.float32), (1, 2)))
    nr = tab.shape[1]
    tab = tab.reshape(NA_HEADS // 2, 2, nr, GRID_W, 2 * GRID_W)
    return jnp.transpose(tab, (0, 2, 1, 3, 4)).reshape(NA_HEADS // 2, nr, 2 * GRID_W, 2 * GRID_W)


def _s5_matrices(a_re, a_im, log_dt, b_re, b_im, c_re, c_im, d_skip):
    L, G, P, C = S5_CHUNK, SSM_GROUPS, SSM_STATE, SSM_GROUP
    f32 = F32
    hi = lax.Precision.HIGHEST
    lr = jnp.minimum(a_re.astype(f32), -1e-4)
    li = a_im.astype(f32)
    dt = jnp.exp(log_dt.astype(f32))[..., None]
    steps_n = jnp.arange(L + 1, dtype=f32)[:, None, None, None]
    pw_mag = jnp.exp(steps_n * (lr * dt))
    pw_r = pw_mag * jnp.cos(steps_n * (li * dt))
    pw_i = pw_mag * jnp.sin(steps_n * (li * dt))
    ab_r, ab_i = pw_r[1], pw_i[1]
    nr = ab_r - 1.0
    den = lr * lr + li * li
    fr = ((nr * lr + ab_i * li) / den)[:, :, None, :]
    fi = ((ab_i * lr - nr * li) / den)[:, :, None, :]
    bt_r = jnp.swapaxes(b_re.astype(f32), -1, -2)
    bt_i = jnp.swapaxes(b_im.astype(f32), -1, -2)
    bb_r = fr * bt_r - fi * bt_i
    bb_i = fr * bt_i + fi * bt_r
    cr, ci = c_re.astype(f32), c_im.astype(f32)
    m_r = pw_r[:, :, :, None, :] * bb_r[None] - pw_i[:, :, :, None, :] * bb_i[None]
    m_i = pw_r[:, :, :, None, :] * bb_i[None] + pw_i[:, :, :, None, :] * bb_r[None]
    k_lag = (jnp.einsum('dgop,ndgip->ndgio', cr, m_r[:L], precision=hi)
             - jnp.einsum('dgop,ndgip->ndgio', ci, m_i[:L], precision=hi))
    k_signed = jnp.concatenate([k_lag[:0:-1, 1], k_lag[:1, 0] + k_lag[:1, 1], k_lag[1:, 0]])
    tile_c = jnp.asarray(np.tile(np.eye(C, dtype=np.float32), (1, L)))
    rep_t = jnp.asarray(np.repeat(np.eye(L, dtype=np.float32), C, axis=1))
    k_wide = jnp.einsum('ngio,ol->ngil', k_signed, tile_c, precision=hi)
    lag = (lax.broadcasted_iota(jnp.int32, (1, L, 1, L * C), 3) // C
           - lax.broadcasted_iota(jnp.int32, (1, L, 1, L * C), 1))
    g_mat = 0.0
    for n in range(2 * L - 1):
        g_mat = g_mat + jnp.where(lag == n - (L - 1), k_wide[n][:, None, :, :], 0.0)
    g_mat = g_mat.reshape(G, L * C, L * C)

    def h_mat(direction, powers):
        re = jnp.transpose(m_r[powers, direction], (1, 0, 2, 3)).reshape(G, L * C, P)
        im = jnp.transpose(m_i[powers, direction], (1, 0, 2, 3)).reshape(G, L * C, P)
        return jnp.concatenate([re, im], axis=-1)

    hf_mat = h_mat(0, slice(L - 1, None, -1))
    hb_mat = h_mat(1, slice(0, L))

    def q_rows(direction, powers):
        spread = lambda c: jnp.einsum('gcp,cl->gpl', c[direction], tile_c, precision=hi)
        along_t = lambda w: jnp.einsum('tgp,tl->gpl', w[powers, direction], rep_t, precision=hi)
        c_r, c_i, w_r, w_i = spread(cr), spread(ci), along_t(pw_r), along_t(pw_i)
        return jnp.concatenate([c_r * w_r - c_i * w_i, -(c_r * w_i + c_i * w_r)], axis=1)

    q_mat = jnp.concatenate([q_rows(0, slice(1, L + 1)), q_rows(1, slice(L, 0, -1))], axis=1)
    npair = G // 2
    a8 = jnp.stack([pw_r[L, 0], pw_i[L, 0], pw_r[L, 1], pw_i[L, 1]])
    a8 = jnp.transpose(a8.reshape(4, W_TILES, (S5_TILE_GROUPS // 2) * 2 * P), (1, 0, 2))
    dsk = jnp.tile(d_skip.astype(f32).reshape(G, 1, C), (1, 1, L)).reshape(npair, 1, 2 * L * C)
    bf = lambda m: m.astype(BF16)
    return bf(g_mat), bf(hf_mat), bf(hb_mat), bf(q_mat), a8, dsk


def kernel(x, p, norm_scale, w_in, lru_conv_w, lru_conv_b, lru_w_r, lru_b_r, lru_w_i, lru_b_i, lru_lambda, na_q_gain, na_k_gain, na_rel_bias, ssm_a_re, ssm_a_im, ssm_log_dt, ssm_b_re, ssm_b_im, ssm_c_re, ssm_c_im, ssm_d, ssm_glu_w, ssm_glu_b, pool_w, pool_scale, w_branch, w_out, ple_proj, ple_gate):
    b, s, d = x.shape
    assert b == SUBLANES and d == D_MODEL and s // GRID_W >= NA_ROWS
    for rows_per_step in (PROJ_ROWS, MERGE_ROWS, LRU_STEPS * b, POOL_STEPS * b, S5_BLOCK_CHUNKS * S5_CHUNK * b,
                          ATTN_ROWS * GRID_W * b):
        assert (s * b) % rows_per_step == 0
    depth = w_in.shape[0]

    norm = norm_scale.astype(F32)[:, None, :]
    w_in16 = w_in.astype(BF16)
    ones_bd = _block_diag(jnp.ones((MXU_DIM // NA_HEAD_DIM, NA_HEAD_DIM, NA_HEAD_DIM), BF16))
    q_gain = (jnp.tile(na_q_gain.astype(F32), (1, NA_HEADS)) * (NA_HEAD_DIM ** -0.5 * LOG2E))[:, None, :]
    k_gain = jnp.tile(na_k_gain.astype(F32), (1, NA_HEADS))[:, None, :]
    conv_w = lru_conv_w.astype(F32)
    conv_b = lru_conv_b.astype(F32)[:, None, :]
    gate_w = (0.5 * jnp.concatenate([_block_diag(lru_w_r), _block_diag(lru_w_i)], axis=-1)).astype(BF16)
    gate_b = (0.5 * jnp.concatenate([lru_b_r, lru_b_i], axis=-1).astype(F32))[:, :, None, :]
    c8 = (-0.5 * LRU_C * LOG2E * jax.nn.softplus(-lru_lambda.astype(F32)))[:, :, None, :]
    bias_tab = jax.vmap(_attn_bias_tables)(na_rel_bias.astype(F32) * LOG2E)
    s5_mats = jax.vmap(_s5_matrices)(ssm_a_re, ssm_a_im, ssm_log_dt, ssm_b_re, ssm_b_im, ssm_c_re, ssm_c_im, ssm_d)
    glu_w = ssm_glu_w.astype(BF16)
    glu_b = ssm_glu_b.astype(F32)[:, None, :]
    pool_bd = _block_diag(pool_w).astype(BF16)
    pool_sc = pool_scale.astype(F32)[:, None, :]
    branch_scale = jnp.asarray([0.5] + [1.0] * (N_BRANCH - 1), F32)[None, :, None, None]
    wbr = (w_branch.astype(F32) * branch_scale).astype(BF16)
    wout = w_out.astype(BF16)
    pproj = ple_proj.astype(BF16)
    pgate = ple_gate.astype(BF16)

    xt = x
    for i in range(depth):
        xc, q, k, v, cx, dx = _inproj(xt, i, norm, w_in16, ones_bd, q_gain, k_gain, conv_w, conv_b, b)
        hf, hb = _lru(xc, i, gate_w, gate_b, c8)
        yb = _attn(q, k, v, i, bias_tab, b)
        yc = _s5(cx, i, s5_mats)
        yd = _pool(dx, i, pool_bd, pool_sc, s)
        xt = _merge(xt, p, i, hf, hb, yb, yc, yd, norm, w_in16, wbr, wout, pgate, pproj, glu_w, glu_b, b,
                    out_batch_major=(i == depth - 1))
    return xt
```
